```python
import jax, jax.numpy as jnp
from jax import lax
import numpy as np

D_MODEL = 2048
BATCH = 8
SEQ = 4096
DEPTH = 4

CHUNK = 64
Q_BLOCK = 128
EPS = 1e-6
NEG_INF = -1e30
ROPE_BASE = 10000.0

RET_HEADS = 8
RET_DK = 128
RET_DV = 256
RET_QK_W = RET_HEADS * RET_DK
RET_V_W = RET_HEADS * RET_DV

MLA_HEADS = 16
MLA_Q_RANK = 512
MLA_KV_RANK = 512
MLA_NOPE = 128
MLA_ROPE = 64
MLA_DV = 128
MLA_V_W = MLA_HEADS * MLA_DV

N_BRANCH = 2
IN_SPLITS = (RET_QK_W, RET_QK_W, RET_V_W, RET_V_W, MLA_Q_RANK, MLA_KV_RANK, MLA_ROPE, MLA_V_W, N_BRANCH * D_MODEL)
D_IN = 2 * RET_QK_W + 2 * RET_V_W + MLA_Q_RANK + MLA_KV_RANK + MLA_ROPE + MLA_V_W + N_BRANCH * D_MODEL

kernel_name = "hybrid_retention_mla_adaln_trunk"


def rms_norm(x, g):
    xf = x.astype(jnp.float32)
    y = xf * lax.rsqrt(jnp.mean(xf * xf, axis=-1, keepdims=True) + EPS)
    return (y * g.astype(jnp.float32)).astype(x.dtype)


def rope_tables(positions, dim):
    inv = 1.0 / (ROPE_BASE ** (jnp.arange(0, dim, 2, dtype=jnp.float32) / dim))
    ang = positions.astype(jnp.float32)[..., None] * inv
    return jnp.cos(ang), jnp.sin(ang)


def apply_rope(x, cos, sin):
    half = x.shape[-1] // 2
    x1, x2 = x[..., :half], x[..., half:]
    cos = cos.astype(x.dtype)
    sin = sin.astype(x.dtype)
    return jnp.concatenate([x1 * cos - x2 * sin, x1 * sin + x2 * cos], axis=-1)


def retention(q, k, v):
    B, S, H, dk = q.shape
    dv = v.shape[-1]
    nc = S // CHUNK
    f32 = jnp.float32
    log_gamma = jnp.log(1.0 - 2.0 ** (-5.0 - jnp.arange(H, dtype=f32)))
    qf = q.astype(f32).reshape(B, nc, CHUNK, H, dk)
    kf = (k.astype(f32) * (dk ** -0.5)).reshape(B, nc, CHUNK, H, dk)
    vf = v.astype(f32).reshape(B, nc, CHUNK, H, dv)
    idx = jnp.arange(CHUNK, dtype=f32)
    dmat = jnp.exp(jnp.abs(idx[:, None] - idx[None, :])[None] * log_gamma[:, None, None])
    scores = jnp.einsum('bnihd,bnjhd->bnhij', qf, kf) * dmat[None, None]
    o_intra = jnp.einsum('bnhij,bnjhe->bnihe', scores, vf)
    xi = jnp.exp((idx + 1.0)[:, None] * log_gamma[None, :])
    zeta = jnp.exp((CHUNK - 1.0 - idx)[:, None] * log_gamma[None, :])
    decay_chunk = jnp.exp(CHUNK * log_gamma)
    q_x = (qf * xi[None, None, :, :, None]).transpose(1, 0, 2, 3, 4)
    k_z = (kf * zeta[None, None, :, :, None]).transpose(1, 0, 2, 3, 4)
    v_t = vf.transpose(1, 0, 2, 3, 4)

    def step(state, inp):
        qc, kc, vc = inp
        o = jnp.einsum('bihd,bhde->bihe', qc, state)
        state = state * decay_chunk[None, :, None, None] + jnp.einsum('bjhd,bjhe->bhde', kc, vc)
        return state, o

    init = jnp.zeros((B, H, dk, dv), f32)
    _, o_cross = lax.scan(step, init, (q_x, k_z, v_t))
    o = o_intra + o_cross.transpose(1, 0, 2, 3, 4)
    return o.reshape(B, S, H, dv)


def head_group_norm(o):
    mu = jnp.mean(o, axis=-1, keepdims=True)
    var = jnp.mean(jnp.square(o - mu), axis=-1, keepdims=True)
    return (o - mu) * lax.rsqrt(var + EPS)


def mla_attention(q_nope, q_rope, k_nope, k_rope, v):
    B, S, H, _ = q_nope.shape
    nb = S // Q_BLOCK
    scale = (MLA_NOPE + MLA_ROPE) ** -0.5
    key_chunk = jnp.arange(S) // CHUNK

    def block(i):
        qs = i * Q_BLOCK
        qn = lax.dynamic_slice_in_dim(q_nope, qs, Q_BLOCK, axis=1)
        qr = lax.dynamic_slice_in_dim(q_rope, qs, Q_BLOCK, axis=1)
        s = (jnp.einsum('bqhd,bkhd->bhqk', qn, k_nope)
             + jnp.einsum('bqhd,bkd->bhqk', qr, k_rope)).astype(jnp.float32) * scale
        q_chunk = (qs + jnp.arange(Q_BLOCK)) // CHUNK
        mask = key_chunk[None, :] <= q_chunk[:, None]
        s = jnp.where(mask[None, None], s, NEG_INF)
        p = jax.nn.softmax(s, axis=-1)
        return jnp.einsum('bhqk,bkhd->bqhd', p.astype(v.dtype), v)

    out = lax.map(block, jnp.arange(nb))
    return out.transpose(1, 0, 2, 3, 4).reshape(B, S, H * v.shape[-1])


def hybrid_layer(x, c_act, cos_r, sin_r, cos_m, sin_m,
                 w_mod, b_mod, g_norm, w_in, g_cq, g_ckv, w_uq, w_ukv,
                 w_ret_proj, w_mla_proj, w_out):
    B, S, D = x.shape
    mod = c_act @ w_mod + b_mod
    shift, scale, gate = jnp.split(mod, 3, axis=-1)
    h = rms_norm(x, g_norm) * (1.0 + scale[:, None, :]) + shift[:, None, :]

    proj = h @ w_in
    points = np.cumsum(IN_SPLITS)[:-1].tolist()
    rq, rk, rv, rg, cq, ckv, kr, mg, bg = jnp.split(proj, points, axis=-1)

    rq = apply_rope(rq.reshape(B, S, RET_HEADS, RET_DK), cos_r[:, :, None], sin_r[:, :, None])
    rk = apply_rope(rk.reshape(B, S, RET_HEADS, RET_DK), cos_r[:, :, None], sin_r[:, :, None])
    rv = rv.reshape(B, S, RET_HEADS, RET_DV)
    o_ret = head_group_norm(retention(rq, rk, rv)).reshape(B, S, RET_V_W).astype(x.dtype)
    y_ret = (o_ret * jax.nn.silu(rg)) @ w_ret_proj

    q = (rms_norm(cq, g_cq) @ w_uq).reshape(B, S, MLA_HEADS, MLA_NOPE + MLA_ROPE)
    q_nope, q_rope = q[..., :MLA_NOPE], q[..., MLA_NOPE:]
    q_rope = apply_rope(q_rope, cos_m[:, :, None], sin_m[:, :, None])
    kv = (rms_norm(ckv, g_ckv) @ w_ukv).reshape(B, S, MLA_HEADS, MLA_NOPE + MLA_DV)
    k_nope, v = kv[..., :MLA_NOPE], kv[..., MLA_NOPE:]
    k_rope = apply_rope(kr, cos_m, sin_m)
    o_mla = mla_attention(q_nope, q_rope, k_nope, k_rope, v)
    y_mla = (o_mla * jax.nn.silu(mg)) @ w_mla_proj

    g_a, g_b = jnp.split(jax.nn.sigmoid(bg), 2, axis=-1)
    merged = g_a * y_ret + g_b * y_mla
    out = merged @ w_out
    return x + gate[:, None, :] * out


def _fwd_setup_inputs(seed: int = 0) -> dict:
    key = jax.random.key(seed)
    ks = jax.random.split(key, 16)
    f32 = jnp.float32

    def nrm(k, shape, fan_in, mult=1.0):
        return jax.random.normal(k, shape, f32) * (mult * fan_in ** -0.5)

    x = jax.random.normal(ks[0], (BATCH, SEQ, D_MODEL), f32)
    c = jax.random.normal(ks[1], (BATCH, D_MODEL), f32)
    positions = (jnp.arange(SEQ, dtype=jnp.int32)[None, :]
                 + jax.random.randint(ks[2], (BATCH, 1), 0, 1024, dtype=jnp.int32))
    w_mod = nrm(ks[3], (DEPTH, D_MODEL, 3 * D_MODEL), D_MODEL, 0.5)
    b_mod = 0.01 * jax.random.normal(ks[4], (DEPTH, 3 * D_MODEL), f32)
    g_norm = 1.0 + 0.02 * jax.random.normal(ks[5], (DEPTH, D_MODEL), f32)
    w_in = nrm(ks[6], (DEPTH, D_MODEL, D_IN), D_MODEL)
    g_cq = 1.0 + 0.02 * jax.random.normal(ks[7], (DEPTH, MLA_Q_RANK), f32)
    g_ckv = 1.0 + 0.02 * jax.random.normal(ks[8], (DEPTH, MLA_KV_RANK), f32)
    w_uq = nrm(ks[9], (DEPTH, MLA_Q_RANK, MLA_HEADS * (MLA_NOPE + MLA_ROPE)), MLA_Q_RANK)
    w_ukv = nrm(ks[10], (DEPTH, MLA_KV_RANK, MLA_HEADS * (MLA_NOPE + MLA_DV)), MLA_KV_RANK)
    w_ret_proj = nrm(ks[11], (DEPTH, RET_V_W, D_MODEL), RET_V_W)
    w_mla_proj = nrm(ks[12], (DEPTH, MLA_V_W, D_MODEL), MLA_V_W)
    w_out = nrm(ks[13], (DEPTH, D_MODEL, D_MODEL), D_MODEL)
    g_final = 1.0 + 0.02 * jax.random.normal(ks[14], (D_MODEL,), f32)
    return {"x": x, "c": c, "positions": positions, "w_mod": w_mod, "b_mod": b_mod,
            "g_norm": g_norm, "w_in": w_in, "g_cq": g_cq, "g_ckv": g_ckv, "w_uq": w_uq,
            "w_ukv": w_ukv, "w_ret_proj": w_ret_proj, "w_mla_proj": w_mla_proj,
            "w_out": w_out, "g_final": g_final}


def _fwd_reference(x, c, positions, w_mod, b_mod, g_norm, w_in, g_cq, g_ckv, w_uq, w_ukv,
              w_ret_proj, w_mla_proj, w_out, g_final):
    c_act = jax.nn.silu(c)
    cos_r, sin_r = rope_tables(positions, RET_DK)
    cos_m, sin_m = rope_tables(positions, MLA_ROPE)
    for l in range(DEPTH):
        x = hybrid_layer(x, c_act, cos_r, sin_r, cos_m, sin_m,
                         w_mod[l], b_mod[l], g_norm[l], w_in[l], g_cq[l], g_ckv[l],
                         w_uq[l], w_ukv[l], w_ret_proj[l], w_mla_proj[l], w_out[l])
    return rms_norm(x, g_final)


import jax as _jax
import jax.numpy as _jnp

TWIN_FORMAT = 'train_step'
FWD_PARAMS = ['x', 'c', 'positions', 'w_mod', 'b_mod', 'g_norm', 'w_in', 'g_cq', 'g_ckv', 'w_uq', 'w_ukv', 'w_ret_proj', 'w_mla_proj', 'w_out', 'g_final']
TWIN_WEIGHTS = ['w_mod', 'b_mod', 'g_norm', 'w_in', 'g_cq', 'g_ckv', 'w_uq', 'w_ukv', 'w_ret_proj', 'w_mla_proj', 'w_out', 'g_final']
TWIN_DIFF_INPUT = 'x'
TWIN_INPUTS = ['x', 'c', 'positions', 'w_mod', 'b_mod', 'g_norm', 'w_in', 'g_cq', 'g_ckv', 'w_uq', 'w_ukv', 'w_ret_proj', 'w_mla_proj', 'w_out', 'g_final', 'loss_target', 'm_w_mod', 'm_b_mod', 'm_g_norm', 'm_w_in', 'm_g_cq', 'm_g_ckv', 'm_w_uq', 'm_w_ukv', 'm_w_ret_proj', 'm_w_mla_proj', 'm_w_out', 'm_g_final', 'v_w_mod', 'v_b_mod', 'v_g_norm', 'v_w_in', 'v_g_cq', 'v_g_ckv', 'v_w_uq', 'v_w_ukv', 'v_w_ret_proj', 'v_w_mla_proj', 'v_w_out', 'v_g_final']
TWIN_OUTPUTS = ['loss', 'grad_x', 'grad_w_mod', 'grad_b_mod', 'grad_g_norm', 'grad_w_in', 'grad_g_cq', 'grad_g_ckv', 'grad_w_uq', 'grad_w_ukv', 'grad_w_ret_proj', 'grad_w_mla_proj', 'grad_w_out', 'grad_g_final', 'delta_w_mod', 'delta_b_mod', 'delta_g_norm', 'delta_w_in', 'delta_g_cq', 'delta_g_ckv', 'delta_w_uq', 'delta_w_ukv', 'delta_w_ret_proj', 'delta_w_mla_proj', 'delta_w_out', 'delta_g_final', 'new_m_w_mod', 'new_m_b_mod', 'new_m_g_norm', 'new_m_w_in', 'new_m_g_cq', 'new_m_g_ckv', 'new_m_w_uq', 'new_m_w_ukv', 'new_m_w_ret_proj', 'new_m_w_mla_proj', 'new_m_w_out', 'new_m_g_final', 'new_v_w_mod', 'new_v_b_mod', 'new_v_g_norm', 'new_v_w_in', 'new_v_g_cq', 'new_v_g_ckv', 'new_v_w_uq', 'new_v_w_ukv', 'new_v_w_ret_proj', 'new_v_w_mla_proj', 'new_v_w_out', 'new_v_g_final']
TWIN_LEAF_KINDS = {'loss': 'loss', 'grad_x': 'grad_x', 'grad_w_mod': 'grad_w', 'grad_b_mod': 'grad_w', 'grad_g_norm': 'grad_w', 'grad_w_in': 'grad_w', 'grad_g_cq': 'grad_w', 'grad_g_ckv': 'grad_w', 'grad_w_uq': 'grad_w', 'grad_w_ukv': 'grad_w', 'grad_w_ret_proj': 'grad_w', 'grad_w_mla_proj': 'grad_w', 'grad_w_out': 'grad_w', 'grad_g_final': 'grad_w', 'delta_w_mod': 'delta_w', 'delta_b_mod': 'delta_w', 'delta_g_norm': 'delta_w', 'delta_w_in': 'delta_w', 'delta_g_cq': 'delta_w', 'delta_g_ckv': 'delta_w', 'delta_w_uq': 'delta_w', 'delta_w_ukv': 'delta_w', 'delta_w_ret_proj': 'delta_w', 'delta_w_mla_proj': 'delta_w', 'delta_w_out': 'delta_w', 'delta_g_final': 'delta_w', 'new_m_w_mod': 'new_m', 'new_m_b_mod': 'new_m', 'new_m_g_norm': 'new_m', 'new_m_w_in': 'new_m', 'new_m_g_cq': 'new_m', 'new_m_g_ckv': 'new_m', 'new_m_w_uq': 'new_m', 'new_m_w_ukv': 'new_m', 'new_m_w_ret_proj': 'new_m', 'new_m_w_mla_proj': 'new_m', 'new_m_w_out': 'new_m', 'new_m_g_final': 'new_m', 'new_v_w_mod': 'new_v', 'new_v_b_mod': 'new_v', 'new_v_g_norm': 'new_v', 'new_v_w_in': 'new_v', 'new_v_g_cq': 'new_v', 'new_v_g_ckv': 'new_v', 'new_v_w_uq': 'new_v', 'new_v_w_ukv': 'new_v', 'new_v_w_ret_proj': 'new_v', 'new_v_w_mla_proj': 'new_v', 'new_v_w_out': 'new_v', 'new_v_g_final': 'new_v'}


def _forward(args):
    return _fwd_reference(*[args[k] for k in FWD_PARAMS])


def _output_shape():
    def fwd():
        inp = _fwd_setup_inputs(0)
        return _fwd_reference(*[inp[k] for k in FWD_PARAMS])
    out = _jax.eval_shape(fwd)
    return out.shape, out.dtype

N_MICROBATCH = 1
ADAM_LR = 0.001
ADAM_B1 = 0.9
ADAM_B2 = 0.999
ADAM_EPS = 1e-08
ADAM_WD = 0.01
ADAM_STEP = 10
PER_EXAMPLE_BATCH_AXIS = {'x': 0, 'c': 0, 'positions': 0, 'loss_target': 0}
SHARED_INPUTS = []
_WEIGHT_DTYPES = {'w_mod': _jnp.float32, 'b_mod': _jnp.float32, 'g_norm': _jnp.float32, 'w_in': _jnp.float32, 'g_cq': _jnp.float32, 'g_ckv': _jnp.float32, 'w_uq': _jnp.float32, 'w_ukv': _jnp.float32, 'w_ret_proj': _jnp.float32, 'w_mla_proj': _jnp.float32, 'w_out': _jnp.float32, 'g_final': _jnp.float32}
MOMENT_SCALE = {'w_mod': 1.818279e-02, 'b_mod': 3.087730e-02, 'g_norm': 1.895277e-02, 'w_in': 8.296867e-03, 'g_cq': 2.663909e-03, 'g_ckv': 6.563710e-03, 'w_uq': 1.072948e-03, 'w_ukv': 2.257327e-03, 'w_ret_proj': 9.614054e-03, 'w_mla_proj': 3.009155e-03, 'w_out': 1.006470e-02, 'g_final': 1.598140e+01}


def _to_microbatches(a, axis):
    t = _jnp.moveaxis(a, axis, 0)
    t = t.reshape((N_MICROBATCH, t.shape[0] // N_MICROBATCH) + t.shape[1:])
    return _jnp.moveaxis(t, 1, axis + 1)


def setup_inputs(seed: int = 0) -> dict:
    inp = _fwd_setup_inputs(seed)
    key = _jax.random.fold_in(_jax.random.key(seed), 7919)
    shape, _ = _output_shape()
    out = dict(inp)
    out["loss_target"] = _jax.random.normal(_jax.random.fold_in(key, 0), shape, _jnp.float32)
    for i, name in enumerate(TWIN_WEIGHTS):
        w = inp[name].astype(_jnp.float32)
        if MOMENT_SCALE is None:
            s = _jnp.sqrt(_jnp.mean(_jnp.square(w)) + 1e-30)
        else:
            s = MOMENT_SCALE[name]
        km, kv = _jax.random.split(_jax.random.fold_in(key, i + 1))
        out[name] = w
        out["m_" + name] = s * _jax.random.normal(km, w.shape, _jnp.float32)
        out["v_" + name] = (s * s) * _jax.random.uniform(kv, w.shape, _jnp.float32, 0.5, 1.5)
    if N_MICROBATCH > 1:
        for name, axis in PER_EXAMPLE_BATCH_AXIS.items():
            out[name] = _to_microbatches(out[name], axis)
    return {'x': out['x'], 'c': out['c'], 'positions': out['positions'], 'w_mod': out['w_mod'], 'b_mod': out['b_mod'], 'g_norm': out['g_norm'], 'w_in': out['w_in'], 'g_cq': out['g_cq'], 'g_ckv': out['g_ckv'], 'w_uq': out['w_uq'], 'w_ukv': out['w_ukv'], 'w_ret_proj': out['w_ret_proj'], 'w_mla_proj': out['w_mla_proj'], 'w_out': out['w_out'], 'g_final': out['g_final'], 'loss_target': out['loss_target'], 'm_w_mod': out['m_w_mod'], 'm_b_mod': out['m_b_mod'], 'm_g_norm': out['m_g_norm'], 'm_w_in': out['m_w_in'], 'm_g_cq': out['m_g_cq'], 'm_g_ckv': out['m_g_ckv'], 'm_w_uq': out['m_w_uq'], 'm_w_ukv': out['m_w_ukv'], 'm_w_ret_proj': out['m_w_ret_proj'], 'm_w_mla_proj': out['m_w_mla_proj'], 'm_w_out': out['m_w_out'], 'm_g_final': out['m_g_final'], 'v_w_mod': out['v_w_mod'], 'v_b_mod': out['v_b_mod'], 'v_g_norm': out['v_g_norm'], 'v_w_in': out['v_w_in'], 'v_g_cq': out['v_g_cq'], 'v_g_ckv': out['v_g_ckv'], 'v_w_uq': out['v_w_uq'], 'v_w_ukv': out['v_w_ukv'], 'v_w_ret_proj': out['v_w_ret_proj'], 'v_w_mla_proj': out['v_w_mla_proj'], 'v_w_out': out['v_w_out'], 'v_g_final': out['v_g_final']}


def _loss(weights, diff, rest, loss_target):
    with _jax.named_scope("forward"):
        args = {**rest, TWIN_DIFF_INPUT: diff, **{k: w.astype(_WEIGHT_DTYPES[k]) for k, w in weights.items()}}
        y = _forward(args)
    with _jax.named_scope("loss_head"):
        err = _jnp.square(y.astype(_jnp.float32) - loss_target)
        return 0.5 * _jnp.sum(_jnp.mean(err, axis=-1)) if err.ndim else 0.5 * err


def _adamw(w, g, m, v):
    m = ADAM_B1 * m + (1.0 - ADAM_B1) * g
    v = ADAM_B2 * v + (1.0 - ADAM_B2) * _jnp.square(g)
    m_hat = m / (1.0 - ADAM_B1 ** ADAM_STEP)
    v_hat = v / (1.0 - ADAM_B2 ** ADAM_STEP)
    delta = -ADAM_LR * (m_hat / (_jnp.sqrt(v_hat) + ADAM_EPS) + ADAM_WD * w)
    return delta, m, v


def reference(x, c, positions, w_mod, b_mod, g_norm, w_in, g_cq, g_ckv, w_uq, w_ukv, w_ret_proj, w_mla_proj, w_out, g_final, loss_target, m_w_mod, m_b_mod, m_g_norm, m_w_in, m_g_cq, m_g_ckv, m_w_uq, m_w_ukv, m_w_ret_proj, m_w_mla_proj, m_w_out, m_g_final, v_w_mod, v_b_mod, v_g_norm, v_w_in, v_g_cq, v_g_ckv, v_w_uq, v_w_ukv, v_w_ret_proj, v_w_mla_proj, v_w_out, v_g_final):
    given = dict(x=x, c=c, positions=positions, w_mod=w_mod, b_mod=b_mod, g_norm=g_norm, w_in=w_in, g_cq=g_cq, g_ckv=g_ckv, w_uq=w_uq, w_ukv=w_ukv, w_ret_proj=w_ret_proj, w_mla_proj=w_mla_proj, w_out=w_out, g_final=g_final, loss_target=loss_target, m_w_mod=m_w_mod, m_b_mod=m_b_mod, m_g_norm=m_g_norm, m_w_in=m_w_in, m_g_cq=m_g_cq, m_g_ckv=m_g_ckv, m_w_uq=m_w_uq, m_w_ukv=m_w_ukv, m_w_ret_proj=m_w_ret_proj, m_w_mla_proj=m_w_mla_proj, m_w_out=m_w_out, m_g_final=m_g_final, v_w_mod=v_w_mod, v_b_mod=v_b_mod, v_g_norm=v_g_norm, v_w_in=v_w_in, v_g_cq=v_g_cq, v_g_ckv=v_g_ckv, v_w_uq=v_w_uq, v_w_ukv=v_w_ukv, v_w_ret_proj=v_w_ret_proj, v_w_mla_proj=v_w_mla_proj, v_w_out=v_w_out, v_g_final=v_g_final)
    weights = {n: given[n] for n in TWIN_WEIGHTS}
    shared = {n: given[n] for n in SHARED_INPUTS}
    per_example = {n: given[n] for n in ['x', 'c', 'positions']}
    grad_fn = _jax.value_and_grad(_loss, argnums=(0, 1))

    def one_microbatch(ex, loss_target):
        ex = dict(ex)
        diff = ex.pop(TWIN_DIFF_INPUT)
        return grad_fn(weights, diff, {**shared, **ex}, loss_target)

    if N_MICROBATCH == 1:
        loss, (grad_w, grad_x) = one_microbatch(per_example, given["loss_target"])
    else:
        def body(carry, xs):
            loss_sum, grad_sum = carry
            l_k, (gw_k, gx_k) = one_microbatch(xs[0], xs[1])
            with _jax.named_scope("update"):
                return (loss_sum + l_k, _jax.tree.map(_jnp.add, grad_sum, gw_k)), gx_k

        init = (_jnp.zeros((), _jnp.float32), _jax.tree.map(_jnp.zeros_like, weights))
        (loss, grad_w), grad_x = _jax.lax.scan(body, init, (per_example, given["loss_target"]))
    with _jax.named_scope("update"):
        delta_w, new_m, new_v = {}, {}, {}
        for n in TWIN_WEIGHTS:
            delta_w[n], new_m[n], new_v[n] = _adamw(weights[n], grad_w[n], given["m_" + n], given["v_" + n])
    return (loss, grad_x, *[grad_w[n] for n in TWIN_WEIGHTS], *[delta_w[n] for n in TWIN_WEIGHTS],
            *[new_m[n] for n in TWIN_WEIGHTS], *[new_v[n] for n in TWIN_WEIGHTS])
```

```python
import functools

import jax
import jax.numpy as jnp
import numpy as np
from jax import lax
from jax.experimental import pallas as pl
from jax.experimental.pallas import tpu as pltpu

F32 = jnp.float32
BF16 = jnp.bfloat16
MESH = pl.DeviceIdType.MESH

VMEM_LIMIT_BYTES = 52 * 1024 * 1024
LANE = 128

CHUNK = 64
EPS = 1e-6
NEG_INF = -1e30
ROPE_BASE = 10000.0
RET_DK = 128
RET_DV = 256
MLA_NOPE = 128
MLA_ROPE = 64
MLA_DV = 128
MLA_QW = 256
ATTN_BLOCK = 512
RET_BLOCK = 256

ADAM_LR = 0.001
ADAM_B1 = 0.9
ADAM_B2 = 0.999
ADAM_EPS = 1e-08
ADAM_WD = 0.01
ADAM_STEP = 10


def _pcall(body, **kw):
    return pl.pallas_call(body, **kw)


def _params(n_grid):
    return pltpu.CompilerParams(dimension_semantics=("arbitrary",) * n_grid, vmem_limit_bytes=VMEM_LIMIT_BYTES)


def _pick(dim, target, mult=LANE):
    if dim <= target:
        return dim
    best = None
    for t in range(mult, target + 1, mult):
        if dim % t == 0:
            best = t
    assert best is not None, (dim, target, mult)
    return best


def matmul(a, b, *, ta=False, tb=False, b_split=False, out_split=False, out_dtype=BF16, name, tm=512, tn=1024, tk=2048):
    (M, K) = (a.shape[1], a.shape[0]) if ta else a.shape
    b_rows, b_cols = (b.shape[1], 2 * b.shape[2]) if b_split else b.shape
    (K2, N) = (b_cols, b_rows) if tb else (b_rows, b_cols)
    assert K == K2, (a.shape, b.shape, ta, tb)
    n_cap = N // 2 if (out_split or (b_split and not tb)) else N
    k_cap = K // 2 if (b_split and tb) else K
    tm, tn, tk = _pick(M, tm, 8 if M < LANE else LANE), _pick(n_cap, min(tn, n_cap)), _pick(k_cap, min(tk, k_cap))
    nk = K // tk
    njh, nkh = (N // 2) // tn, (K // 2) // tk
    dn = (((0 if ta else 1,), (1 if tb else 0,)), ((), ()))

    def body(a_ref, b_ref, o_ref, *scratch):
        b_blk = b_ref[0] if b_split else b_ref[...]
        prod = lax.dot_general(a_ref[...].astype(BF16), b_blk.astype(BF16), dn, preferred_element_type=F32)

        def store(v):
            if out_split:
                o_ref[0] = v.astype(o_ref.dtype)
            else:
                o_ref[...] = v.astype(o_ref.dtype)

        if nk == 1:
            store(prod)
            return
        (acc_ref,) = scratch
        k = pl.program_id(2)

        @pl.when(k == 0)
        def _():
            acc_ref[...] = prod

        @pl.when(k > 0)
        def _():
            acc_ref[...] += prod

        @pl.when(k == nk - 1)
        def _():
            store(acc_ref[...])

    a_spec = pl.BlockSpec((tk, tm), lambda i, j, k: (k, i)) if ta else pl.BlockSpec((tm, tk), lambda i, j, k: (i, k))
    if b_split and tb:
        b_spec = pl.BlockSpec((1, tn, tk), lambda i, j, k: (k // nkh, j, k % nkh))
    elif b_split:
        b_spec = pl.BlockSpec((1, tk, tn), lambda i, j, k: (j // njh, k, j % njh))
    elif tb:
        b_spec = pl.BlockSpec((tn, tk), lambda i, j, k: (j, k))
    else:
        b_spec = pl.BlockSpec((tk, tn), lambda i, j, k: (k, j))
    if out_split:
        out_spec = pl.BlockSpec((1, tm, tn), lambda i, j, k: (j // njh, i, j % njh))
        out_shape = jax.ShapeDtypeStruct((2, M, N // 2), out_dtype)
    else:
        out_spec = pl.BlockSpec((tm, tn), lambda i, j, k: (i, j))
        out_shape = jax.ShapeDtypeStruct((M, N), out_dtype)
    return _pcall(
        body,
        name=name,
        grid=(M // tm, N // tn, nk),
        in_specs=[a_spec, b_spec],
        out_specs=out_spec,
        out_shape=out_shape,
        scratch_shapes=[] if nk == 1 else [pltpu.VMEM((tm, tn), F32)],
        compiler_params=_params(3),
    )(a, b)


def win(arr, off, width):
    assert off % width == 0 and off + width <= arr.shape[1], (arr.shape, off, width)
    return (arr, off // width, width)


def rowwise(fn, rows, vecs, outs, accs=(), *, tm, name):
    rows = [r if isinstance(r, tuple) else (r, 0, r.shape[-1]) for r in rows]
    S = rows[0][0].shape[-2]
    tm = _pick(S, tm, 8)
    n_rows, n_vecs, n_outs = len(rows), len(vecs), len(outs)

    def body(*refs):
        ins = [r[...] for r in refs[: n_rows + n_vecs]]
        out_refs = refs[n_rows + n_vecs : n_rows + n_vecs + n_outs]
        acc_refs = refs[n_rows + n_vecs + n_outs :]
        res, acc = fn(*ins)
        for r, v in zip(out_refs, res, strict=True):
            r[...] = v.astype(r.dtype)
        if acc_refs:
            i = pl.program_id(0)

            @pl.when(i == 0)
            def _():
                for r, v in zip(acc_refs, acc, strict=True):
                    r[...] = v

            @pl.when(i > 0)
            def _():
                for r, v in zip(acc_refs, acc, strict=True):
                    r[...] += v

    in_specs = []
    for arr, blk, w in rows:
        if arr.ndim == 3:
            in_specs.append(pl.BlockSpec((arr.shape[0], tm, w), lambda i: (0, i, 0)))
        else:
            in_specs.append(pl.BlockSpec((tm, w), functools.partial(lambda i, blk: (i, blk), blk=blk)))
    for v in vecs:
        in_specs.append(pl.BlockSpec(v.shape, functools.partial(lambda i, nd: (0,) * nd, nd=v.ndim)))
    out_specs = [pl.BlockSpec((tm, w), lambda i: (i, 0)) for w, _ in outs]
    out_specs += [pl.BlockSpec((1, w), lambda i: (0, 0)) for w in accs]
    out_shape = [jax.ShapeDtypeStruct((S, w), dt) for w, dt in outs]
    out_shape += [jax.ShapeDtypeStruct((1, w), F32) for w in accs]
    res = _pcall(
        body,
        name=name,
        grid=(S // tm,),
        in_specs=in_specs,
        out_specs=out_specs,
        out_shape=out_shape,
        compiler_params=_params(1),
    )(*[r[0] for r in rows], *vecs)
    return res


def _sum0(v):
    return jnp.sum(v, axis=0, keepdims=True)


def _sigmoid(z):
    return 1.0 / (1.0 + jnp.exp(-z))


def _silu(z):
    return z * _sigmoid(z)


def _dsilu(z):
    s = _sigmoid(z)
    return s * (1.0 + z * (1.0 - s))


def _swap_half(x, half):
    if 2 * half == LANE:
        return pltpu.roll(x, half, 1)
    lane = lax.broadcasted_iota(jnp.int32, x.shape, 1)
    return jnp.where((lane % (2 * half)) < half, pltpu.roll(x, LANE - half, 1), pltpu.roll(x, half, 1))


def _rope_tile(x, cosf, sinf, half, inverse=False):
    sw = _swap_half(x, half)
    return x * cosf - sw * sinf if inverse else x * cosf + sw * sinf


def _tri_tables(n, by_key):
    if by_key:
        pairs = [(i, j) for j in range(n) for i in range(j, n)]
    else:
        pairs = [(i, j) for i in range(n) for j in range(i + 1)]
    return (np.array([p[0] for p in pairs], np.int32), np.array([p[1] for p in pairs], np.int32))


def _attn_scores(q, kn, kr, i, j, tq, scale, masked):
    k = jnp.concatenate([kn, kr], axis=1)
    s = lax.dot_general(q, k, (((1,), (1,)), ((), ())), preferred_element_type=F32) * scale
    if masked:
        rc = (i * tq + lax.broadcasted_iota(jnp.int32, s.shape, 0)) // CHUNK
        cc = (j * tq + lax.broadcasted_iota(jnp.int32, s.shape, 1)) // CHUNK
        s = jnp.where(cc <= rc, s, NEG_INF)
    return s, k


def attn_fwd(q, kv, kr, proj, mg_off, *, heads, tq, name):
    S = q.shape[0]
    tq = _pick(S, tq, CHUNK)
    n = S // tq
    it, jt = _tri_tables(n, by_key=False)
    scale = float((MLA_NOPE + MLA_ROPE) ** -0.5)
    mg_blk = mg_off // MLA_DV

    def body(it_ref, jt_ref, q_ref, kn_ref, v_ref, kr_ref, mg_ref, o_ref, a_ref, lse_ref, m_sc, l_sc, acc_sc):
        t = pl.program_id(1)
        i, j = it_ref[t], jt_ref[t]

        @pl.when(j == 0)
        def _():
            m_sc[...] = jnp.full(m_sc.shape, NEG_INF, F32)
            l_sc[...] = jnp.zeros(l_sc.shape, F32)
            acc_sc[...] = jnp.zeros(acc_sc.shape, F32)

        def step(masked):
            s, _ = _attn_scores(q_ref[...], kn_ref[...], kr_ref[...], i, j, tq, scale, masked)
            m_prev = m_sc[...]
            m_new = jnp.maximum(m_prev, jnp.max(s, axis=1, keepdims=True))
            p = jnp.exp(s - m_new)
            alpha = jnp.exp(m_prev - m_new)
            l_sc[...] = alpha * l_sc[...] + jnp.sum(p, axis=1, keepdims=True)
            acc_sc[...] = alpha * acc_sc[...] + jnp.dot(p.astype(BF16), v_ref[...], preferred_element_type=F32)
            m_sc[...] = m_new

        @pl.when(j < i)
        def _():
            step(False)

        @pl.when(j == i)
        def _():
            step(True)
            o = acc_sc[...] / l_sc[...]
            o_ref[...] = o
            a_ref[...] = (o * _silu(mg_ref[...].astype(F32))).astype(a_ref.dtype)
            lse_ref[0] = m_sc[...] + jnp.log(l_sc[...])

    grid_spec = pltpu.PrefetchScalarGridSpec(
        num_scalar_prefetch=2,
        grid=(heads, len(it)),
        in_specs=[
            pl.BlockSpec((tq, MLA_QW), lambda h, t, it, jt: (it[t], h)),
            pl.BlockSpec((tq, MLA_NOPE), lambda h, t, it, jt: (jt[t], 2 * h)),
            pl.BlockSpec((tq, MLA_DV), lambda h, t, it, jt: (jt[t], 2 * h + 1)),
            pl.BlockSpec((tq, LANE), lambda h, t, it, jt: (jt[t], 0)),
            pl.BlockSpec((tq, MLA_DV), lambda h, t, it, jt: (it[t], mg_blk + h)),
        ],
        out_specs=[
            pl.BlockSpec((tq, MLA_DV), lambda h, t, it, jt: (it[t], h)),
            pl.BlockSpec((tq, MLA_DV), lambda h, t, it, jt: (it[t], h)),
            pl.BlockSpec((1, tq, 1), lambda h, t, it, jt: (h, it[t], 0)),
        ],
        scratch_shapes=[pltpu.VMEM((tq, 1), F32), pltpu.VMEM((tq, 1), F32), pltpu.VMEM((tq, MLA_DV), F32)],
    )
    return _pcall(
        body,
        name=name,
        grid_spec=grid_spec,
        out_shape=[
            jax.ShapeDtypeStruct((S, heads * MLA_DV), F32),
            jax.ShapeDtypeStruct((S, heads * MLA_DV), BF16),
            jax.ShapeDtypeStruct((heads, S, 1), F32),
        ],
        compiler_params=_params(2),
    )(jnp.asarray(it), jnp.asarray(jt), q, kv, kv, kr, proj)


def attn_bwd_dq(q, kv, kr, do, o, lse, *, heads, tq, name):
    S = q.shape[0]
    tq = _pick(S, tq, CHUNK)
    n = S // tq
    it, jt = _tri_tables(n, by_key=False)
    scale = float((MLA_NOPE + MLA_ROPE) ** -0.5)

    def body(it_ref, jt_ref, q_ref, kn_ref, v_ref, kr_ref, do_ref, o_ref, lse_ref, dq_ref, acc_sc, dl_sc):
        t = pl.program_id(1)
        i, j = it_ref[t], jt_ref[t]

        @pl.when(j == 0)
        def _():
            acc_sc[...] = jnp.zeros(acc_sc.shape, F32)
            dl_sc[...] = jnp.sum(do_ref[...].astype(F32) * o_ref[...], axis=1, keepdims=True)

        def step(masked):
            s, k = _attn_scores(q_ref[...], kn_ref[...], kr_ref[...], i, j, tq, scale, masked)
            p = jnp.exp(s - lse_ref[0])
            dp = lax.dot_general(do_ref[...], v_ref[...], (((1,), (1,)), ((), ())), preferred_element_type=F32)
            ds = (p * (dp - dl_sc[...]) * scale).astype(BF16)
            acc_sc[...] += jnp.dot(ds, k, preferred_element_type=F32)

        @pl.when(j < i)
        def _():
            step(False)

        @pl.when(j == i)
        def _():
            step(True)
            dq_ref[...] = acc_sc[...].astype(dq_ref.dtype)

    grid_spec = pltpu.PrefetchScalarGridSpec(
        num_scalar_prefetch=2,
        grid=(heads, len(it)),
        in_specs=[
            pl.BlockSpec((tq, MLA_QW), lambda h, t, it, jt: (it[t], h)),
            pl.BlockSpec((tq, MLA_NOPE), lambda h, t, it, jt: (jt[t], 2 * h)),
            pl.BlockSpec((tq, MLA_DV), lambda h, t, it, jt: (jt[t], 2 * h + 1)),
            pl.BlockSpec((tq, LANE), lambda h, t, it, jt: (jt[t], 0)),
            pl.BlockSpec((tq, MLA_DV), lambda h, t, it, jt: (it[t], h)),
            pl.BlockSpec((tq, MLA_DV), lambda h, t, it, jt: (it[t], h)),
            pl.BlockSpec((1, tq, 1), lambda h, t, it, jt: (h, it[t], 0)),
        ],
        out_specs=pl.BlockSpec((tq, MLA_QW), lambda h, t, it, jt: (it[t], h)),
        scratch_shapes=[pltpu.VMEM((tq, MLA_QW), F32), pltpu.VMEM((tq, 1), F32)],
    )
    return _pcall(
        body,
        name=name,
        grid_spec=grid_spec,
        out_shape=jax.ShapeDtypeStruct((S, heads * MLA_QW), BF16),
        compiler_params=_params(2),
    )(jnp.asarray(it), jnp.asarray(jt), q, kv, kv, kr, do, o, lse)


def attn_bwd_dkv(q, kv, kr, do, o, lse, *, heads, tq, name):
    S = q.shape[0]
    tq = _pick(S, tq, CHUNK)
    n = S // tq
    it, jt = _tri_tables(n, by_key=True)
    scale = float((MLA_NOPE + MLA_ROPE) ** -0.5)

    def body(it_ref, jt_ref, q_ref, kn_ref, v_ref, kr_ref, do_ref, o_ref, lse_ref, dkv_ref, dkr_ref, dk_sc, dv_sc):
        t = pl.program_id(1)
        i, j = it_ref[t], jt_ref[t]

        @pl.when(i == j)
        def _():
            dk_sc[...] = jnp.zeros(dk_sc.shape, F32)
            dv_sc[...] = jnp.zeros(dv_sc.shape, F32)

        def step(masked):
            q_blk = q_ref[...]
            do_blk = do_ref[...]
            s, _ = _attn_scores(q_blk, kn_ref[...], kr_ref[...], i, j, tq, scale, masked)
            p = jnp.exp(s - lse_ref[0])
            delta = jnp.sum(do_blk.astype(F32) * o_ref[...], axis=1, keepdims=True)
            dv_sc[...] += lax.dot_general(p.astype(BF16), do_blk, (((0,), (0,)), ((), ())), preferred_element_type=F32)
            dp = lax.dot_general(do_blk, v_ref[...], (((1,), (1,)), ((), ())), preferred_element_type=F32)
            ds = (p * (dp - delta) * scale).astype(BF16)
            dk_sc[...] += lax.dot_general(ds, q_blk, (((0,), (0,)), ((), ())), preferred_element_type=F32)

        @pl.when(i == j)
        def _():
            step(True)

        @pl.when(i > j)
        def _():
            step(False)

        @pl.when(i == n - 1)
        def _():
            dkv_ref[:, :MLA_NOPE] = dk_sc[:, :MLA_NOPE].astype(dkv_ref.dtype)
            dkv_ref[:, MLA_NOPE:] = dv_sc[...].astype(dkv_ref.dtype)
            dkr_ref[0] = dk_sc[:, MLA_NOPE:]

    grid_spec = pltpu.PrefetchScalarGridSpec(
        num_scalar_prefetch=2,
        grid=(heads, len(it)),
        in_specs=[
            pl.BlockSpec((tq, MLA_QW), lambda h, t, it, jt: (it[t], h)),
            pl.BlockSpec((tq, MLA_NOPE), lambda h, t, it, jt: (jt[t], 2 * h)),
            pl.BlockSpec((tq, MLA_DV), lambda h, t, it, jt: (jt[t], 2 * h + 1)),
            pl.BlockSpec((tq, LANE), lambda h, t, it, jt: (jt[t], 0)),
            pl.BlockSpec((tq, MLA_DV), lambda h, t, it, jt: (it[t], h)),
            pl.BlockSpec((tq, MLA_DV), lambda h, t, it, jt: (it[t], h)),
            pl.BlockSpec((1, tq, 1), lambda h, t, it, jt: (h, it[t], 0)),
        ],
        out_specs=[
            pl.BlockSpec((tq, MLA_NOPE + MLA_DV), lambda h, t, it, jt: (jt[t], h)),
            pl.BlockSpec((1, tq, LANE), lambda h, t, it, jt: (h, jt[t], 0)),
        ],
        scratch_shapes=[pltpu.VMEM((tq, MLA_QW), F32), pltpu.VMEM((tq, MLA_DV), F32)],
    )
    return _pcall(
        body,
        name=name,
        grid_spec=grid_spec,
        out_shape=[
            jax.ShapeDtypeStruct((S, heads * (MLA_NOPE + MLA_DV)), BF16),
            jax.ShapeDtypeStruct((heads, S, LANE), F32),
        ],
        compiler_params=_params(2),
    )(jnp.asarray(it), jnp.asarray(jt), q, kv, kv, kr, do, o, lse)


def _ret_consts(heads):
    h = np.arange(heads, dtype=np.float32)
    lg = np.log(np.float32(1.0) - np.float32(2.0) ** (np.float32(-5.0) - h)).astype(np.float32)
    idx = np.arange(CHUNK, dtype=np.float32)
    dmat = np.exp(np.abs(idx[:, None] - idx[None, :])[None] * lg[:, None, None]).astype(np.float32)
    xi = np.exp((idx + 1.0)[None, :] * lg[:, None]).astype(np.float32)
    zeta = np.exp((CHUNK - 1.0 - idx)[None, :] * lg[:, None]).astype(np.float32)
    dec = np.exp(np.float32(CHUNK) * lg).astype(np.float32)
    xi = np.broadcast_to(xi[:, :, None], (heads, CHUNK, RET_DK)).copy()
    zeta = np.broadcast_to(zeta[:, :, None], (heads, CHUNK, RET_DK)).copy()
    dec = np.broadcast_to(dec[:, None, None], (heads, 8, LANE)).copy()
    return jnp.asarray(dmat), jnp.asarray(xi), jnp.asarray(zeta), jnp.asarray(dec)


_NT = (((1,), (1,)), ((), ()))
_TN = (((0,), (0,)), ((), ()))


def _dot(a, b, dn=(((1,), (0,)), ((), ()))):
    return lax.dot_general(a.astype(BF16), b.astype(BF16), dn, preferred_element_type=F32)


def _const_specs(heads):
    return [
        pl.BlockSpec((1, CHUNK, CHUNK), lambda h, b: (h, 0, 0)),
        pl.BlockSpec((1, CHUNK, RET_DK), lambda h, b: (h, 0, 0)),
        pl.BlockSpec((1, CHUNK, RET_DK), lambda h, b: (h, 0, 0)),
        pl.BlockSpec((1, 8, LANE), lambda h, b: (h, 0, 0)),
    ]


def ret_fwd(q, k, proj, v_off, rg_off, *, heads, tb, name):
    S = q.shape[0]
    tb = _pick(S, tb, CHUNK)
    cpb, nb = tb // CHUNK, S // tb
    v_blk, rg_blk = v_off // RET_DV, rg_off // RET_DV

    def body(q_ref, k_ref, v_ref, rg_ref, dm_ref, xi_ref, ze_ref, dec_ref, o_ref, a_ref, st_ref, r_sc):
        @pl.when(pl.program_id(1) == 0)
        def _():
            r_sc[...] = jnp.zeros(r_sc.shape, F32)

        dm, xi, ze, dec = dm_ref[0], xi_ref[0], ze_ref[0], dec_ref[0, 0:1, 0:1]
        for c in range(cpb):
            sl = pl.ds(c * CHUNK, CHUNK)
            qc, kc, vc = q_ref[sl, :], k_ref[sl, :], v_ref[sl, :]
            r = r_sc[...]
            st_ref[0, c] = r.astype(BF16)
            s = _dot(qc, kc, _NT) * dm
            o = _dot(s, vc) + _dot(qc.astype(F32) * xi, r)
            r_sc[...] = r * dec + _dot(kc.astype(F32) * ze, vc, _TN)
            mu = jnp.mean(o, axis=1, keepdims=True)
            d = o - mu
            n = d * lax.rsqrt(jnp.mean(d * d, axis=1, keepdims=True) + EPS)
            o_ref[sl, :] = o
            a_ref[sl, :] = (n * _silu(rg_ref[sl, :].astype(F32))).astype(a_ref.dtype)

    return _pcall(
        body,
        name=name,
        grid=(heads, nb),
        in_specs=[
            pl.BlockSpec((tb, RET_DK), lambda h, b: (b, h)),
            pl.BlockSpec((tb, RET_DK), lambda h, b: (b, h)),
            pl.BlockSpec((tb, RET_DV), lambda h, b: (b, v_blk + h)),
            pl.BlockSpec((tb, RET_DV), lambda h, b: (b, rg_blk + h)),
            *_const_specs(heads),
        ],
        out_specs=[
            pl.BlockSpec((tb, RET_DV), lambda h, b: (b, h)),
            pl.BlockSpec((tb, RET_DV), lambda h, b: (b, h)),
            pl.BlockSpec((1, cpb, RET_DK, RET_DV), lambda h, b: (h, b, 0, 0)),
        ],
        out_shape=[
            jax.ShapeDtypeStruct((S, heads * RET_DV), F32),
            jax.ShapeDtypeStruct((S, heads * RET_DV), BF16),
            jax.ShapeDtypeStruct((heads, S // CHUNK, RET_DK, RET_DV), BF16),
        ],
        scratch_shapes=[pltpu.VMEM((RET_DK, RET_DV), F32)],
        compiler_params=_params(2),
    )(q, k, proj, proj, *_ret_consts(heads))


def ret_bwd(q, k, proj, v_off, states, do, *, heads, tb, name):
    S = q.shape[0]
    tb = _pick(S, tb, CHUNK)
    cpb, nb = tb // CHUNK, S // tb
    v_blk = v_off // RET_DV

    def body(q_ref, k_ref, v_ref, st_ref, do_ref, dm_ref, xi_ref, ze_ref, dec_ref, dq_ref, dk_ref, dv_ref, dr_sc):
        @pl.when(pl.program_id(1) == 0)
        def _():
            dr_sc[...] = jnp.zeros(dr_sc.shape, F32)

        dm, xi, ze, dec = dm_ref[0], xi_ref[0], ze_ref[0], dec_ref[0, 0:1, 0:1]
        for c in reversed(range(cpb)):
            sl = pl.ds(c * CHUNK, CHUNK)
            qc, kc, vc, doc = q_ref[sl, :], k_ref[sl, :], v_ref[sl, :], do_ref[sl, :]
            r_prev = st_ref[0, c]
            dr = dr_sc[...]
            a = _dot(qc, kc, _NT) * dm
            ds = _dot(doc, vc, _NT) * dm
            kz = kc.astype(F32) * ze
            dq_ref[sl, :] = (_dot(ds, kc) + _dot(doc, r_prev, _NT) * xi).astype(dq_ref.dtype)
            dk_ref[sl, :] = (_dot(ds, qc, _TN) + _dot(vc, dr, _NT) * ze).astype(dk_ref.dtype)
            dv_ref[sl, :] = (_dot(a, doc, _TN) + _dot(kz, dr)).astype(dv_ref.dtype)
            dr_sc[...] = dr * dec + _dot(qc.astype(F32) * xi, doc, _TN)

    rev = lambda h, b: (nb - 1 - b, h)
    return _pcall(
        body,
        name=name,
        grid=(heads, nb),
        in_specs=[
            pl.BlockSpec((tb, RET_DK), rev),
            pl.BlockSpec((tb, RET_DK), rev),
            pl.BlockSpec((tb, RET_DV), lambda h, b: (nb - 1 - b, v_blk + h)),
            pl.BlockSpec((1, cpb, RET_DK, RET_DV), lambda h, b: (h, nb - 1 - b, 0, 0)),
            pl.BlockSpec((tb, RET_DV), rev),
            *_const_specs(heads),
        ],
        out_specs=[pl.BlockSpec((tb, RET_DK), rev), pl.BlockSpec((tb, RET_DK), rev), pl.BlockSpec((tb, RET_DV), rev)],
        out_shape=[
            jax.ShapeDtypeStruct((S, heads * RET_DK), BF16),
            jax.ShapeDtypeStruct((S, heads * RET_DK), BF16),
            jax.ShapeDtypeStruct((S, heads * RET_DV), BF16),
        ],
        scratch_shapes=[pltpu.VMEM((RET_DK, RET_DV), F32)],
        compiler_params=_params(2),
    )(q, k, proj, states, do, *_ret_consts(heads))


def _me():
    return (lax.axis_index("x"), lax.axis_index("y"), lax.axis_index("c"))


def _comm(name, ins, out_shapes, n_local, n_remote, plan):
    n_in, n_out = len(ins), len(out_shapes)

    def body(*refs):
        in_refs, out_refs = refs[:n_in], refs[n_in : n_in + n_out]
        send_sems, recv_sems, local_sems = refs[n_in + n_out :]
        me = _me()
        local_plan, remote_plan = plan(me, in_refs, out_refs)
        assert len(local_plan) == n_local and len(remote_plan) == n_remote, (len(local_plan), len(remote_plan))
        local_copies = [pltpu.make_async_copy(src, dst, local_sems.at[n]) for n, (src, dst) in enumerate(local_plan)]
        for cp in local_copies:
            cp.start()
        sends = [
            pltpu.make_async_remote_copy(
                src_ref=src, dst_ref=dst, send_sem=send_sems.at[n], recv_sem=recv_sems.at[n], device_id=peer, device_id_type=MESH
            )
            for n, (src, dst, peer, _, _) in enumerate(remote_plan)
        ]
        for cp, (_, _, _, _, after) in zip(sends, remote_plan):
            if after is None:
                cp.start()
        for n, (src, _, _, landing, after) in enumerate(remote_plan):
            assert after is None or after < n
            pltpu.make_async_remote_copy(
                src_ref=src, dst_ref=landing, send_sem=send_sems.at[n], recv_sem=recv_sems.at[n], device_id=me, device_id_type=MESH
            ).wait_recv()
            for cp, (_, _, _, _, after_m) in zip(sends, remote_plan):
                if after_m == n:
                    cp.start()
        for cp in sends:
            cp.wait_send()
        for cp in local_copies:
            cp.wait()

    any_spec = pl.BlockSpec(memory_space=pl.ANY)
    return _pcall(
        body,
        name=name,
        in_specs=[any_spec] * n_in,
        out_specs=[any_spec] * n_out,
        out_shape=out_shapes,
        scratch_shapes=[
            pltpu.SemaphoreType.DMA((n_remote,)),
            pltpu.SemaphoreType.DMA((n_remote,)),
            pltpu.SemaphoreType.DMA((max(n_local, 1),)),
        ],
    )(*ins)


_CHIP_FLIPS = ((1, 0), (0, 1), (1, 1))


class Placement:
    def __init__(self, rows, full_rows, cuts=(), offset=None, zero_rows=None):
        edges = [0, *sorted(cuts), rows]
        self.rows, self.full_rows = rows, full_rows
        self.runs = [(a, b - a) for a, b in zip(edges[:-1], edges[1:]) if b > a]
        self.offset = offset if offset is not None else (lambda j, start: j * rows + start)
        self.zero_rows = zero_rows

    def at(self, ref_2d_plus, j, start, size):
        return pl.ds(pl.multiple_of(self.offset(j, start), 16), size)


def gather_layer(prepped, zeros, places, l, name):
    n_w = len(prepped)
    n_chip = sum(len(p.runs) for p in places) * 3
    n_own = sum(len(p.runs) for p in places) * 2 + sum(2 for p in places if p.zero_rows)

    def plan(me, in_refs, out_refs):
        x, y, c = me
        j = 2 * x + y
        sib = (x, y, 1 - c)
        from_chips, own, forwards = [], [], []
        for i, o, p in zip(in_refs[:n_w], out_refs, places):
            for start, size in p.runs:
                for dx, dy in _CHIP_FLIPS:
                    px, py = x ^ dx, y ^ dy
                    jp = 2 * px + py
                    mine_there = o.at[c, p.at(o, j, start, size)]
                    theirs_here = o.at[c, p.at(o, jp, start, size)]
                    n = len(from_chips)
                    from_chips.append((i.at[l, c, pl.ds(start, size)], mine_there, (px, py, c), theirs_here, None))
                    forwards.append((theirs_here, theirs_here, sib, o.at[1 - c, p.at(o, jp, start, size)], n))
                for h in range(2):
                    place = o.at[h, p.at(o, j, start, size)]
                    own.append((i.at[l, h, pl.ds(start, size)], place, sib, place, None))
            if p.zero_rows:
                for h in range(2):
                    place = o.at[h, pl.ds(p.zero_rows[0], p.zero_rows[1])]
                    own.append((in_refs[n_w], place, sib, place, None))
        return [], from_chips + own + forwards

    shapes = [jax.ShapeDtypeStruct((2, p.full_rows, w.shape[3]), w.dtype) for w, p in zip(prepped, places)]
    return _comm(name, [*prepped, zeros], shapes, 0, 2 * n_chip + n_own, plan)


def sibling_scatter(arrays, name):
    def plan(me, in_refs, out_refs):
        x, y, c = me
        return [], [(i.at[1 - c], o, (x, y, 1 - c), o, None) for i, o in zip(in_refs, out_refs)]

    shapes = [jax.ShapeDtypeStruct(a.shape[1:], a.dtype) for a in arrays]
    return _comm(name, arrays, shapes, 0, len(arrays), plan)


def sibling_swap(arrays, name):
    def plan(me, in_refs, out_refs):
        x, y, c = me
        return [], [(i, o, (x, y, 1 - c), o, None) for i, o in zip(in_refs, out_refs)]

    shapes = [jax.ShapeDtypeStruct(a.shape, a.dtype) for a in arrays]
    return _comm(name, arrays, shapes, 0, len(arrays), plan)


def scatter_to_chips(arrays, places, name):
    def plan(me, in_refs, out_refs):
        x, y, c = me
        j = 2 * x + y
        local_plan, remote_plan = [], []
        for i, o, p in zip(in_refs, out_refs, places):
            for start, size in p.runs:
                local_plan.append((i.at[p.at(i, j, start, size)], o.at[3, pl.ds(start, size)]))
                for k, (dx, dy) in enumerate(_CHIP_FLIPS):
                    px, py = x ^ dx, y ^ dy
                    landing = o.at[k, pl.ds(start, size)]
                    remote_plan.append((i.at[p.at(i, 2 * px + py, start, size)], landing, (px, py, c), landing, None))
        return local_plan, remote_plan

    n_runs = sum(len(p.runs) for p in places)
    shapes = [jax.ShapeDtypeStruct((4, p.rows, a.shape[1]), a.dtype) for a, p in zip(arrays, places)]
    return _comm(name, arrays, shapes, n_runs, 3 * n_runs, plan)


def allgather8(block, name):
    def plan(me, in_refs, out_refs):
        x, y, c = me
        (i,), (o,) = in_refs, out_refs
        mine = 4 * x + 2 * y + c
        remote_plan = []
        for flip in range(1, 8):
            px, py, pc = x ^ (flip >> 2), y ^ ((flip >> 1) & 1), c ^ (flip & 1)
            remote_plan.append((i, o.at[mine], (px, py, pc), o.at[4 * px + 2 * py + pc], None))
        return [(i, o.at[mine])], remote_plan

    return _comm(name, [block], [jax.ShapeDtypeStruct((8, *block.shape), block.dtype)], 1, 7, plan)[0]


def transpose_split(a, out_dtype, name):
    L, R, C = a.shape
    rh = R // 2
    rt = _pick(rh, 256)
    n = rh // rt

    def body(x_ref, o_ref):
        o_ref[0, 0] = x_ref[0].T.astype(o_ref.dtype)

    return _pcall(
        body,
        name=name,
        grid=(L, 2, n),
        in_specs=[pl.BlockSpec((1, rt, C), lambda l, h, i: (l, h * n + i, 0))],
        out_specs=pl.BlockSpec((1, 1, C, rt), lambda l, h, i: (l, h, 0, i)),
        out_shape=jax.ShapeDtypeStruct((L, 2, C, rh), out_dtype),
        compiler_params=_params(3),
    )(a)


def cast_split(a, out_dtype, name):
    L, R, C = a.shape

    def body(x_ref, o_ref):
        o_ref[0, 0] = x_ref[0].astype(o_ref.dtype)

    return _pcall(
        body,
        name=name,
        grid=(L, 2),
        in_specs=[pl.BlockSpec((1, R, C // 2), lambda l, h: (l, 0, h))],
        out_specs=pl.BlockSpec((1, 1, R, C // 2), lambda l, h: (l, h, 0, 0)),
        out_shape=jax.ShapeDtypeStruct((L, 2, R, C // 2), out_dtype),
        compiler_params=_params(2),
    )(a)


def untranspose_halves(mine, theirs, my_c, stack, l, n_layers, name):
    C, rh = mine.shape
    n_in = 2 if stack is None else 3
    rt = _pick(rh, 256)
    n = rh // rt

    def body(c_ref, a_ref, b_ref, *rest):
        o_ref = rest[-1]
        h = pl.program_id(0)
        o_ref[0] = jnp.where(h == c_ref[0], a_ref[...], b_ref[...]).T

    grid_spec = pltpu.PrefetchScalarGridSpec(
        num_scalar_prefetch=1,
        grid=(2, n),
        in_specs=[pl.BlockSpec((C, rt), lambda h, i, c: (0, i))] * 2
        + ([] if stack is None else [pl.BlockSpec(memory_space=pl.ANY)]),
        out_specs=pl.BlockSpec((1, rt, C), lambda h, i, c: (l, h * n + i, 0)),
    )
    return _pcall(
        body,
        name=name,
        grid_spec=grid_spec,
        out_shape=jax.ShapeDtypeStruct((n_layers, 2 * rh, C), F32),
        input_output_aliases={} if stack is None else {n_in: 0},
        compiler_params=_params(2),
    )(my_c.reshape(1), mine, theirs, *([] if stack is None else [stack]))


def merge_halves(mine, theirs, my_c, stack, l, n_layers, name):
    R, ch = mine.shape
    n_in = 2 if stack is None else 3

    def body(c_ref, a_ref, b_ref, *rest):
        o_ref = rest[-1]
        o_ref[0] = jnp.where(pl.program_id(0) == c_ref[0], a_ref[...], b_ref[...])

    grid_spec = pltpu.PrefetchScalarGridSpec(
        num_scalar_prefetch=1,
        grid=(2,),
        in_specs=[pl.BlockSpec((R, ch), lambda h, c: (0, 0))] * 2 + ([] if stack is None else [pl.BlockSpec(memory_space=pl.ANY)]),
        out_specs=pl.BlockSpec((1, R, ch), lambda h, c: (l, 0, h)),
    )
    return _pcall(
        body,
        name=name,
        grid_spec=grid_spec,
        out_shape=jax.ShapeDtypeStruct((n_layers, R, 2 * ch), F32),
        input_output_aliases={} if stack is None else {n_in: 0},
        compiler_params=_params(1),
    )(my_c.reshape(1), mine, theirs, *([] if stack is None else [stack]))


def add_pair(g, r, my_c, name):
    _, rows, w = g.shape
    tm = _pick(rows, max(16, min(1024, (1 << 20) // w // 16 * 16)), 16)

    def body(c_ref, g_ref, r_ref, o_ref):
        o_ref[...] = (g_ref[0].astype(F32) + r_ref[...].astype(F32)).astype(o_ref.dtype)

    grid_spec = pltpu.PrefetchScalarGridSpec(
        num_scalar_prefetch=1,
        grid=(rows // tm,),
        in_specs=[pl.BlockSpec((1, tm, w), lambda i, c: (c[0], i, 0)), pl.BlockSpec((tm, w), lambda i, c: (i, 0))],
        out_specs=pl.BlockSpec((tm, w), lambda i, c: (i, 0)),
    )
    return _pcall(
        body, name=name, grid_spec=grid_spec, out_shape=jax.ShapeDtypeStruct((rows, w), BF16), compiler_params=_params(1)
    )(my_c.reshape(1), g, r)


class Dims:
    def __init__(self, S, D, L, ret_heads, mla_heads, q_rank, kv_rank):
        self.S, self.D, self.L, self.HR, self.HM, self.QR, self.KR = S, D, L, ret_heads, mla_heads, q_rank, kv_rank
        self.RQ, self.RV, self.MV = ret_heads * RET_DK, ret_heads * RET_DV, mla_heads * MLA_DV
        self.D_IN = 2 * self.RQ + 2 * self.RV + q_rank + kv_rank + MLA_ROPE + self.MV + 2 * D
        self.lo = 2 * self.RQ + 2 * self.RV
        self.mid = q_rank + kv_rank + MLA_ROPE
        self.DP = self.D_IN + LANE - MLA_ROPE
        self.o_rq, self.o_rk, self.o_rv, self.o_rg = 0, self.RQ, 2 * self.RQ, 2 * self.RQ + self.RV
        self.o_mg = self.lo
        self.o_bga = self.o_mg + self.MV
        self.o_bgb = self.o_bga + D
        self.o_cq = self.o_bgb + D
        self.o_ckv = self.o_cq + q_rank
        self.o_kr = self.o_ckv + kv_rank

    def pad_uq_rows(self, w):
        heads = w.shape[-2] // (MLA_NOPE + MLA_ROPE)
        w = w.reshape(*w.shape[:-2], heads, MLA_NOPE + MLA_ROPE, w.shape[-1])
        w = jnp.pad(w, [(0, 0)] * (w.ndim - 2) + [(0, MLA_QW - MLA_NOPE - MLA_ROPE), (0, 0)])
        return w.reshape(*w.shape[:-3], heads * MLA_QW, w.shape[-1])

    def unpad_uq_rows(self, w):
        heads = w.shape[-2] // MLA_QW
        w = w.reshape(*w.shape[:-2], heads, MLA_QW, w.shape[-1])[..., : MLA_NOPE + MLA_ROPE, :]
        return w.reshape(*w.shape[:-3], heads * (MLA_NOPE + MLA_ROPE), w.shape[-1])

    def placements(self):
        rows_in = self.D_IN // 4
        lo, mid, n_hi = self.lo, self.mid, self.D_IN - self.lo - self.mid
        cuts = {b % rows_in for b in (lo, lo + mid)} - {0}

        def offset_in(j, start):
            g = j * rows_in + start
            return jnp.where(g < lo, g, jnp.where(g < lo + mid, g + n_hi, g - mid))

        return [
            Placement(rows_in, self.DP, cuts, offset_in, zero_rows=(self.D_IN, self.DP - self.D_IN)),
            Placement(self.HM * MLA_QW // 4, self.HM * MLA_QW),
            Placement(self.HM * (MLA_NOPE + MLA_DV) // 4, self.HM * (MLA_NOPE + MLA_DV)),
            Placement(self.RV // 4, self.RV),
            Placement(self.MV // 4, self.MV),
            Placement(self.D // 4, self.D),
        ]


def _rope_tables(positions, dim, width):
    inv = 1.0 / (ROPE_BASE ** (jnp.arange(0, dim, 2, dtype=F32) / dim))
    ang = positions.astype(F32)[:, None] * inv
    cos, sin = jnp.cos(ang), jnp.sin(ang)
    pad = jnp.zeros((positions.shape[0], width - dim), F32)
    return jnp.concatenate([cos, cos, pad], axis=1), jnp.concatenate([-sin, sin, pad], axis=1)


def _tiles(x):
    return [x[:, t * LANE : (t + 1) * LANE] for t in range(x.shape[1] // LANE)]


def _cat(parts):
    return parts[0] if len(parts) == 1 else jnp.concatenate(parts, axis=1)


def _rms(x, eps=EPS):
    return lax.rsqrt(jnp.mean(x * x, axis=1, keepdims=True) + eps)


def layer_fwd(dm, l, x, shift, scale, gate, g_norm, g_cq, g_ckv, w, tabs):
    cos_r, sin_r, cos_m, sin_m = tabs
    nm = lambda s: f"l{l}_{s}"

    def f_norm(x, g, scale, shift):
        return [x * _rms(x) * g * (1.0 + scale) + shift], []

    (h,) = rowwise(f_norm, [x], [g_norm, scale, shift], [(dm.D, BF16)], tm=256, name=nm("norm"))
    proj = matmul(h, w["in"], tb=True, b_split=True, name=nm("mm_in"), tn=1920)

    def f_rope_ret(rq, rk, cos, sin):
        rq, rk = rq.astype(F32), rk.astype(F32)
        q = _cat([_rope_tile(t, cos, sin, RET_DK // 2) for t in _tiles(rq)])
        k = _cat([_rope_tile(t, cos, sin, RET_DK // 2) * (RET_DK**-0.5) for t in _tiles(rk)])
        return [q, k], []

    rq, rk = rowwise(
        f_rope_ret,
        [win(proj, dm.o_rq, dm.RQ), win(proj, dm.o_rk, dm.RQ), cos_r, sin_r],
        [],
        [(dm.RQ, BF16)] * 2,
        tm=512,
        name=nm("rope_ret"),
    )
    o_ret, a_ret, states = ret_fwd(rq, rk, proj, dm.o_rv, dm.o_rg, heads=dm.HR, tb=RET_BLOCK, name=nm("ret_fwd"))

    def f_prep(cq, ckv, kr, cos, sin, g_cq, g_ckv):
        cq, ckv, kr = cq.astype(F32), ckv.astype(F32), kr.astype(F32)
        return [cq * _rms(cq) * g_cq, ckv * _rms(ckv) * g_ckv, _rope_tile(kr, cos, sin, MLA_ROPE // 2)], []

    cqn, ckvn, krr = rowwise(
        f_prep,
        [win(proj, dm.o_cq, dm.QR), win(proj, dm.o_ckv, dm.KR), win(proj, dm.o_kr, LANE), cos_m, sin_m],
        [g_cq, g_ckv],
        [(dm.QR, BF16), (dm.KR, BF16), (LANE, BF16)],
        tm=512,
        name=nm("mla_prep"),
    )
    q_raw = matmul(cqn, w["uq"], tb=True, b_split=True, name=nm("mm_uq"))
    kv = matmul(ckvn, w["ukv"], tb=True, b_split=True, name=nm("mm_ukv"))

    def f_rope_q(q, cos, sin):
        t = _tiles(q.astype(F32))
        return [_cat([t[n] if n % 2 == 0 else _rope_tile(t[n], cos, sin, MLA_ROPE // 2) for n in range(len(t))])], []

    (q,) = rowwise(f_rope_q, [q_raw, cos_m, sin_m], [], [(dm.HM * MLA_QW, BF16)], tm=256, name=nm("rope_q"))
    o_mla, a_mla, lse = attn_fwd(q, kv, krr, proj, dm.o_mg, heads=dm.HM, tq=ATTN_BLOCK, name=nm("attn_fwd"))

    y_ret = matmul(a_ret, w["ret"], b_split=True, out_dtype=F32, name=nm("mm_ret"))
    y_mla = matmul(a_mla, w["mla"], b_split=True, out_dtype=F32, name=nm("mm_mla"))

    def f_merge(y_ret, y_mla, bga, bgb):
        return [_sigmoid(bga.astype(F32)) * y_ret + _sigmoid(bgb.astype(F32)) * y_mla], []

    (merged,) = rowwise(
        f_merge, [y_ret, y_mla, win(proj, dm.o_bga, dm.D), win(proj, dm.o_bgb, dm.D)], [], [(dm.D, BF16)], tm=256, name=nm("merge")
    )
    out = matmul(merged, w["out"], b_split=True, out_dtype=F32, name=nm("mm_out"))

    def f_resid(x, out, gate):
        return [x + gate * out], []

    (x_new,) = rowwise(f_resid, [x, out], [gate], [(dm.D, F32)], tm=256, name=nm("resid"))
    saved = dict(
        x=x, h=h, proj=proj, rq=rq, rk=rk, o_ret=o_ret, a_ret=a_ret, states=states, cqn=cqn, ckvn=ckvn, krr=krr, q=q, kv=kv,
        o_mla=o_mla, a_mla=a_mla, lse=lse, y_ret=y_ret, y_mla=y_mla, merged=merged, out=out,
    )
    return x_new, saved


def layer_bwd(dm, l, dx_out, sv, shift, scale, gate, g_norm, g_cq, g_ckv, w, tabs):
    cos_r, sin_r, cos_m, sin_m = tabs
    nm = lambda s: f"l{l}_{s}"
    proj = sv["proj"]

    def b_resid(dx, out, gate):
        return [dx * gate], [_sum0(dx * out)]

    dout, d_gate = rowwise(b_resid, [dx_out, sv["out"]], [gate], [(dm.D, BF16)], [dm.D], tm=256, name=nm("resid_bwd"))
    dmerged = matmul(dout, w["out"], tb=True, b_split=True, name=nm("mm_dmerged"))
    dw_out = matmul(sv["merged"], dout, ta=True, out_split=True, name=nm("mm_dw_out"))

    def b_merge(dmg, y_ret, y_mla, bga, bgb):
        dmg = dmg.astype(F32)
        ga, gb = _sigmoid(bga.astype(F32)), _sigmoid(bgb.astype(F32))
        return [dmg * ga, dmg * gb, dmg * y_ret * ga * (1.0 - ga), dmg * y_mla * gb * (1.0 - gb)], []

    dy_ret, dy_mla, d_bga, d_bgb = rowwise(
        b_merge,
        [dmerged, sv["y_ret"], sv["y_mla"], win(proj, dm.o_bga, dm.D), win(proj, dm.o_bgb, dm.D)],
        [],
        [(dm.D, BF16)] * 4,
        tm=256,
        name=nm("merge_bwd"),
    )
    da_ret = matmul(dy_ret, w["ret"], tb=True, b_split=True, name=nm("mm_da_ret"))
    dw_ret = matmul(sv["a_ret"], dy_ret, ta=True, out_split=True, name=nm("mm_dw_ret"))
    da_mla = matmul(dy_mla, w["mla"], tb=True, b_split=True, name=nm("mm_da_mla"))
    dw_mla = matmul(sv["a_mla"], dy_mla, ta=True, out_split=True, name=nm("mm_dw_mla"))

    def b_ret_gate(da, rg, o):
        da, rg = da.astype(F32), rg.astype(F32)
        do_parts, drg_parts = [], []
        for hh in range(dm.HR):
            sl = slice(hh * RET_DV, (hh + 1) * RET_DV)
            oh, dah, rgh = o[:, sl], da[:, sl], rg[:, sl]
            mu = jnp.mean(oh, axis=1, keepdims=True)
            d = oh - mu
            r = lax.rsqrt(jnp.mean(d * d, axis=1, keepdims=True) + EPS)
            n = d * r
            dn = dah * _silu(rgh)
            drg_parts.append(dah * n * _dsilu(rgh))
            do_parts.append(r * (dn - jnp.mean(dn, axis=1, keepdims=True) - n * jnp.mean(dn * n, axis=1, keepdims=True)))
        return [_cat(do_parts), _cat(drg_parts)], []

    do_ret, d_rg = rowwise(
        b_ret_gate, [da_ret, win(proj, dm.o_rg, dm.RV), sv["o_ret"]], [], [(dm.RV, BF16)] * 2, tm=256, name=nm("ret_gate_bwd")
    )
    dq_rot, dk_rot, d_rv = ret_bwd(sv["rq"], sv["rk"], proj, dm.o_rv, sv["states"], do_ret, heads=dm.HR, tb=RET_BLOCK, name=nm("ret_bwd"))

    def b_rope_ret(dq, dk, cos, sin):
        dq, dk = dq.astype(F32), dk.astype(F32)
        q = _cat([_rope_tile(t, cos, sin, RET_DK // 2, inverse=True) for t in _tiles(dq)])
        k = _cat([_rope_tile(t, cos, sin, RET_DK // 2, inverse=True) * (RET_DK**-0.5) for t in _tiles(dk)])
        return [q, k], []

    d_rq, d_rk = rowwise(b_rope_ret, [dq_rot, dk_rot, cos_r, sin_r], [], [(dm.RQ, BF16)] * 2, tm=512, name=nm("rope_ret_bwd"))

    def b_mla_gate(da, mg, o):
        da, mg = da.astype(F32), mg.astype(F32)
        return [da * _silu(mg), da * o * _dsilu(mg)], []

    do_mla, d_mg = rowwise(
        b_mla_gate, [da_mla, win(proj, dm.o_mg, dm.MV), sv["o_mla"]], [], [(dm.MV, BF16)] * 2, tm=256, name=nm("mla_gate_bwd")
    )
    dq_att = attn_bwd_dq(sv["q"], sv["kv"], sv["krr"], do_mla, sv["o_mla"], sv["lse"], heads=dm.HM, tq=ATTN_BLOCK, name=nm("attn_dq"))
    dkv, dkr_heads = attn_bwd_dkv(
        sv["q"], sv["kv"], sv["krr"], do_mla, sv["o_mla"], sv["lse"], heads=dm.HM, tq=ATTN_BLOCK, name=nm("attn_dkv")
    )

    def b_rope_q(dq, cos, sin):
        t = _tiles(dq.astype(F32))
        return [_cat([t[n] if n % 2 == 0 else _rope_tile(t[n], cos, sin, MLA_ROPE // 2, inverse=True) for n in range(len(t))])], []

    (dq_raw,) = rowwise(b_rope_q, [dq_att, cos_m, sin_m], [], [(dm.HM * MLA_QW, BF16)], tm=256, name=nm("rope_q_bwd"))
    dcqn = matmul(dq_raw, w["uq"], b_split=True, name=nm("mm_dcqn"))
    dw_uq = matmul(dq_raw, sv["cqn"], ta=True, out_split=True, name=nm("mm_dw_uq"))
    dckvn = matmul(dkv, w["ukv"], b_split=True, name=nm("mm_dckvn"))
    dw_ukv = matmul(dkv, sv["ckvn"], ta=True, out_split=True, name=nm("mm_dw_ukv"))

    def b_prep(dcqn, dckvn, cq, ckv, cos, sin, dkr_h, g_cq, g_ckv):
        outs, accs = [], []
        for dn, z, g in ((dcqn, cq, g_cq), (dckvn, ckv, g_ckv)):
            dn, z = dn.astype(F32), z.astype(F32)
            n = z * _rms(z)
            dng = dn * g
            outs.append(_rms(z) * (dng - n * jnp.mean(dng * n, axis=1, keepdims=True)))
            accs.append(_sum0(dn * n))
        dkr = dkr_h[0]
        for hh in range(1, dm.HM):
            dkr = dkr + dkr_h[hh]
        outs.append(_rope_tile(dkr, cos, sin, MLA_ROPE // 2, inverse=True))
        return outs, accs

    d_cq, d_ckv, d_kr, dg_cq, dg_ckv = rowwise(
        b_prep,
        [dcqn, dckvn, win(proj, dm.o_cq, dm.QR), win(proj, dm.o_ckv, dm.KR), cos_m, sin_m, dkr_heads],
        [g_cq, g_ckv],
        [(dm.QR, BF16), (dm.KR, BF16), (LANE, BF16)],
        [dm.QR, dm.KR],
        tm=256,
        name=nm("mla_prep_bwd"),
    )

    dproj = jnp.concatenate([d_rq, d_rk, d_rv, d_rg, d_mg, d_bga, d_bgb, d_cq, d_ckv, d_kr], axis=1)
    dh = matmul(dproj, w["in"], b_split=True, out_dtype=F32, name=nm("mm_dh"), tk=1920)
    dw_in = matmul(dproj, sv["h"], ta=True, out_split=True, name=nm("mm_dw_in"))

    def b_norm(dh, x, dx_res, g, scale):
        r = _rms(x)
        xn = x * r
        dxn = dh * g * (1.0 + scale)
        dx = dx_res + r * (dxn - xn * jnp.mean(dxn * xn, axis=1, keepdims=True))
        return [dx], [_sum0(dh), _sum0(dh * xn * g), _sum0(dh * (1.0 + scale) * xn)]

    dx, d_shift, d_scale, dg_norm = rowwise(
        b_norm, [dh, sv["x"], dx_out], [g_norm, scale], [(dm.D, F32)], [dm.D] * 3, tm=256, name=nm("norm_bwd")
    )
    dws = dict(w_in=dw_in, w_uq=dw_uq, w_ukv=dw_ukv, w_ret_proj=dw_ret, w_mla_proj=dw_mla, w_out=dw_out)
    dvec = dict(mod=jnp.concatenate([d_shift, d_scale, d_gate], axis=1), g_norm=dg_norm, g_cq=dg_cq, g_ckv=dg_ckv)
    return dx, dws, dvec


def adamw(w, g, m, v, name):
    shape = w.shape
    cols = shape[-1]
    view = lambda a: a.reshape(-1, cols)

    def f(w, g, m, v):
        m = ADAM_B1 * m + (1.0 - ADAM_B1) * g
        v = ADAM_B2 * v + (1.0 - ADAM_B2) * (g * g)
        m_hat = m / (1.0 - ADAM_B1**ADAM_STEP)
        v_hat = v / (1.0 - ADAM_B2**ADAM_STEP)
        delta = -ADAM_LR * (m_hat / (jnp.sqrt(v_hat) + ADAM_EPS) + ADAM_WD * w)
        return [delta, m, v], []

    tm = max(8, min(512, (400_000 // cols) // 8 * 8))
    delta, m, v = rowwise(f, [view(w), view(g), view(m), view(v)], [], [(cols, F32)] * 3, tm=tm, name=name)
    return delta.reshape(shape), m.reshape(shape), v.reshape(shape)


def _add_rows(fn, rows, cols, dtype, name):
    tm = max(8, min(512, (400_000 // cols) // 8 * 8))
    return rowwise(lambda *a: ([fn(*a)], []), rows, [], [(cols, dtype)], tm=tm, name=name)[0]


BIG = ("w_in", "w_uq", "w_ukv", "w_ret_proj", "w_mla_proj", "w_out")
COL_SHARDED = ("w_in", "w_uq", "w_ukv")


def kernel(x, c, positions, w_mod, b_mod, g_norm, w_in, g_cq, g_ckv, w_uq, w_ukv, w_ret_proj, w_mla_proj, w_out, g_final, loss_target, m_w_mod, m_b_mod, m_g_norm, m_w_in, m_g_cq, m_g_ckv, m_w_uq, m_w_ukv, m_w_ret_proj, m_w_mla_proj, m_w_out, m_g_final, v_w_mod, v_b_mod, v_g_norm, v_w_in, v_g_cq, v_g_ckv, v_w_uq, v_w_ukv, v_w_ret_proj, v_w_mla_proj, v_w_out, v_g_final):
    weights = dict(w_mod=w_mod, b_mod=b_mod, g_norm=g_norm, w_in=w_in, g_cq=g_cq, g_ckv=g_ckv, w_uq=w_uq, w_ukv=w_ukv,
                   w_ret_proj=w_ret_proj, w_mla_proj=w_mla_proj, w_out=w_out, g_final=g_final)
    m_in = dict(w_mod=m_w_mod, b_mod=m_b_mod, g_norm=m_g_norm, w_in=m_w_in, g_cq=m_g_cq, g_ckv=m_g_ckv, w_uq=m_w_uq,
                w_ukv=m_w_ukv, w_ret_proj=m_w_ret_proj, w_mla_proj=m_w_mla_proj, w_out=m_w_out, g_final=m_g_final)
    v_in = dict(w_mod=v_w_mod, b_mod=v_b_mod, g_norm=v_g_norm, w_in=v_w_in, g_cq=v_g_cq, g_ckv=v_g_ckv, w_uq=v_w_uq,
                w_ukv=v_w_ukv, w_ret_proj=v_w_ret_proj, w_mla_proj=v_w_mla_proj, w_out=v_w_out, g_final=v_g_final)
    order = ("w_mod", "b_mod", "g_norm", "w_in", "g_cq", "g_ckv", "w_uq", "w_ukv", "w_ret_proj", "w_mla_proj", "w_out", "g_final")

    x = x[0]
    target = loss_target[0]
    S, D = x.shape
    L = w_mod.shape[0]
    dm = Dims(S, D, L, w_ret_proj.shape[1] * 4 // RET_DV, w_mla_proj.shape[1] * 4 // MLA_DV, g_cq.shape[1], g_ckv.shape[1])
    my_x, my_y, my_c = _me()
    my_chip = 2 * my_x + my_y
    my_dev = 2 * my_chip + my_c
    C3 = w_mod.shape[2]

    prepped = [
        transpose_split(w_in, BF16, "prep_w_in"),
        dm.pad_uq_rows(transpose_split(w_uq, BF16, "prep_w_uq")),
        transpose_split(w_ukv, BF16, "prep_w_ukv"),
        cast_split(w_ret_proj, BF16, "prep_w_ret"),
        cast_split(w_mla_proj, BF16, "prep_w_mla"),
        cast_split(w_out, BF16, "prep_w_out"),
    ]
    places = dm.placements()
    zero_rows = jnp.zeros((places[0].zero_rows[1], D // 2), BF16)
    layer_w = []
    for l in range(L):
        g = gather_layer(prepped, zero_rows, places, l, f"l{l}_gather_w")
        layer_w.append(dict(zip(("in", "uq", "ukv", "ret", "mla", "out"), g)))

    c_all = allgather8(c, "gather_c").reshape(8, D)
    (c_act,) = rowwise(lambda z: ([_silu(z)], []), [c_all], [], [(D, BF16)], tm=8, name="silu_c")
    mod_part = jnp.stack([matmul(c_act, w_mod[l], out_dtype=F32, name=f"l{l}_mm_mod", tn=C3) for l in range(L)])
    mod_all = allgather8(mod_part, "gather_mod")
    mod_all = mod_all.reshape(4, 2, L, 8, C3)[:, 0].transpose(1, 2, 0, 3).reshape(L, 8, 3 * D) + b_mod[:, None, :]
    mod = lax.dynamic_index_in_dim(mod_all, my_dev, axis=1, keepdims=False)

    pos = positions[0]
    tabs = (*_rope_tables(pos, RET_DK, LANE), *_rope_tables(pos, MLA_ROPE, LANE))

    def vecs(l):
        return (mod[l : l + 1, :D], mod[l : l + 1, D : 2 * D], mod[l : l + 1, 2 * D :],
                g_norm[l : l + 1], g_cq[l : l + 1], g_ckv[l : l + 1])

    saved = []
    for l in range(L):
        x, sv = layer_fwd(dm, l, x, *vecs(l), layer_w[l], tabs)
        saved.append(sv)

    def f_loss(x, t, g):
        xn = x * _rms(x)
        err = xn * g - t
        dy = err * (1.0 / D)
        dxn = dy * g
        dx = _rms(x) * (dxn - xn * jnp.mean(dxn * xn, axis=1, keepdims=True))
        part = jnp.sum(jnp.sum(err * err, axis=1, keepdims=True), axis=0, keepdims=True) * (0.5 / D)
        return [dx], [jnp.broadcast_to(part, (1, LANE)), _sum0(dy * xn)]

    dx, loss_part, dg_final = rowwise(f_loss, [x, target], [g_final.reshape(1, D)], [(D, F32)], [LANE, D], tm=256, name="loss_head")

    stacks = {n: None for n in BIG}
    dvec = {n: [None] * L for n in ("mod", "g_norm", "g_cq", "g_ckv")}
    for l in reversed(range(L)):
        dx, dw_l, dv_l = layer_bwd(dm, l, dx, saved[l], *vecs(l), layer_w[l], tabs)
        for n in dvec:
            dvec[n][l] = dv_l[n]
        partial = [dw_l[n] for n in BIG]
        from_sibling = sibling_scatter(partial, f"l{l}_rs_cores")
        pair = [add_pair(g, r, my_c, f"l{l}_rs_add2_{n}") for n, g, r in zip(BIG, partial, from_sibling)]
        from_chips = scatter_to_chips(pair, places, f"l{l}_rs_chips")
        mine = [
            _add_rows(lambda a: ((a[3].astype(F32) + a[0].astype(F32)) + a[1].astype(F32)) + a[2].astype(F32), [r], r.shape[-1],
                      F32, f"l{l}_rs_add4_{n}")
            for n, r in zip(BIG, from_chips)
        ]
        theirs = sibling_swap(mine, f"l{l}_rs_share")
        for n, a, b in zip(BIG, mine, theirs):
            if n == "w_uq":
                a, b = dm.unpad_uq_rows(a), dm.unpad_uq_rows(b)
            finish = untranspose_halves if n in COL_SHARDED else merge_halves
            stacks[n] = finish(a, b, my_c, stacks[n], l, L, f"l{l}_grad_{n}")
    grad_x = dx[None]

    pieces = [loss_part] + [jnp.concatenate(dvec[n], axis=1) for n in ("mod", "g_norm", "g_cq", "g_ckv")] + [dg_final]
    widths = [p.shape[1] for p in pieces]
    small_all = allgather8(jnp.concatenate(pieces, axis=1), "gather_small").reshape(8, sum(widths))

    def sum8_body(a_ref, o_ref):
        acc = a_ref[0:1, :]
        for d in range(1, 8):
            acc = acc + a_ref[d : d + 1, :]
        o_ref[...] = acc

    small = _pcall(sum8_body, name="sum_small", out_shape=jax.ShapeDtypeStruct((1, sum(widths)), F32))(small_all)
    offs = np.cumsum([0] + widths)
    loss = small[0, 0]
    g_small = {
        "b_mod": small[0, offs[1] : offs[2]].reshape(L, 3 * D),
        "g_norm": small[0, offs[2] : offs[3]].reshape(L, D),
        "g_cq": small[0, offs[3] : offs[4]].reshape(L, dm.QR),
        "g_ckv": small[0, offs[4] : offs[5]].reshape(L, dm.KR),
        "g_final": small[0, offs[5] : offs[6]],
    }

    dmod_all = small_all[:, offs[1] : offs[2]].reshape(8, L, 3 * D)
    dmod_mine = lax.dynamic_slice_in_dim(dmod_all, my_chip * C3, C3, axis=2)
    pad8 = lambda a: jnp.pad(a, ((0, LANE - 8), (0, 0)))
    grads = dict(g_small)
    grads["w_mod"] = jnp.stack(
        [matmul(pad8(c_act), pad8(dmod_mine[:, l]), ta=True, out_dtype=F32, name=f"l{l}_mm_dw_mod", tn=C3) for l in range(L)]
    )

    for n in BIG:
        grads[n] = stacks[n]

    deltas, new_m, new_v = {}, {}, {}
    for n in order:
        wv, gv, mv, vv = weights[n], grads[n], m_in[n], v_in[n]
        if wv.ndim == 1:
            wv, gv, mv, vv = (a.reshape(1, -1) for a in (wv, gv, mv, vv))
        d_, m_, v_ = adamw(wv, gv, mv, vv, f"adamw_{n}")
        deltas[n], new_m[n], new_v[n] = (a.reshape(weights[n].shape) for a in (d_, m_, v_))
        grads[n] = grads[n].reshape(weights[n].shape)

    return (loss, grad_x, *[grads[n] for n in order], *[deltas[n] for n in order], *[new_m[n] for n in order],
            *[new_v[n] for n in order])
```

```python
import functools

import jax
import jax.numpy as jnp
import numpy as np
from jax import lax
from jax.experimental import pallas as pl
from jax.experimental.pallas import tpu as pltpu

F32 = jnp.float32
BF16 = jnp.bfloat16
MESH = pl.DeviceIdType.MESH

VMEM_LIMIT_BYTES = 52 * 1024 * 1024
LANE = 128

CHUNK = 64
EPS = 1e-6
NEG_INF = -1e30
ROPE_BASE = 10000.0
RET_DK = 128
RET_DV = 256
MLA_NOPE = 128
MLA_ROPE = 64
MLA_DV = 128
MLA_QW = 256
MLA_SCALE = float((MLA_NOPE + MLA_ROPE) ** -0.5)
LOG2E = float(np.log2(np.e))
LN2 = float(np.log(2.0))
ATTN_BLOCK = 512
RET_BLOCK = 256

ADAM_LR = 0.001
ADAM_B1 = 0.9
ADAM_B2 = 0.999
ADAM_EPS = 1e-08
ADAM_WD = 0.01
ADAM_STEP = 10


def _pcall(body, **kw):
    return pl.pallas_call(body, **kw)


def _params(n_grid):
    return pltpu.CompilerParams(dimension_semantics=("arbitrary",) * n_grid, vmem_limit_bytes=VMEM_LIMIT_BYTES)


def _pick(dim, target, mult=LANE):
    if dim <= target:
        return dim
    best = None
    for t in range(mult, target + 1, mult):
        if dim % t == 0:
            best = t
    assert best is not None, (dim, target, mult)
    return best


def matmul(a, b, *, ta=False, tb=False, b_split=False, out_split=False, out_dtype=BF16, name, tm=512, tn=1024, tk=2048):
    (M, K) = (a.shape[1], a.shape[0]) if ta else a.shape
    b_rows, b_cols = (b.shape[1], 2 * b.shape[2]) if b_split else b.shape
    (K2, N) = (b_cols, b_rows) if tb else (b_rows, b_cols)
    assert K == K2, (a.shape, b.shape, ta, tb)
    whole_k = b_split and tb and tk >= K and not ta
    whole_n = out_split and tn >= N and not (b_split and not tb)
    n_cap = N if whole_n else (N // 2 if (out_split or (b_split and not tb)) else N)
    k_cap = K if whole_k else (K // 2 if (b_split and tb) else K)
    tm, tn, tk = _pick(M, tm, 8 if M < LANE else LANE), _pick(n_cap, min(tn, n_cap)), _pick(k_cap, min(tk, k_cap))
    nk = K // tk
    njh, nkh = max((N // 2) // tn, 1), max((K // 2) // tk, 1)
    dn = (((0 if ta else 1,), (1 if tb else 0,)), ((), ()))

    def body(a_ref, b_ref, o_ref, *scratch):
        if whole_k:
            a_blk = a_ref[...].astype(BF16)
            prod = lax.dot_general(a_blk[:, : K // 2], b_ref[0].astype(BF16), dn, preferred_element_type=F32)
            prod += lax.dot_general(a_blk[:, K // 2 :], b_ref[1].astype(BF16), dn, preferred_element_type=F32)
        else:
            b_blk = b_ref[0] if b_split else b_ref[...]
            prod = lax.dot_general(a_ref[...].astype(BF16), b_blk.astype(BF16), dn, preferred_element_type=F32)

        def store(v):
            if whole_n:
                o_ref[0] = v[:, : N // 2].astype(o_ref.dtype)
                o_ref[1] = v[:, N // 2 :].astype(o_ref.dtype)
            elif out_split:
                o_ref[0] = v.astype(o_ref.dtype)
            else:
                o_ref[...] = v.astype(o_ref.dtype)

        if nk == 1:
            store(prod)
            return
        (acc_ref,) = scratch
        k = pl.program_id(2)

        @pl.when(k == 0)
        def _():
            acc_ref[...] = prod

        @pl.when(k > 0)
        def _():
            acc_ref[...] += prod

        @pl.when(k == nk - 1)
        def _():
            store(acc_ref[...])

    a_spec = pl.BlockSpec((tk, tm), lambda i, j, k: (k, i)) if ta else pl.BlockSpec((tm, tk), lambda i, j, k: (i, k))
    if whole_k:
        b_spec = pl.BlockSpec((2, tn, K // 2), lambda i, j, k: (0, j, 0))
    elif b_split and tb:
        b_spec = pl.BlockSpec((1, tn, tk), lambda i, j, k: (k // nkh, j, k % nkh))
    elif b_split:
        b_spec = pl.BlockSpec((1, tk, tn), lambda i, j, k: (j // njh, k, j % njh))
    elif tb:
        b_spec = pl.BlockSpec((tn, tk), lambda i, j, k: (j, k))
    else:
        b_spec = pl.BlockSpec((tk, tn), lambda i, j, k: (k, j))
    if whole_n:
        out_spec = pl.BlockSpec((2, tm, N // 2), lambda i, j, k: (0, i, 0))
        out_shape = jax.ShapeDtypeStruct((2, M, N // 2), out_dtype)
    elif out_split:
        out_spec = pl.BlockSpec((1, tm, tn), lambda i, j, k: (j // njh, i, j % njh))
        out_shape = jax.ShapeDtypeStruct((2, M, N // 2), out_dtype)
    else:
        out_spec = pl.BlockSpec((tm, tn), lambda i, j, k: (i, j))
        out_shape = jax.ShapeDtypeStruct((M, N), out_dtype)
    return _pcall(
        body,
        name=name,
        grid=(M // tm, N // tn, nk),
        in_specs=[a_spec, b_spec],
        out_specs=out_spec,
        out_shape=out_shape,
        scratch_shapes=[] if nk == 1 else [pltpu.VMEM((tm, tn), F32)],
        compiler_params=_params(3),
    )(a, b)


def win(arr, off, width):
    assert off % width == 0 and off + width <= arr.shape[1], (arr.shape, off, width)
    return (arr, off // width, width)


def rowwise(fn, rows, vecs, outs, accs=(), *, tm, name):
    rows = [r if isinstance(r, tuple) else (r, 0, r.shape[-1]) for r in rows]
    S = rows[0][0].shape[-2]
    tm = _pick(S, tm, 8)
    n_rows, n_vecs, n_outs = len(rows), len(vecs), len(outs)

    def body(*refs):
        ins = [r[...] for r in refs[: n_rows + n_vecs]]
        out_refs = refs[n_rows + n_vecs : n_rows + n_vecs + n_outs]
        acc_refs = refs[n_rows + n_vecs + n_outs :]
        res, acc = fn(*ins)
        for r, v in zip(out_refs, res, strict=True):
            r[...] = v.astype(r.dtype)
        if acc_refs:
            i = pl.program_id(0)

            @pl.when(i == 0)
            def _():
                for r, v in zip(acc_refs, acc, strict=True):
                    r[...] = v

            @pl.when(i > 0)
            def _():
                for r, v in zip(acc_refs, acc, strict=True):
                    r[...] += v

    in_specs = []
    for arr, blk, w in rows:
        if arr.ndim == 3:
            in_specs.append(pl.BlockSpec((arr.shape[0], tm, w), lambda i: (0, i, 0)))
        else:
            in_specs.append(pl.BlockSpec((tm, w), functools.partial(lambda i, blk: (i, blk), blk=blk)))
    for v in vecs:
        in_specs.append(pl.BlockSpec(v.shape, functools.partial(lambda i, nd: (0,) * nd, nd=v.ndim)))
    out_specs = [pl.BlockSpec((tm, w), lambda i: (i, 0)) for w, _ in outs]
    out_specs += [pl.BlockSpec((1, w), lambda i: (0, 0)) for w in accs]
    out_shape = [jax.ShapeDtypeStruct((S, w), dt) for w, dt in outs]
    out_shape += [jax.ShapeDtypeStruct((1, w), F32) for w in accs]
    res = _pcall(
        body,
        name=name,
        grid=(S // tm,),
        in_specs=in_specs,
        out_specs=out_specs,
        out_shape=out_shape,
        compiler_params=_params(1),
    )(*[r[0] for r in rows], *vecs)
    return res


def _sum0(v):
    return jnp.sum(v, axis=0, keepdims=True)


def _sigmoid(z):
    return 1.0 / (1.0 + jnp.exp(-z))


def _silu(z):
    return z * _sigmoid(z)


def _dsilu(z):
    s = _sigmoid(z)
    return s * (1.0 + z * (1.0 - s))


def _swap_half(x, half):
    if 2 * half == LANE:
        return pltpu.roll(x, half, 1)
    lane = lax.broadcasted_iota(jnp.int32, x.shape, 1)
    return jnp.where((lane % (2 * half)) < half, pltpu.roll(x, LANE - half, 1), pltpu.roll(x, half, 1))


def _rope_tile(x, cosf, sinf, half, inverse=False):
    sw = _swap_half(x, half)
    return x * cosf - sw * sinf if inverse else x * cosf + sw * sinf


def _tri_tables(n, by_key):
    if by_key:
        pairs = [(i, j) for j in range(n) for i in range(j, n)]
    else:
        pairs = [(i, j) for i in range(n) for j in range(i + 1)]
    return (np.array([p[0] for p in pairs], np.int32), np.array([p[1] for p in pairs], np.int32))


def _attn_scores(q, kn, kr, i, j, tq, masked):
    k = jnp.concatenate([kn, kr], axis=1)
    s = lax.dot_general(q, k, (((1,), (1,)), ((), ())), preferred_element_type=F32)
    if masked:
        rc = (i * tq + lax.broadcasted_iota(jnp.int32, s.shape, 0)) // CHUNK
        cc = (j * tq + lax.broadcasted_iota(jnp.int32, s.shape, 1)) // CHUNK
        s = jnp.where(cc <= rc, s, NEG_INF)
    return s, k


def attn_fwd(q, kv, kr, proj, mg_off, *, heads, tq, name, host=None):
    S = q.shape[0]
    tq = _pick(S, tq, CHUNK)
    n = S // tq
    it, jt = _tri_tables(n, by_key=False)
    T = len(it)
    mg_blk = mg_off // MLA_DV
    n_hi, n_ho = (len(host.ins), len(host.out_shapes)) if host else (0, 0)

    def body(it_ref, jt_ref, q_ref, kn_ref, v_ref, kr_ref, mg_ref, *rest):
        host_in, rest = rest[:n_hi], rest[n_hi:]
        (o_ref, a_ref, lse_ref), rest = rest[:3], rest[3:]
        host_out, rest = rest[:n_ho], rest[n_ho:]
        (m_sc, l_sc, acc_sc), host_sems = rest[:3], rest[3:]
        h, t = pl.program_id(0), pl.program_id(1)
        i, j = it_ref[t], jt_ref[t]
        if host:
            _host_steps(host, (host_in, host_out, host_sems), h, t, heads, T, before=True)

        @pl.when(j == 0)
        def _():
            m_sc[...] = jnp.full(m_sc.shape, NEG_INF, F32)
            l_sc[...] = jnp.zeros(l_sc.shape, F32)
            acc_sc[...] = jnp.zeros(acc_sc.shape, F32)

        def step(masked):
            s, _ = _attn_scores(q_ref[...], kn_ref[...], kr_ref[...], i, j, tq, masked)
            m_prev = m_sc[...]
            m_new = jnp.maximum(m_prev, jnp.max(s, axis=1, keepdims=True))
            p = jnp.exp2((s - m_new).astype(BF16))
            alpha = jnp.exp2(m_prev - m_new)
            ones = jnp.ones((p.shape[1], LANE), BF16)
            l_sc[...] = alpha * l_sc[...] + jnp.dot(p, ones, preferred_element_type=F32)
            acc_sc[...] = alpha * acc_sc[...] + jnp.dot(p, v_ref[...], preferred_element_type=F32)
            m_sc[...] = m_new

        @pl.when(j < i)
        def _():
            step(False)

        @pl.when(j == i)
        def _():
            step(True)
            l = l_sc[...]
            o = acc_sc[...] / l
            o_ref[...] = o
            a_ref[...] = (o * _silu(mg_ref[...].astype(F32))).astype(a_ref.dtype)
            lse_ref[0] = m_sc[...] + jnp.log2(l[:, :1])

        if host:
            _host_steps(host, (host_in, host_out, host_sems), h, t, heads, T, before=False)

    grid_spec = pltpu.PrefetchScalarGridSpec(
        num_scalar_prefetch=2,
        grid=(heads, T),
        in_specs=[
            pl.BlockSpec((tq, MLA_QW), lambda h, t, it, jt: (it[t], h)),
            pl.BlockSpec((tq, MLA_NOPE), lambda h, t, it, jt: (jt[t], 2 * h)),
            pl.BlockSpec((tq, MLA_DV), lambda h, t, it, jt: (jt[t], 2 * h + 1)),
            pl.BlockSpec((tq, LANE), lambda h, t, it, jt: (jt[t], 0)),
            pl.BlockSpec((tq, MLA_DV), lambda h, t, it, jt: (it[t], mg_blk + h)),
            *(host.in_specs if host else []),
        ],
        out_specs=[
            pl.BlockSpec((tq, MLA_DV), lambda h, t, it, jt: (it[t], h)),
            pl.BlockSpec((tq, MLA_DV), lambda h, t, it, jt: (it[t], h)),
            pl.BlockSpec((1, tq, 1), lambda h, t, it, jt: (h, it[t], 0)),
            *(host.out_specs if host else []),
        ],
        scratch_shapes=[
            pltpu.VMEM((tq, 1), F32),
            pltpu.VMEM((tq, LANE), F32),
            pltpu.VMEM((tq, MLA_DV), F32),
            *(host.scratch if host else []),
        ],
    )
    return _pcall(
        body,
        name=name,
        grid_spec=grid_spec,
        out_shape=[
            jax.ShapeDtypeStruct((S, heads * MLA_DV), F32),
            jax.ShapeDtypeStruct((S, heads * MLA_DV), BF16),
            jax.ShapeDtypeStruct((heads, S, 1), F32),
            *(host.out_shapes if host else []),
        ],
        compiler_params=_params(2),
    )(jnp.asarray(it), jnp.asarray(jt), q, kv, kv, kr, proj, *(host.ins if host else []))


def attn_bwd(q, kv, kr, do, o, lse, *, heads, tq, name, host=None):
    S = q.shape[0]
    tq = _pick(S, tq, CHUNK)
    n = S // tq
    it, jt = _tri_tables(n, by_key=True)
    T = len(it)
    n_hi, n_ho = (len(host.ins), len(host.out_shapes)) if host else (0, 0)

    def body(it_ref, jt_ref, q_ref, kn_ref, v_ref, kr_ref, do_ref, o_ref, lse_ref, *rest):
        host_in, rest = rest[:n_hi], rest[n_hi:]
        (dq_ref, dkv_ref, dkr_ref), rest = rest[:3], rest[3:]
        host_out, rest = rest[:n_ho], rest[n_ho:]
        (dq_acc, dk_sc, dv_sc), host_sems = rest[:3], rest[3:]
        h, t = pl.program_id(0), pl.program_id(1)
        i, j = it_ref[t], jt_ref[t]
        if host:
            _host_steps(host, (host_in, host_out, host_sems), h, t, heads, T, before=True)

        @pl.when(t == 0)
        def _():
            dq_acc[...] = jnp.zeros(dq_acc.shape, F32)

        @pl.when(i == j)
        def _():
            dk_sc[...] = jnp.zeros(dk_sc.shape, F32)
            dv_sc[...] = jnp.zeros(dv_sc.shape, F32)

        def step(masked):
            q_blk, do_blk = q_ref[...], do_ref[...]
            s, k = _attn_scores(q_blk, kn_ref[...], kr_ref[...], i, j, tq, masked)
            p = jnp.exp2(s - lse_ref[0])
            delta = jnp.sum(do_blk.astype(F32) * o_ref[...], axis=1, keepdims=True)
            dp = lax.dot_general(do_blk, v_ref[...], _NT, preferred_element_type=F32)
            ds = (p * (dp - delta)).astype(BF16)
            dv_sc[...] += lax.dot_general(p.astype(BF16), do_blk, _TN, preferred_element_type=F32)
            dk_sc[...] += lax.dot_general(ds, q_blk, _TN, preferred_element_type=F32)
            rows = pl.ds(pl.multiple_of(i * tq, tq), tq)
            dq_acc[rows, :] += jnp.dot(ds, k, preferred_element_type=F32)

        @pl.when(i == j)
        def _():
            step(True)

        @pl.when(i > j)
        def _():
            step(False)

        @pl.when(i == n - 1)
        def _():
            dkv_ref[:, :MLA_NOPE] = (dk_sc[:, :MLA_NOPE] * LN2).astype(dkv_ref.dtype)
            dkv_ref[:, MLA_NOPE:] = dv_sc[...].astype(dkv_ref.dtype)
            dkr_ref[0] = dk_sc[:, MLA_NOPE:] * LN2

        @pl.when(t == T - 1)
        def _():
            dq_ref[...] = (dq_acc[...] * MLA_SCALE).astype(dq_ref.dtype)

        if host:
            _host_steps(host, (host_in, host_out, host_sems), h, t, heads, T, before=False)

    grid_spec = pltpu.PrefetchScalarGridSpec(
        num_scalar_prefetch=2,
        grid=(heads, T),
        in_specs=[
            pl.BlockSpec((tq, MLA_QW), lambda h, t, it, jt: (it[t], h)),
            pl.BlockSpec((tq, MLA_NOPE), lambda h, t, it, jt: (jt[t], 2 * h)),
            pl.BlockSpec((tq, MLA_DV), lambda h, t, it, jt: (jt[t], 2 * h + 1)),
            pl.BlockSpec((tq, LANE), lambda h, t, it, jt: (jt[t], 0)),
            pl.BlockSpec((tq, MLA_DV), lambda h, t, it, jt: (it[t], h)),
            pl.BlockSpec((tq, MLA_DV), lambda h, t, it, jt: (it[t], h)),
            pl.BlockSpec((1, tq, 1), lambda h, t, it, jt: (h, it[t], 0)),
            *(host.in_specs if host else []),
        ],
        out_specs=[
            pl.BlockSpec((S, MLA_QW), lambda h, t, it, jt: (0, h)),
            pl.BlockSpec((tq, MLA_NOPE + MLA_DV), lambda h, t, it, jt: (jt[t], h)),
            pl.BlockSpec((1, tq, LANE), lambda h, t, it, jt: (h, jt[t], 0)),
            *(host.out_specs if host else []),
        ],
        scratch_shapes=[
            pltpu.VMEM((S, MLA_QW), F32),
            pltpu.VMEM((tq, MLA_QW), F32),
            pltpu.VMEM((tq, MLA_DV), F32),
            *(host.scratch if host else []),
        ],
    )
    return _pcall(
        body,
        name=name,
        grid_spec=grid_spec,
        out_shape=[
            jax.ShapeDtypeStruct((S, heads * MLA_QW), BF16),
            jax.ShapeDtypeStruct((S, heads * (MLA_NOPE + MLA_DV)), BF16),
            jax.ShapeDtypeStruct((heads, S, LANE), F32),
            *(host.out_shapes if host else []),
        ],
        compiler_params=_params(2),
    )(jnp.asarray(it), jnp.asarray(jt), q, kv, kv, kr, do, o, lse, *(host.ins if host else []))


def _host_steps(host, refs, h, t, heads, n_steps, before):
    if before:

        @pl.when((h == 0) & (t == 0))
        def _():
            host.start(*refs)

        @pl.when((h == heads // 2) & (t == 0))
        def _():
            host.relay(*refs)

    else:

        @pl.when((h == heads - 1) & (t == n_steps - 1))
        def _():
            host.finish(*refs)


def _ret_consts(heads):
    h = np.arange(heads, dtype=np.float32)
    lg = np.log(np.float32(1.0) - np.float32(2.0) ** (np.float32(-5.0) - h)).astype(np.float32)
    idx = np.arange(CHUNK, dtype=np.float32)
    dmat = np.exp(np.abs(idx[:, None] - idx[None, :])[None] * lg[:, None, None]).astype(np.float32)
    xi = np.exp((idx + 1.0)[None, :] * lg[:, None]).astype(np.float32)
    zeta = np.exp((CHUNK - 1.0 - idx)[None, :] * lg[:, None]).astype(np.float32)
    dec = np.exp(np.float32(CHUNK) * lg).astype(np.float32)
    xi = np.broadcast_to(xi[:, :, None], (heads, CHUNK, RET_DK)).copy()
    zeta = np.broadcast_to(zeta[:, :, None], (heads, CHUNK, RET_DK)).copy()
    dec = np.broadcast_to(dec[:, None, None], (heads, 8, LANE)).copy()
    return jnp.asarray(dmat), jnp.asarray(xi), jnp.asarray(zeta), jnp.asarray(dec)


_NT = (((1,), (1,)), ((), ()))
_TN = (((0,), (0,)), ((), ()))


def _dot(a, b, dn=(((1,), (0,)), ((), ()))):
    return lax.dot_general(a.astype(BF16), b.astype(BF16), dn, preferred_element_type=F32)


def _const_specs(heads):
    return [
        pl.BlockSpec((1, CHUNK, CHUNK), lambda h, b: (h, 0, 0)),
        pl.BlockSpec((1, CHUNK, RET_DK), lambda h, b: (h, 0, 0)),
        pl.BlockSpec((1, CHUNK, RET_DK), lambda h, b: (h, 0, 0)),
        pl.BlockSpec((1, 8, LANE), lambda h, b: (h, 0, 0)),
    ]


def ret_fwd(q, k, proj, v_off, rg_off, *, heads, tb, name):
    S = q.shape[0]
    tb = _pick(S, tb, CHUNK)
    cpb, nb = tb // CHUNK, S // tb
    v_blk, rg_blk = v_off // RET_DV, rg_off // RET_DV

    def body(q_ref, k_ref, v_ref, rg_ref, dm_ref, xi_ref, ze_ref, dec_ref, o_ref, a_ref, st_ref, r_sc):
        @pl.when(pl.program_id(1) == 0)
        def _():
            r_sc[...] = jnp.zeros(r_sc.shape, F32)

        dm, xi, ze, dec = dm_ref[0], xi_ref[0], ze_ref[0], dec_ref[0, 0:1, 0:1]
        for c in range(cpb):
            sl = pl.ds(c * CHUNK, CHUNK)
            qc, kc, vc = q_ref[sl, :], k_ref[sl, :], v_ref[sl, :]
            r = r_sc[...]
            st_ref[0, c] = r.astype(BF16)
            s = _dot(qc, kc, _NT) * dm
            o = _dot(s, vc) + _dot(qc.astype(F32) * xi, r)
            r_sc[...] = r * dec + _dot(kc.astype(F32) * ze, vc, _TN)
            mu = jnp.mean(o, axis=1, keepdims=True)
            d = o - mu
            n = d * lax.rsqrt(jnp.mean(d * d, axis=1, keepdims=True) + EPS)
            o_ref[sl, :] = o
            a_ref[sl, :] = (n * _silu(rg_ref[sl, :].astype(F32))).astype(a_ref.dtype)

    return _pcall(
        body,
        name=name,
        grid=(heads, nb),
        in_specs=[
            pl.BlockSpec((tb, RET_DK), lambda h, b: (b, h)),
            pl.BlockSpec((tb, RET_DK), lambda h, b: (b, h)),
            pl.BlockSpec((tb, RET_DV), lambda h, b: (b, v_blk + h)),
            pl.BlockSpec((tb, RET_DV), lambda h, b: (b, rg_blk + h)),
            *_const_specs(heads),
        ],
        out_specs=[
            pl.BlockSpec((tb, RET_DV), lambda h, b: (b, h)),
            pl.BlockSpec((tb, RET_DV), lambda h, b: (b, h)),
            pl.BlockSpec((1, cpb, RET_DK, RET_DV), lambda h, b: (h, b, 0, 0)),
        ],
        out_shape=[
            jax.ShapeDtypeStruct((S, heads * RET_DV), F32),
            jax.ShapeDtypeStruct((S, heads * RET_DV), BF16),
            jax.ShapeDtypeStruct((heads, S // CHUNK, RET_DK, RET_DV), BF16),
        ],
        scratch_shapes=[pltpu.VMEM((RET_DK, RET_DV), F32)],
        compiler_params=_params(2),
    )(q, k, proj, proj, *_ret_consts(heads))


def ret_bwd(q, k, proj, v_off, states, do, *, heads, tb, name):
    S = q.shape[0]
    tb = _pick(S, tb, CHUNK)
    cpb, nb = tb // CHUNK, S // tb
    v_blk = v_off // RET_DV

    def body(q_ref, k_ref, v_ref, st_ref, do_ref, dm_ref, xi_ref, ze_ref, dec_ref, dq_ref, dk_ref, dv_ref, dr_sc):
        @pl.when(pl.program_id(1) == 0)
        def _():
            dr_sc[...] = jnp.zeros(dr_sc.shape, F32)

        dm, xi, ze, dec = dm_ref[0], xi_ref[0], ze_ref[0], dec_ref[0, 0:1, 0:1]
        for c in reversed(range(cpb)):
            sl = pl.ds(c * CHUNK, CHUNK)
            qc, kc, vc, doc = q_ref[sl, :], k_ref[sl, :], v_ref[sl, :], do_ref[sl, :]
            r_prev = st_ref[0, c]
            dr = dr_sc[...]
            a = _dot(qc, kc, _NT) * dm
            ds = _dot(doc, vc, _NT) * dm
            kz = kc.astype(F32) * ze
            dq_ref[sl, :] = (_dot(ds, kc) + _dot(doc, r_prev, _NT) * xi).astype(dq_ref.dtype)
            dk_ref[sl, :] = (_dot(ds, qc, _TN) + _dot(vc, dr, _NT) * ze).astype(dk_ref.dtype)
            dv_ref[sl, :] = (_dot(a, doc, _TN) + _dot(kz, dr)).astype(dv_ref.dtype)
            dr_sc[...] = dr * dec + _dot(qc.astype(F32) * xi, doc, _TN)

    rev = lambda h, b: (nb - 1 - b, h)
    return _pcall(
        body,
        name=name,
        grid=(heads, nb),
        in_specs=[
            pl.BlockSpec((tb, RET_DK), rev),
            pl.BlockSpec((tb, RET_DK), rev),
            pl.BlockSpec((tb, RET_DV), lambda h, b: (nb - 1 - b, v_blk + h)),
            pl.BlockSpec((1, cpb, RET_DK, RET_DV), lambda h, b: (h, nb - 1 - b, 0, 0)),
            pl.BlockSpec((tb, RET_DV), rev),
            *_const_specs(heads),
        ],
        out_specs=[pl.BlockSpec((tb, RET_DK), rev), pl.BlockSpec((tb, RET_DK), rev), pl.BlockSpec((tb, RET_DV), rev)],
        out_shape=[
            jax.ShapeDtypeStruct((S, heads * RET_DK), BF16),
            jax.ShapeDtypeStruct((S, heads * RET_DK), BF16),
            jax.ShapeDtypeStruct((S, heads * RET_DV), BF16),
        ],
        scratch_shapes=[pltpu.VMEM((RET_DK, RET_DV), F32)],
        compiler_params=_params(2),
    )(q, k, proj, states, do, *_ret_consts(heads))


def _me():
    return (lax.axis_index("x"), lax.axis_index("y"), lax.axis_index("c"))


def _comm(name, ins, out_shapes, n_local, n_remote, plan):
    return run_exchange(name, Exchange(ins, out_shapes, n_local, n_remote, plan))


def run_exchange(name, ex):
    def body(*refs):
        ex.start(*ex.split(refs))
        ex.relay(*ex.split(refs))
        ex.finish(*ex.split(refs))

    return _pcall(
        body, name=name, in_specs=ex.in_specs, out_specs=ex.out_specs, out_shape=ex.out_shapes, scratch_shapes=ex.scratch
    )(*ex.ins)


class Exchange:
    def __init__(self, ins, out_shapes, n_local, n_remote, plan):
        self.ins, self.out_shapes, self.n_local, self.n_remote, self.plan = list(ins), list(out_shapes), n_local, n_remote, plan
        any_spec = pl.BlockSpec(memory_space=pl.ANY)
        self.in_specs, self.out_specs = [any_spec] * len(self.ins), [any_spec] * len(self.out_shapes)
        self.scratch = [
            pltpu.SemaphoreType.DMA((n_remote,)),
            pltpu.SemaphoreType.DMA((n_remote,)),
            pltpu.SemaphoreType.DMA((max(n_local, 1),)),
        ]

    def split(self, refs):
        n_in, n_out = len(self.ins), len(self.out_shapes)
        return refs[:n_in], refs[n_in : n_in + n_out], refs[n_in + n_out :]

    def _copies(self, in_refs, out_refs, sems):
        send_sems, recv_sems, local_sems = sems
        me = _me()
        local_plan, remote_plan = self.plan(me, in_refs, out_refs)
        assert len(local_plan) == self.n_local and len(remote_plan) == self.n_remote, (len(local_plan), len(remote_plan))
        def local_copy(n):
            src, dst = local_plan[n]
            return pltpu.make_async_copy(src, dst, local_sems.at[n])

        def send(n):
            src, dst, peer, _, _ = remote_plan[n]
            return pltpu.make_async_remote_copy(
                src_ref=src, dst_ref=dst, send_sem=send_sems.at[n], recv_sem=recv_sems.at[n], device_id=peer, device_id_type=MESH
            )

        def arrival(n):
            src, _, _, landing, _ = remote_plan[n]
            return pltpu.make_async_remote_copy(
                src_ref=src, dst_ref=landing, send_sem=send_sems.at[n], recv_sem=recv_sems.at[n], device_id=me, device_id_type=MESH
            )

        after = [a for (_, _, _, _, a) in remote_plan]
        assert all(a is None or a < n for n, a in enumerate(after))
        return local_copy, send, arrival, after

    def start(self, in_refs, out_refs, sems):
        local_copy, send, _, after = self._copies(in_refs, out_refs, sems)
        for n in range(self.n_local):
            local_copy(n).start()
        for n, a in enumerate(after):
            if a is None:
                send(n).start()

    def relay(self, in_refs, out_refs, sems):
        _, send, arrival, after = self._copies(in_refs, out_refs, sems)
        for n in sorted({a for a in after if a is not None}):
            arrival(n).wait_recv()
            for m, a in enumerate(after):
                if a == n:
                    send(m).start()

    def finish(self, in_refs, out_refs, sems):
        local_copy, send, arrival, after = self._copies(in_refs, out_refs, sems)
        relayed = {a for a in after if a is not None}
        for n in range(self.n_remote):
            if n not in relayed:
                arrival(n).wait_recv()
        for n in range(self.n_remote):
            send(n).wait_send()
        for n in range(self.n_local):
            local_copy(n).wait()


_CHIP_FLIPS = ((1, 0), (0, 1), (1, 1))


class Placement:
    def __init__(self, rows, full_rows, cuts=(), offset=None, zero_rows=None):
        edges = [0, *sorted(cuts), rows]
        self.rows, self.full_rows = rows, full_rows
        self.runs = [(a, b - a) for a, b in zip(edges[:-1], edges[1:]) if b > a]
        self.offset = offset if offset is not None else (lambda j, start: j * rows + start)
        self.zero_rows = zero_rows

    def at(self, ref_2d_plus, j, start, size):
        return pl.ds(pl.multiple_of(self.offset(j, start), 16), size)


def gather_layer(prepped, zeros, places, l, name):
    n_w = len(prepped)
    n_chip = sum(len(p.runs) for p in places) * 3
    n_own = sum(len(p.runs) for p in places) * 2 + sum(2 for p in places if p.zero_rows)

    def plan(me, in_refs, out_refs):
        x, y, c = me
        j = 2 * x + y
        sib = (x, y, 1 - c)
        from_chips, own, forwards = [], [], []
        for i, o, p in zip(in_refs[:n_w], out_refs, places):
            for start, size in p.runs:
                for dx, dy in _CHIP_FLIPS:
                    px, py = x ^ dx, y ^ dy
                    jp = 2 * px + py
                    mine_there = o.at[c, p.at(o, j, start, size)]
                    theirs_here = o.at[c, p.at(o, jp, start, size)]
                    n = len(from_chips)
                    from_chips.append((i.at[l, c, pl.ds(start, size)], mine_there, (px, py, c), theirs_here, None))
                    forwards.append((theirs_here, theirs_here, sib, o.at[1 - c, p.at(o, jp, start, size)], n))
                for h in range(2):
                    place = o.at[h, p.at(o, j, start, size)]
                    own.append((i.at[l, h, pl.ds(start, size)], place, sib, place, None))
            if p.zero_rows:
                for h in range(2):
                    place = o.at[h, pl.ds(p.zero_rows[0], p.zero_rows[1])]
                    own.append((in_refs[n_w], place, sib, place, None))
        return [], from_chips + own + forwards

    shapes = [jax.ShapeDtypeStruct((2, p.full_rows, w.shape[3]), w.dtype) for w, p in zip(prepped, places)]
    ex = Exchange([*prepped, zeros], shapes, 0, 2 * n_chip + n_own, plan)
    return ex if name is None else run_exchange(name, ex)


def sibling_scatter(arrays, name):
    def plan(me, in_refs, out_refs):
        x, y, c = me
        return [], [(i.at[1 - c], o, (x, y, 1 - c), o, None) for i, o in zip(in_refs, out_refs)]

    shapes = [jax.ShapeDtypeStruct(a.shape[1:], a.dtype) for a in arrays]
    return _comm(name, arrays, shapes, 0, len(arrays), plan)


def sibling_swap(arrays, name):
    def plan(me, in_refs, out_refs):
        x, y, c = me
        return [], [(i, o, (x, y, 1 - c), o, None) for i, o in zip(in_refs, out_refs)]

    shapes = [jax.ShapeDtypeStruct(a.shape, a.dtype) for a in arrays]
    return _comm(name, arrays, shapes, 0, len(arrays), plan)


def scatter_to_chips(arrays, places, name):
    def plan(me, in_refs, out_refs):
        x, y, c = me
        j = 2 * x + y
        local_plan, remote_plan = [], []
        for i, o, p in zip(in_refs, out_refs, places):
            for start, size in p.runs:
                local_plan.append((i.at[p.at(i, j, start, size)], o.at[3, pl.ds(start, size)]))
                for k, (dx, dy) in enumerate(_CHIP_FLIPS):
                    px, py = x ^ dx, y ^ dy
                    landing = o.at[k, pl.ds(start, size)]
                    remote_plan.append((i.at[p.at(i, 2 * px + py, start, size)], landing, (px, py, c), landing, None))
        return local_plan, remote_plan

    n_runs = sum(len(p.runs) for p in places)
    shapes = [jax.ShapeDtypeStruct((4, p.rows, a.shape[1]), a.dtype) for a, p in zip(arrays, places)]
    ex = Exchange(arrays, shapes, n_runs, 3 * n_runs, plan)
    return ex if name is None else run_exchange(name, ex)


def allgather8(block, name):
    def plan(me, in_refs, out_refs):
        x, y, c = me
        (i,), (o,) = in_refs, out_refs
        mine = 4 * x + 2 * y + c
        remote_plan = []
        for flip in range(1, 8):
            px, py, pc = x ^ (flip >> 2), y ^ ((flip >> 1) & 1), c ^ (flip & 1)
            remote_plan.append((i, o.at[mine], (px, py, pc), o.at[4 * px + 2 * py + pc], None))
        return [(i, o.at[mine])], remote_plan

    return _comm(name, [block], [jax.ShapeDtypeStruct((8, *block.shape), block.dtype)], 1, 7, plan)[0]


def transpose_split(a, out_dtype, name):
    L, R, C = a.shape
    rh = R // 2
    rt = _pick(rh, 256)
    n = rh // rt

    def body(x_ref, o_ref):
        o_ref[0, 0] = x_ref[0].T.astype(o_ref.dtype)

    return _pcall(
        body,
        name=name,
        grid=(L, 2, n),
        in_specs=[pl.BlockSpec((1, rt, C), lambda l, h, i: (l, h * n + i, 0))],
        out_specs=pl.BlockSpec((1, 1, C, rt), lambda l, h, i: (l, h, 0, i)),
        out_shape=jax.ShapeDtypeStruct((L, 2, C, rh), out_dtype),
        compiler_params=_params(3),
    )(a)


def cast_split(a, out_dtype, name):
    L, R, C = a.shape

    def body(x_ref, o_ref):
        o_ref[0, 0] = x_ref[0].astype(o_ref.dtype)

    return _pcall(
        body,
        name=name,
        grid=(L, 2),
        in_specs=[pl.BlockSpec((1, R, C // 2), lambda l, h: (l, 0, h))],
        out_specs=pl.BlockSpec((1, 1, R, C // 2), lambda l, h: (l, h, 0, 0)),
        out_shape=jax.ShapeDtypeStruct((L, 2, R, C // 2), out_dtype),
        compiler_params=_params(2),
    )(a)


def untranspose_halves(mine, theirs, my_c, stack, l, n_layers, name):
    C, rh = mine.shape
    n_in = 2 if stack is None else 3
    rt = _pick(rh, 256)
    n = rh // rt

    def body(c_ref, a_ref, b_ref, *rest):
        o_ref = rest[-1]
        h = pl.program_id(0)
        o_ref[0] = jnp.where(h == c_ref[0], a_ref[...], b_ref[...]).T

    grid_spec = pltpu.PrefetchScalarGridSpec(
        num_scalar_prefetch=1,
        grid=(2, n),
        in_specs=[pl.BlockSpec((C, rt), lambda h, i, c: (0, i))] * 2
        + ([] if stack is None else [pl.BlockSpec(memory_space=pl.ANY)]),
        out_specs=pl.BlockSpec((1, rt, C), lambda h, i, c: (l, h * n + i, 0)),
    )
    return _pcall(
        body,
        name=name,
        grid_spec=grid_spec,
        out_shape=jax.ShapeDtypeStruct((n_layers, 2 * rh, C), F32),
        input_output_aliases={} if stack is None else {n_in: 0},
        compiler_params=_params(2),
    )(my_c.reshape(1), mine, theirs, *([] if stack is None else [stack]))


def merge_halves(mine, theirs, my_c, stack, l, n_layers, name):
    R, ch = mine.shape
    n_in = 2 if stack is None else 3

    def body(c_ref, a_ref, b_ref, *rest):
        o_ref = rest[-1]
        o_ref[0] = jnp.where(pl.program_id(0) == c_ref[0], a_ref[...], b_ref[...])

    grid_spec = pltpu.PrefetchScalarGridSpec(
        num_scalar_prefetch=1,
        grid=(2,),
        in_specs=[pl.BlockSpec((R, ch), lambda h, c: (0, 0))] * 2 + ([] if stack is None else [pl.BlockSpec(memory_space=pl.ANY)]),
        out_specs=pl.BlockSpec((1, R, ch), lambda h, c: (l, 0, h)),
    )
    return _pcall(
        body,
        name=name,
        grid_spec=grid_spec,
        out_shape=jax.ShapeDtypeStruct((n_layers, R, 2 * ch), F32),
        input_output_aliases={} if stack is None else {n_in: 0},
        compiler_params=_params(1),
    )(my_c.reshape(1), mine, theirs, *([] if stack is None else [stack]))


def add_pair(g, r, my_c, name):
    _, rows, w = g.shape
    tm = _pick(rows, max(16, min(1024, (1 << 20) // w // 16 * 16)), 16)

    def body(c_ref, g_ref, r_ref, o_ref):
        o_ref[...] = (g_ref[0].astype(F32) + r_ref[...].astype(F32)).astype(o_ref.dtype)

    grid_spec = pltpu.PrefetchScalarGridSpec(
        num_scalar_prefetch=1,
        grid=(rows // tm,),
        in_specs=[pl.BlockSpec((1, tm, w), lambda i, c: (c[0], i, 0)), pl.BlockSpec((tm, w), lambda i, c: (i, 0))],
        out_specs=pl.BlockSpec((tm, w), lambda i, c: (i, 0)),
    )
    return _pcall(
        body, name=name, grid_spec=grid_spec, out_shape=jax.ShapeDtypeStruct((rows, w), BF16), compiler_params=_params(1)
    )(my_c.reshape(1), g, r)


class Dims:
    def __init__(self, S, D, L, ret_heads, mla_heads, q_rank, kv_rank):
        self.S, self.D, self.L, self.HR, self.HM, self.QR, self.KR = S, D, L, ret_heads, mla_heads, q_rank, kv_rank
        self.RQ, self.RV, self.MV = ret_heads * RET_DK, ret_heads * RET_DV, mla_heads * MLA_DV
        self.D_IN = 2 * self.RQ + 2 * self.RV + q_rank + kv_rank + MLA_ROPE + self.MV + 2 * D
        self.lo = 2 * self.RQ + 2 * self.RV
        self.mid = q_rank + kv_rank + MLA_ROPE
        self.DP = self.D_IN + LANE - MLA_ROPE
        self.o_rq, self.o_rk, self.o_rv, self.o_rg = 0, self.RQ, 2 * self.RQ, 2 * self.RQ + self.RV
        self.o_mg = self.lo
        self.o_bga = self.o_mg + self.MV
        self.o_bgb = self.o_bga + D
        self.o_cq = self.o_bgb + D
        self.o_ckv = self.o_cq + q_rank
        self.o_kr = self.o_ckv + kv_rank

    def pad_uq_rows(self, w):
        heads = w.shape[-2] // (MLA_NOPE + MLA_ROPE)
        w = w.reshape(*w.shape[:-2], heads, MLA_NOPE + MLA_ROPE, w.shape[-1])
        w = jnp.pad(w, [(0, 0)] * (w.ndim - 2) + [(0, MLA_QW - MLA_NOPE - MLA_ROPE), (0, 0)])
        return w.reshape(*w.shape[:-3], heads * MLA_QW, w.shape[-1])

    def unpad_uq_rows(self, w):
        heads = w.shape[-2] // MLA_QW
        w = w.reshape(*w.shape[:-2], heads, MLA_QW, w.shape[-1])[..., : MLA_NOPE + MLA_ROPE, :]
        return w.reshape(*w.shape[:-3], heads * (MLA_NOPE + MLA_ROPE), w.shape[-1])

    def placements(self):
        rows_in = self.D_IN // 4
        lo, mid, n_hi = self.lo, self.mid, self.D_IN - self.lo - self.mid
        cuts = {b % rows_in for b in (lo, lo + mid)} - {0}

        def offset_in(j, start):
            g = j * rows_in + start
            return jnp.where(g < lo, g, jnp.where(g < lo + mid, g + n_hi, g - mid))

        return [
            Placement(rows_in, self.DP, cuts, offset_in, zero_rows=(self.D_IN, self.DP - self.D_IN)),
            Placement(self.HM * MLA_QW // 4, self.HM * MLA_QW),
            Placement(self.HM * (MLA_NOPE + MLA_DV) // 4, self.HM * (MLA_NOPE + MLA_DV)),
            Placement(self.RV // 4, self.RV),
            Placement(self.MV // 4, self.MV),
            Placement(self.D // 4, self.D),
        ]


def _rope_tables(positions, dim, width):
    inv = 1.0 / (ROPE_BASE ** (jnp.arange(0, dim, 2, dtype=F32) / dim))
    ang = positions.astype(F32)[:, None] * inv
    cos, sin = jnp.cos(ang), jnp.sin(ang)
    pad = jnp.zeros((positions.shape[0], width - dim), F32)
    return jnp.concatenate([cos, cos, pad], axis=1), jnp.concatenate([-sin, sin, pad], axis=1)


def _tiles(x):
    return [x[:, t * LANE : (t + 1) * LANE] for t in range(x.shape[1] // LANE)]


def _cat(parts):
    return parts[0] if len(parts) == 1 else jnp.concatenate(parts, axis=1)


def _rms(x, eps=EPS):
    return lax.rsqrt(jnp.mean(x * x, axis=1, keepdims=True) + eps)


def layer_fwd(dm, l, x, shift, scale, gate, g_norm, g_cq, g_ckv, w, tabs, host=None):
    cos_r, sin_r, cos_m, sin_m = tabs
    nm = lambda s: f"l{l}_{s}"

    def f_norm(x, g, scale, shift):
        return [x * _rms(x) * g * (1.0 + scale) + shift], []

    (h,) = rowwise(f_norm, [x], [g_norm, scale, shift], [(dm.D, BF16)], tm=256, name=nm("norm"))
    proj = matmul(h, w["in"], tb=True, b_split=True, name=nm("mm_in"), tn=1920)

    def f_rope_ret(rq, rk, cos, sin):
        rq, rk = rq.astype(F32), rk.astype(F32)
        q = _cat([_rope_tile(t, cos, sin, RET_DK // 2) for t in _tiles(rq)])
        k = _cat([_rope_tile(t, cos, sin, RET_DK // 2) * (RET_DK**-0.5) for t in _tiles(rk)])
        return [q, k], []

    rq, rk = rowwise(
        f_rope_ret,
        [win(proj, dm.o_rq, dm.RQ), win(proj, dm.o_rk, dm.RQ), cos_r, sin_r],
        [],
        [(dm.RQ, BF16)] * 2,
        tm=512,
        name=nm("rope_ret"),
    )
    o_ret, a_ret, states = ret_fwd(rq, rk, proj, dm.o_rv, dm.o_rg, heads=dm.HR, tb=RET_BLOCK, name=nm("ret_fwd"))

    def f_prep(cq, ckv, kr, cos, sin, g_cq, g_ckv):
        cq, ckv, kr = cq.astype(F32), ckv.astype(F32), kr.astype(F32)
        return [cq * _rms(cq) * g_cq, ckv * _rms(ckv) * g_ckv, _rope_tile(kr, cos, sin, MLA_ROPE // 2)], []

    cqn, ckvn, krr = rowwise(
        f_prep,
        [win(proj, dm.o_cq, dm.QR), win(proj, dm.o_ckv, dm.KR), win(proj, dm.o_kr, LANE), cos_m, sin_m],
        [g_cq, g_ckv],
        [(dm.QR, BF16), (dm.KR, BF16), (LANE, BF16)],
        tm=512,
        name=nm("mla_prep"),
    )
    q_raw = matmul(cqn, w["uq"], tb=True, b_split=True, name=nm("mm_uq"))
    kv = matmul(ckvn, w["ukv"], tb=True, b_split=True, name=nm("mm_ukv"))

    def f_rope_q(q, cos, sin):
        t = _tiles(q.astype(F32))
        rot = [t[n] if n % 2 == 0 else _rope_tile(t[n], cos, sin, MLA_ROPE // 2) for n in range(len(t))]
        return [_cat([r * (MLA_SCALE * LOG2E) for r in rot])], []

    (q,) = rowwise(f_rope_q, [q_raw, cos_m, sin_m], [], [(dm.HM * MLA_QW, BF16)], tm=256, name=nm("rope_q"))
    o_mla, a_mla, lse, *hosted = attn_fwd(
        q, kv, krr, proj, dm.o_mg, heads=dm.HM, tq=ATTN_BLOCK, name=nm("attn_fwd"), host=host
    )

    y_ret = matmul(a_ret, w["ret"], b_split=True, out_dtype=F32, name=nm("mm_ret"))
    y_mla = matmul(a_mla, w["mla"], b_split=True, out_dtype=F32, name=nm("mm_mla"))

    def f_merge(y_ret, y_mla, bga, bgb):
        return [_sigmoid(bga.astype(F32)) * y_ret + _sigmoid(bgb.astype(F32)) * y_mla], []

    (merged,) = rowwise(
        f_merge, [y_ret, y_mla, win(proj, dm.o_bga, dm.D), win(proj, dm.o_bgb, dm.D)], [], [(dm.D, BF16)], tm=256, name=nm("merge")
    )
    out = matmul(merged, w["out"], b_split=True, out_dtype=F32, name=nm("mm_out"))

    def f_resid(x, out, gate):
        return [x + gate * out], []

    (x_new,) = rowwise(f_resid, [x, out], [gate], [(dm.D, F32)], tm=256, name=nm("resid"))
    saved = dict(
        x=x, h=h, proj=proj, rq=rq, rk=rk, o_ret=o_ret, a_ret=a_ret, states=states, cqn=cqn, ckvn=ckvn, krr=krr, q=q, kv=kv,
        o_mla=o_mla, a_mla=a_mla, lse=lse, y_ret=y_ret, y_mla=y_mla, merged=merged, out=out,
    )
    return x_new, saved, hosted


def layer_bwd(dm, l, dx_out, sv, shift, scale, gate, g_norm, g_cq, g_ckv, w, tabs, host=None):
    cos_r, sin_r, cos_m, sin_m = tabs
    nm = lambda s: f"l{l}_{s}"
    proj = sv["proj"]

    def b_resid(dx, out, gate):
        return [dx * gate], [_sum0(dx * out)]

    dout, d_gate = rowwise(b_resid, [dx_out, sv["out"]], [gate], [(dm.D, BF16)], [dm.D], tm=256, name=nm("resid_bwd"))
    dmerged = matmul(dout, w["out"], tb=True, b_split=True, name=nm("mm_dmerged"))
    dw_out = matmul(sv["merged"], dout, ta=True, out_split=True, tn=2048, name=nm("mm_dw_out"))

    def b_merge(dmg, y_ret, y_mla, bga, bgb):
        dmg = dmg.astype(F32)
        ga, gb = _sigmoid(bga.astype(F32)), _sigmoid(bgb.astype(F32))
        return [dmg * ga, dmg * gb, dmg * y_ret * ga * (1.0 - ga), dmg * y_mla * gb * (1.0 - gb)], []

    dy_ret, dy_mla, d_bga, d_bgb = rowwise(
        b_merge,
        [dmerged, sv["y_ret"], sv["y_mla"], win(proj, dm.o_bga, dm.D), win(proj, dm.o_bgb, dm.D)],
        [],
        [(dm.D, BF16)] * 4,
        tm=256,
        name=nm("merge_bwd"),
    )
    da_ret = matmul(dy_ret, w["ret"], tb=True, b_split=True, name=nm("mm_da_ret"))
    dw_ret = matmul(sv["a_ret"], dy_ret, ta=True, out_split=True, tn=2048, name=nm("mm_dw_ret"))
    da_mla = matmul(dy_mla, w["mla"], tb=True, b_split=True, name=nm("mm_da_mla"))
    dw_mla = matmul(sv["a_mla"], dy_mla, ta=True, out_split=True, tn=2048, name=nm("mm_dw_mla"))

    def b_ret_gate(da, rg, o):
        da, rg = da.astype(F32), rg.astype(F32)
        do_parts, drg_parts = [], []
        for hh in range(dm.HR):
            sl = slice(hh * RET_DV, (hh + 1) * RET_DV)
            oh, dah, rgh = o[:, sl], da[:, sl], rg[:, sl]
            mu = jnp.mean(oh, axis=1, keepdims=True)
            d = oh - mu
            r = lax.rsqrt(jnp.mean(d * d, axis=1, keepdims=True) + EPS)
            n = d * r
            dn = dah * _silu(rgh)
            drg_parts.append(dah * n * _dsilu(rgh))
            do_parts.append(r * (dn - jnp.mean(dn, axis=1, keepdims=True) - n * jnp.mean(dn * n, axis=1, keepdims=True)))
        return [_cat(do_parts), _cat(drg_parts)], []

    do_ret, d_rg = rowwise(
        b_ret_gate, [da_ret, win(proj, dm.o_rg, dm.RV), sv["o_ret"]], [], [(dm.RV, BF16)] * 2, tm=256, name=nm("ret_gate_bwd")
    )
    dq_rot, dk_rot, d_rv = ret_bwd(sv["rq"], sv["rk"], proj, dm.o_rv, sv["states"], do_ret, heads=dm.HR, tb=RET_BLOCK, name=nm("ret_bwd"))

    def b_rope_ret(dq, dk, cos, sin):
        dq, dk = dq.astype(F32), dk.astype(F32)
        q = _cat([_rope_tile(t, cos, sin, RET_DK // 2, inverse=True) for t in _tiles(dq)])
        k = _cat([_rope_tile(t, cos, sin, RET_DK // 2, inverse=True) * (RET_DK**-0.5) for t in _tiles(dk)])
        return [q, k], []

    d_rq, d_rk = rowwise(b_rope_ret, [dq_rot, dk_rot, cos_r, sin_r], [], [(dm.RQ, BF16)] * 2, tm=512, name=nm("rope_ret_bwd"))

    def b_mla_gate(da, mg, o):
        da, mg = da.astype(F32), mg.astype(F32)
        return [da * _silu(mg), da * o * _dsilu(mg)], []

    do_mla, d_mg = rowwise(
        b_mla_gate, [da_mla, win(proj, dm.o_mg, dm.MV), sv["o_mla"]], [], [(dm.MV, BF16)] * 2, tm=256, name=nm("mla_gate_bwd")
    )
    dq_att, dkv, dkr_heads, *hosted = attn_bwd(
        sv["q"], sv["kv"], sv["krr"], do_mla, sv["o_mla"], sv["lse"], heads=dm.HM, tq=ATTN_BLOCK, name=nm("attn_bwd"), host=host
    )

    def b_rope_q(dq, cos, sin):
        t = _tiles(dq.astype(F32))
        return [_cat([t[n] if n % 2 == 0 else _rope_tile(t[n], cos, sin, MLA_ROPE // 2, inverse=True) for n in range(len(t))])], []

    (dq_raw,) = rowwise(b_rope_q, [dq_att, cos_m, sin_m], [], [(dm.HM * MLA_QW, BF16)], tm=256, name=nm("rope_q_bwd"))
    dcqn = matmul(dq_raw, w["uq"], b_split=True, name=nm("mm_dcqn"))
    dw_uq = matmul(dq_raw, sv["cqn"], ta=True, out_split=True, tn=2048, name=nm("mm_dw_uq"))
    dckvn = matmul(dkv, w["ukv"], b_split=True, name=nm("mm_dckvn"))
    dw_ukv = matmul(dkv, sv["ckvn"], ta=True, out_split=True, tn=2048, name=nm("mm_dw_ukv"))

    def b_prep(dcqn, dckvn, cq, ckv, cos, sin, dkr_h, g_cq, g_ckv):
        outs, accs = [], []
        for dn, z, g in ((dcqn, cq, g_cq), (dckvn, ckv, g_ckv)):
            dn, z = dn.astype(F32), z.astype(F32)
            n = z * _rms(z)
            dng = dn * g
            outs.append(_rms(z) * (dng - n * jnp.mean(dng * n, axis=1, keepdims=True)))
            accs.append(_sum0(dn * n))
        dkr = dkr_h[0]
        for hh in range(1, dm.HM):
            dkr = dkr + dkr_h[hh]
        outs.append(_rope_tile(dkr, cos, sin, MLA_ROPE // 2, inverse=True))
        return outs, accs

    d_cq, d_ckv, d_kr, dg_cq, dg_ckv = rowwise(
        b_prep,
        [dcqn, dckvn, win(proj, dm.o_cq, dm.QR), win(proj, dm.o_ckv, dm.KR), cos_m, sin_m, dkr_heads],
        [g_cq, g_ckv],
        [(dm.QR, BF16), (dm.KR, BF16), (LANE, BF16)],
        [dm.QR, dm.KR],
        tm=256,
        name=nm("mla_prep_bwd"),
    )

    dproj = jnp.concatenate([d_rq, d_rk, d_rv, d_rg, d_mg, d_bga, d_bgb, d_cq, d_ckv, d_kr], axis=1)
    dh = matmul(dproj, w["in"], b_split=True, out_dtype=F32, name=nm("mm_dh"), tk=1920)
    dw_in = matmul(dproj, sv["h"], ta=True, out_split=True, tn=2048, name=nm("mm_dw_in"))

    def b_norm(dh, x, dx_res, g, scale):
        r = _rms(x)
        xn = x * r
        dxn = dh * g * (1.0 + scale)
        dx = dx_res + r * (dxn - xn * jnp.mean(dxn * xn, axis=1, keepdims=True))
        return [dx], [_sum0(dh), _sum0(dh * xn * g), _sum0(dh * (1.0 + scale) * xn)]

    dx, d_shift, d_scale, dg_norm = rowwise(
        b_norm, [dh, sv["x"], dx_out], [g_norm, scale], [(dm.D, F32)], [dm.D] * 3, tm=256, name=nm("norm_bwd")
    )
    dws = dict(w_in=dw_in, w_uq=dw_uq, w_ukv=dw_ukv, w_ret_proj=dw_ret, w_mla_proj=dw_mla, w_out=dw_out)
    dvec = dict(mod=jnp.concatenate([d_shift, d_scale, d_gate], axis=1), g_norm=dg_norm, g_cq=dg_cq, g_ckv=dg_ckv)
    return dx, dws, dvec, hosted


def adamw(w, g, m, v, name):
    shape = w.shape
    cols = shape[-1]
    view = lambda a: a.reshape(-1, cols)

    def f(w, g, m, v):
        m = ADAM_B1 * m + (1.0 - ADAM_B1) * g
        v = ADAM_B2 * v + (1.0 - ADAM_B2) * (g * g)
        m_hat = m / (1.0 - ADAM_B1**ADAM_STEP)
        v_hat = v / (1.0 - ADAM_B2**ADAM_STEP)
        delta = -ADAM_LR * (m_hat / (jnp.sqrt(v_hat) + ADAM_EPS) + ADAM_WD * w)
        return [delta, m, v], []

    tm = max(8, min(512, (400_000 // cols) // 8 * 8))
    delta, m, v = rowwise(f, [view(w), view(g), view(m), view(v)], [], [(cols, F32)] * 3, tm=tm, name=name)
    return delta.reshape(shape), m.reshape(shape), v.reshape(shape)


def _add_rows(fn, rows, cols, dtype, name):
    tm = max(8, min(512, (400_000 // cols) // 8 * 8))
    return rowwise(lambda *a: ([fn(*a)], []), rows, [], [(cols, dtype)], tm=tm, name=name)[0]


BIG = ("w_in", "w_uq", "w_ukv", "w_ret_proj", "w_mla_proj", "w_out")
COL_SHARDED = ("w_in", "w_uq", "w_ukv")


def kernel(x, c, positions, w_mod, b_mod, g_norm, w_in, g_cq, g_ckv, w_uq, w_ukv, w_ret_proj, w_mla_proj, w_out, g_final, loss_target, m_w_mod, m_b_mod, m_g_norm, m_w_in, m_g_cq, m_g_ckv, m_w_uq, m_w_ukv, m_w_ret_proj, m_w_mla_proj, m_w_out, m_g_final, v_w_mod, v_b_mod, v_g_norm, v_w_in, v_g_cq, v_g_ckv, v_w_uq, v_w_ukv, v_w_ret_proj, v_w_mla_proj, v_w_out, v_g_final):
    weights = dict(w_mod=w_mod, b_mod=b_mod, g_norm=g_norm, w_in=w_in, g_cq=g_cq, g_ckv=g_ckv, w_uq=w_uq, w_ukv=w_ukv,
                   w_ret_proj=w_ret_proj, w_mla_proj=w_mla_proj, w_out=w_out, g_final=g_final)
    m_in = dict(w_mod=m_w_mod, b_mod=m_b_mod, g_norm=m_g_norm, w_in=m_w_in, g_cq=m_g_cq, g_ckv=m_g_ckv, w_uq=m_w_uq,
                w_ukv=m_w_ukv, w_ret_proj=m_w_ret_proj, w_mla_proj=m_w_mla_proj, w_out=m_w_out, g_final=m_g_final)
    v_in = dict(w_mod=v_w_mod, b_mod=v_b_mod, g_norm=v_g_norm, w_in=v_w_in, g_cq=v_g_cq, g_ckv=v_g_ckv, w_uq=v_w_uq,
                w_ukv=v_w_ukv, w_ret_proj=v_w_ret_proj, w_mla_proj=v_w_mla_proj, w_out=v_w_out, g_final=v_g_final)
    order = ("w_mod", "b_mod", "g_norm", "w_in", "g_cq", "g_ckv", "w_uq", "w_ukv", "w_ret_proj", "w_mla_proj", "w_out", "g_final")

    x = x[0]
    target = loss_target[0]
    S, D = x.shape
    L = w_mod.shape[0]
    dm = Dims(S, D, L, w_ret_proj.shape[1] * 4 // RET_DV, w_mla_proj.shape[1] * 4 // MLA_DV, g_cq.shape[1], g_ckv.shape[1])
    my_x, my_y, my_c = _me()
    my_chip = 2 * my_x + my_y
    my_dev = 2 * my_chip + my_c
    C3 = w_mod.shape[2]

    prepped = [
        transpose_split(w_in, BF16, "prep_w_in"),
        dm.pad_uq_rows(transpose_split(w_uq, BF16, "prep_w_uq")),
        transpose_split(w_ukv, BF16, "prep_w_ukv"),
        cast_split(w_ret_proj, BF16, "prep_w_ret"),
        cast_split(w_mla_proj, BF16, "prep_w_mla"),
        cast_split(w_out, BF16, "prep_w_out"),
    ]
    places = dm.placements()
    zero_rows = jnp.zeros((places[0].zero_rows[1], D // 2), BF16)
    w_keys = ("in", "uq", "ukv", "ret", "mla", "out")
    layer_w = [dict(zip(w_keys, gather_layer(prepped, zero_rows, places, 0, "l0_gather_w")))]

    c_all = allgather8(c, "gather_c").reshape(8, D)
    (c_act,) = rowwise(lambda z: ([_silu(z)], []), [c_all], [], [(D, BF16)], tm=8, name="silu_c")
    mod_part = jnp.stack([matmul(c_act, w_mod[l], out_dtype=F32, name=f"l{l}_mm_mod", tn=C3) for l in range(L)])
    mod_all = allgather8(mod_part, "gather_mod")
    mod_all = mod_all.reshape(4, 2, L, 8, C3)[:, 0].transpose(1, 2, 0, 3).reshape(L, 8, 3 * D) + b_mod[:, None, :]
    mod = lax.dynamic_index_in_dim(mod_all, my_dev, axis=1, keepdims=False)

    pos = positions[0]
    tabs = (*_rope_tables(pos, RET_DK, LANE), *_rope_tables(pos, MLA_ROPE, LANE))

    def vecs(l):
        return (mod[l : l + 1, :D], mod[l : l + 1, D : 2 * D], mod[l : l + 1, 2 * D :],
                g_norm[l : l + 1], g_cq[l : l + 1], g_ckv[l : l + 1])

    saved = []
    for l in range(L):
        host = gather_layer(prepped, zero_rows, places, l + 1, None) if l + 1 < L else None
        x, sv, hosted = layer_fwd(dm, l, x, *vecs(l), layer_w[l], tabs, host=host)
        saved.append(sv)
        if host:
            layer_w.append(dict(zip(w_keys, hosted)))

    def f_loss(x, t, g):
        xn = x * _rms(x)
        err = xn * g - t
        dy = err * (1.0 / D)
        dxn = dy * g
        dx = _rms(x) * (dxn - xn * jnp.mean(dxn * xn, axis=1, keepdims=True))
        part = jnp.sum(jnp.sum(err * err, axis=1, keepdims=True), axis=0, keepdims=True) * (0.5 / D)
        return [dx], [jnp.broadcast_to(part, (1, LANE)), _sum0(dy * xn)]

    dx, loss_part, dg_final = rowwise(f_loss, [x, target], [g_final.reshape(1, D)], [(D, F32)], [LANE, D], tm=256, name="loss_head")

    stacks = {n: None for n in BIG}
    dvec = {n: [None] * L for n in ("mod", "g_norm", "g_cq", "g_ckv")}

    def finish_grads(l, from_chips):
        mine = [
            _add_rows(lambda a: ((a[3].astype(F32) + a[0].astype(F32)) + a[1].astype(F32)) + a[2].astype(F32), [r], r.shape[-1],
                      F32, f"l{l}_rs_add4_{n}")
            for n, r in zip(BIG, from_chips)
        ]
        theirs = sibling_swap(mine, f"l{l}_rs_share")
        for n, a, b in zip(BIG, mine, theirs):
            if n == "w_uq":
                a, b = dm.unpad_uq_rows(a), dm.unpad_uq_rows(b)
            finish = untranspose_halves if n in COL_SHARDED else merge_halves
            stacks[n] = finish(a, b, my_c, stacks[n], l, L, f"l{l}_grad_{n}")

    pair = None
    for l in reversed(range(L)):
        host = scatter_to_chips(pair, places, None) if pair is not None else None
        dx, dw_l, dv_l, hosted = layer_bwd(dm, l, dx, saved[l], *vecs(l), layer_w[l], tabs, host=host)
        if host:
            finish_grads(l + 1, hosted)
        for n in dvec:
            dvec[n][l] = dv_l[n]
        partial = [dw_l[n] for n in BIG]
        from_sibling = sibling_scatter(partial, f"l{l}_rs_cores")
        pair = [add_pair(g, r, my_c, f"l{l}_rs_add2_{n}") for n, g, r in zip(BIG, partial, from_sibling)]
    finish_grads(0, scatter_to_chips(pair, places, "l0_rs_chips"))
    grad_x = dx[None]

    pieces = [loss_part] + [jnp.concatenate(dvec[n], axis=1) for n in ("mod", "g_norm", "g_cq", "g_ckv")] + [dg_final]
    widths = [p.shape[1] for p in pieces]
    small_all = allgather8(jnp.concatenate(pieces, axis=1), "gather_small").reshape(8, sum(widths))

    def sum8_body(a_ref, o_ref):
        acc = a_ref[0:1, :]
        for d in range(1, 8):
            acc = acc + a_ref[d : d + 1, :]
        o_ref[...] = acc

    small = _pcall(sum8_body, name="sum_small", out_shape=jax.ShapeDtypeStruct((1, sum(widths)), F32))(small_all)
    offs = np.cumsum([0] + widths)
    loss = small[0, 0]
    g_small = {
        "b_mod": small[0, offs[1] : offs[2]].reshape(L, 3 * D),
        "g_norm": small[0, offs[2] : offs[3]].reshape(L, D),
        "g_cq": small[0, offs[3] : offs[4]].reshape(L, dm.QR),
        "g_ckv": small[0, offs[4] : offs[5]].reshape(L, dm.KR),
        "g_final": small[0, offs[5] : offs[6]],
    }

    dmod_all = small_all[:, offs[1] : offs[2]].reshape(8, L, 3 * D)
    dmod_mine = lax.dynamic_slice_in_dim(dmod_all, my_chip * C3, C3, axis=2)
    pad8 = lambda a: jnp.pad(a, ((0, LANE - 8), (0, 0)))
    grads = dict(g_small)
    grads["w_mod"] = jnp.stack(
        [matmul(pad8(c_act), pad8(dmod_mine[:, l]), ta=True, out_dtype=F32, name=f"l{l}_mm_dw_mod", tn=C3) for l in range(L)]
    )

    for n in BIG:
        grads[n] = stacks[n]

    deltas, new_m, new_v = {}, {}, {}
    for n in order:
        wv, gv, mv, vv = weights[n], grads[n], m_in[n], v_in[n]
        if wv.ndim == 1:
            wv, gv, mv, vv = (a.reshape(1, -1) for a in (wv, gv, mv, vv))
        d_, m_, v_ = adamw(wv, gv, mv, vv, f"adamw_{n}")
        deltas[n], new_m[n], new_v[n] = (a.reshape(weights[n].shape) for a in (d_, m_, v_))
        grads[n] = grads[n].reshape(weights[n].shape)

    return (loss, grad_x, *[grads[n] for n in order], *[deltas[n] for n in order], *[new_m[n] for n in order],
            *[new_v[n] for n in order])
```

```python
import functools

import jax
import jax.numpy as jnp
import numpy as np
from jax import lax
from jax.experimental import pallas as pl
from jax.experimental.pallas import tpu as pltpu

F32 = jnp.float32
BF16 = jnp.bfloat16
MESH = pl.DeviceIdType.MESH

VMEM_LIMIT_BYTES = 52 * 1024 * 1024
LANE = 128

CHUNK = 64
EPS = 1e-6
NEG_INF = -1e30
ROPE_BASE = 10000.0
RET_DK = 128
RET_DV = 256
MLA_NOPE = 128
MLA_ROPE = 64
MLA_DV = 128
MLA_QW = 256
MLA_SCALE = float((MLA_NOPE + MLA_ROPE) ** -0.5)
LOG2E = float(np.log2(np.e))
LN2 = float(np.log(2.0))
ATTN_BLOCK = 512
ATTN_HEADS_PER_STEP = 2
RET_BLOCK = 256
RET_HEADS_PER_STEP = 4

ADAM_LR = 0.001
ADAM_B1 = 0.9
ADAM_B2 = 0.999
ADAM_EPS = 1e-08
ADAM_WD = 0.01
ADAM_STEP = 10


def _pcall(body, **kw):
    return pl.pallas_call(body, **kw)


def _params(n_grid):
    return pltpu.CompilerParams(dimension_semantics=("arbitrary",) * n_grid, vmem_limit_bytes=VMEM_LIMIT_BYTES)


def _pick(dim, target, mult=LANE):
    if dim <= target:
        return dim
    best = None
    for t in range(mult, target + 1, mult):
        if dim % t == 0:
            best = t
    assert best is not None, (dim, target, mult)
    return best


def matmul(a, b, *, ta=False, tb=False, b_split=False, out_split=False, out_dtype=BF16, name, tm=512, tn=1024, tk=2048):
    (M, K) = (a.shape[1], a.shape[0]) if ta else a.shape
    b_rows, b_cols = (b.shape[1], 2 * b.shape[2]) if b_split else b.shape
    (K2, N) = (b_cols, b_rows) if tb else (b_rows, b_cols)
    assert K == K2, (a.shape, b.shape, ta, tb)
    whole_k = b_split and tb and tk >= K and not ta
    whole_n = out_split and tn >= N and not (b_split and not tb)
    whole_bn = b_split and not tb and tn >= N and not out_split
    n_cap = N if (whole_n or whole_bn) else (N // 2 if (out_split or (b_split and not tb)) else N)
    k_cap = K if whole_k else (K // 2 if (b_split and tb) else K)
    tm, tn, tk = _pick(M, tm, 8 if M < LANE else LANE), _pick(n_cap, min(tn, n_cap)), _pick(k_cap, min(tk, k_cap))
    nk = K // tk
    njh, nkh = max((N // 2) // tn, 1), max((K // 2) // tk, 1)
    dn = (((0 if ta else 1,), (1 if tb else 0,)), ((), ()))

    def body(a_ref, b_ref, o_ref, *scratch):
        if whole_k:
            a_blk = a_ref[...].astype(BF16)
            prod = lax.dot_general(a_blk[:, : K // 2], b_ref[0].astype(BF16), dn, preferred_element_type=F32)
            prod += lax.dot_general(a_blk[:, K // 2 :], b_ref[1].astype(BF16), dn, preferred_element_type=F32)
        elif whole_bn:
            a_blk = a_ref[...].astype(BF16)
            halves = [lax.dot_general(a_blk, b_ref[half].astype(BF16), dn, preferred_element_type=F32) for half in range(2)]
            prod = jnp.concatenate(halves, axis=1)
        else:
            b_blk = b_ref[0] if b_split else b_ref[...]
            prod = lax.dot_general(a_ref[...].astype(BF16), b_blk.astype(BF16), dn, preferred_element_type=F32)

        def store(v):
            if whole_n:
                o_ref[0] = v[:, : N // 2].astype(o_ref.dtype)
                o_ref[1] = v[:, N // 2 :].astype(o_ref.dtype)
            elif out_split:
                o_ref[0] = v.astype(o_ref.dtype)
            else:
                o_ref[...] = v.astype(o_ref.dtype)

        if nk == 1:
            store(prod)
            return
        (acc_ref,) = scratch
        k = pl.program_id(2)

        @pl.when(k == 0)
        def _():
            acc_ref[...] = prod

        @pl.when(k > 0)
        def _():
            acc_ref[...] += prod

        @pl.when(k == nk - 1)
        def _():
            store(acc_ref[...])

    a_spec = pl.BlockSpec((tk, tm), lambda i, j, k: (k, i)) if ta else pl.BlockSpec((tm, tk), lambda i, j, k: (i, k))
    if whole_k:
        b_spec = pl.BlockSpec((2, tn, K // 2), lambda i, j, k: (0, j, 0))
    elif whole_bn:
        b_spec = pl.BlockSpec((2, tk, N // 2), lambda i, j, k: (0, k, 0))
    elif b_split and tb:
        b_spec = pl.BlockSpec((1, tn, tk), lambda i, j, k: (k // nkh, j, k % nkh))
    elif b_split:
        b_spec = pl.BlockSpec((1, tk, tn), lambda i, j, k: (j // njh, k, j % njh))
    elif tb:
        b_spec = pl.BlockSpec((tn, tk), lambda i, j, k: (j, k))
    else:
        b_spec = pl.BlockSpec((tk, tn), lambda i, j, k: (k, j))
    if whole_n:
        out_spec = pl.BlockSpec((2, tm, N // 2), lambda i, j, k: (0, i, 0))
        out_shape = jax.ShapeDtypeStruct((2, M, N // 2), out_dtype)
    elif out_split:
        out_spec = pl.BlockSpec((1, tm, tn), lambda i, j, k: (j // njh, i, j % njh))
        out_shape = jax.ShapeDtypeStruct((2, M, N // 2), out_dtype)
    else:
        out_spec = pl.BlockSpec((tm, tn), lambda i, j, k: (i, j))
        out_shape = jax.ShapeDtypeStruct((M, N), out_dtype)
    return _pcall(
        body,
        name=name,
        grid=(M // tm, N // tn, nk),
        in_specs=[a_spec, b_spec],
        out_specs=out_spec,
        out_shape=out_shape,
        scratch_shapes=[] if nk == 1 else [pltpu.VMEM((tm, tn), F32)],
        compiler_params=_params(3),
    )(a, b)


def win(arr, off, width):
    assert off % width == 0 and off + width <= arr.shape[1], (arr.shape, off, width)
    return (arr, off // width, width)


def rowwise(fn, rows, vecs, outs, accs=(), *, tm, name):
    rows = [r if isinstance(r, tuple) else (r, 0, r.shape[-1]) for r in rows]
    S = rows[0][0].shape[-2]
    tm = _pick(S, tm, 8)
    n_rows, n_vecs, n_outs = len(rows), len(vecs), len(outs)

    def body(*refs):
        ins = [r[...] for r in refs[: n_rows + n_vecs]]
        out_refs = refs[n_rows + n_vecs : n_rows + n_vecs + n_outs]
        acc_refs = refs[n_rows + n_vecs + n_outs :]
        res, acc = fn(*ins)
        for r, v in zip(out_refs, res, strict=True):
            r[...] = v.astype(r.dtype)
        if acc_refs:
            i = pl.program_id(0)

            @pl.when(i == 0)
            def _():
                for r, v in zip(acc_refs, acc, strict=True):
                    r[...] = v

            @pl.when(i > 0)
            def _():
                for r, v in zip(acc_refs, acc, strict=True):
                    r[...] += v

    in_specs = []
    for arr, blk, w in rows:
        if arr.ndim == 3:
            in_specs.append(pl.BlockSpec((arr.shape[0], tm, w), lambda i: (0, i, 0)))
        else:
            in_specs.append(pl.BlockSpec((tm, w), functools.partial(lambda i, blk: (i, blk), blk=blk)))
    for v in vecs:
        in_specs.append(pl.BlockSpec(v.shape, functools.partial(lambda i, nd: (0,) * nd, nd=v.ndim)))
    out_specs = [pl.BlockSpec((tm, w), lambda i: (i, 0)) for w, _ in outs]
    out_specs += [pl.BlockSpec((1, w), lambda i: (0, 0)) for w in accs]
    out_shape = [jax.ShapeDtypeStruct((S, w), dt) for w, dt in outs]
    out_shape += [jax.ShapeDtypeStruct((1, w), F32) for w in accs]
    res = _pcall(
        body,
        name=name,
        grid=(S // tm,),
        in_specs=in_specs,
        out_specs=out_specs,
        out_shape=out_shape,
        compiler_params=_params(1),
    )(*[r[0] for r in rows], *vecs)
    return res


def _sum0(v):
    return jnp.sum(v, axis=0, keepdims=True)


def _sigmoid(z):
    return 1.0 / (1.0 + jnp.exp(-z))


def _silu(z):
    return z * _sigmoid(z)


def _dsilu(z):
    s = _sigmoid(z)
    return s * (1.0 + z * (1.0 - s))


def _swap_half(x, half):
    if 2 * half == LANE:
        return pltpu.roll(x, half, 1)
    lane = lax.broadcasted_iota(jnp.int32, x.shape, 1)
    return jnp.where((lane % (2 * half)) < half, pltpu.roll(x, LANE - half, 1), pltpu.roll(x, half, 1))


def _rope_tile(x, cosf, sinf, half, inverse=False):
    sw = _swap_half(x, half)
    return x * cosf - sw * sinf if inverse else x * cosf + sw * sinf


def _tri_tables(n, by_key):
    if by_key:
        pairs = [(i, j) for j in range(n) for i in range(j, n)]
    else:
        pairs = [(i, j) for i in range(n) for j in range(i + 1)]
    return (np.array([p[0] for p in pairs], np.int32), np.array([p[1] for p in pairs], np.int32))


def _attn_scores(q, kn, kr, i, j, tq, masked):
    k = jnp.concatenate([kn, kr], axis=1)
    s = lax.dot_general(q, k, (((1,), (1,)), ((), ())), preferred_element_type=F32)
    if masked:
        rc = (i * tq + lax.broadcasted_iota(jnp.int32, s.shape, 0)) // CHUNK
        cc = (j * tq + lax.broadcasted_iota(jnp.int32, s.shape, 1)) // CHUNK
        s = jnp.where(cc <= rc, s, NEG_INF)
    return s, k


def attn_fwd(q, kv, kr, proj, mg_off, *, heads, tq, name, host=None):
    S = q.shape[0]
    tq = _pick(S, tq, CHUNK)
    n = S // tq
    it, jt = _tri_tables(n, by_key=False)
    T = len(it)
    mg_blk = mg_off // MLA_DV
    n_hi, n_ho = (len(host.ins), len(host.out_shapes)) if host else (0, 0)

    hp = ATTN_HEADS_PER_STEP
    assert heads % hp == 0 and mg_blk % hp == 0
    groups = heads // hp

    def body(it_ref, jt_ref, q_ref, kv_ref, kr_ref, mg_ref, *rest):
        host_in, rest = rest[:n_hi], rest[n_hi:]
        (o_ref, a_ref, lse_ref), rest = rest[:3], rest[3:]
        host_out, rest = rest[:n_ho], rest[n_ho:]
        (m_sc, l_sc, acc_sc), host_sems = rest[:3], rest[3:]
        h, t = pl.program_id(0), pl.program_id(1)
        i, j = it_ref[t], jt_ref[t]
        if host:
            _host_steps(host, (host_in, host_out, host_sems), h, t, groups, T, before=True)

        @pl.when(j == 0)
        def _():
            m_sc[...] = jnp.full(m_sc.shape, NEG_INF, F32)
            l_sc[...] = jnp.zeros(l_sc.shape, F32)
            acc_sc[...] = jnp.zeros(acc_sc.shape, F32)

        def step(masked):
            kr = kr_ref[...]
            for u in range(hp):
                q = q_ref[:, u * MLA_QW : (u + 1) * MLA_QW]
                kn = kv_ref[:, u * MLA_QW : u * MLA_QW + MLA_NOPE]
                v = kv_ref[:, u * MLA_QW + MLA_NOPE : (u + 1) * MLA_QW]
                s, _ = _attn_scores(q, kn, kr, i, j, tq, masked)
                m_prev = m_sc[u]
                m_new = jnp.maximum(m_prev, jnp.max(s, axis=1, keepdims=True))
                p = jnp.exp2((s - m_new).astype(BF16))
                alpha = jnp.exp2(m_prev - m_new)
                ones = jnp.ones((p.shape[1], LANE), BF16)
                l_sc[u] = alpha * l_sc[u] + jnp.dot(p, ones, preferred_element_type=F32)
                acc_sc[u] = alpha * acc_sc[u] + jnp.dot(p, v, preferred_element_type=F32)
                m_sc[u] = m_new

        @pl.when(j < i)
        def _():
            step(False)

        @pl.when(j == i)
        def _():
            step(True)
            for u in range(hp):
                cols = slice(u * MLA_DV, (u + 1) * MLA_DV)
                l = l_sc[u]
                o = acc_sc[u] / l
                o_ref[:, cols] = o
                a_ref[:, cols] = (o * _silu(mg_ref[:, cols].astype(F32))).astype(a_ref.dtype)
                lse_ref[u] = m_sc[u] + jnp.log2(l[:, :1])

        if host:
            _host_steps(host, (host_in, host_out, host_sems), h, t, groups, T, before=False)

    grid_spec = pltpu.PrefetchScalarGridSpec(
        num_scalar_prefetch=2,
        grid=(groups, T),
        in_specs=[
            pl.BlockSpec((tq, hp * MLA_QW), lambda h, t, it, jt: (it[t], h)),
            pl.BlockSpec((tq, hp * MLA_QW), lambda h, t, it, jt: (jt[t], h)),
            pl.BlockSpec((tq, LANE), lambda h, t, it, jt: (jt[t], 0)),
            pl.BlockSpec((tq, hp * MLA_DV), lambda h, t, it, jt: (it[t], mg_blk // hp + h)),
            *(host.in_specs if host else []),
        ],
        out_specs=[
            pl.BlockSpec((tq, hp * MLA_DV), lambda h, t, it, jt: (it[t], h)),
            pl.BlockSpec((tq, hp * MLA_DV), lambda h, t, it, jt: (it[t], h)),
            pl.BlockSpec((hp, tq, 1), lambda h, t, it, jt: (h, it[t], 0)),
            *(host.out_specs if host else []),
        ],
        scratch_shapes=[
            pltpu.VMEM((hp, tq, 1), F32),
            pltpu.VMEM((hp, tq, LANE), F32),
            pltpu.VMEM((hp, tq, MLA_DV), F32),
            *(host.scratch if host else []),
        ],
    )
    return _pcall(
        body,
        name=name,
        grid_spec=grid_spec,
        out_shape=[
            jax.ShapeDtypeStruct((S, heads * MLA_DV), F32),
            jax.ShapeDtypeStruct((S, heads * MLA_DV), BF16),
            jax.ShapeDtypeStruct((heads, S, 1), F32),
            *(host.out_shapes if host else []),
        ],
        compiler_params=_params(2),
    )(jnp.asarray(it), jnp.asarray(jt), q, kv, kr, proj, *(host.ins if host else []))


def attn_bwd(q, kv, kr, do, o, lse, *, heads, tq, name, host=None):
    S = q.shape[0]
    tq = _pick(S, tq, CHUNK)
    n = S // tq
    it, jt = _tri_tables(n, by_key=True)
    T = len(it)
    n_hi, n_ho = (len(host.ins), len(host.out_shapes)) if host else (0, 0)

    hp = ATTN_HEADS_PER_STEP
    assert heads % hp == 0
    groups = heads // hp

    def body(it_ref, jt_ref, q_ref, kv_ref, kr_ref, do_ref, o_ref, lse_ref, *rest):
        host_in, rest = rest[:n_hi], rest[n_hi:]
        (dq_ref, dkv_ref, dkr_ref), rest = rest[:3], rest[3:]
        host_out, rest = rest[:n_ho], rest[n_ho:]
        (dq_acc, dk_sc, dv_sc), host_sems = rest[:3], rest[3:]
        h, t = pl.program_id(0), pl.program_id(1)
        i, j = it_ref[t], jt_ref[t]
        if host:
            _host_steps(host, (host_in, host_out, host_sems), h, t, groups, T, before=True)

        @pl.when(t == 0)
        def _():
            dq_acc[...] = jnp.zeros(dq_acc.shape, F32)

        @pl.when(i == j)
        def _():
            dk_sc[...] = jnp.zeros(dk_sc.shape, F32)
            dv_sc[...] = jnp.zeros(dv_sc.shape, F32)

        def step(masked):
            kr = kr_ref[...]
            rows = pl.ds(pl.multiple_of(i * tq, tq), tq)
            for u in range(hp):
                wide, narrow = slice(u * MLA_QW, (u + 1) * MLA_QW), slice(u * MLA_DV, (u + 1) * MLA_DV)
                q_blk, do_blk = q_ref[:, wide], do_ref[:, narrow]
                kn = kv_ref[:, u * MLA_QW : u * MLA_QW + MLA_NOPE]
                v = kv_ref[:, u * MLA_QW + MLA_NOPE : (u + 1) * MLA_QW]
                s, k = _attn_scores(q_blk, kn, kr, i, j, tq, masked)
                p = jnp.exp2(s - lse_ref[u])
                delta = jnp.sum(do_blk.astype(F32) * o_ref[:, narrow], axis=1, keepdims=True)
                dp = lax.dot_general(do_blk, v, _NT, preferred_element_type=F32)
                ds = (p * (dp - delta)).astype(BF16)
                dv_sc[u] += lax.dot_general(p.astype(BF16), do_blk, _TN, preferred_element_type=F32)
                dk_sc[u] += lax.dot_general(ds, q_blk, _TN, preferred_element_type=F32)
                dq_acc[rows, wide] += jnp.dot(ds, k, preferred_element_type=F32)

        @pl.when(i == j)
        def _():
            step(True)

        @pl.when(i > j)
        def _():
            step(False)

        @pl.when(i == n - 1)
        def _():
            for u in range(hp):
                dkv_ref[:, u * MLA_QW : u * MLA_QW + MLA_NOPE] = (dk_sc[u, :, :MLA_NOPE] * LN2).astype(dkv_ref.dtype)
                dkv_ref[:, u * MLA_QW + MLA_NOPE : (u + 1) * MLA_QW] = dv_sc[u].astype(dkv_ref.dtype)
                dkr_ref[u] = dk_sc[u, :, MLA_NOPE:] * LN2

        @pl.when(t == T - 1)
        def _():
            dq_ref[...] = (dq_acc[...] * MLA_SCALE).astype(dq_ref.dtype)

        if host:
            _host_steps(host, (host_in, host_out, host_sems), h, t, groups, T, before=False)

    grid_spec = pltpu.PrefetchScalarGridSpec(
        num_scalar_prefetch=2,
        grid=(groups, T),
        in_specs=[
            pl.BlockSpec((tq, hp * MLA_QW), lambda h, t, it, jt: (it[t], h)),
            pl.BlockSpec((tq, hp * MLA_QW), lambda h, t, it, jt: (jt[t], h)),
            pl.BlockSpec((tq, LANE), lambda h, t, it, jt: (jt[t], 0)),
            pl.BlockSpec((tq, hp * MLA_DV), lambda h, t, it, jt: (it[t], h)),
            pl.BlockSpec((tq, hp * MLA_DV), lambda h, t, it, jt: (it[t], h)),
            pl.BlockSpec((hp, tq, 1), lambda h, t, it, jt: (h, it[t], 0)),
            *(host.in_specs if host else []),
        ],
        out_specs=[
            pl.BlockSpec((S, hp * MLA_QW), lambda h, t, it, jt: (0, h)),
            pl.BlockSpec((tq, hp * MLA_QW), lambda h, t, it, jt: (jt[t], h)),
            pl.BlockSpec((hp, tq, LANE), lambda h, t, it, jt: (h, jt[t], 0)),
            *(host.out_specs if host else []),
        ],
        scratch_shapes=[
            pltpu.VMEM((S, hp * MLA_QW), F32),
            pltpu.VMEM((hp, tq, MLA_QW), F32),
            pltpu.VMEM((hp, tq, MLA_DV), F32),
            *(host.scratch if host else []),
        ],
    )
    return _pcall(
        body,
        name=name,
        grid_spec=grid_spec,
        out_shape=[
            jax.ShapeDtypeStruct((S, heads * MLA_QW), BF16),
            jax.ShapeDtypeStruct((S, heads * (MLA_NOPE + MLA_DV)), BF16),
            jax.ShapeDtypeStruct((heads, S, LANE), F32),
            *(host.out_shapes if host else []),
        ],
        compiler_params=_params(2),
    )(jnp.asarray(it), jnp.asarray(jt), q, kv, kr, do, o, lse, *(host.ins if host else []))


def _host_steps(host, refs, h, t, heads, n_steps, before):
    if before:

        @pl.when((h == 0) & (t == 0))
        def _():
            host.start(*refs)

        @pl.when((h == heads // 2) & (t == 0))
        def _():
            host.relay(*refs)

    else:

        @pl.when((h == heads - 1) & (t == n_steps - 1))
        def _():
            host.finish(*refs)


def _ret_consts(heads):
    h = np.arange(heads, dtype=np.float32)
    lg = np.log(np.float32(1.0) - np.float32(2.0) ** (np.float32(-5.0) - h)).astype(np.float32)
    idx = np.arange(CHUNK, dtype=np.float32)
    dmat = np.exp(np.abs(idx[:, None] - idx[None, :])[None] * lg[:, None, None]).astype(np.float32)
    xi = np.exp((idx + 1.0)[None, :] * lg[:, None]).astype(np.float32)
    zeta = np.exp((CHUNK - 1.0 - idx)[None, :] * lg[:, None]).astype(np.float32)
    dec = np.exp(np.float32(CHUNK) * lg).astype(np.float32)
    xi = np.broadcast_to(xi[:, :, None], (heads, CHUNK, RET_DK)).copy()
    zeta = np.broadcast_to(zeta[:, :, None], (heads, CHUNK, RET_DK)).copy()
    dec = np.broadcast_to(dec[:, None, None], (heads, 8, LANE)).copy()
    return jnp.asarray(dmat), jnp.asarray(xi), jnp.asarray(zeta), jnp.asarray(dec)


_NT = (((1,), (1,)), ((), ()))
_TN = (((0,), (0,)), ((), ()))


def _dot(a, b, dn=(((1,), (0,)), ((), ()))):
    return lax.dot_general(a.astype(BF16), b.astype(BF16), dn, preferred_element_type=F32)


def _const_specs(hp):
    return [
        pl.BlockSpec((hp, CHUNK, CHUNK), lambda h, b: (h, 0, 0)),
        pl.BlockSpec((hp, CHUNK, RET_DK), lambda h, b: (h, 0, 0)),
        pl.BlockSpec((hp, CHUNK, RET_DK), lambda h, b: (h, 0, 0)),
        pl.BlockSpec((hp, 8, LANE), lambda h, b: (h, 0, 0)),
    ]


def _ret_group(heads, *blks):
    hp = min(RET_HEADS_PER_STEP, heads)
    assert heads % hp == 0 and all(b % hp == 0 for b in blks)
    return hp


def ret_fwd(q, k, proj, v_off, rg_off, *, heads, tb, name):
    S = q.shape[0]
    tb = _pick(S, tb, CHUNK)
    cpb, nb = tb // CHUNK, S // tb
    v_blk, rg_blk = v_off // RET_DV, rg_off // RET_DV
    hp = _ret_group(heads, v_blk, rg_blk)

    def body(q_ref, k_ref, v_ref, rg_ref, dm_ref, xi_ref, ze_ref, dec_ref, o_ref, a_ref, st_ref, r_sc):
        @pl.when(pl.program_id(1) == 0)
        def _():
            r_sc[...] = jnp.zeros(r_sc.shape, F32)

        for c in range(cpb):
            sl = pl.ds(c * CHUNK, CHUNK)
            for u in range(hp):
                dm, xi, ze, dec = dm_ref[u], xi_ref[u], ze_ref[u], dec_ref[u, 0:1, 0:1]
                narrow, wide = slice(u * RET_DK, (u + 1) * RET_DK), slice(u * RET_DV, (u + 1) * RET_DV)
                qc, kc, vc = q_ref[sl, narrow], k_ref[sl, narrow], v_ref[sl, wide]
                r = r_sc[u]
                st_ref[u, c] = r.astype(BF16)
                s = _dot(qc, kc, _NT) * dm
                o = _dot(s, vc) + _dot(qc.astype(F32) * xi, r)
                r_sc[u] = r * dec + _dot(kc.astype(F32) * ze, vc, _TN)
                mu = jnp.mean(o, axis=1, keepdims=True)
                d = o - mu
                n = d * lax.rsqrt(jnp.mean(d * d, axis=1, keepdims=True) + EPS)
                o_ref[sl, wide] = o
                a_ref[sl, wide] = (n * _silu(rg_ref[sl, wide].astype(F32))).astype(a_ref.dtype)

    return _pcall(
        body,
        name=name,
        grid=(heads // hp, nb),
        in_specs=[
            pl.BlockSpec((tb, hp * RET_DK), lambda h, b: (b, h)),
            pl.BlockSpec((tb, hp * RET_DK), lambda h, b: (b, h)),
            pl.BlockSpec((tb, hp * RET_DV), lambda h, b: (b, v_blk // hp + h)),
            pl.BlockSpec((tb, hp * RET_DV), lambda h, b: (b, rg_blk // hp + h)),
            *_const_specs(hp),
        ],
        out_specs=[
            pl.BlockSpec((tb, hp * RET_DV), lambda h, b: (b, h)),
            pl.BlockSpec((tb, hp * RET_DV), lambda h, b: (b, h)),
            pl.BlockSpec((hp, cpb, RET_DK, RET_DV), lambda h, b: (h, b, 0, 0)),
        ],
        out_shape=[
            jax.ShapeDtypeStruct((S, heads * RET_DV), F32),
            jax.ShapeDtypeStruct((S, heads * RET_DV), BF16),
            jax.ShapeDtypeStruct((heads, S // CHUNK, RET_DK, RET_DV), BF16),
        ],
        scratch_shapes=[pltpu.VMEM((hp, RET_DK, RET_DV), F32)],
        compiler_params=_params(2),
    )(q, k, proj, proj, *_ret_consts(heads))


def ret_bwd(q, k, proj, v_off, states, do, *, heads, tb, name):
    S = q.shape[0]
    tb = _pick(S, tb, CHUNK)
    cpb, nb = tb // CHUNK, S // tb
    v_blk = v_off // RET_DV
    hp = _ret_group(heads, v_blk)

    def body(q_ref, k_ref, v_ref, st_ref, do_ref, dm_ref, xi_ref, ze_ref, dec_ref, dq_ref, dk_ref, dv_ref, dr_sc):
        @pl.when(pl.program_id(1) == 0)
        def _():
            dr_sc[...] = jnp.zeros(dr_sc.shape, F32)

        for c in reversed(range(cpb)):
            sl = pl.ds(c * CHUNK, CHUNK)
            for u in range(hp):
                dm, xi, ze, dec = dm_ref[u], xi_ref[u], ze_ref[u], dec_ref[u, 0:1, 0:1]
                narrow, wide = slice(u * RET_DK, (u + 1) * RET_DK), slice(u * RET_DV, (u + 1) * RET_DV)
                qc, kc, vc, doc = q_ref[sl, narrow], k_ref[sl, narrow], v_ref[sl, wide], do_ref[sl, wide]
                r_prev = st_ref[u, c]
                dr = dr_sc[u]
                a = _dot(qc, kc, _NT) * dm
                ds = _dot(doc, vc, _NT) * dm
                kz = kc.astype(F32) * ze
                dq_ref[sl, narrow] = (_dot(ds, kc) + _dot(doc, r_prev, _NT) * xi).astype(dq_ref.dtype)
                dk_ref[sl, narrow] = (_dot(ds, qc, _TN) + _dot(vc, dr, _NT) * ze).astype(dk_ref.dtype)
                dv_ref[sl, wide] = (_dot(a, doc, _TN) + _dot(kz, dr)).astype(dv_ref.dtype)
                dr_sc[u] = dr * dec + _dot(qc.astype(F32) * xi, doc, _TN)

    rev = lambda h, b: (nb - 1 - b, h)
    return _pcall(
        body,
        name=name,
        grid=(heads // hp, nb),
        in_specs=[
            pl.BlockSpec((tb, hp * RET_DK), rev),
            pl.BlockSpec((tb, hp * RET_DK), rev),
            pl.BlockSpec((tb, hp * RET_DV), lambda h, b: (nb - 1 - b, v_blk // hp + h)),
            pl.BlockSpec((hp, cpb, RET_DK, RET_DV), lambda h, b: (h, nb - 1 - b, 0, 0)),
            pl.BlockSpec((tb, hp * RET_DV), rev),
            *_const_specs(hp),
        ],
        out_specs=[
            pl.BlockSpec((tb, hp * RET_DK), rev),
            pl.BlockSpec((tb, hp * RET_DK), rev),
            pl.BlockSpec((tb, hp * RET_DV), rev),
        ],
        out_shape=[
            jax.ShapeDtypeStruct((S, heads * RET_DK), BF16),
            jax.ShapeDtypeStruct((S, heads * RET_DK), BF16),
            jax.ShapeDtypeStruct((S, heads * RET_DV), BF16),
        ],
        scratch_shapes=[pltpu.VMEM((hp, RET_DK, RET_DV), F32)],
        compiler_params=_params(2),
    )(q, k, proj, states, do, *_ret_consts(heads))


def _me():
    return (lax.axis_index("x"), lax.axis_index("y"), lax.axis_index("c"))


def _comm(name, ins, out_shapes, n_local, n_remote, plan):
    return run_exchange(name, Exchange(ins, out_shapes, n_local, n_remote, plan))


def run_exchange(name, ex):
    def body(*refs):
        ex.start(*ex.split(refs))
        ex.relay(*ex.split(refs))
        ex.finish(*ex.split(refs))

    return _pcall(
        body, name=name, in_specs=ex.in_specs, out_specs=ex.out_specs, out_shape=ex.out_shapes, scratch_shapes=ex.scratch
    )(*ex.ins)


class Exchange:
    def __init__(self, ins, out_shapes, n_local, n_remote, plan):
        self.ins, self.out_shapes, self.n_local, self.n_remote, self.plan = list(ins), list(out_shapes), n_local, n_remote, plan
        any_spec = pl.BlockSpec(memory_space=pl.ANY)
        self.in_specs, self.out_specs = [any_spec] * len(self.ins), [any_spec] * len(self.out_shapes)
        self.scratch = [
            pltpu.SemaphoreType.DMA((n_remote,)),
            pltpu.SemaphoreType.DMA((n_remote,)),
            pltpu.SemaphoreType.DMA((max(n_local, 1),)),
        ]

    def split(self, refs):
        n_in, n_out = len(self.ins), len(self.out_shapes)
        return refs[:n_in], refs[n_in : n_in + n_out], refs[n_in + n_out :]

    def _copies(self, in_refs, out_refs, sems):
        send_sems, recv_sems, local_sems = sems
        me = _me()
        local_plan, remote_plan = self.plan(me, in_refs, out_refs)
        assert len(local_plan) == self.n_local and len(remote_plan) == self.n_remote, (len(local_plan), len(remote_plan))
        def local_copy(n):
            src, dst = local_plan[n]
            return pltpu.make_async_copy(src, dst, local_sems.at[n])

        def send(n):
            src, dst, peer, _, _ = remote_plan[n]
            return pltpu.make_async_remote_copy(
                src_ref=src, dst_ref=dst, send_sem=send_sems.at[n], recv_sem=recv_sems.at[n], device_id=peer, device_id_type=MESH
            )

        def arrival(n):
            src, _, _, landing, _ = remote_plan[n]
            return pltpu.make_async_remote_copy(
                src_ref=src, dst_ref=landing, send_sem=send_sems.at[n], recv_sem=recv_sems.at[n], device_id=me, device_id_type=MESH
            )

        after = [a for (_, _, _, _, a) in remote_plan]
        assert all(a is None or a < n for n, a in enumerate(after))
        return local_copy, send, arrival, after

    def start(self, in_refs, out_refs, sems):
        local_copy, send, _, after = self._copies(in_refs, out_refs, sems)
        for n in range(self.n_local):
            local_copy(n).start()
        for n, a in enumerate(after):
            if a is None:
                send(n).start()

    def relay(self, in_refs, out_refs, sems):
        _, send, arrival, after = self._copies(in_refs, out_refs, sems)
        for n in sorted({a for a in after if a is not None}):
            arrival(n).wait_recv()
            for m, a in enumerate(after):
                if a == n:
                    send(m).start()

    def finish(self, in_refs, out_refs, sems):
        local_copy, send, arrival, after = self._copies(in_refs, out_refs, sems)
        relayed = {a for a in after if a is not None}
        for n in range(self.n_remote):
            if n not in relayed:
                arrival(n).wait_recv()
        for n in range(self.n_remote):
            send(n).wait_send()
        for n in range(self.n_local):
            local_copy(n).wait()


_CHIP_FLIPS = ((1, 0), (0, 1), (1, 1))


class Placement:
    def __init__(self, rows, full_rows, cuts=(), offset=None, zero_rows=None):
        edges = [0, *sorted(cuts), rows]
        self.rows, self.full_rows = rows, full_rows
        self.runs = [(a, b - a) for a, b in zip(edges[:-1], edges[1:]) if b > a]
        self.offset = offset if offset is not None else (lambda j, start: j * rows + start)
        self.zero_rows = zero_rows

    def at(self, ref_2d_plus, j, start, size):
        return pl.ds(pl.multiple_of(self.offset(j, start), 16), size)


def gather_layer(prepped, zeros, places, l, name):
    n_w = len(prepped)
    n_chip = sum(len(p.runs) for p in places) * 3
    n_own = sum(len(p.runs) for p in places) * 2 + sum(2 for p in places if p.zero_rows)

    def plan(me, in_refs, out_refs):
        x, y, c = me
        j = 2 * x + y
        sib = (x, y, 1 - c)
        from_chips, own, forwards = [], [], []
        for i, o, p in zip(in_refs[:n_w], out_refs, places):
            for start, size in p.runs:
                for dx, dy in _CHIP_FLIPS:
                    px, py = x ^ dx, y ^ dy
                    jp = 2 * px + py
                    mine_there = o.at[c, p.at(o, j, start, size)]
                    theirs_here = o.at[c, p.at(o, jp, start, size)]
                    n = len(from_chips)
                    from_chips.append((i.at[l, c, pl.ds(start, size)], mine_there, (px, py, c), theirs_here, None))
                    forwards.append((theirs_here, theirs_here, sib, o.at[1 - c, p.at(o, jp, start, size)], n))
                for h in range(2):
                    place = o.at[h, p.at(o, j, start, size)]
                    own.append((i.at[l, h, pl.ds(start, size)], place, sib, place, None))
            if p.zero_rows:
                for h in range(2):
                    place = o.at[h, pl.ds(p.zero_rows[0], p.zero_rows[1])]
                    own.append((in_refs[n_w], place, sib, place, None))
        return [], from_chips + own + forwards

    shapes = [jax.ShapeDtypeStruct((2, p.full_rows, w.shape[3]), w.dtype) for w, p in zip(prepped, places)]
    ex = Exchange([*prepped, zeros], shapes, 0, 2 * n_chip + n_own, plan)
    return ex if name is None else run_exchange(name, ex)


def sibling_scatter(arrays, name):
    def plan(me, in_refs, out_refs):
        x, y, c = me
        return [], [(i.at[1 - c], o, (x, y, 1 - c), o, None) for i, o in zip(in_refs, out_refs)]

    shapes = [jax.ShapeDtypeStruct(a.shape[1:], a.dtype) for a in arrays]
    return _comm(name, arrays, shapes, 0, len(arrays), plan)


def sibling_swap(arrays, name):
    def plan(me, in_refs, out_refs):
        x, y, c = me
        return [], [(i, o, (x, y, 1 - c), o, None) for i, o in zip(in_refs, out_refs)]

    shapes = [jax.ShapeDtypeStruct(a.shape, a.dtype) for a in arrays]
    return _comm(name, arrays, shapes, 0, len(arrays), plan)


def scatter_to_chips(arrays, places, name):
    def plan(me, in_refs, out_refs):
        x, y, c = me
        j = 2 * x + y
        local_plan, remote_plan = [], []
        for i, o, p in zip(in_refs, out_refs, places):
            for start, size in p.runs:
                local_plan.append((i.at[p.at(i, j, start, size)], o.at[3, pl.ds(start, size)]))
                for k, (dx, dy) in enumerate(_CHIP_FLIPS):
                    px, py = x ^ dx, y ^ dy
                    landing = o.at[k, pl.ds(start, size)]
                    remote_plan.append((i.at[p.at(i, 2 * px + py, start, size)], landing, (px, py, c), landing, None))
        return local_plan, remote_plan

    n_runs = sum(len(p.runs) for p in places)
    shapes = [jax.ShapeDtypeStruct((4, p.rows, a.shape[1]), a.dtype) for a, p in zip(arrays, places)]
    ex = Exchange(arrays, shapes, n_runs, 3 * n_runs, plan)
    return ex if name is None else run_exchange(name, ex)


def allgather8(block, name):
    def plan(me, in_refs, out_refs):
        x, y, c = me
        (i,), (o,) = in_refs, out_refs
        mine = 4 * x + 2 * y + c
        remote_plan = []
        for flip in range(1, 8):
            px, py, pc = x ^ (flip >> 2), y ^ ((flip >> 1) & 1), c ^ (flip & 1)
            remote_plan.append((i, o.at[mine], (px, py, pc), o.at[4 * px + 2 * py + pc], None))
        return [(i, o.at[mine])], remote_plan

    return _comm(name, [block], [jax.ShapeDtypeStruct((8, *block.shape), block.dtype)], 1, 7, plan)[0]


def transpose_split(a, out_dtype, name):
    L, R, C = a.shape
    rh = R // 2
    rt = _pick(rh, 256)
    n = rh // rt

    def body(x_ref, o_ref):
        o_ref[0, 0] = x_ref[0].T.astype(o_ref.dtype)

    return _pcall(
        body,
        name=name,
        grid=(L, 2, n),
        in_specs=[pl.BlockSpec((1, rt, C), lambda l, h, i: (l, h * n + i, 0))],
        out_specs=pl.BlockSpec((1, 1, C, rt), lambda l, h, i: (l, h, 0, i)),
        out_shape=jax.ShapeDtypeStruct((L, 2, C, rh), out_dtype),
        compiler_params=_params(3),
    )(a)


def cast_split(a, out_dtype, name):
    L, R, C = a.shape
    rt = _pick(R, 512, 16)

    def body(x_ref, o_ref):
        o_ref[0, 0] = x_ref[0].astype(o_ref.dtype)

    return _pcall(
        body,
        name=name,
        grid=(L, 2, R // rt),
        in_specs=[pl.BlockSpec((1, rt, C // 2), lambda l, h, i: (l, i, h))],
        out_specs=pl.BlockSpec((1, 1, rt, C // 2), lambda l, h, i: (l, h, i, 0)),
        out_shape=jax.ShapeDtypeStruct((L, 2, R, C // 2), out_dtype),
        compiler_params=_params(3),
    )(a)


def untranspose_halves(mine, theirs, my_c, stack, l, n_layers, name):
    C, rh = mine.shape
    n_in = 2 if stack is None else 3
    rt = _pick(rh, 256)
    n = rh // rt

    def body(c_ref, a_ref, b_ref, *rest):
        o_ref = rest[-1]
        h = pl.program_id(0)
        o_ref[0] = jnp.where(h == c_ref[0], a_ref[...], b_ref[...]).T

    grid_spec = pltpu.PrefetchScalarGridSpec(
        num_scalar_prefetch=1,
        grid=(2, n),
        in_specs=[pl.BlockSpec((C, rt), lambda h, i, c: (0, i))] * 2
        + ([] if stack is None else [pl.BlockSpec(memory_space=pl.ANY)]),
        out_specs=pl.BlockSpec((1, rt, C), lambda h, i, c: (l, h * n + i, 0)),
    )
    return _pcall(
        body,
        name=name,
        grid_spec=grid_spec,
        out_shape=jax.ShapeDtypeStruct((n_layers, 2 * rh, C), F32),
        input_output_aliases={} if stack is None else {n_in: 0},
        compiler_params=_params(2),
    )(my_c.reshape(1), mine, theirs, *([] if stack is None else [stack]))


def merge_halves(mine, theirs, my_c, stack, l, n_layers, name):
    R, ch = mine.shape
    n_in = 2 if stack is None else 3
    rt = _pick(R, 512, 8)

    def body(c_ref, a_ref, b_ref, *rest):
        o_ref = rest[-1]
        o_ref[0] = jnp.where(pl.program_id(0) == c_ref[0], a_ref[...], b_ref[...])

    grid_spec = pltpu.PrefetchScalarGridSpec(
        num_scalar_prefetch=1,
        grid=(2, R // rt),
        in_specs=[pl.BlockSpec((rt, ch), lambda h, i, c: (i, 0))] * 2
        + ([] if stack is None else [pl.BlockSpec(memory_space=pl.ANY)]),
        out_specs=pl.BlockSpec((1, rt, ch), lambda h, i, c: (l, i, h)),
    )
    return _pcall(
        body,
        name=name,
        grid_spec=grid_spec,
        out_shape=jax.ShapeDtypeStruct((n_layers, R, 2 * ch), F32),
        input_output_aliases={} if stack is None else {n_in: 0},
        compiler_params=_params(2),
    )(my_c.reshape(1), mine, theirs, *([] if stack is None else [stack]))


def add_pair(g, r, my_c, name):
    _, rows, w = g.shape
    tm = _pick(rows, max(16, min(1024, (1 << 20) // w // 16 * 16)), 16)

    def body(c_ref, g_ref, r_ref, o_ref):
        o_ref[...] = (g_ref[0].astype(F32) + r_ref[...].astype(F32)).astype(o_ref.dtype)

    grid_spec = pltpu.PrefetchScalarGridSpec(
        num_scalar_prefetch=1,
        grid=(rows // tm,),
        in_specs=[pl.BlockSpec((1, tm, w), lambda i, c: (c[0], i, 0)), pl.BlockSpec((tm, w), lambda i, c: (i, 0))],
        out_specs=pl.BlockSpec((tm, w), lambda i, c: (i, 0)),
    )
    return _pcall(
        body, name=name, grid_spec=grid_spec, out_shape=jax.ShapeDtypeStruct((rows, w), BF16), compiler_params=_params(1)
    )(my_c.reshape(1), g, r)


class Dims:
    def __init__(self, S, D, L, ret_heads, mla_heads, q_rank, kv_rank):
        self.S, self.D, self.L, self.HR, self.HM, self.QR, self.KR = S, D, L, ret_heads, mla_heads, q_rank, kv_rank
        self.RQ, self.RV, self.MV = ret_heads * RET_DK, ret_heads * RET_DV, mla_heads * MLA_DV
        self.D_IN = 2 * self.RQ + 2 * self.RV + q_rank + kv_rank + MLA_ROPE + self.MV + 2 * D
        self.lo = 2 * self.RQ + 2 * self.RV
        self.mid = q_rank + kv_rank + MLA_ROPE
        self.DP = self.D_IN + LANE - MLA_ROPE
        self.o_rq, self.o_rk, self.o_rv, self.o_rg = 0, self.RQ, 2 * self.RQ, 2 * self.RQ + self.RV
        self.o_mg = self.lo
        self.o_bga = self.o_mg + self.MV
        self.o_bgb = self.o_bga + D
        self.o_cq = self.o_bgb + D
        self.o_ckv = self.o_cq + q_rank
        self.o_kr = self.o_ckv + kv_rank

    def pad_uq_rows(self, w):
        heads = w.shape[-2] // (MLA_NOPE + MLA_ROPE)
        w = w.reshape(*w.shape[:-2], heads, MLA_NOPE + MLA_ROPE, w.shape[-1])
        w = jnp.pad(w, [(0, 0)] * (w.ndim - 2) + [(0, MLA_QW - MLA_NOPE - MLA_ROPE), (0, 0)])
        return w.reshape(*w.shape[:-3], heads * MLA_QW, w.shape[-1])

    def unpad_uq_rows(self, w):
        heads = w.shape[-2] // MLA_QW
        w = w.reshape(*w.shape[:-2], heads, MLA_QW, w.shape[-1])[..., : MLA_NOPE + MLA_ROPE, :]
        return w.reshape(*w.shape[:-3], heads * (MLA_NOPE + MLA_ROPE), w.shape[-1])

    def placements(self):
        rows_in = self.D_IN // 4
        lo, mid, n_hi = self.lo, self.mid, self.D_IN - self.lo - self.mid
        cuts = {b % rows_in for b in (lo, lo + mid)} - {0}

        def offset_in(j, start):
            g = j * rows_in + start
            return jnp.where(g < lo, g, jnp.where(g < lo + mid, g + n_hi, g - mid))

        return [
            Placement(rows_in, self.DP, cuts, offset_in, zero_rows=(self.D_IN, self.DP - self.D_IN)),
            Placement(self.HM * MLA_QW // 4, self.HM * MLA_QW),
            Placement(self.HM * (MLA_NOPE + MLA_DV) // 4, self.HM * (MLA_NOPE + MLA_DV)),
            Placement(self.RV // 4, self.RV),
            Placement(self.MV // 4, self.MV),
            Placement(self.D // 4, self.D),
        ]


def _rope_tables(positions, dim, width):
    inv = 1.0 / (ROPE_BASE ** (jnp.arange(0, dim, 2, dtype=F32) / dim))
    ang = positions.astype(F32)[:, None] * inv
    cos, sin = jnp.cos(ang), jnp.sin(ang)
    pad = jnp.zeros((positions.shape[0], width - dim), F32)
    return jnp.concatenate([cos, cos, pad], axis=1), jnp.concatenate([-sin, sin, pad], axis=1)


def _tiles(x):
    return [x[:, t * LANE : (t + 1) * LANE] for t in range(x.shape[1] // LANE)]


def _cat(parts):
    return parts[0] if len(parts) == 1 else jnp.concatenate(parts, axis=1)


def _rms(x, eps=EPS):
    return lax.rsqrt(jnp.mean(x * x, axis=1, keepdims=True) + eps)


def layer_fwd(dm, l, x, shift, scale, gate, g_norm, g_cq, g_ckv, w, tabs, host=None):
    cos_r, sin_r, cos_m, sin_m = tabs
    nm = lambda s: f"l{l}_{s}"

    def f_norm(x, g, scale, shift):
        return [x * _rms(x) * g * (1.0 + scale) + shift], []

    (h,) = rowwise(f_norm, [x], [g_norm, scale, shift], [(dm.D, BF16)], tm=256, name=nm("norm"))
    proj = matmul(h, w["in"], tb=True, b_split=True, name=nm("mm_in"), tn=1920)

    def f_rope_ret(rq, rk, cos, sin):
        rq, rk = rq.astype(F32), rk.astype(F32)
        q = _cat([_rope_tile(t, cos, sin, RET_DK // 2) for t in _tiles(rq)])
        k = _cat([_rope_tile(t, cos, sin, RET_DK // 2) * (RET_DK**-0.5) for t in _tiles(rk)])
        return [q, k], []

    rq, rk = rowwise(
        f_rope_ret,
        [win(proj, dm.o_rq, dm.RQ), win(proj, dm.o_rk, dm.RQ), cos_r, sin_r],
        [],
        [(dm.RQ, BF16)] * 2,
        tm=512,
        name=nm("rope_ret"),
    )
    o_ret, a_ret, states = ret_fwd(rq, rk, proj, dm.o_rv, dm.o_rg, heads=dm.HR, tb=RET_BLOCK, name=nm("ret_fwd"))

    def f_prep(cq, ckv, kr, cos, sin, g_cq, g_ckv):
        cq, ckv, kr = cq.astype(F32), ckv.astype(F32), kr.astype(F32)
        return [cq * _rms(cq) * g_cq, ckv * _rms(ckv) * g_ckv, _rope_tile(kr, cos, sin, MLA_ROPE // 2)], []

    cqn, ckvn, krr = rowwise(
        f_prep,
        [win(proj, dm.o_cq, dm.QR), win(proj, dm.o_ckv, dm.KR), win(proj, dm.o_kr, LANE), cos_m, sin_m],
        [g_cq, g_ckv],
        [(dm.QR, BF16), (dm.KR, BF16), (LANE, BF16)],
        tm=512,
        name=nm("mla_prep"),
    )
    q_raw = matmul(cqn, w["uq"], tb=True, b_split=True, name=nm("mm_uq"))
    kv = matmul(ckvn, w["ukv"], tb=True, b_split=True, name=nm("mm_ukv"))

    def f_rope_q(q, cos, sin):
        t = _tiles(q.astype(F32))
        rot = [t[n] if n % 2 == 0 else _rope_tile(t[n], cos, sin, MLA_ROPE // 2) for n in range(len(t))]
        return [_cat([r * (MLA_SCALE * LOG2E) for r in rot])], []

    (q,) = rowwise(f_rope_q, [q_raw, cos_m, sin_m], [], [(dm.HM * MLA_QW, BF16)], tm=256, name=nm("rope_q"))
    o_mla, a_mla, lse, *hosted = attn_fwd(
        q, kv, krr, proj, dm.o_mg, heads=dm.HM, tq=ATTN_BLOCK, name=nm("attn_fwd"), host=host
    )

    y_ret = matmul(a_ret, w["ret"], b_split=True, out_dtype=F32, name=nm("mm_ret"))
    y_mla = matmul(a_mla, w["mla"], b_split=True, out_dtype=F32, name=nm("mm_mla"))

    def f_merge(y_ret, y_mla, bga, bgb):
        return [_sigmoid(bga.astype(F32)) * y_ret + _sigmoid(bgb.astype(F32)) * y_mla], []

    (merged,) = rowwise(
        f_merge, [y_ret, y_mla, win(proj, dm.o_bga, dm.D), win(proj, dm.o_bgb, dm.D)], [], [(dm.D, BF16)], tm=256, name=nm("merge")
    )
    out = matmul(merged, w["out"], b_split=True, out_dtype=F32, name=nm("mm_out"))

    def f_resid(x, out, gate):
        return [x + gate * out], []

    (x_new,) = rowwise(f_resid, [x, out], [gate], [(dm.D, F32)], tm=256, name=nm("resid"))
    saved = dict(
        x=x, h=h, proj=proj, rq=rq, rk=rk, o_ret=o_ret, a_ret=a_ret, states=states, cqn=cqn, ckvn=ckvn, krr=krr, q=q, kv=kv,
        o_mla=o_mla, a_mla=a_mla, lse=lse, y_ret=y_ret, y_mla=y_mla, merged=merged, out=out,
    )
    return x_new, saved, hosted


def layer_bwd(dm, l, dx_out, sv, shift, scale, gate, g_norm, g_cq, g_ckv, w, tabs, host=None):
    cos_r, sin_r, cos_m, sin_m = tabs
    nm = lambda s: f"l{l}_{s}"
    proj = sv["proj"]

    def b_resid(dx, out, gate):
        return [dx * gate], [_sum0(dx * out)]

    dout, d_gate = rowwise(b_resid, [dx_out, sv["out"]], [gate], [(dm.D, BF16)], [dm.D], tm=256, name=nm("resid_bwd"))
    dmerged = matmul(dout, w["out"], tb=True, b_split=True, name=nm("mm_dmerged"))
    dw_out = matmul(sv["merged"], dout, ta=True, out_split=True, tn=2048, name=nm("mm_dw_out"))

    def b_merge(dmg, y_ret, y_mla, bga, bgb):
        dmg = dmg.astype(F32)
        ga, gb = _sigmoid(bga.astype(F32)), _sigmoid(bgb.astype(F32))
        return [dmg * ga, dmg * gb, dmg * y_ret * ga * (1.0 - ga), dmg * y_mla * gb * (1.0 - gb)], []

    dy_ret, dy_mla, d_bga, d_bgb = rowwise(
        b_merge,
        [dmerged, sv["y_ret"], sv["y_mla"], win(proj, dm.o_bga, dm.D), win(proj, dm.o_bgb, dm.D)],
        [],
        [(dm.D, BF16)] * 4,
        tm=256,
        name=nm("merge_bwd"),
    )
    da_ret = matmul(dy_ret, w["ret"], tb=True, b_split=True, name=nm("mm_da_ret"))
    dw_ret = matmul(sv["a_ret"], dy_ret, ta=True, out_split=True, tn=2048, name=nm("mm_dw_ret"))
    da_mla = matmul(dy_mla, w["mla"], tb=True, b_split=True, name=nm("mm_da_mla"))
    dw_mla = matmul(sv["a_mla"], dy_mla, ta=True, out_split=True, tn=2048, name=nm("mm_dw_mla"))

    def b_ret_gate(da, rg, o):
        da, rg = da.astype(F32), rg.astype(F32)
        do_parts, drg_parts = [], []
        for hh in range(dm.HR):
            sl = slice(hh * RET_DV, (hh + 1) * RET_DV)
            oh, dah, rgh = o[:, sl], da[:, sl], rg[:, sl]
            mu = jnp.mean(oh, axis=1, keepdims=True)
            d = oh - mu
            r = lax.rsqrt(jnp.mean(d * d, axis=1, keepdims=True) + EPS)
            n = d * r
            dn = dah * _silu(rgh)
            drg_parts.append(dah * n * _dsilu(rgh))
            do_parts.append(r * (dn - jnp.mean(dn, axis=1, keepdims=True) - n * jnp.mean(dn * n, axis=1, keepdims=True)))
        return [_cat(do_parts), _cat(drg_parts)], []

    do_ret, d_rg = rowwise(
        b_ret_gate, [da_ret, win(proj, dm.o_rg, dm.RV), sv["o_ret"]], [], [(dm.RV, BF16)] * 2, tm=256, name=nm("ret_gate_bwd")
    )
    dq_rot, dk_rot, d_rv = ret_bwd(sv["rq"], sv["rk"], proj, dm.o_rv, sv["states"], do_ret, heads=dm.HR, tb=RET_BLOCK, name=nm("ret_bwd"))

    def b_rope_ret(dq, dk, cos, sin):
        dq, dk = dq.astype(F32), dk.astype(F32)
        q = _cat([_rope_tile(t, cos, sin, RET_DK // 2, inverse=True) for t in _tiles(dq)])
        k = _cat([_rope_tile(t, cos, sin, RET_DK // 2, inverse=True) * (RET_DK**-0.5) for t in _tiles(dk)])
        return [q, k], []

    d_rq, d_rk = rowwise(b_rope_ret, [dq_rot, dk_rot, cos_r, sin_r], [], [(dm.RQ, BF16)] * 2, tm=512, name=nm("rope_ret_bwd"))

    def b_mla_gate(da, mg, o):
        da, mg = da.astype(F32), mg.astype(F32)
        return [da * _silu(mg), da * o * _dsilu(mg)], []

    do_mla, d_mg = rowwise(
        b_mla_gate, [da_mla, win(proj, dm.o_mg, dm.MV), sv["o_mla"]], [], [(dm.MV, BF16)] * 2, tm=256, name=nm("mla_gate_bwd")
    )
    dq_att, dkv, dkr_heads, *hosted = attn_bwd(
        sv["q"], sv["kv"], sv["krr"], do_mla, sv["o_mla"], sv["lse"], heads=dm.HM, tq=ATTN_BLOCK, name=nm("attn_bwd"), host=host
    )

    def b_rope_q(dq, cos, sin):
        t = _tiles(dq.astype(F32))
        return [_cat([t[n] if n % 2 == 0 else _rope_tile(t[n], cos, sin, MLA_ROPE // 2, inverse=True) for n in range(len(t))])], []

    (dq_raw,) = rowwise(b_rope_q, [dq_att, cos_m, sin_m], [], [(dm.HM * MLA_QW, BF16)], tm=256, name=nm("rope_q_bwd"))
    dcqn = matmul(dq_raw, w["uq"], b_split=True, name=nm("mm_dcqn"))
    dw_uq = matmul(dq_raw, sv["cqn"], ta=True, out_split=True, tn=2048, name=nm("mm_dw_uq"))
    dckvn = matmul(dkv, w["ukv"], b_split=True, name=nm("mm_dckvn"))
    dw_ukv = matmul(dkv, sv["ckvn"], ta=True, out_split=True, tn=2048, name=nm("mm_dw_ukv"))

    def b_prep(dcqn, dckvn, cq, ckv, cos, sin, dkr_h, g_cq, g_ckv):
        outs, accs = [], []
        for dn, z, g in ((dcqn, cq, g_cq), (dckvn, ckv, g_ckv)):
            dn, z = dn.astype(F32), z.astype(F32)
            n = z * _rms(z)
            dng = dn * g
            outs.append(_rms(z) * (dng - n * jnp.mean(dng * n, axis=1, keepdims=True)))
            accs.append(_sum0(dn * n))
        dkr = dkr_h[0]
        for hh in range(1, dm.HM):
            dkr = dkr + dkr_h[hh]
        outs.append(_rope_tile(dkr, cos, sin, MLA_ROPE // 2, inverse=True))
        return outs, accs

    d_cq, d_ckv, d_kr, dg_cq, dg_ckv = rowwise(
        b_prep,
        [dcqn, dckvn, win(proj, dm.o_cq, dm.QR), win(proj, dm.o_ckv, dm.KR), cos_m, sin_m, dkr_heads],
        [g_cq, g_ckv],
        [(dm.QR, BF16), (dm.KR, BF16), (LANE, BF16)],
        [dm.QR, dm.KR],
        tm=256,
        name=nm("mla_prep_bwd"),
    )

    dproj = jnp.concatenate([d_rq, d_rk, d_rv, d_rg, d_mg, d_bga, d_bgb, d_cq, d_ckv, d_kr], axis=1)
    dh = matmul(dproj, w["in"], b_split=True, out_dtype=F32, name=nm("mm_dh"), tn=2048, tk=1920)
    dw_in = matmul(dproj, sv["h"], ta=True, out_split=True, tn=2048, name=nm("mm_dw_in"))

    def b_norm(dh, x, dx_res, g, scale):
        r = _rms(x)
        xn = x * r
        dxn = dh * g * (1.0 + scale)
        dx = dx_res + r * (dxn - xn * jnp.mean(dxn * xn, axis=1, keepdims=True))
        return [dx], [_sum0(dh), _sum0(dh * xn * g), _sum0(dh * (1.0 + scale) * xn)]

    dx, d_shift, d_scale, dg_norm = rowwise(
        b_norm, [dh, sv["x"], dx_out], [g_norm, scale], [(dm.D, F32)], [dm.D] * 3, tm=256, name=nm("norm_bwd")
    )
    dws = dict(w_in=dw_in, w_uq=dw_uq, w_ukv=dw_ukv, w_ret_proj=dw_ret, w_mla_proj=dw_mla, w_out=dw_out)
    dvec = dict(mod=jnp.concatenate([d_shift, d_scale, d_gate], axis=1), g_norm=dg_norm, g_cq=dg_cq, g_ckv=dg_ckv)
    return dx, dws, dvec, hosted


def adamw(w, g, m, v, name):
    shape = w.shape
    cols = shape[-1]
    view = lambda a: a.reshape(-1, cols)

    def f(w, g, m, v):
        m = ADAM_B1 * m + (1.0 - ADAM_B1) * g
        v = ADAM_B2 * v + (1.0 - ADAM_B2) * (g * g)
        m_hat = m / (1.0 - ADAM_B1**ADAM_STEP)
        v_hat = v / (1.0 - ADAM_B2**ADAM_STEP)
        delta = -ADAM_LR * (m_hat / (jnp.sqrt(v_hat) + ADAM_EPS) + ADAM_WD * w)
        return [delta, m, v], []

    tm = max(8, min(512, (400_000 // cols) // 8 * 8))
    delta, m, v = rowwise(f, [view(w), view(g), view(m), view(v)], [], [(cols, F32)] * 3, tm=tm, name=name)
    return delta.reshape(shape), m.reshape(shape), v.reshape(shape)


def _add_rows(fn, rows, cols, dtype, name):
    tm = max(8, min(512, (400_000 // cols) // 8 * 8))
    return rowwise(lambda *a: ([fn(*a)], []), rows, [], [(cols, dtype)], tm=tm, name=name)[0]


BIG = ("w_in", "w_uq", "w_ukv", "w_ret_proj", "w_mla_proj", "w_out")
COL_SHARDED = ("w_in", "w_uq", "w_ukv")


def kernel(x, c, positions, w_mod, b_mod, g_norm, w_in, g_cq, g_ckv, w_uq, w_ukv, w_ret_proj, w_mla_proj, w_out, g_final, loss_target, m_w_mod, m_b_mod, m_g_norm, m_w_in, m_g_cq, m_g_ckv, m_w_uq, m_w_ukv, m_w_ret_proj, m_w_mla_proj, m_w_out, m_g_final, v_w_mod, v_b_mod, v_g_norm, v_w_in, v_g_cq, v_g_ckv, v_w_uq, v_w_ukv, v_w_ret_proj, v_w_mla_proj, v_w_out, v_g_final):
    weights = dict(w_mod=w_mod, b_mod=b_mod, g_norm=g_norm, w_in=w_in, g_cq=g_cq, g_ckv=g_ckv, w_uq=w_uq, w_ukv=w_ukv,
                   w_ret_proj=w_ret_proj, w_mla_proj=w_mla_proj, w_out=w_out, g_final=g_final)
    m_in = dict(w_mod=m_w_mod, b_mod=m_b_mod, g_norm=m_g_norm, w_in=m_w_in, g_cq=m_g_cq, g_ckv=m_g_ckv, w_uq=m_w_uq,
                w_ukv=m_w_ukv, w_ret_proj=m_w_ret_proj, w_mla_proj=m_w_mla_proj, w_out=m_w_out, g_final=m_g_final)
    v_in = dict(w_mod=v_w_mod, b_mod=v_b_mod, g_norm=v_g_norm, w_in=v_w_in, g_cq=v_g_cq, g_ckv=v_g_ckv, w_uq=v_w_uq,
                w_ukv=v_w_ukv, w_ret_proj=v_w_ret_proj, w_mla_proj=v_w_mla_proj, w_out=v_w_out, g_final=v_g_final)
    order = ("w_mod", "b_mod", "g_norm", "w_in", "g_cq", "g_ckv", "w_uq", "w_ukv", "w_ret_proj", "w_mla_proj", "w_out", "g_final")

    x = x[0]
    target = loss_target[0]
    S, D = x.shape
    L = w_mod.shape[0]
    dm = Dims(S, D, L, w_ret_proj.shape[1] * 4 // RET_DV, w_mla_proj.shape[1] * 4 // MLA_DV, g_cq.shape[1], g_ckv.shape[1])
    my_x, my_y, my_c = _me()
    my_chip = 2 * my_x + my_y
    my_dev = 2 * my_chip + my_c
    C3 = w_mod.shape[2]

    for table in (weights, m_in, v_in):
        table["w_in"] = jnp.swapaxes(table["w_in"], 1, 2)
    prepped = [
        cast_split(weights["w_in"], BF16, "prep_w_in"),
        dm.pad_uq_rows(transpose_split(w_uq, BF16, "prep_w_uq")),
        transpose_split(w_ukv, BF16, "prep_w_ukv"),
        cast_split(w_ret_proj, BF16, "prep_w_ret"),
        cast_split(w_mla_proj, BF16, "prep_w_mla"),
        cast_split(w_out, BF16, "prep_w_out"),
    ]
    places = dm.placements()
    zero_rows = jnp.zeros((places[0].zero_rows[1], D // 2), BF16)
    w_keys = ("in", "uq", "ukv", "ret", "mla", "out")
    layer_w = [dict(zip(w_keys, gather_layer(prepped, zero_rows, places, 0, "l0_gather_w")))]

    c_all = allgather8(c, "gather_c").reshape(8, D)
    (c_act,) = rowwise(lambda z: ([_silu(z)], []), [c_all], [], [(D, BF16)], tm=8, name="silu_c")
    mod_part = jnp.stack([matmul(c_act, w_mod[l], out_dtype=F32, name=f"l{l}_mm_mod", tn=C3) for l in range(L)])
    mod_all = allgather8(mod_part, "gather_mod")
    mod_all = mod_all.reshape(4, 2, L, 8, C3)[:, 0].transpose(1, 2, 0, 3).reshape(L, 8, 3 * D) + b_mod[:, None, :]
    mod = lax.dynamic_index_in_dim(mod_all, my_dev, axis=1, keepdims=False)

    pos = positions[0]
    tabs = (*_rope_tables(pos, RET_DK, LANE), *_rope_tables(pos, MLA_ROPE, LANE))

    def vecs(l):
        return (mod[l : l + 1, :D], mod[l : l + 1, D : 2 * D], mod[l : l + 1, 2 * D :],
                g_norm[l : l + 1], g_cq[l : l + 1], g_ckv[l : l + 1])

    saved = []
    for l in range(L):
        host = gather_layer(prepped, zero_rows, places, l + 1, None) if l + 1 < L else None
        x, sv, hosted = layer_fwd(dm, l, x, *vecs(l), layer_w[l], tabs, host=host)
        saved.append(sv)
        if host:
            layer_w.append(dict(zip(w_keys, hosted)))

    def f_loss(x, t, g):
        xn = x * _rms(x)
        err = xn * g - t
        dy = err * (1.0 / D)
        dxn = dy * g
        dx = _rms(x) * (dxn - xn * jnp.mean(dxn * xn, axis=1, keepdims=True))
        part = jnp.sum(jnp.sum(err * err, axis=1, keepdims=True), axis=0, keepdims=True) * (0.5 / D)
        return [dx], [jnp.broadcast_to(part, (1, LANE)), _sum0(dy * xn)]

    dx, loss_part, dg_final = rowwise(f_loss, [x, target], [g_final.reshape(1, D)], [(D, F32)], [LANE, D], tm=256, name="loss_head")

    stacks = {n: None for n in BIG}
    dvec = {n: [None] * L for n in ("mod", "g_norm", "g_cq", "g_ckv")}

    def finish_grads(l, from_chips):
        mine = [
            _add_rows(lambda a: ((a[3].astype(F32) + a[0].astype(F32)) + a[1].astype(F32)) + a[2].astype(F32), [r], r.shape[-1],
                      F32, f"l{l}_rs_add4_{n}")
            for n, r in zip(BIG, from_chips)
        ]
        theirs = sibling_swap(mine, f"l{l}_rs_share")
        for n, a, b in zip(BIG, mine, theirs):
            if n == "w_uq":
                a, b = dm.unpad_uq_rows(a), dm.unpad_uq_rows(b)
            finish = untranspose_halves if n in ("w_uq", "w_ukv") else merge_halves
            stacks[n] = finish(a, b, my_c, stacks[n], l, L, f"l{l}_grad_{n}")

    pair = None
    for l in reversed(range(L)):
        host = scatter_to_chips(pair, places, None) if pair is not None else None
        dx, dw_l, dv_l, hosted = layer_bwd(dm, l, dx, saved[l], *vecs(l), layer_w[l], tabs, host=host)
        if host:
            finish_grads(l + 1, hosted)
        for n in dvec:
            dvec[n][l] = dv_l[n]
        partial = [dw_l[n] for n in BIG]
        from_sibling = sibling_scatter(partial, f"l{l}_rs_cores")
        pair = [add_pair(g, r, my_c, f"l{l}_rs_add2_{n}") for n, g, r in zip(BIG, partial, from_sibling)]
    finish_grads(0, scatter_to_chips(pair, places, "l0_rs_chips"))
    grad_x = dx[None]

    pieces = [loss_part] + [jnp.concatenate(dvec[n], axis=1) for n in ("mod", "g_norm", "g_cq", "g_ckv")] + [dg_final]
    widths = [p.shape[1] for p in pieces]
    small_all = allgather8(jnp.concatenate(pieces, axis=1), "gather_small").reshape(8, sum(widths))

    def sum8_body(a_ref, o_ref):
        acc = a_ref[0:1, :]
        for d in range(1, 8):
            acc = acc + a_ref[d : d + 1, :]
        o_ref[...] = acc

    small = _pcall(sum8_body, name="sum_small", out_shape=jax.ShapeDtypeStruct((1, sum(widths)), F32))(small_all)
    offs = np.cumsum([0] + widths)
    loss = small[0, 0]
    g_small = {
        "b_mod": small[0, offs[1] : offs[2]].reshape(L, 3 * D),
        "g_norm": small[0, offs[2] : offs[3]].reshape(L, D),
        "g_cq": small[0, offs[3] : offs[4]].reshape(L, dm.QR),
        "g_ckv": small[0, offs[4] : offs[5]].reshape(L, dm.KR),
        "g_final": small[0, offs[5] : offs[6]],
    }

    dmod_all = small_all[:, offs[1] : offs[2]].reshape(8, L, 3 * D)
    dmod_mine = lax.dynamic_slice_in_dim(dmod_all, my_chip * C3, C3, axis=2)
    pad8 = lambda a: jnp.pad(a, ((0, LANE - 8), (0, 0)))
    grads = dict(g_small)
    grads["w_mod"] = jnp.stack(
        [matmul(pad8(c_act), pad8(dmod_mine[:, l]), ta=True, out_dtype=F32, name=f"l{l}_mm_dw_mod", tn=C3) for l in range(L)]
    )

    for n in BIG:
        grads[n] = stacks[n]

    deltas, new_m, new_v = {}, {}, {}
    for n in order:
        wv, gv, mv, vv = weights[n], grads[n], m_in[n], v_in[n]
        if wv.ndim == 1:
            wv, gv, mv, vv = (a.reshape(1, -1) for a in (wv, gv, mv, vv))
        d_, m_, v_ = adamw(wv, gv, mv, vv, f"adamw_{n}")
        deltas[n], new_m[n], new_v[n] = (a.reshape(weights[n].shape) for a in (d_, m_, v_))
        grads[n] = grads[n].reshape(weights[n].shape)
    for table in (grads, deltas, new_m, new_v):
        table["w_in"] = jnp.swapaxes(table["w_in"], 1, 2)

    return (loss, grad_x, *[grads[n] for n in order], *[deltas[n] for n in order], *[new_m[n] for n in order],
            *[new_v[n] for n in order])
```

```python
import functools

import jax
import jax.numpy as jnp
import numpy as np
from jax import lax
from jax.experimental import pallas as pl
from jax.experimental.pallas import tpu as pltpu

F32 = jnp.float32
BF16 = jnp.bfloat16
MESH = pl.DeviceIdType.MESH

VMEM_LIMIT_BYTES = 52 * 1024 * 1024
LANE = 128

CHUNK = 64
EPS = 1e-6
NEG_INF = -1e30
ROPE_BASE = 10000.0
RET_DK = 128
RET_DV = 256
MLA_NOPE = 128
MLA_ROPE = 64
MLA_DV = 128
MLA_QW = 256
MLA_SCALE = float((MLA_NOPE + MLA_ROPE) ** -0.5)
LOG2E = float(np.log2(np.e))
LN2 = float(np.log(2.0))
ATTN_BLOCK = 512
ATTN_HEADS_PER_STEP = 2
RET_BLOCK = 256
RET_HEADS_PER_STEP = 4

ADAM_LR = 0.001
ADAM_B1 = 0.9
ADAM_B2 = 0.999
ADAM_EPS = 1e-08
ADAM_WD = 0.01
ADAM_STEP = 10


def _pcall(body, **kw):
    return pl.pallas_call(body, **kw)


def _params(n_grid):
    return pltpu.CompilerParams(dimension_semantics=("arbitrary",) * n_grid, vmem_limit_bytes=VMEM_LIMIT_BYTES)


def _pick(dim, target, mult=LANE):
    if dim <= target:
        return dim
    best = None
    for t in range(mult, target + 1, mult):
        if dim % t == 0:
            best = t
    assert best is not None, (dim, target, mult)
    return best


def matmul(
    a, b, *, ta=False, tb=False, b_split=False, out_split=False, out_dtype=BF16, name, tm=512, tn=1024, tk=2048, host=None
):
    (M, K) = (a.shape[1], a.shape[0]) if ta else a.shape
    b_rows, b_cols = (b.shape[1], 2 * b.shape[2]) if b_split else b.shape
    (K2, N) = (b_cols, b_rows) if tb else (b_rows, b_cols)
    assert K == K2, (a.shape, b.shape, ta, tb)
    whole_k = b_split and tb and tk >= K and not ta
    whole_n = out_split and tn >= N and not (b_split and not tb)
    whole_bn = b_split and not tb and tn >= N and not out_split
    n_cap = N if (whole_n or whole_bn) else (N // 2 if (out_split or (b_split and not tb)) else N)
    k_cap = K if whole_k else (K // 2 if (b_split and tb) else K)
    tm, tn, tk = _pick(M, tm, 8 if M < LANE else LANE), _pick(n_cap, min(tn, n_cap)), _pick(k_cap, min(tk, k_cap))
    nk = K // tk
    njh, nkh = max((N // 2) // tn, 1), max((K // 2) // tk, 1)
    dn = (((0 if ta else 1,), (1 if tb else 0,)), ((), ()))

    def body(a_ref, b_ref, o_ref, *scratch):
        if whole_k:
            a_blk = a_ref[...].astype(BF16)
            prod = lax.dot_general(a_blk[:, : K // 2], b_ref[0].astype(BF16), dn, preferred_element_type=F32)
            prod += lax.dot_general(a_blk[:, K // 2 :], b_ref[1].astype(BF16), dn, preferred_element_type=F32)
        elif whole_bn:
            a_blk = a_ref[...].astype(BF16)
            halves = [lax.dot_general(a_blk, b_ref[half].astype(BF16), dn, preferred_element_type=F32) for half in range(2)]
            prod = jnp.concatenate(halves, axis=1)
        else:
            b_blk = b_ref[0] if b_split else b_ref[...]
            prod = lax.dot_general(a_ref[...].astype(BF16), b_blk.astype(BF16), dn, preferred_element_type=F32)

        def store(v):
            if whole_n:
                o_ref[0] = v[:, : N // 2].astype(o_ref.dtype)
                o_ref[1] = v[:, N // 2 :].astype(o_ref.dtype)
            elif out_split:
                o_ref[0] = v.astype(o_ref.dtype)
            else:
                o_ref[...] = v.astype(o_ref.dtype)

        if nk == 1:
            store(prod)
            return
        (acc_ref,) = scratch
        k = pl.program_id(2)

        @pl.when(k == 0)
        def _():
            acc_ref[...] = prod

        @pl.when(k > 0)
        def _():
            acc_ref[...] += prod

        @pl.when(k == nk - 1)
        def _():
            store(acc_ref[...])

    a_spec = pl.BlockSpec((tk, tm), lambda i, j, k: (k, i)) if ta else pl.BlockSpec((tm, tk), lambda i, j, k: (i, k))
    if whole_k:
        b_spec = pl.BlockSpec((2, tn, K // 2), lambda i, j, k: (0, j, 0))
    elif whole_bn:
        b_spec = pl.BlockSpec((2, tk, N // 2), lambda i, j, k: (0, k, 0))
    elif b_split and tb:
        b_spec = pl.BlockSpec((1, tn, tk), lambda i, j, k: (k // nkh, j, k % nkh))
    elif b_split:
        b_spec = pl.BlockSpec((1, tk, tn), lambda i, j, k: (j // njh, k, j % njh))
    elif tb:
        b_spec = pl.BlockSpec((tn, tk), lambda i, j, k: (j, k))
    else:
        b_spec = pl.BlockSpec((tk, tn), lambda i, j, k: (k, j))
    if whole_n:
        out_spec = pl.BlockSpec((2, tm, N // 2), lambda i, j, k: (0, i, 0))
        out_shape = jax.ShapeDtypeStruct((2, M, N // 2), out_dtype)
    elif out_split:
        out_spec = pl.BlockSpec((1, tm, tn), lambda i, j, k: (j // njh, i, j % njh))
        out_shape = jax.ShapeDtypeStruct((2, M, N // 2), out_dtype)
    else:
        out_spec = pl.BlockSpec((tm, tn), lambda i, j, k: (i, j))
        out_shape = jax.ShapeDtypeStruct((M, N), out_dtype)
    grid = (M // tm, N // tn, nk)
    acc_shapes = [] if nk == 1 else [pltpu.VMEM((tm, tn), F32)]
    if host is None:
        return _pcall(
            body, name=name, grid=grid, in_specs=[a_spec, b_spec], out_specs=out_spec, out_shape=out_shape,
            scratch_shapes=acc_shapes, compiler_params=_params(3),
        )(a, b)

    n_hi, n_ho, n_steps = len(host.ins), len(host.out_shapes), grid[0] * grid[1] * grid[2]

    def hosting_body(a_ref, b_ref, *rest):
        host_in, rest = rest[:n_hi], rest[n_hi:]
        o_ref, rest = rest[0], rest[1:]
        host_out, rest = rest[:n_ho], rest[n_ho:]
        scratch, host_sems = rest[: len(acc_shapes)], rest[len(acc_shapes) :]
        step = (pl.program_id(0) * grid[1] + pl.program_id(1)) * grid[2] + pl.program_id(2)
        refs = (host_in, host_out, host_sems)

        @pl.when(step == 0)
        def _():
            host.start(*refs)

        @pl.when(step == n_steps // 2)
        def _():
            host.relay(*refs)

        body(a_ref, b_ref, o_ref, *scratch)

        @pl.when(step == n_steps - 1)
        def _():
            host.finish(*refs)

    return _pcall(
        hosting_body,
        name=name,
        grid=grid,
        in_specs=[a_spec, b_spec, *host.in_specs],
        out_specs=[out_spec, *host.out_specs],
        out_shape=[out_shape, *host.out_shapes],
        scratch_shapes=[*acc_shapes, *host.scratch],
        compiler_params=_params(3),
    )(a, b, *host.ins)


def win(arr, off, width):
    assert off % width == 0 and off + width <= arr.shape[1], (arr.shape, off, width)
    return (arr, off // width, width)


def rowwise(fn, rows, vecs, outs, accs=(), *, tm, name):
    rows = [r if isinstance(r, tuple) else (r, 0, r.shape[-1]) for r in rows]
    S = rows[0][0].shape[-2]
    tm = _pick(S, tm, 8)
    n_rows, n_vecs, n_outs = len(rows), len(vecs), len(outs)

    def body(*refs):
        ins = [r[...] for r in refs[: n_rows + n_vecs]]
        out_refs = refs[n_rows + n_vecs : n_rows + n_vecs + n_outs]
        acc_refs = refs[n_rows + n_vecs + n_outs :]
        res, acc = fn(*ins)
        for r, v in zip(out_refs, res, strict=True):
            r[...] = v.astype(r.dtype)
        if acc_refs:
            i = pl.program_id(0)

            @pl.when(i == 0)
            def _():
                for r, v in zip(acc_refs, acc, strict=True):
                    r[...] = v

            @pl.when(i > 0)
            def _():
                for r, v in zip(acc_refs, acc, strict=True):
                    r[...] += v

    in_specs = []
    for arr, blk, w in rows:
        if arr.ndim == 3:
            in_specs.append(pl.BlockSpec((arr.shape[0], tm, w), lambda i: (0, i, 0)))
        else:
            in_specs.append(pl.BlockSpec((tm, w), functools.partial(lambda i, blk: (i, blk), blk=blk)))
    for v in vecs:
        in_specs.append(pl.BlockSpec(v.shape, functools.partial(lambda i, nd: (0,) * nd, nd=v.ndim)))
    out_specs = [pl.BlockSpec((tm, w), lambda i: (i, 0)) for w, _ in outs]
    out_specs += [pl.BlockSpec((1, w), lambda i: (0, 0)) for w in accs]
    out_shape = [jax.ShapeDtypeStruct((S, w), dt) for w, dt in outs]
    out_shape += [jax.ShapeDtypeStruct((1, w), F32) for w in accs]
    res = _pcall(
        body,
        name=name,
        grid=(S // tm,),
        in_specs=in_specs,
        out_specs=out_specs,
        out_shape=out_shape,
        compiler_params=_params(1),
    )(*[r[0] for r in rows], *vecs)
    return res


def _sum0(v):
    return jnp.sum(v, axis=0, keepdims=True)


def _sigmoid(z):
    return 1.0 / (1.0 + jnp.exp(-z))


def _silu(z):
    return z * _sigmoid(z)


def _dsilu(z):
    s = _sigmoid(z)
    return s * (1.0 + z * (1.0 - s))


def _swap_half(x, half):
    if 2 * half == LANE:
        return pltpu.roll(x, half, 1)
    lane = lax.broadcasted_iota(jnp.int32, x.shape, 1)
    return jnp.where((lane % (2 * half)) < half, pltpu.roll(x, LANE - half, 1), pltpu.roll(x, half, 1))


def _rope_tile(x, cosf, sinf, half, inverse=False):
    sw = _swap_half(x, half)
    return x * cosf - sw * sinf if inverse else x * cosf + sw * sinf


def _tri_tables(n, by_key):
    if by_key:
        pairs = [(i, j) for j in range(n) for i in range(j, n)]
    else:
        pairs = [(i, j) for i in range(n) for j in range(i + 1)]
    return (np.array([p[0] for p in pairs], np.int32), np.array([p[1] for p in pairs], np.int32))


def _attn_scores(q, kn, kr, i, j, tq, masked):
    k = jnp.concatenate([kn, kr], axis=1)
    s = lax.dot_general(q, k, (((1,), (1,)), ((), ())), preferred_element_type=F32)
    if masked:
        rc = (i * tq + lax.broadcasted_iota(jnp.int32, s.shape, 0)) // CHUNK
        cc = (j * tq + lax.broadcasted_iota(jnp.int32, s.shape, 1)) // CHUNK
        s = jnp.where(cc <= rc, s, NEG_INF)
    return s, k


def attn_fwd(q, kv, kr, proj, mg_off, *, heads, tq, name, host=None):
    S = q.shape[0]
    tq = _pick(S, tq, CHUNK)
    n = S // tq
    it, jt = _tri_tables(n, by_key=False)
    T = len(it)
    mg_blk = mg_off // MLA_DV
    n_hi, n_ho = (len(host.ins), len(host.out_shapes)) if host else (0, 0)

    hp = ATTN_HEADS_PER_STEP
    assert heads % hp == 0 and mg_blk % hp == 0
    groups = heads // hp

    def body(it_ref, jt_ref, q_ref, kv_ref, kr_ref, mg_ref, *rest):
        host_in, rest = rest[:n_hi], rest[n_hi:]
        (o_ref, a_ref, lse_ref), rest = rest[:3], rest[3:]
        host_out, rest = rest[:n_ho], rest[n_ho:]
        (m_sc, l_sc, acc_sc), host_sems = rest[:3], rest[3:]
        h, t = pl.program_id(0), pl.program_id(1)
        i, j = it_ref[t], jt_ref[t]
        if host:
            _host_steps(host, (host_in, host_out, host_sems), h, t, groups, T, before=True)

        @pl.when(j == 0)
        def _():
            m_sc[...] = jnp.full(m_sc.shape, NEG_INF, F32)
            l_sc[...] = jnp.zeros(l_sc.shape, F32)
            acc_sc[...] = jnp.zeros(acc_sc.shape, F32)

        def step(masked):
            kr = kr_ref[...]
            for u in range(hp):
                q = q_ref[:, u * MLA_QW : (u + 1) * MLA_QW]
                kn = kv_ref[:, u * MLA_QW : u * MLA_QW + MLA_NOPE]
                v = kv_ref[:, u * MLA_QW + MLA_NOPE : (u + 1) * MLA_QW]
                s, _ = _attn_scores(q, kn, kr, i, j, tq, masked)
                m_prev = m_sc[u]
                m_new = jnp.maximum(m_prev, jnp.max(s, axis=1, keepdims=True))
                p = jnp.exp2((s - m_new).astype(BF16))
                alpha = jnp.exp2(m_prev - m_new)
                ones = jnp.ones((p.shape[1], LANE), BF16)
                l_sc[u] = alpha * l_sc[u] + jnp.dot(p, ones, preferred_element_type=F32)
                acc_sc[u] = alpha * acc_sc[u] + jnp.dot(p, v, preferred_element_type=F32)
                m_sc[u] = m_new

        @pl.when(j < i)
        def _():
            step(False)

        @pl.when(j == i)
        def _():
            step(True)
            for u in range(hp):
                cols = slice(u * MLA_DV, (u + 1) * MLA_DV)
                l = l_sc[u]
                o = acc_sc[u] / l
                o_ref[:, cols] = o
                a_ref[:, cols] = (o * _silu(mg_ref[:, cols].astype(F32))).astype(a_ref.dtype)
                lse_ref[u] = m_sc[u] + jnp.log2(l[:, :1])

        if host:
            _host_steps(host, (host_in, host_out, host_sems), h, t, groups, T, before=False)

    grid_spec = pltpu.PrefetchScalarGridSpec(
        num_scalar_prefetch=2,
        grid=(groups, T),
        in_specs=[
            pl.BlockSpec((tq, hp * MLA_QW), lambda h, t, it, jt: (it[t], h)),
            pl.BlockSpec((tq, hp * MLA_QW), lambda h, t, it, jt: (jt[t], h)),
            pl.BlockSpec((tq, LANE), lambda h, t, it, jt: (jt[t], 0)),
            pl.BlockSpec((tq, hp * MLA_DV), lambda h, t, it, jt: (it[t], mg_blk // hp + h)),
            *(host.in_specs if host else []),
        ],
        out_specs=[
            pl.BlockSpec((tq, hp * MLA_DV), lambda h, t, it, jt: (it[t], h)),
            pl.BlockSpec((tq, hp * MLA_DV), lambda h, t, it, jt: (it[t], h)),
            pl.BlockSpec((hp, tq, 1), lambda h, t, it, jt: (h, it[t], 0)),
            *(host.out_specs if host else []),
        ],
        scratch_shapes=[
            pltpu.VMEM((hp, tq, 1), F32),
            pltpu.VMEM((hp, tq, LANE), F32),
            pltpu.VMEM((hp, tq, MLA_DV), F32),
            *(host.scratch if host else []),
        ],
    )
    return _pcall(
        body,
        name=name,
        grid_spec=grid_spec,
        out_shape=[
            jax.ShapeDtypeStruct((S, heads * MLA_DV), F32),
            jax.ShapeDtypeStruct((S, heads * MLA_DV), BF16),
            jax.ShapeDtypeStruct((heads, S, 1), F32),
            *(host.out_shapes if host else []),
        ],
        compiler_params=_params(2),
    )(jnp.asarray(it), jnp.asarray(jt), q, kv, kr, proj, *(host.ins if host else []))


def attn_bwd(q, kv, kr, do, o, lse, *, heads, tq, name, host=None):
    S = q.shape[0]
    tq = _pick(S, tq, CHUNK)
    n = S // tq
    it, jt = _tri_tables(n, by_key=True)
    T = len(it)
    n_hi, n_ho = (len(host.ins), len(host.out_shapes)) if host else (0, 0)

    hp = ATTN_HEADS_PER_STEP
    assert heads % hp == 0
    groups = heads // hp

    def body(it_ref, jt_ref, q_ref, kv_ref, kr_ref, do_ref, o_ref, lse_ref, *rest):
        host_in, rest = rest[:n_hi], rest[n_hi:]
        (dq_ref, dkv_ref, dkr_ref), rest = rest[:3], rest[3:]
        host_out, rest = rest[:n_ho], rest[n_ho:]
        (dq_acc, dk_sc, dv_sc), host_sems = rest[:3], rest[3:]
        h, t = pl.program_id(0), pl.program_id(1)
        i, j = it_ref[t], jt_ref[t]
        if host:
            _host_steps(host, (host_in, host_out, host_sems), h, t, groups, T, before=True)

        @pl.when(t == 0)
        def _():
            dq_acc[...] = jnp.zeros(dq_acc.shape, F32)

        @pl.when(i == j)
        def _():
            dk_sc[...] = jnp.zeros(dk_sc.shape, F32)
            dv_sc[...] = jnp.zeros(dv_sc.shape, F32)

        def step(masked):
            kr = kr_ref[...]
            rows = pl.ds(pl.multiple_of(i * tq, tq), tq)
            for u in range(hp):
                wide, narrow = slice(u * MLA_QW, (u + 1) * MLA_QW), slice(u * MLA_DV, (u + 1) * MLA_DV)
                q_blk, do_blk = q_ref[:, wide], do_ref[:, narrow]
                kn = kv_ref[:, u * MLA_QW : u * MLA_QW + MLA_NOPE]
                v = kv_ref[:, u * MLA_QW + MLA_NOPE : (u + 1) * MLA_QW]
                s, k = _attn_scores(q_blk, kn, kr, i, j, tq, masked)
                p = jnp.exp2(s - lse_ref[u])
                delta = jnp.sum(do_blk.astype(F32) * o_ref[:, narrow], axis=1, keepdims=True)
                dp = lax.dot_general(do_blk, v, _NT, preferred_element_type=F32)
                ds = (p * (dp - delta)).astype(BF16)
                dv_sc[u] += lax.dot_general(p.astype(BF16), do_blk, _TN, preferred_element_type=F32)
                dk_sc[u] += lax.dot_general(ds, q_blk, _TN, preferred_element_type=F32)
                dq_acc[rows, wide] += jnp.dot(ds, k, preferred_element_type=F32)

        @pl.when(i == j)
        def _():
            step(True)

        @pl.when(i > j)
        def _():
            step(False)

        @pl.when(i == n - 1)
        def _():
            for u in range(hp):
                dkv_ref[:, u * MLA_QW : u * MLA_QW + MLA_NOPE] = (dk_sc[u, :, :MLA_NOPE] * LN2).astype(dkv_ref.dtype)
                dkv_ref[:, u * MLA_QW + MLA_NOPE : (u + 1) * MLA_QW] = dv_sc[u].astype(dkv_ref.dtype)
                dkr_ref[u] = dk_sc[u, :, MLA_NOPE:] * LN2

        @pl.when(t == T - 1)
        def _():
            dq_ref[...] = (dq_acc[...] * MLA_SCALE).astype(dq_ref.dtype)

        if host:
            _host_steps(host, (host_in, host_out, host_sems), h, t, groups, T, before=False)

    grid_spec = pltpu.PrefetchScalarGridSpec(
        num_scalar_prefetch=2,
        grid=(groups, T),
        in_specs=[
            pl.BlockSpec((tq, hp * MLA_QW), lambda h, t, it, jt: (it[t], h)),
            pl.BlockSpec((tq, hp * MLA_QW), lambda h, t, it, jt: (jt[t], h)),
            pl.BlockSpec((tq, LANE), lambda h, t, it, jt: (jt[t], 0)),
            pl.BlockSpec((tq, hp * MLA_DV), lambda h, t, it, jt: (it[t], h)),
            pl.BlockSpec((tq, hp * MLA_DV), lambda h, t, it, jt: (it[t], h)),
            pl.BlockSpec((hp, tq, 1), lambda h, t, it, jt: (h, it[t], 0)),
            *(host.in_specs if host else []),
        ],
        out_specs=[
            pl.BlockSpec((S, hp * MLA_QW), lambda h, t, it, jt: (0, h)),
            pl.BlockSpec((tq, hp * MLA_QW), lambda h, t, it, jt: (jt[t], h)),
            pl.BlockSpec((hp, tq, LANE), lambda h, t, it, jt: (h, jt[t], 0)),
            *(host.out_specs if host else []),
        ],
        scratch_shapes=[
            pltpu.VMEM((S, hp * MLA_QW), F32),
            pltpu.VMEM((hp, tq, MLA_QW), F32),
            pltpu.VMEM((hp, tq, MLA_DV), F32),
            *(host.scratch if host else []),
        ],
    )
    return _pcall(
        body,
        name=name,
        grid_spec=grid_spec,
        out_shape=[
            jax.ShapeDtypeStruct((S, heads * MLA_QW), BF16),
            jax.ShapeDtypeStruct((S, heads * (MLA_NOPE + MLA_DV)), BF16),
            jax.ShapeDtypeStruct((heads, S, LANE), F32),
            *(host.out_shapes if host else []),
        ],
        compiler_params=_params(2),
    )(jnp.asarray(it), jnp.asarray(jt), q, kv, kr, do, o, lse, *(host.ins if host else []))


def _host_steps(host, refs, h, t, heads, n_steps, before):
    if before:

        @pl.when((h == 0) & (t == 0))
        def _():
            host.start(*refs)

        @pl.when((h == heads // 2) & (t == 0))
        def _():
            host.relay(*refs)

    else:

        @pl.when((h == heads - 1) & (t == n_steps - 1))
        def _():
            host.finish(*refs)


def _ret_consts(heads):
    h = np.arange(heads, dtype=np.float32)
    lg = np.log(np.float32(1.0) - np.float32(2.0) ** (np.float32(-5.0) - h)).astype(np.float32)
    idx = np.arange(CHUNK, dtype=np.float32)
    dmat = np.exp(np.abs(idx[:, None] - idx[None, :])[None] * lg[:, None, None]).astype(np.float32)
    xi = np.exp((idx + 1.0)[None, :] * lg[:, None]).astype(np.float32)
    zeta = np.exp((CHUNK - 1.0 - idx)[None, :] * lg[:, None]).astype(np.float32)
    dec = np.exp(np.float32(CHUNK) * lg).astype(np.float32)
    xi = np.broadcast_to(xi[:, :, None], (heads, CHUNK, RET_DK)).copy()
    zeta = np.broadcast_to(zeta[:, :, None], (heads, CHUNK, RET_DK)).copy()
    dec = np.broadcast_to(dec[:, None, None], (heads, 8, LANE)).copy()
    return jnp.asarray(dmat), jnp.asarray(xi), jnp.asarray(zeta), jnp.asarray(dec)


_NT = (((1,), (1,)), ((), ()))
_TN = (((0,), (0,)), ((), ()))


def _dot(a, b, dn=(((1,), (0,)), ((), ()))):
    return lax.dot_general(a.astype(BF16), b.astype(BF16), dn, preferred_element_type=F32)


def _const_specs(hp):
    return [
        pl.BlockSpec((hp, CHUNK, CHUNK), lambda h, b: (h, 0, 0)),
        pl.BlockSpec((hp, CHUNK, RET_DK), lambda h, b: (h, 0, 0)),
        pl.BlockSpec((hp, CHUNK, RET_DK), lambda h, b: (h, 0, 0)),
        pl.BlockSpec((hp, 8, LANE), lambda h, b: (h, 0, 0)),
    ]


def _ret_group(heads, *blks):
    hp = min(RET_HEADS_PER_STEP, heads)
    assert heads % hp == 0 and all(b % hp == 0 for b in blks)
    return hp


def ret_fwd(q, k, proj, v_off, rg_off, *, heads, tb, name):
    S = q.shape[0]
    tb = _pick(S, tb, CHUNK)
    cpb, nb = tb // CHUNK, S // tb
    v_blk, rg_blk = v_off // RET_DV, rg_off // RET_DV
    hp = _ret_group(heads, v_blk, rg_blk)

    def body(q_ref, k_ref, v_ref, rg_ref, dm_ref, xi_ref, ze_ref, dec_ref, o_ref, a_ref, st_ref, r_sc):
        @pl.when(pl.program_id(1) == 0)
        def _():
            r_sc[...] = jnp.zeros(r_sc.shape, F32)

        for c in range(cpb):
            sl = pl.ds(c * CHUNK, CHUNK)
            for u in range(hp):
                dm, xi, ze, dec = dm_ref[u], xi_ref[u], ze_ref[u], dec_ref[u, 0:1, 0:1]
                narrow, wide = slice(u * RET_DK, (u + 1) * RET_DK), slice(u * RET_DV, (u + 1) * RET_DV)
                qc, kc, vc = q_ref[sl, narrow], k_ref[sl, narrow], v_ref[sl, wide]
                r = r_sc[u]
                st_ref[u, c] = r.astype(BF16)
                s = _dot(qc, kc, _NT) * dm
                o = _dot(s, vc) + _dot(qc.astype(F32) * xi, r)
                r_sc[u] = r * dec + _dot(kc.astype(F32) * ze, vc, _TN)
                mu = jnp.mean(o, axis=1, keepdims=True)
                d = o - mu
                n = d * lax.rsqrt(jnp.mean(d * d, axis=1, keepdims=True) + EPS)
                o_ref[sl, wide] = o
                a_ref[sl, wide] = (n * _silu(rg_ref[sl, wide].astype(F32))).astype(a_ref.dtype)

    return _pcall(
        body,
        name=name,
        grid=(heads // hp, nb),
        in_specs=[
            pl.BlockSpec((tb, hp * RET_DK), lambda h, b: (b, h)),
            pl.BlockSpec((tb, hp * RET_DK), lambda h, b: (b, h)),
            pl.BlockSpec((tb, hp * RET_DV), lambda h, b: (b, v_blk // hp + h)),
            pl.BlockSpec((tb, hp * RET_DV), lambda h, b: (b, rg_blk // hp + h)),
            *_const_specs(hp),
        ],
        out_specs=[
            pl.BlockSpec((tb, hp * RET_DV), lambda h, b: (b, h)),
            pl.BlockSpec((tb, hp * RET_DV), lambda h, b: (b, h)),
            pl.BlockSpec((hp, cpb, RET_DK, RET_DV), lambda h, b: (h, b, 0, 0)),
        ],
        out_shape=[
            jax.ShapeDtypeStruct((S, heads * RET_DV), F32),
            jax.ShapeDtypeStruct((S, heads * RET_DV), BF16),
            jax.ShapeDtypeStruct((heads, S // CHUNK, RET_DK, RET_DV), BF16),
        ],
        scratch_shapes=[pltpu.VMEM((hp, RET_DK, RET_DV), F32)],
        compiler_params=_params(2),
    )(q, k, proj, proj, *_ret_consts(heads))


def ret_bwd(q, k, proj, v_off, states, do, *, heads, tb, name):
    S = q.shape[0]
    tb = _pick(S, tb, CHUNK)
    cpb, nb = tb // CHUNK, S // tb
    v_blk = v_off // RET_DV
    hp = _ret_group(heads, v_blk)

    def body(q_ref, k_ref, v_ref, st_ref, do_ref, dm_ref, xi_ref, ze_ref, dec_ref, dq_ref, dk_ref, dv_ref, dr_sc):
        @pl.when(pl.program_id(1) == 0)
        def _():
            dr_sc[...] = jnp.zeros(dr_sc.shape, F32)

        for c in reversed(range(cpb)):
            sl = pl.ds(c * CHUNK, CHUNK)
            for u in range(hp):
                dm, xi, ze, dec = dm_ref[u], xi_ref[u], ze_ref[u], dec_ref[u, 0:1, 0:1]
                narrow, wide = slice(u * RET_DK, (u + 1) * RET_DK), slice(u * RET_DV, (u + 1) * RET_DV)
                qc, kc, vc, doc = q_ref[sl, narrow], k_ref[sl, narrow], v_ref[sl, wide], do_ref[sl, wide]
                r_prev = st_ref[u, c]
                dr = dr_sc[u]
                a = _dot(qc, kc, _NT) * dm
                ds = _dot(doc, vc, _NT) * dm
                kz = kc.astype(F32) * ze
                dq_ref[sl, narrow] = (_dot(ds, kc) + _dot(doc, r_prev, _NT) * xi).astype(dq_ref.dtype)
                dk_ref[sl, narrow] = (_dot(ds, qc, _TN) + _dot(vc, dr, _NT) * ze).astype(dk_ref.dtype)
                dv_ref[sl, wide] = (_dot(a, doc, _TN) + _dot(kz, dr)).astype(dv_ref.dtype)
                dr_sc[u] = dr * dec + _dot(qc.astype(F32) * xi, doc, _TN)

    rev = lambda h, b: (nb - 1 - b, h)
    return _pcall(
        body,
        name=name,
        grid=(heads // hp, nb),
        in_specs=[
            pl.BlockSpec((tb, hp * RET_DK), rev),
            pl.BlockSpec((tb, hp * RET_DK), rev),
            pl.BlockSpec((tb, hp * RET_DV), lambda h, b: (nb - 1 - b, v_blk // hp + h)),
            pl.BlockSpec((hp, cpb, RET_DK, RET_DV), lambda h, b: (h, nb - 1 - b, 0, 0)),
            pl.BlockSpec((tb, hp * RET_DV), rev),
            *_const_specs(hp),
        ],
        out_specs=[
            pl.BlockSpec((tb, hp * RET_DK), rev),
            pl.BlockSpec((tb, hp * RET_DK), rev),
            pl.BlockSpec((tb, hp * RET_DV), rev),
        ],
        out_shape=[
            jax.ShapeDtypeStruct((S, heads * RET_DK), BF16),
            jax.ShapeDtypeStruct((S, heads * RET_DK), BF16),
            jax.ShapeDtypeStruct((S, heads * RET_DV), BF16),
        ],
        scratch_shapes=[pltpu.VMEM((hp, RET_DK, RET_DV), F32)],
        compiler_params=_params(2),
    )(q, k, proj, states, do, *_ret_consts(heads))


def _me():
    return (lax.axis_index("x"), lax.axis_index("y"), lax.axis_index("c"))


def _comm(name, ins, out_shapes, n_local, n_remote, plan):
    return run_exchange(name, Exchange(ins, out_shapes, n_local, n_remote, plan))


def run_exchange(name, ex):
    def body(*refs):
        ex.start(*ex.split(refs))
        ex.relay(*ex.split(refs))
        ex.finish(*ex.split(refs))

    return _pcall(
        body, name=name, in_specs=ex.in_specs, out_specs=ex.out_specs, out_shape=ex.out_shapes, scratch_shapes=ex.scratch
    )(*ex.ins)


class Exchange:
    def __init__(self, ins, out_shapes, n_local, n_remote, plan):
        self.ins, self.out_shapes, self.n_local, self.n_remote, self.plan = list(ins), list(out_shapes), n_local, n_remote, plan
        any_spec = pl.BlockSpec(memory_space=pl.ANY)
        self.in_specs, self.out_specs = [any_spec] * len(self.ins), [any_spec] * len(self.out_shapes)
        self.scratch = [
            pltpu.SemaphoreType.DMA((n_remote,)),
            pltpu.SemaphoreType.DMA((n_remote,)),
            pltpu.SemaphoreType.DMA((max(n_local, 1),)),
        ]

    def split(self, refs):
        n_in, n_out = len(self.ins), len(self.out_shapes)
        return refs[:n_in], refs[n_in : n_in + n_out], refs[n_in + n_out :]

    def _copies(self, in_refs, out_refs, sems):
        send_sems, recv_sems, local_sems = sems
        me = _me()
        local_plan, remote_plan = self.plan(me, in_refs, out_refs)
        assert len(local_plan) == self.n_local and len(remote_plan) == self.n_remote, (len(local_plan), len(remote_plan))
        def local_copy(n):
            src, dst = local_plan[n]
            return pltpu.make_async_copy(src, dst, local_sems.at[n])

        def send(n):
            src, dst, peer, _, _ = remote_plan[n]
            return pltpu.make_async_remote_copy(
                src_ref=src, dst_ref=dst, send_sem=send_sems.at[n], recv_sem=recv_sems.at[n], device_id=peer, device_id_type=MESH
            )

        def arrival(n):
            src, _, _, landing, _ = remote_plan[n]
            return pltpu.make_async_remote_copy(
                src_ref=src, dst_ref=landing, send_sem=send_sems.at[n], recv_sem=recv_sems.at[n], device_id=me, device_id_type=MESH
            )

        after = [a for (_, _, _, _, a) in remote_plan]
        assert all(a is None or a < n for n, a in enumerate(after))
        return local_copy, send, arrival, after

    def start(self, in_refs, out_refs, sems):
        local_copy, send, _, after = self._copies(in_refs, out_refs, sems)
        for n in range(self.n_local):
            local_copy(n).start()
        for n, a in enumerate(after):
            if a is None:
                send(n).start()

    def relay(self, in_refs, out_refs, sems):
        _, send, arrival, after = self._copies(in_refs, out_refs, sems)
        for n in sorted({a for a in after if a is not None}):
            arrival(n).wait_recv()
            for m, a in enumerate(after):
                if a == n:
                    send(m).start()

    def finish(self, in_refs, out_refs, sems):
        local_copy, send, arrival, after = self._copies(in_refs, out_refs, sems)
        relayed = {a for a in after if a is not None}
        for n in range(self.n_remote):
            if n not in relayed:
                arrival(n).wait_recv()
        for n in range(self.n_remote):
            send(n).wait_send()
        for n in range(self.n_local):
            local_copy(n).wait()


_CHIP_FLIPS = ((1, 0), (0, 1), (1, 1))


class Placement:
    def __init__(self, rows, full_rows, cuts=(), offset=None, zero_rows=None):
        edges = [0, *sorted(cuts), rows]
        self.rows, self.full_rows = rows, full_rows
        self.runs = [(a, b - a) for a, b in zip(edges[:-1], edges[1:]) if b > a]
        self.offset = offset if offset is not None else (lambda j, start: j * rows + start)
        self.zero_rows = zero_rows

    def at(self, ref_2d_plus, j, start, size):
        return pl.ds(pl.multiple_of(self.offset(j, start), 16), size)


def gather_layer(prepped, zeros, places, l, name):
    n_w = len(prepped)
    n_chip = sum(len(p.runs) for p in places) * 3
    n_own = sum(len(p.runs) for p in places) * 2 + sum(2 for p in places if p.zero_rows)

    def plan(me, in_refs, out_refs):
        x, y, c = me
        j = 2 * x + y
        sib = (x, y, 1 - c)
        from_chips, own, forwards = [], [], []
        for i, o, p in zip(in_refs[:n_w], out_refs, places):
            for start, size in p.runs:
                for dx, dy in _CHIP_FLIPS:
                    px, py = x ^ dx, y ^ dy
                    jp = 2 * px + py
                    mine_there = o.at[c, p.at(o, j, start, size)]
                    theirs_here = o.at[c, p.at(o, jp, start, size)]
                    n = len(from_chips)
                    from_chips.append((i.at[l, c, pl.ds(start, size)], mine_there, (px, py, c), theirs_here, None))
                    forwards.append((theirs_here, theirs_here, sib, o.at[1 - c, p.at(o, jp, start, size)], n))
                for h in range(2):
                    place = o.at[h, p.at(o, j, start, size)]
                    own.append((i.at[l, h, pl.ds(start, size)], place, sib, place, None))
            if p.zero_rows:
                for h in range(2):
                    place = o.at[h, pl.ds(p.zero_rows[0], p.zero_rows[1])]
                    own.append((in_refs[n_w], place, sib, place, None))
        return [], from_chips + own + forwards

    shapes = [jax.ShapeDtypeStruct((2, p.full_rows, w.shape[3]), w.dtype) for w, p in zip(prepped, places)]
    ins = [*prepped, zeros] if any(p.zero_rows for p in places) else list(prepped)
    ex = Exchange(ins, shapes, 0, 2 * n_chip + n_own, plan)
    return ex if name is None else run_exchange(name, ex)


def sibling_scatter(arrays, name):
    def plan(me, in_refs, out_refs):
        x, y, c = me
        return [], [(i.at[1 - c], o, (x, y, 1 - c), o, None) for i, o in zip(in_refs, out_refs)]

    shapes = [jax.ShapeDtypeStruct(a.shape[1:], a.dtype) for a in arrays]
    ex = Exchange(arrays, shapes, 0, len(arrays), plan)
    return ex if name is None else run_exchange(name, ex)


def sibling_swap(arrays, name):
    def plan(me, in_refs, out_refs):
        x, y, c = me
        return [], [(i, o, (x, y, 1 - c), o, None) for i, o in zip(in_refs, out_refs)]

    shapes = [jax.ShapeDtypeStruct(a.shape, a.dtype) for a in arrays]
    return _comm(name, arrays, shapes, 0, len(arrays), plan)


def scatter_to_chips(arrays, places, name):
    def plan(me, in_refs, out_refs):
        x, y, c = me
        j = 2 * x + y
        local_plan, remote_plan = [], []
        for i, o, p in zip(in_refs, out_refs, places):
            for start, size in p.runs:
                local_plan.append((i.at[p.at(i, j, start, size)], o.at[3, pl.ds(start, size)]))
                for k, (dx, dy) in enumerate(_CHIP_FLIPS):
                    px, py = x ^ dx, y ^ dy
                    landing = o.at[k, pl.ds(start, size)]
                    remote_plan.append((i.at[p.at(i, 2 * px + py, start, size)], landing, (px, py, c), landing, None))
        return local_plan, remote_plan

    n_runs = sum(len(p.runs) for p in places)
    shapes = [jax.ShapeDtypeStruct((4, p.rows, a.shape[1]), a.dtype) for a, p in zip(arrays, places)]
    ex = Exchange(arrays, shapes, n_runs, 3 * n_runs, plan)
    return ex if name is None else run_exchange(name, ex)


def allgather8(block, name):
    def plan(me, in_refs, out_refs):
        x, y, c = me
        (i,), (o,) = in_refs, out_refs
        mine = 4 * x + 2 * y + c
        remote_plan = []
        for flip in range(1, 8):
            px, py, pc = x ^ (flip >> 2), y ^ ((flip >> 1) & 1), c ^ (flip & 1)
            remote_plan.append((i, o.at[mine], (px, py, pc), o.at[4 * px + 2 * py + pc], None))
        return [(i, o.at[mine])], remote_plan

    return _comm(name, [block], [jax.ShapeDtypeStruct((8, *block.shape), block.dtype)], 1, 7, plan)[0]


def transpose_split(a, out_dtype, name):
    L, R, C = a.shape
    rh = R // 2
    rt = _pick(rh, 256)
    n = rh // rt

    def body(x_ref, o_ref):
        o_ref[0, 0] = x_ref[0].T.astype(o_ref.dtype)

    return _pcall(
        body,
        name=name,
        grid=(L, 2, n),
        in_specs=[pl.BlockSpec((1, rt, C), lambda l, h, i: (l, h * n + i, 0))],
        out_specs=pl.BlockSpec((1, 1, C, rt), lambda l, h, i: (l, h, 0, i)),
        out_shape=jax.ShapeDtypeStruct((L, 2, C, rh), out_dtype),
        compiler_params=_params(3),
    )(a)


def cast_split(a, out_dtype, name):
    L, R, C = a.shape
    rt = _pick(R, 512, 16)

    def body(x_ref, o_ref):
        o_ref[0, 0] = x_ref[0].astype(o_ref.dtype)

    return _pcall(
        body,
        name=name,
        grid=(L, 2, R // rt),
        in_specs=[pl.BlockSpec((1, rt, C // 2), lambda l, h, i: (l, i, h))],
        out_specs=pl.BlockSpec((1, 1, rt, C // 2), lambda l, h, i: (l, h, i, 0)),
        out_shape=jax.ShapeDtypeStruct((L, 2, R, C // 2), out_dtype),
        compiler_params=_params(3),
    )(a)


def untranspose_halves(mine, theirs, my_c, stack, l, n_layers, name):
    C, rh = mine.shape
    n_in = 2 if stack is None else 3
    rt = _pick(rh, 256)
    n = rh // rt

    def body(c_ref, a_ref, b_ref, *rest):
        o_ref = rest[-1]
        h = pl.program_id(0)
        o_ref[0] = jnp.where(h == c_ref[0], a_ref[...], b_ref[...]).T

    grid_spec = pltpu.PrefetchScalarGridSpec(
        num_scalar_prefetch=1,
        grid=(2, n),
        in_specs=[pl.BlockSpec((C, rt), lambda h, i, c: (0, i))] * 2
        + ([] if stack is None else [pl.BlockSpec(memory_space=pl.ANY)]),
        out_specs=pl.BlockSpec((1, rt, C), lambda h, i, c: (l, h * n + i, 0)),
    )
    return _pcall(
        body,
        name=name,
        grid_spec=grid_spec,
        out_shape=jax.ShapeDtypeStruct((n_layers, 2 * rh, C), F32),
        input_output_aliases={} if stack is None else {n_in: 0},
        compiler_params=_params(2),
    )(my_c.reshape(1), mine, theirs, *([] if stack is None else [stack]))


def merge_halves(mine, theirs, my_c, stack, l, n_layers, name):
    R, ch = mine.shape
    n_in = 2 if stack is None else 3
    rt = _pick(R, 512, 8)

    def body(c_ref, a_ref, b_ref, *rest):
        o_ref = rest[-1]
        o_ref[0] = jnp.where(pl.program_id(0) == c_ref[0], a_ref[...], b_ref[...])

    grid_spec = pltpu.PrefetchScalarGridSpec(
        num_scalar_prefetch=1,
        grid=(2, R // rt),
        in_specs=[pl.BlockSpec((rt, ch), lambda h, i, c: (i, 0))] * 2
        + ([] if stack is None else [pl.BlockSpec(memory_space=pl.ANY)]),
        out_specs=pl.BlockSpec((1, rt, ch), lambda h, i, c: (l, i, h)),
    )
    return _pcall(
        body,
        name=name,
        grid_spec=grid_spec,
        out_shape=jax.ShapeDtypeStruct((n_layers, R, 2 * ch), F32),
        input_output_aliases={} if stack is None else {n_in: 0},
        compiler_params=_params(2),
    )(my_c.reshape(1), mine, theirs, *([] if stack is None else [stack]))


def add_pair(g, r, my_c, name):
    _, rows, w = g.shape
    tm = _pick(rows, max(16, min(1024, (1 << 20) // w // 16 * 16)), 16)

    def body(c_ref, g_ref, r_ref, o_ref):
        o_ref[...] = (g_ref[0].astype(F32) + r_ref[...].astype(F32)).astype(o_ref.dtype)

    grid_spec = pltpu.PrefetchScalarGridSpec(
        num_scalar_prefetch=1,
        grid=(rows // tm,),
        in_specs=[pl.BlockSpec((1, tm, w), lambda i, c: (c[0], i, 0)), pl.BlockSpec((tm, w), lambda i, c: (i, 0))],
        out_specs=pl.BlockSpec((tm, w), lambda i, c: (i, 0)),
    )
    return _pcall(
        body, name=name, grid_spec=grid_spec, out_shape=jax.ShapeDtypeStruct((rows, w), BF16), compiler_params=_params(1)
    )(my_c.reshape(1), g, r)


class Dims:
    def __init__(self, S, D, L, ret_heads, mla_heads, q_rank, kv_rank):
        self.S, self.D, self.L, self.HR, self.HM, self.QR, self.KR = S, D, L, ret_heads, mla_heads, q_rank, kv_rank
        self.RQ, self.RV, self.MV = ret_heads * RET_DK, ret_heads * RET_DV, mla_heads * MLA_DV
        self.D_IN = 2 * self.RQ + 2 * self.RV + q_rank + kv_rank + MLA_ROPE + self.MV + 2 * D
        self.lo = 2 * self.RQ + 2 * self.RV
        self.mid = q_rank + kv_rank + MLA_ROPE
        self.DP = self.D_IN + LANE - MLA_ROPE
        self.o_rq, self.o_rk, self.o_rv, self.o_rg = 0, self.RQ, 2 * self.RQ, 2 * self.RQ + self.RV
        self.o_mg = self.lo
        self.o_bga = self.o_mg + self.MV
        self.o_bgb = self.o_bga + D
        self.o_cq = self.o_bgb + D
        self.o_ckv = self.o_cq + q_rank
        self.o_kr = self.o_ckv + kv_rank

    def pad_uq_rows(self, w):
        heads = w.shape[-2] // (MLA_NOPE + MLA_ROPE)
        w = w.reshape(*w.shape[:-2], heads, MLA_NOPE + MLA_ROPE, w.shape[-1])
        w = jnp.pad(w, [(0, 0)] * (w.ndim - 2) + [(0, MLA_QW - MLA_NOPE - MLA_ROPE), (0, 0)])
        return w.reshape(*w.shape[:-3], heads * MLA_QW, w.shape[-1])

    def unpad_uq_rows(self, w):
        heads = w.shape[-2] // MLA_QW
        w = w.reshape(*w.shape[:-2], heads, MLA_QW, w.shape[-1])[..., : MLA_NOPE + MLA_ROPE, :]
        return w.reshape(*w.shape[:-3], heads * (MLA_NOPE + MLA_ROPE), w.shape[-1])

    def placements(self):
        rows_in = self.D_IN // 4
        lo, mid, n_hi = self.lo, self.mid, self.D_IN - self.lo - self.mid
        cuts = {b % rows_in for b in (lo, lo + mid)} - {0}

        def offset_in(j, start):
            g = j * rows_in + start
            return jnp.where(g < lo, g, jnp.where(g < lo + mid, g + n_hi, g - mid))

        return [
            Placement(rows_in, self.DP, cuts, offset_in, zero_rows=(self.D_IN, self.DP - self.D_IN)),
            Placement(self.HM * MLA_QW // 4, self.HM * MLA_QW),
            Placement(self.HM * (MLA_NOPE + MLA_DV) // 4, self.HM * (MLA_NOPE + MLA_DV)),
            Placement(self.RV // 4, self.RV),
            Placement(self.MV // 4, self.MV),
            Placement(self.D // 4, self.D),
        ]


def _rope_tables(positions, dim, width):
    inv = 1.0 / (ROPE_BASE ** (jnp.arange(0, dim, 2, dtype=F32) / dim))
    ang = positions.astype(F32)[:, None] * inv
    cos, sin = jnp.cos(ang), jnp.sin(ang)
    pad = jnp.zeros((positions.shape[0], width - dim), F32)
    return jnp.concatenate([cos, cos, pad], axis=1), jnp.concatenate([-sin, sin, pad], axis=1)


def _tiles(x):
    return [x[:, t * LANE : (t + 1) * LANE] for t in range(x.shape[1] // LANE)]


def _cat(parts):
    return parts[0] if len(parts) == 1 else jnp.concatenate(parts, axis=1)


def _rms(x, eps=EPS):
    return lax.rsqrt(jnp.mean(x * x, axis=1, keepdims=True) + eps)


def layer_fwd(dm, l, x, shift, scale, gate, g_norm, g_cq, g_ckv, w, tabs, host=None, host_in=None):
    cos_r, sin_r, cos_m, sin_m = tabs
    nm = lambda s: f"l{l}_{s}"

    def f_norm(x, g, scale, shift):
        return [x * _rms(x) * g * (1.0 + scale) + shift], []

    (h,) = rowwise(f_norm, [x], [g_norm, scale, shift], [(dm.D, BF16)], tm=256, name=nm("norm"))
    proj, *late = _as_list(matmul(h, w["in"], tb=True, b_split=True, name=nm("mm_in"), tn=1920, host=host_in))
    if host_in is not None:
        w = dict(w, **dict(zip(("uq", "ukv", "ret", "mla", "out"), late)))

    def f_rope_ret(rq, rk, cos, sin):
        rq, rk = rq.astype(F32), rk.astype(F32)
        q = _cat([_rope_tile(t, cos, sin, RET_DK // 2) for t in _tiles(rq)])
        k = _cat([_rope_tile(t, cos, sin, RET_DK // 2) * (RET_DK**-0.5) for t in _tiles(rk)])
        return [q, k], []

    rq, rk = rowwise(
        f_rope_ret,
        [win(proj, dm.o_rq, dm.RQ), win(proj, dm.o_rk, dm.RQ), cos_r, sin_r],
        [],
        [(dm.RQ, BF16)] * 2,
        tm=512,
        name=nm("rope_ret"),
    )
    o_ret, a_ret, states = ret_fwd(rq, rk, proj, dm.o_rv, dm.o_rg, heads=dm.HR, tb=RET_BLOCK, name=nm("ret_fwd"))

    def f_prep(cq, ckv, kr, cos, sin, g_cq, g_ckv):
        cq, ckv, kr = cq.astype(F32), ckv.astype(F32), kr.astype(F32)
        return [cq * _rms(cq) * g_cq, ckv * _rms(ckv) * g_ckv, _rope_tile(kr, cos, sin, MLA_ROPE // 2)], []

    cqn, ckvn, krr = rowwise(
        f_prep,
        [win(proj, dm.o_cq, dm.QR), win(proj, dm.o_ckv, dm.KR), win(proj, dm.o_kr, LANE), cos_m, sin_m],
        [g_cq, g_ckv],
        [(dm.QR, BF16), (dm.KR, BF16), (LANE, BF16)],
        tm=512,
        name=nm("mla_prep"),
    )
    q_raw = matmul(cqn, w["uq"], tb=True, b_split=True, name=nm("mm_uq"), tn=4096)
    kv = matmul(ckvn, w["ukv"], tb=True, b_split=True, name=nm("mm_ukv"), tn=4096)

    def f_rope_q(q, cos, sin):
        t = _tiles(q.astype(F32))
        rot = [t[n] if n % 2 == 0 else _rope_tile(t[n], cos, sin, MLA_ROPE // 2) for n in range(len(t))]
        return [_cat([r * (MLA_SCALE * LOG2E) for r in rot])], []

    (q,) = rowwise(f_rope_q, [q_raw, cos_m, sin_m], [], [(dm.HM * MLA_QW, BF16)], tm=256, name=nm("rope_q"))
    o_mla, a_mla, lse, *hosted = attn_fwd(
        q, kv, krr, proj, dm.o_mg, heads=dm.HM, tq=ATTN_BLOCK, name=nm("attn_fwd"), host=host
    )

    y_ret = matmul(a_ret, w["ret"], b_split=True, out_dtype=F32, name=nm("mm_ret"))
    y_mla = matmul(a_mla, w["mla"], b_split=True, out_dtype=F32, name=nm("mm_mla"))

    def f_merge(y_ret, y_mla, bga, bgb):
        return [_sigmoid(bga.astype(F32)) * y_ret + _sigmoid(bgb.astype(F32)) * y_mla], []

    (merged,) = rowwise(
        f_merge, [y_ret, y_mla, win(proj, dm.o_bga, dm.D), win(proj, dm.o_bgb, dm.D)], [], [(dm.D, BF16)], tm=256, name=nm("merge")
    )
    out = matmul(merged, w["out"], b_split=True, out_dtype=F32, name=nm("mm_out"))

    def f_resid(x, out, gate):
        return [x + gate * out], []

    (x_new,) = rowwise(f_resid, [x, out], [gate], [(dm.D, F32)], tm=256, name=nm("resid"))
    saved = dict(
        x=x, h=h, proj=proj, rq=rq, rk=rk, o_ret=o_ret, a_ret=a_ret, states=states, cqn=cqn, ckvn=ckvn, krr=krr, q=q, kv=kv,
        o_mla=o_mla, a_mla=a_mla, lse=lse, y_ret=y_ret, y_mla=y_mla, merged=merged, out=out,
    )
    return x_new, saved, hosted, w


def _as_list(v):
    return list(v) if isinstance(v, (list, tuple)) else [v]


def layer_bwd(dm, l, dx_out, sv, shift, scale, gate, g_norm, g_cq, g_ckv, w, tabs, host=None, exchange_dh=None):
    cos_r, sin_r, cos_m, sin_m = tabs
    nm = lambda s: f"l{l}_{s}"
    proj = sv["proj"]

    def b_resid(dx, out, gate):
        return [dx * gate], [_sum0(dx * out)]

    dout, d_gate = rowwise(b_resid, [dx_out, sv["out"]], [gate], [(dm.D, BF16)], [dm.D], tm=256, name=nm("resid_bwd"))
    dmerged = matmul(dout, w["out"], tb=True, b_split=True, name=nm("mm_dmerged"))
    dw_out = matmul(sv["merged"], dout, ta=True, out_split=True, tn=2048, name=nm("mm_dw_out"))

    def b_merge(dmg, y_ret, y_mla, bga, bgb):
        dmg = dmg.astype(F32)
        ga, gb = _sigmoid(bga.astype(F32)), _sigmoid(bgb.astype(F32))
        return [dmg * ga, dmg * gb, dmg * y_ret * ga * (1.0 - ga), dmg * y_mla * gb * (1.0 - gb)], []

    dy_ret, dy_mla, d_bga, d_bgb = rowwise(
        b_merge,
        [dmerged, sv["y_ret"], sv["y_mla"], win(proj, dm.o_bga, dm.D), win(proj, dm.o_bgb, dm.D)],
        [],
        [(dm.D, BF16)] * 4,
        tm=256,
        name=nm("merge_bwd"),
    )
    da_ret = matmul(dy_ret, w["ret"], tb=True, b_split=True, name=nm("mm_da_ret"))
    dw_ret = matmul(sv["a_ret"], dy_ret, ta=True, out_split=True, tn=2048, name=nm("mm_dw_ret"))
    da_mla = matmul(dy_mla, w["mla"], tb=True, b_split=True, name=nm("mm_da_mla"))
    dw_mla = matmul(sv["a_mla"], dy_mla, ta=True, out_split=True, tn=2048, name=nm("mm_dw_mla"))

    def b_ret_gate(da, rg, o):
        da, rg = da.astype(F32), rg.astype(F32)
        do_parts, drg_parts = [], []
        for hh in range(dm.HR):
            sl = slice(hh * RET_DV, (hh + 1) * RET_DV)
            oh, dah, rgh = o[:, sl], da[:, sl], rg[:, sl]
            mu = jnp.mean(oh, axis=1, keepdims=True)
            d = oh - mu
            r = lax.rsqrt(jnp.mean(d * d, axis=1, keepdims=True) + EPS)
            n = d * r
            dn = dah * _silu(rgh)
            drg_parts.append(dah * n * _dsilu(rgh))
            do_parts.append(r * (dn - jnp.mean(dn, axis=1, keepdims=True) - n * jnp.mean(dn * n, axis=1, keepdims=True)))
        return [_cat(do_parts), _cat(drg_parts)], []

    do_ret, d_rg = rowwise(
        b_ret_gate, [da_ret, win(proj, dm.o_rg, dm.RV), sv["o_ret"]], [], [(dm.RV, BF16)] * 2, tm=256, name=nm("ret_gate_bwd")
    )
    dq_rot, dk_rot, d_rv = ret_bwd(sv["rq"], sv["rk"], proj, dm.o_rv, sv["states"], do_ret, heads=dm.HR, tb=RET_BLOCK, name=nm("ret_bwd"))

    def b_rope_ret(dq, dk, cos, sin):
        dq, dk = dq.astype(F32), dk.astype(F32)
        q = _cat([_rope_tile(t, cos, sin, RET_DK // 2, inverse=True) for t in _tiles(dq)])
        k = _cat([_rope_tile(t, cos, sin, RET_DK // 2, inverse=True) * (RET_DK**-0.5) for t in _tiles(dk)])
        return [q, k], []

    d_rq, d_rk = rowwise(b_rope_ret, [dq_rot, dk_rot, cos_r, sin_r], [], [(dm.RQ, BF16)] * 2, tm=512, name=nm("rope_ret_bwd"))

    def b_mla_gate(da, mg, o):
        da, mg = da.astype(F32), mg.astype(F32)
        return [da * _silu(mg), da * o * _dsilu(mg)], []

    do_mla, d_mg = rowwise(
        b_mla_gate, [da_mla, win(proj, dm.o_mg, dm.MV), sv["o_mla"]], [], [(dm.MV, BF16)] * 2, tm=256, name=nm("mla_gate_bwd")
    )
    dq_att, dkv, dkr_heads, *hosted = attn_bwd(
        sv["q"], sv["kv"], sv["krr"], do_mla, sv["o_mla"], sv["lse"], heads=dm.HM, tq=ATTN_BLOCK, name=nm("attn_bwd"), host=host
    )

    def b_rope_q(dq, cos, sin):
        t = _tiles(dq.astype(F32))
        return [_cat([t[n] if n % 2 == 0 else _rope_tile(t[n], cos, sin, MLA_ROPE // 2, inverse=True) for n in range(len(t))])], []

    (dq_raw,) = rowwise(b_rope_q, [dq_att, cos_m, sin_m], [], [(dm.HM * MLA_QW, BF16)], tm=256, name=nm("rope_q_bwd"))
    dcqn = matmul(dq_raw, w["uq"], b_split=True, name=nm("mm_dcqn"))
    dw_uq = matmul(dq_raw, sv["cqn"], ta=True, out_split=True, tn=2048, name=nm("mm_dw_uq"))
    dckvn = matmul(dkv, w["ukv"], b_split=True, name=nm("mm_dckvn"))
    dw_ukv = matmul(dkv, sv["ckvn"], ta=True, out_split=True, tn=2048, name=nm("mm_dw_ukv"))

    def b_prep(dcqn, dckvn, cq, ckv, cos, sin, dkr_h, g_cq, g_ckv):
        outs, accs = [], []
        for dn, z, g in ((dcqn, cq, g_cq), (dckvn, ckv, g_ckv)):
            dn, z = dn.astype(F32), z.astype(F32)
            n = z * _rms(z)
            dng = dn * g
            outs.append(_rms(z) * (dng - n * jnp.mean(dng * n, axis=1, keepdims=True)))
            accs.append(_sum0(dn * n))
        dkr = dkr_h[0]
        for hh in range(1, dm.HM):
            dkr = dkr + dkr_h[hh]
        outs.append(_rope_tile(dkr, cos, sin, MLA_ROPE // 2, inverse=True))
        return outs, accs

    d_cq, d_ckv, d_kr, dg_cq, dg_ckv = rowwise(
        b_prep,
        [dcqn, dckvn, win(proj, dm.o_cq, dm.QR), win(proj, dm.o_ckv, dm.KR), cos_m, sin_m, dkr_heads],
        [g_cq, g_ckv],
        [(dm.QR, BF16), (dm.KR, BF16), (LANE, BF16)],
        [dm.QR, dm.KR],
        tm=256,
        name=nm("mla_prep_bwd"),
    )

    dproj = jnp.concatenate([d_rq, d_rk, d_rv, d_rg, d_mg, d_bga, d_bgb, d_cq, d_ckv, d_kr], axis=1)
    dw_in = matmul(dproj, sv["h"], ta=True, out_split=True, tn=2048, name=nm("mm_dw_in"))
    dws = dict(w_in=dw_in, w_uq=dw_uq, w_ukv=dw_ukv, w_ret_proj=dw_ret, w_mla_proj=dw_mla, w_out=dw_out)
    host_dh = exchange_dh(dws) if exchange_dh else None
    dh, *hosted_dh = _as_list(
        matmul(dproj, w["in"], b_split=True, out_dtype=F32, name=nm("mm_dh"), tn=2048, tk=1920, host=host_dh)
    )

    def b_norm(dh, x, dx_res, g, scale):
        r = _rms(x)
        xn = x * r
        dxn = dh * g * (1.0 + scale)
        dx = dx_res + r * (dxn - xn * jnp.mean(dxn * xn, axis=1, keepdims=True))
        return [dx], [_sum0(dh), _sum0(dh * xn * g), _sum0(dh * (1.0 + scale) * xn)]

    dx, d_shift, d_scale, dg_norm = rowwise(
        b_norm, [dh, sv["x"], dx_out], [g_norm, scale], [(dm.D, F32)], [dm.D] * 3, tm=256, name=nm("norm_bwd")
    )
    dvec = dict(mod=jnp.concatenate([d_shift, d_scale, d_gate], axis=1), g_norm=dg_norm, g_cq=dg_cq, g_ckv=dg_ckv)
    return dx, dws, dvec, hosted, hosted_dh


def adamw(w, g, m, v, name):
    shape = w.shape
    cols = shape[-1]
    view = lambda a: a.reshape(-1, cols)

    def f(w, g, m, v):
        m = ADAM_B1 * m + (1.0 - ADAM_B1) * g
        v = ADAM_B2 * v + (1.0 - ADAM_B2) * (g * g)
        m_hat = m / (1.0 - ADAM_B1**ADAM_STEP)
        v_hat = v / (1.0 - ADAM_B2**ADAM_STEP)
        delta = -ADAM_LR * (m_hat / (jnp.sqrt(v_hat) + ADAM_EPS) + ADAM_WD * w)
        return [delta, m, v], []

    tm = max(8, min(512, (400_000 // cols) // 8 * 8))
    delta, m, v = rowwise(f, [view(w), view(g), view(m), view(v)], [], [(cols, F32)] * 3, tm=tm, name=name)
    return delta.reshape(shape), m.reshape(shape), v.reshape(shape)


def _add_rows(fn, rows, cols, dtype, name):
    tm = max(8, min(512, (400_000 // cols) // 8 * 8))
    return rowwise(lambda *a: ([fn(*a)], []), rows, [], [(cols, dtype)], tm=tm, name=name)[0]


BIG = ("w_in", "w_uq", "w_ukv", "w_ret_proj", "w_mla_proj", "w_out")
COL_SHARDED = ("w_in", "w_uq", "w_ukv")


def kernel(x, c, positions, w_mod, b_mod, g_norm, w_in, g_cq, g_ckv, w_uq, w_ukv, w_ret_proj, w_mla_proj, w_out, g_final, loss_target, m_w_mod, m_b_mod, m_g_norm, m_w_in, m_g_cq, m_g_ckv, m_w_uq, m_w_ukv, m_w_ret_proj, m_w_mla_proj, m_w_out, m_g_final, v_w_mod, v_b_mod, v_g_norm, v_w_in, v_g_cq, v_g_ckv, v_w_uq, v_w_ukv, v_w_ret_proj, v_w_mla_proj, v_w_out, v_g_final):
    weights = dict(w_mod=w_mod, b_mod=b_mod, g_norm=g_norm, w_in=w_in, g_cq=g_cq, g_ckv=g_ckv, w_uq=w_uq, w_ukv=w_ukv,
                   w_ret_proj=w_ret_proj, w_mla_proj=w_mla_proj, w_out=w_out, g_final=g_final)
    m_in = dict(w_mod=m_w_mod, b_mod=m_b_mod, g_norm=m_g_norm, w_in=m_w_in, g_cq=m_g_cq, g_ckv=m_g_ckv, w_uq=m_w_uq,
                w_ukv=m_w_ukv, w_ret_proj=m_w_ret_proj, w_mla_proj=m_w_mla_proj, w_out=m_w_out, g_final=m_g_final)
    v_in = dict(w_mod=v_w_mod, b_mod=v_b_mod, g_norm=v_g_norm, w_in=v_w_in, g_cq=v_g_cq, g_ckv=v_g_ckv, w_uq=v_w_uq,
                w_ukv=v_w_ukv, w_ret_proj=v_w_ret_proj, w_mla_proj=v_w_mla_proj, w_out=v_w_out, g_final=v_g_final)
    order = ("w_mod", "b_mod", "g_norm", "w_in", "g_cq", "g_ckv", "w_uq", "w_ukv", "w_ret_proj", "w_mla_proj", "w_out", "g_final")

    x = x[0]
    target = loss_target[0]
    S, D = x.shape
    L = w_mod.shape[0]
    dm = Dims(S, D, L, w_ret_proj.shape[1] * 4 // RET_DV, w_mla_proj.shape[1] * 4 // MLA_DV, g_cq.shape[1], g_ckv.shape[1])
    my_x, my_y, my_c = _me()
    my_chip = 2 * my_x + my_y
    my_dev = 2 * my_chip + my_c
    C3 = w_mod.shape[2]

    for table in (weights, m_in, v_in):
        table["w_in"] = jnp.swapaxes(table["w_in"], 1, 2)
    prepped = [
        cast_split(weights["w_in"], BF16, "prep_w_in"),
        dm.pad_uq_rows(transpose_split(w_uq, BF16, "prep_w_uq")),
        transpose_split(w_ukv, BF16, "prep_w_ukv"),
        cast_split(w_ret_proj, BF16, "prep_w_ret"),
        cast_split(w_mla_proj, BF16, "prep_w_mla"),
        cast_split(w_out, BF16, "prep_w_out"),
    ]
    places = dm.placements()
    zero_rows = jnp.zeros((places[0].zero_rows[1], D // 2), BF16)
    w_keys = ("in", "uq", "ukv", "ret", "mla", "out")
    layer_w = [dict(zip(w_keys, gather_layer(prepped[:1], zero_rows, places[:1], 0, "l0_gather_w_in")))]
    rest_of_layer0 = gather_layer(prepped[1:], None, places[1:], 0, None)

    c_all = allgather8(c, "gather_c").reshape(8, D)
    (c_act,) = rowwise(lambda z: ([_silu(z)], []), [c_all], [], [(D, BF16)], tm=8, name="silu_c")
    mod_part = jnp.stack([matmul(c_act, w_mod[l], out_dtype=F32, name=f"l{l}_mm_mod", tn=C3) for l in range(L)])
    mod_all = allgather8(mod_part, "gather_mod")
    mod_all = mod_all.reshape(4, 2, L, 8, C3)[:, 0].transpose(1, 2, 0, 3).reshape(L, 8, 3 * D) + b_mod[:, None, :]
    mod = lax.dynamic_index_in_dim(mod_all, my_dev, axis=1, keepdims=False)

    pos = positions[0]
    tabs = (*_rope_tables(pos, RET_DK, LANE), *_rope_tables(pos, MLA_ROPE, LANE))

    def vecs(l):
        return (mod[l : l + 1, :D], mod[l : l + 1, D : 2 * D], mod[l : l + 1, 2 * D :],
                g_norm[l : l + 1], g_cq[l : l + 1], g_ckv[l : l + 1])

    saved = []
    for l in range(L):
        host = gather_layer(prepped, zero_rows, places, l + 1, None) if l + 1 < L else None
        x, sv, hosted, layer_w[l] = layer_fwd(
            dm, l, x, *vecs(l), layer_w[l], tabs, host=host, host_in=rest_of_layer0 if l == 0 else None
        )
        saved.append(sv)
        if host:
            layer_w.append(dict(zip(w_keys, hosted)))

    def f_loss(x, t, g):
        xn = x * _rms(x)
        err = xn * g - t
        dy = err * (1.0 / D)
        dxn = dy * g
        dx = _rms(x) * (dxn - xn * jnp.mean(dxn * xn, axis=1, keepdims=True))
        part = jnp.sum(jnp.sum(err * err, axis=1, keepdims=True), axis=0, keepdims=True) * (0.5 / D)
        return [dx], [jnp.broadcast_to(part, (1, LANE)), _sum0(dy * xn)]

    dx, loss_part, dg_final = rowwise(f_loss, [x, target], [g_final.reshape(1, D)], [(D, F32)], [LANE, D], tm=256, name="loss_head")

    stacks = {n: None for n in BIG}
    dvec = {n: [None] * L for n in ("mod", "g_norm", "g_cq", "g_ckv")}

    def finish_grads(l, from_chips):
        mine = [
            _add_rows(lambda a: ((a[3].astype(F32) + a[0].astype(F32)) + a[1].astype(F32)) + a[2].astype(F32), [r], r.shape[-1],
                      F32, f"l{l}_rs_add4_{n}")
            for n, r in zip(BIG, from_chips)
        ]
        theirs = sibling_swap(mine, f"l{l}_rs_share")
        for n, a, b in zip(BIG, mine, theirs):
            if n == "w_uq":
                a, b = dm.unpad_uq_rows(a), dm.unpad_uq_rows(b)
            finish = untranspose_halves if n in ("w_uq", "w_ukv") else merge_halves
            stacks[n] = finish(a, b, my_c, stacks[n], l, L, f"l{l}_grad_{n}")

    def pair_sums(l, partial, from_sibling):
        return [add_pair(g, r, my_c, f"l{l}_rs_add2_{n}") for n, g, r in zip(BIG, partial, from_sibling)]

    def core_exchange(dws):
        return sibling_scatter([dws[n] for n in BIG], None)

    def last_exchanges(dws):
        partial = [dws[n] for n in BIG]
        return scatter_to_chips(pair_sums(0, partial, sibling_scatter(partial, "l0_rs_cores")), places, None)

    pair = None
    for l in reversed(range(L)):
        host = scatter_to_chips(pair, places, None) if pair is not None else None
        dx, dw_l, dv_l, hosted, hosted_dh = layer_bwd(
            dm, l, dx, saved[l], *vecs(l), layer_w[l], tabs, host=host, exchange_dh=core_exchange if l > 0 else last_exchanges
        )
        if host:
            finish_grads(l + 1, hosted)
        for n in dvec:
            dvec[n][l] = dv_l[n]
        if l > 0:
            pair = pair_sums(l, [dw_l[n] for n in BIG], hosted_dh)
        else:
            finish_grads(0, hosted_dh)
    grad_x = dx[None]

    pieces = [loss_part] + [jnp.concatenate(dvec[n], axis=1) for n in ("mod", "g_norm", "g_cq", "g_ckv")] + [dg_final]
    widths = [p.shape[1] for p in pieces]
    small_all = allgather8(jnp.concatenate(pieces, axis=1), "gather_small").reshape(8, sum(widths))

    def sum8_body(a_ref, o_ref):
        acc = a_ref[0:1, :]
        for d in range(1, 8):
            acc = acc + a_ref[d : d + 1, :]
        o_ref[...] = acc

    small = _pcall(sum8_body, name="sum_small", out_shape=jax.ShapeDtypeStruct((1, sum(widths)), F32))(small_all)
    offs = np.cumsum([0] + widths)
    loss = small[0, 0]
    g_small = {
        "b_mod": small[0, offs[1] : offs[2]].reshape(L, 3 * D),
        "g_norm": small[0, offs[2] : offs[3]].reshape(L, D),
        "g_cq": small[0, offs[3] : offs[4]].reshape(L, dm.QR),
        "g_ckv": small[0, offs[4] : offs[5]].reshape(L, dm.KR),
        "g_final": small[0, offs[5] : offs[6]],
    }

    dmod_all = small_all[:, offs[1] : offs[2]].reshape(8, L, 3 * D)
    dmod_mine = lax.dynamic_slice_in_dim(dmod_all, my_chip * C3, C3, axis=2)
    pad8 = lambda a: jnp.pad(a, ((0, LANE - 8), (0, 0)))
    grads = dict(g_small)
    grads["w_mod"] = jnp.stack(
        [matmul(pad8(c_act), pad8(dmod_mine[:, l]), ta=True, out_dtype=F32, name=f"l{l}_mm_dw_mod", tn=C3) for l in range(L)]
    )

    for n in BIG:
        grads[n] = stacks[n]

    deltas, new_m, new_v = {}, {}, {}
    for n in order:
        wv, gv, mv, vv = weights[n], grads[n], m_in[n], v_in[n]
        if wv.ndim == 1:
            wv, gv, mv, vv = (a.reshape(1, -1) for a in (wv, gv, mv, vv))
        d_, m_, v_ = adamw(wv, gv, mv, vv, f"adamw_{n}")
        deltas[n], new_m[n], new_v[n] = (a.reshape(weights[n].shape) for a in (d_, m_, v_))
        grads[n] = grads[n].reshape(weights[n].shape)
    for table in (grads, deltas, new_m, new_v):
        table["w_in"] = jnp.swapaxes(table["w_in"], 1, 2)

    return (loss, grad_x, *[grads[n] for n in order], *[deltas[n] for n in order], *[new_m[n] for n in order],
            *[new_v[n] for n in order])
```

```python
import functools

import jax
import jax.numpy as jnp
import numpy as np
from jax import lax
from jax.experimental import pallas as pl
from jax.experimental.pallas import tpu as pltpu

F32 = jnp.float32
BF16 = jnp.bfloat16
MESH = pl.DeviceIdType.MESH

VMEM_LIMIT_BYTES = 52 * 1024 * 1024
LANE = 128

CHUNK = 64
EPS = 1e-6
NEG_INF = -1e30
ROPE_BASE = 10000.0
RET_DK = 128
RET_DV = 256
MLA_NOPE = 128
MLA_ROPE = 64
MLA_DV = 128
MLA_QW = 256
MLA_SCALE = float((MLA_NOPE + MLA_ROPE) ** -0.5)
LOG2E = float(np.log2(np.e))
LN2 = float(np.log(2.0))
ATTN_BLOCK = 512
ATTN_FWD_HEADS_PER_STEP = 4
ATTN_HEADS_PER_STEP = 2
RET_BLOCK = 256
RET_HEADS_PER_STEP = 8

ADAM_LR = 0.001
ADAM_B1 = 0.9
ADAM_B2 = 0.999
ADAM_EPS = 1e-08
ADAM_WD = 0.01
ADAM_STEP = 10


def _pcall(body, **kw):
    return pl.pallas_call(body, **kw)


def _params(n_grid):
    return pltpu.CompilerParams(dimension_semantics=("arbitrary",) * n_grid, vmem_limit_bytes=VMEM_LIMIT_BYTES)


def _pick(dim, target, mult=LANE):
    if dim <= target:
        return dim
    best = None
    for t in range(mult, target + 1, mult):
        if dim % t == 0:
            best = t
    assert best is not None, (dim, target, mult)
    return best


def matmul(
    a, b, *, ta=False, tb=False, b_split=False, out_split=False, out_dtype=BF16, name, tm=512, tn=1024, tk=2048, host=None
):
    (M, K) = (a.shape[1], a.shape[0]) if ta else a.shape
    b_rows, b_cols = (b.shape[1], 2 * b.shape[2]) if b_split else b.shape
    (K2, N) = (b_cols, b_rows) if tb else (b_rows, b_cols)
    assert K == K2, (a.shape, b.shape, ta, tb)
    whole_k = b_split and tb and tk >= K and not ta
    whole_n = out_split and tn >= N and not (b_split and not tb)
    whole_bn = b_split and not tb and tn >= N and not out_split
    n_cap = N if (whole_n or whole_bn) else (N // 2 if (out_split or (b_split and not tb)) else N)
    k_cap = K if whole_k else (K // 2 if (b_split and tb) else K)
    tm, tn, tk = _pick(M, tm, 8 if M < LANE else LANE), _pick(n_cap, min(tn, n_cap)), _pick(k_cap, min(tk, k_cap))
    nk = K // tk
    njh, nkh = max((N // 2) // tn, 1), max((K // 2) // tk, 1)
    dn = (((0 if ta else 1,), (1 if tb else 0,)), ((), ()))

    def body(a_ref, b_ref, o_ref, *scratch):
        if whole_k:
            a_blk = a_ref[...].astype(BF16)
            prod = lax.dot_general(a_blk[:, : K // 2], b_ref[0].astype(BF16), dn, preferred_element_type=F32)
            prod += lax.dot_general(a_blk[:, K // 2 :], b_ref[1].astype(BF16), dn, preferred_element_type=F32)
        elif whole_bn:
            a_blk = a_ref[...].astype(BF16)
            halves = [lax.dot_general(a_blk, b_ref[half].astype(BF16), dn, preferred_element_type=F32) for half in range(2)]
            prod = jnp.concatenate(halves, axis=1)
        else:
            b_blk = b_ref[0] if b_split else b_ref[...]
            prod = lax.dot_general(a_ref[...].astype(BF16), b_blk.astype(BF16), dn, preferred_element_type=F32)

        def store(v):
            if whole_n:
                o_ref[0] = v[:, : N // 2].astype(o_ref.dtype)
                o_ref[1] = v[:, N // 2 :].astype(o_ref.dtype)
            elif out_split:
                o_ref[0] = v.astype(o_ref.dtype)
            else:
                o_ref[...] = v.astype(o_ref.dtype)

        if nk == 1:
            store(prod)
            return
        (acc_ref,) = scratch
        k = pl.program_id(2)

        @pl.when(k == 0)
        def _():
            acc_ref[...] = prod

        @pl.when(k > 0)
        def _():
            acc_ref[...] += prod

        @pl.when(k == nk - 1)
        def _():
            store(acc_ref[...])

    a_spec = pl.BlockSpec((tk, tm), lambda i, j, k: (k, i)) if ta else pl.BlockSpec((tm, tk), lambda i, j, k: (i, k))
    if whole_k:
        b_spec = pl.BlockSpec((2, tn, K // 2), lambda i, j, k: (0, j, 0))
    elif whole_bn:
        b_spec = pl.BlockSpec((2, tk, N // 2), lambda i, j, k: (0, k, 0))
    elif b_split and tb:
        b_spec = pl.BlockSpec((1, tn, tk), lambda i, j, k: (k // nkh, j, k % nkh))
    elif b_split:
        b_spec = pl.BlockSpec((1, tk, tn), lambda i, j, k: (j // njh, k, j % njh))
    elif tb:
        b_spec = pl.BlockSpec((tn, tk), lambda i, j, k: (j, k))
    else:
        b_spec = pl.BlockSpec((tk, tn), lambda i, j, k: (k, j))
    if whole_n:
        out_spec = pl.BlockSpec((2, tm, N // 2), lambda i, j, k: (0, i, 0))
        out_shape = jax.ShapeDtypeStruct((2, M, N // 2), out_dtype)
    elif out_split:
        out_spec = pl.BlockSpec((1, tm, tn), lambda i, j, k: (j // njh, i, j % njh))
        out_shape = jax.ShapeDtypeStruct((2, M, N // 2), out_dtype)
    else:
        out_spec = pl.BlockSpec((tm, tn), lambda i, j, k: (i, j))
        out_shape = jax.ShapeDtypeStruct((M, N), out_dtype)
    grid = (M // tm, N // tn, nk)
    acc_shapes = [] if nk == 1 else [pltpu.VMEM((tm, tn), F32)]
    if host is None:
        return _pcall(
            body, name=name, grid=grid, in_specs=[a_spec, b_spec], out_specs=out_spec, out_shape=out_shape,
            scratch_shapes=acc_shapes, compiler_params=_params(3),
        )(a, b)

    n_hi, n_ho, n_steps = len(host.ins), len(host.out_shapes), grid[0] * grid[1] * grid[2]

    def hosting_body(a_ref, b_ref, *rest):
        host_in, rest = rest[:n_hi], rest[n_hi:]
        o_ref, rest = rest[0], rest[1:]
        host_out, rest = rest[:n_ho], rest[n_ho:]
        scratch, host_sems = rest[: len(acc_shapes)], rest[len(acc_shapes) :]
        step = (pl.program_id(0) * grid[1] + pl.program_id(1)) * grid[2] + pl.program_id(2)
        refs = (host_in, host_out, host_sems)

        @pl.when(step == 0)
        def _():
            host.start(*refs)

        @pl.when(step == n_steps // 2)
        def _():
            host.relay(*refs)

        body(a_ref, b_ref, o_ref, *scratch)

        @pl.when(step == n_steps - 1)
        def _():
            host.finish(*refs)

    return _pcall(
        hosting_body,
        name=name,
        grid=grid,
        in_specs=[a_spec, b_spec, *host.in_specs],
        out_specs=[out_spec, *host.out_specs],
        out_shape=[out_shape, *host.out_shapes],
        scratch_shapes=[*acc_shapes, *host.scratch],
        compiler_params=_params(3),
    )(a, b, *host.ins)


def win(arr, off, width):
    assert off % width == 0 and off + width <= arr.shape[1], (arr.shape, off, width)
    return (arr, off // width, width)


def rowwise(fn, rows, vecs, outs, accs=(), *, tm, name, into=None):
    rows = [r if isinstance(r, tuple) else (r, 0, r.shape[-1]) for r in rows]
    S = rows[0][0].shape[-2]
    tm = _pick(S, tm, 8)
    n_rows, n_vecs, n_outs = len(rows), len(vecs), len(outs)
    aliased = into is not None and not isinstance(into[0], int)
    n_alias = 1 if aliased else 0

    def body(*refs):
        ins = [r[...] for r in refs[: n_rows + n_vecs]]
        refs = refs[: n_rows + n_vecs] + refs[n_rows + n_vecs + n_alias :]
        out_refs = refs[n_rows + n_vecs : n_rows + n_vecs + n_outs]
        acc_refs = refs[n_rows + n_vecs + n_outs :]
        res, acc = fn(*ins)
        for r, v in zip(out_refs, res, strict=True):
            r[...] = v.astype(r.dtype)
        if acc_refs:
            i = pl.program_id(0)

            @pl.when(i == 0)
            def _():
                for r, v in zip(acc_refs, acc, strict=True):
                    r[...] = v

            @pl.when(i > 0)
            def _():
                for r, v in zip(acc_refs, acc, strict=True):
                    r[...] += v

    in_specs = []
    for arr, blk, w in rows:
        if arr.ndim == 3:
            in_specs.append(pl.BlockSpec((arr.shape[0], tm, w), lambda i: (0, i, 0)))
        else:
            in_specs.append(pl.BlockSpec((tm, w), functools.partial(lambda i, blk: (i, blk), blk=blk)))
    for v in vecs:
        in_specs.append(pl.BlockSpec(v.shape, functools.partial(lambda i, nd: (0,) * nd, nd=v.ndim)))
    out_specs = [pl.BlockSpec((tm, w), lambda i: (i, 0)) for w, _ in outs]
    out_specs += [pl.BlockSpec((1, w), lambda i: (0, 0)) for w in accs]
    out_shape = [jax.ShapeDtypeStruct((S, w), dt) for w, dt in outs]
    out_shape += [jax.ShapeDtypeStruct((1, w), F32) for w in accs]
    operands = [*[r[0] for r in rows], *vecs]
    aliases = {}
    if into is not None:
        target, off = into
        w0, dt0 = outs[0]
        total = target if isinstance(target, int) else target.shape[1]
        assert off % w0 == 0 and off + w0 <= total
        out_specs[0] = pl.BlockSpec((tm, w0), functools.partial(lambda i, blk: (i, blk), blk=off // w0))
        out_shape[0] = jax.ShapeDtypeStruct((S, total), dt0)
        if aliased:
            in_specs.append(pl.BlockSpec(memory_space=pl.ANY))
            operands.append(target)
            aliases = {len(operands) - 1: 0}
    res = _pcall(
        body,
        name=name,
        grid=(S // tm,),
        in_specs=in_specs,
        out_specs=out_specs,
        out_shape=out_shape,
        input_output_aliases=aliases,
        compiler_params=_params(1),
    )(*operands)
    return res


def _sum0(v):
    return jnp.sum(v, axis=0, keepdims=True)


def _sigmoid(z):
    return 1.0 / (1.0 + jnp.exp(-z))


def _silu(z):
    return z * _sigmoid(z)


def _dsilu(z):
    s = _sigmoid(z)
    return s * (1.0 + z * (1.0 - s))


def _swap_half(x, half):
    if 2 * half == LANE:
        return pltpu.roll(x, half, 1)
    lane = lax.broadcasted_iota(jnp.int32, x.shape, 1)
    return jnp.where((lane % (2 * half)) < half, pltpu.roll(x, LANE - half, 1), pltpu.roll(x, half, 1))


def _rope_tile(x, cosf, sinf, half, inverse=False):
    sw = _swap_half(x, half)
    return x * cosf - sw * sinf if inverse else x * cosf + sw * sinf


def _tri_tables(n, by_key):
    if by_key:
        pairs = [(i, j) for j in range(n) for i in range(j, n)]
    else:
        pairs = [(i, j) for i in range(n) for j in range(i + 1)]
    return (np.array([p[0] for p in pairs], np.int32), np.array([p[1] for p in pairs], np.int32))


def _attn_scores(q, kn, kr, i, j, tq, masked):
    k = jnp.concatenate([kn, kr], axis=1)
    s = lax.dot_general(q, k, (((1,), (1,)), ((), ())), preferred_element_type=F32)
    if masked:
        rc = (i * tq + lax.broadcasted_iota(jnp.int32, s.shape, 0)) // CHUNK
        cc = (j * tq + lax.broadcasted_iota(jnp.int32, s.shape, 1)) // CHUNK
        s = jnp.where(cc <= rc, s, NEG_INF)
    return s, k


def attn_fwd(q, kv, kr, proj, mg_off, *, heads, tq, name, host=None):
    S = q.shape[0]
    tq = _pick(S, tq, CHUNK)
    n = S // tq
    it, jt = _tri_tables(n, by_key=False)
    T = len(it)
    mg_blk = mg_off // MLA_DV
    n_hi, n_ho = (len(host.ins), len(host.out_shapes)) if host else (0, 0)

    hp = ATTN_FWD_HEADS_PER_STEP
    assert heads % hp == 0 and mg_blk % hp == 0
    groups = heads // hp

    def body(it_ref, jt_ref, q_ref, kv_ref, kr_ref, mg_ref, *rest):
        host_in, rest = rest[:n_hi], rest[n_hi:]
        (o_ref, a_ref, lse_ref), rest = rest[:3], rest[3:]
        host_out, rest = rest[:n_ho], rest[n_ho:]
        (m_sc, acc_sc), host_sems = rest[:2], rest[2:]
        h, t = pl.program_id(0), pl.program_id(1)
        i, j = it_ref[t], jt_ref[t]
        if host:
            _host_steps(host, (host_in, host_out, host_sems), h, t, groups, T, before=True)

        @pl.when(j == 0)
        def _():
            m_sc[...] = jnp.full(m_sc.shape, NEG_INF, F32)
            acc_sc[...] = jnp.zeros(acc_sc.shape, F32)

        def step(masked):
            kr = kr_ref[...]
            for u in range(hp):
                q = q_ref[:, u * MLA_QW : (u + 1) * MLA_QW]
                kn = kv_ref[:, u * MLA_QW : u * MLA_QW + MLA_NOPE]
                v = kv_ref[:, u * MLA_QW + MLA_NOPE : (u + 1) * MLA_QW]
                s, _ = _attn_scores(q, kn, kr, i, j, tq, masked)
                m_prev = m_sc[u]
                m_new = jnp.maximum(m_prev, jnp.max(s, axis=1, keepdims=True))
                p = jnp.exp2((s - m_new).astype(BF16))
                alpha = jnp.exp2(m_prev - m_new)
                v_ones = jnp.concatenate([v, jnp.ones(v.shape, BF16)], axis=1)
                acc_sc[u] = alpha * acc_sc[u] + jnp.dot(p, v_ones, preferred_element_type=F32)
                m_sc[u] = m_new

        @pl.when(j < i)
        def _():
            step(False)

        @pl.when(j == i)
        def _():
            step(True)
            for u in range(hp):
                cols = slice(u * MLA_DV, (u + 1) * MLA_DV)
                l = acc_sc[u, :, MLA_DV:]
                o = acc_sc[u, :, :MLA_DV] / l
                o_ref[:, cols] = o
                a_ref[:, cols] = (o * _silu(mg_ref[:, cols].astype(F32))).astype(a_ref.dtype)
                lse_ref[u] = m_sc[u] + jnp.log2(l[:, :1])

        if host:
            _host_steps(host, (host_in, host_out, host_sems), h, t, groups, T, before=False)

    grid_spec = pltpu.PrefetchScalarGridSpec(
        num_scalar_prefetch=2,
        grid=(groups, T),
        in_specs=[
            pl.BlockSpec((tq, hp * MLA_QW), lambda h, t, it, jt: (it[t], h)),
            pl.BlockSpec((tq, hp * MLA_QW), lambda h, t, it, jt: (jt[t], h)),
            pl.BlockSpec((tq, LANE), lambda h, t, it, jt: (jt[t], 0)),
            pl.BlockSpec((tq, hp * MLA_DV), lambda h, t, it, jt: (it[t], mg_blk // hp + h)),
            *(host.in_specs if host else []),
        ],
        out_specs=[
            pl.BlockSpec((tq, hp * MLA_DV), lambda h, t, it, jt: (it[t], h)),
            pl.BlockSpec((tq, hp * MLA_DV), lambda h, t, it, jt: (it[t], h)),
            pl.BlockSpec((hp, tq, 1), lambda h, t, it, jt: (h, it[t], 0)),
            *(host.out_specs if host else []),
        ],
        scratch_shapes=[
            pltpu.VMEM((hp, tq, 1), F32),
            pltpu.VMEM((hp, tq, 2 * MLA_DV), F32),
            *(host.scratch if host else []),
        ],
    )
    return _pcall(
        body,
        name=name,
        grid_spec=grid_spec,
        out_shape=[
            jax.ShapeDtypeStruct((S, heads * MLA_DV), F32),
            jax.ShapeDtypeStruct((S, heads * MLA_DV), BF16),
            jax.ShapeDtypeStruct((heads, S, 1), F32),
            *(host.out_shapes if host else []),
        ],
        compiler_params=_params(2),
    )(jnp.asarray(it), jnp.asarray(jt), q, kv, kr, proj, *(host.ins if host else []))


def attn_bwd(q, kv, kr, do, o, lse, *, heads, tq, name, host=None):
    S = q.shape[0]
    tq = _pick(S, tq, CHUNK)
    n = S // tq
    it, jt = _tri_tables(n, by_key=True)
    T = len(it)
    n_hi, n_ho = (len(host.ins), len(host.out_shapes)) if host else (0, 0)

    hp = ATTN_HEADS_PER_STEP
    assert heads % hp == 0
    groups = heads // hp

    def body(it_ref, jt_ref, q_ref, kv_ref, kr_ref, do_ref, o_ref, lse_ref, *rest):
        host_in, rest = rest[:n_hi], rest[n_hi:]
        (dq_ref, dkv_ref, dkr_ref), rest = rest[:3], rest[3:]
        host_out, rest = rest[:n_ho], rest[n_ho:]
        (dq_acc, dk_sc, dv_sc), host_sems = rest[:3], rest[3:]
        h, t = pl.program_id(0), pl.program_id(1)
        i, j = it_ref[t], jt_ref[t]
        if host:
            _host_steps(host, (host_in, host_out, host_sems), h, t, groups, T, before=True)

        @pl.when(t == 0)
        def _():
            dq_acc[...] = jnp.zeros(dq_acc.shape, F32)

        @pl.when(i == j)
        def _():
            dk_sc[...] = jnp.zeros(dk_sc.shape, F32)
            dv_sc[...] = jnp.zeros(dv_sc.shape, F32)

        def step(masked):
            kr = kr_ref[...]
            rows = pl.ds(pl.multiple_of(i * tq, tq), tq)
            for u in range(hp):
                wide, narrow = slice(u * MLA_QW, (u + 1) * MLA_QW), slice(u * MLA_DV, (u + 1) * MLA_DV)
                q_blk, do_blk = q_ref[:, wide], do_ref[:, narrow]
                kn = kv_ref[:, u * MLA_QW : u * MLA_QW + MLA_NOPE]
                v = kv_ref[:, u * MLA_QW + MLA_NOPE : (u + 1) * MLA_QW]
                s, k = _attn_scores(q_blk, kn, kr, i, j, tq, masked)
                p = jnp.exp2(s - lse_ref[u])
                delta = jnp.sum(do_blk.astype(F32) * o_ref[:, narrow], axis=1, keepdims=True)
                dp = lax.dot_general(do_blk, v, _NT, preferred_element_type=F32)
                ds = (p * (dp - delta)).astype(BF16)
                dv_sc[u] += lax.dot_general(p.astype(BF16), do_blk, _TN, preferred_element_type=F32)
                dk_sc[u] += lax.dot_general(ds, q_blk, _TN, preferred_element_type=F32)
                dq_acc[rows, wide] += jnp.dot(ds, k, preferred_element_type=F32)

        @pl.when(i == j)
        def _():
            step(True)

        @pl.when(i > j)
        def _():
            step(False)

        @pl.when(i == n - 1)
        def _():
            for u in range(hp):
                dkv_ref[:, u * MLA_QW : u * MLA_QW + MLA_NOPE] = (dk_sc[u, :, :MLA_NOPE] * LN2).astype(dkv_ref.dtype)
                dkv_ref[:, u * MLA_QW + MLA_NOPE : (u + 1) * MLA_QW] = dv_sc[u].astype(dkv_ref.dtype)
                dkr_ref[u] = dk_sc[u, :, MLA_NOPE:] * LN2

        @pl.when(t == T - 1)
        def _():
            dq_ref[...] = (dq_acc[...] * MLA_SCALE).astype(dq_ref.dtype)

        if host:
            _host_steps(host, (host_in, host_out, host_sems), h, t, groups, T, before=False)

    grid_spec = pltpu.PrefetchScalarGridSpec(
        num_scalar_prefetch=2,
        grid=(groups, T),
        in_specs=[
            pl.BlockSpec((tq, hp * MLA_QW), lambda h, t, it, jt: (it[t], h)),
            pl.BlockSpec((tq, hp * MLA_QW), lambda h, t, it, jt: (jt[t], h)),
            pl.BlockSpec((tq, LANE), lambda h, t, it, jt: (jt[t], 0)),
            pl.BlockSpec((tq, hp * MLA_DV), lambda h, t, it, jt: (it[t], h)),
            pl.BlockSpec((tq, hp * MLA_DV), lambda h, t, it, jt: (it[t], h)),
            pl.BlockSpec((hp, tq, 1), lambda h, t, it, jt: (h, it[t], 0)),
            *(host.in_specs if host else []),
        ],
        out_specs=[
            pl.BlockSpec((S, hp * MLA_QW), lambda h, t, it, jt: (0, h)),
            pl.BlockSpec((tq, hp * MLA_QW), lambda h, t, it, jt: (jt[t], h)),
            pl.BlockSpec((hp, tq, LANE), lambda h, t, it, jt: (h, jt[t], 0)),
            *(host.out_specs if host else []),
        ],
        scratch_shapes=[
            pltpu.VMEM((S, hp * MLA_QW), F32),
            pltpu.VMEM((hp, tq, MLA_QW), F32),
            pltpu.VMEM((hp, tq, MLA_DV), F32),
            *(host.scratch if host else []),
        ],
    )
    return _pcall(
        body,
        name=name,
        grid_spec=grid_spec,
        out_shape=[
            jax.ShapeDtypeStruct((S, heads * MLA_QW), BF16),
            jax.ShapeDtypeStruct((S, heads * (MLA_NOPE + MLA_DV)), BF16),
            jax.ShapeDtypeStruct((heads, S, LANE), F32),
            *(host.out_shapes if host else []),
        ],
        compiler_params=_params(2),
    )(jnp.asarray(it), jnp.asarray(jt), q, kv, kr, do, o, lse, *(host.ins if host else []))


def _host_steps(host, refs, h, t, heads, n_steps, before):
    if before:

        @pl.when((h == 0) & (t == 0))
        def _():
            host.start(*refs)

        @pl.when((h == heads // 2) & (t == n_steps // 2))
        def _():
            host.relay(*refs)

    else:

        @pl.when((h == heads - 1) & (t == n_steps - 1))
        def _():
            host.finish(*refs)


def _ret_consts(heads):
    h = np.arange(heads, dtype=np.float32)
    lg = np.log(np.float32(1.0) - np.float32(2.0) ** (np.float32(-5.0) - h)).astype(np.float32)
    idx = np.arange(CHUNK, dtype=np.float32)
    dmat = np.exp(np.abs(idx[:, None] - idx[None, :])[None] * lg[:, None, None]).astype(np.float32)
    xi = np.exp((idx + 1.0)[None, :] * lg[:, None]).astype(np.float32)
    zeta = np.exp((CHUNK - 1.0 - idx)[None, :] * lg[:, None]).astype(np.float32)
    dec = np.exp(np.float32(CHUNK) * lg).astype(np.float32)
    xi = np.broadcast_to(xi[:, :, None], (heads, CHUNK, RET_DK)).copy()
    zeta = np.broadcast_to(zeta[:, :, None], (heads, CHUNK, RET_DK)).copy()
    dec = np.broadcast_to(dec[:, None, None], (heads, 8, LANE)).copy()
    return jnp.asarray(dmat), jnp.asarray(xi), jnp.asarray(zeta), jnp.asarray(dec)


_NT = (((1,), (1,)), ((), ()))
_TN = (((0,), (0,)), ((), ()))


def _dot(a, b, dn=(((1,), (0,)), ((), ()))):
    return lax.dot_general(a.astype(BF16), b.astype(BF16), dn, preferred_element_type=F32)


def _const_specs(hp):
    return [
        pl.BlockSpec((hp, CHUNK, CHUNK), lambda h, b: (h, 0, 0)),
        pl.BlockSpec((hp, CHUNK, RET_DK), lambda h, b: (h, 0, 0)),
        pl.BlockSpec((hp, CHUNK, RET_DK), lambda h, b: (h, 0, 0)),
        pl.BlockSpec((hp, 8, LANE), lambda h, b: (h, 0, 0)),
    ]


def _ret_group(heads, *blks):
    hp = min(RET_HEADS_PER_STEP, heads)
    assert heads % hp == 0 and all(b % hp == 0 for b in blks)
    return hp


def ret_fwd(q, k, proj, v_off, rg_off, *, heads, tb, name):
    S = q.shape[0]
    tb = _pick(S, tb, CHUNK)
    cpb, nb = tb // CHUNK, S // tb
    v_blk, rg_blk = v_off // RET_DV, rg_off // RET_DV
    hp = _ret_group(heads, v_blk, rg_blk)

    def body(q_ref, k_ref, v_ref, rg_ref, dm_ref, xi_ref, ze_ref, dec_ref, o_ref, a_ref, st_ref, r_sc):
        @pl.when(pl.program_id(1) == 0)
        def _():
            r_sc[...] = jnp.zeros(r_sc.shape, F32)

        for c in range(cpb):
            sl = pl.ds(c * CHUNK, CHUNK)
            for u in range(hp):
                dm, xi, ze, dec = dm_ref[u], xi_ref[u], ze_ref[u], dec_ref[u, 0:1, 0:1]
                narrow, wide = slice(u * RET_DK, (u + 1) * RET_DK), slice(u * RET_DV, (u + 1) * RET_DV)
                qc, kc, vc = q_ref[sl, narrow], k_ref[sl, narrow], v_ref[sl, wide]
                r = r_sc[u]
                st_ref[u, c] = r.astype(BF16)
                s = _dot(qc, kc, _NT) * dm
                o = _dot(s, vc) + _dot(qc.astype(F32) * xi, r)
                r_sc[u] = r * dec + _dot(kc.astype(F32) * ze, vc, _TN)
                mu = jnp.mean(o, axis=1, keepdims=True)
                d = o - mu
                n = d * lax.rsqrt(jnp.mean(d * d, axis=1, keepdims=True) + EPS)
                o_ref[sl, wide] = o
                a_ref[sl, wide] = (n * _silu(rg_ref[sl, wide].astype(F32))).astype(a_ref.dtype)

    return _pcall(
        body,
        name=name,
        grid=(heads // hp, nb),
        in_specs=[
            pl.BlockSpec((tb, hp * RET_DK), lambda h, b: (b, h)),
            pl.BlockSpec((tb, hp * RET_DK), lambda h, b: (b, h)),
            pl.BlockSpec((tb, hp * RET_DV), lambda h, b: (b, v_blk // hp + h)),
            pl.BlockSpec((tb, hp * RET_DV), lambda h, b: (b, rg_blk // hp + h)),
            *_const_specs(hp),
        ],
        out_specs=[
            pl.BlockSpec((tb, hp * RET_DV), lambda h, b: (b, h)),
            pl.BlockSpec((tb, hp * RET_DV), lambda h, b: (b, h)),
            pl.BlockSpec((hp, cpb, RET_DK, RET_DV), lambda h, b: (h, b, 0, 0)),
        ],
        out_shape=[
            jax.ShapeDtypeStruct((S, heads * RET_DV), F32),
            jax.ShapeDtypeStruct((S, heads * RET_DV), BF16),
            jax.ShapeDtypeStruct((heads, S // CHUNK, RET_DK, RET_DV), BF16),
        ],
        scratch_shapes=[pltpu.VMEM((hp, RET_DK, RET_DV), F32)],
        compiler_params=_params(2),
    )(q, k, proj, proj, *_ret_consts(heads))


def ret_bwd(q, k, proj, v_off, states, do, *, heads, tb, name, into):
    S = q.shape[0]
    tb = _pick(S, tb, CHUNK)
    cpb, nb = tb // CHUNK, S // tb
    v_blk = v_off // RET_DV
    hp = _ret_group(heads, v_blk)
    target, dv_off = into
    dv_blk = dv_off // (hp * RET_DV)
    assert dv_off % (hp * RET_DV) == 0

    def body(q_ref, k_ref, v_ref, st_ref, do_ref, dm_ref, xi_ref, ze_ref, dec_ref, _, dq_ref, dk_ref, dv_ref, dr_sc):
        @pl.when(pl.program_id(1) == 0)
        def _():
            dr_sc[...] = jnp.zeros(dr_sc.shape, F32)

        for c in reversed(range(cpb)):
            sl = pl.ds(c * CHUNK, CHUNK)
            for u in range(hp):
                dm, xi, ze, dec = dm_ref[u], xi_ref[u], ze_ref[u], dec_ref[u, 0:1, 0:1]
                narrow, wide = slice(u * RET_DK, (u + 1) * RET_DK), slice(u * RET_DV, (u + 1) * RET_DV)
                qc, kc, vc, doc = q_ref[sl, narrow], k_ref[sl, narrow], v_ref[sl, wide], do_ref[sl, wide]
                r_prev = st_ref[u, c]
                dr = dr_sc[u]
                a = _dot(qc, kc, _NT) * dm
                ds = _dot(doc, vc, _NT) * dm
                kz = kc.astype(F32) * ze
                dq_ref[sl, narrow] = (_dot(ds, kc) + _dot(doc, r_prev, _NT) * xi).astype(dq_ref.dtype)
                dk_ref[sl, narrow] = (_dot(ds, qc, _TN) + _dot(vc, dr, _NT) * ze).astype(dk_ref.dtype)
                dv_ref[sl, wide] = (_dot(a, doc, _TN) + _dot(kz, dr)).astype(dv_ref.dtype)
                dr_sc[u] = dr * dec + _dot(qc.astype(F32) * xi, doc, _TN)

    rev = lambda h, b: (nb - 1 - b, h)
    return _pcall(
        body,
        name=name,
        grid=(heads // hp, nb),
        in_specs=[
            pl.BlockSpec((tb, hp * RET_DK), rev),
            pl.BlockSpec((tb, hp * RET_DK), rev),
            pl.BlockSpec((tb, hp * RET_DV), lambda h, b: (nb - 1 - b, v_blk // hp + h)),
            pl.BlockSpec((hp, cpb, RET_DK, RET_DV), lambda h, b: (h, nb - 1 - b, 0, 0)),
            pl.BlockSpec((tb, hp * RET_DV), rev),
            *_const_specs(hp),
            pl.BlockSpec(memory_space=pl.ANY),
        ],
        out_specs=[
            pl.BlockSpec((tb, hp * RET_DK), rev),
            pl.BlockSpec((tb, hp * RET_DK), rev),
            pl.BlockSpec((tb, hp * RET_DV), lambda h, b: (nb - 1 - b, dv_blk + h)),
        ],
        out_shape=[
            jax.ShapeDtypeStruct((S, heads * RET_DK), BF16),
            jax.ShapeDtypeStruct((S, heads * RET_DK), BF16),
            jax.ShapeDtypeStruct(target.shape, target.dtype),
        ],
        input_output_aliases={9: 2},
        scratch_shapes=[pltpu.VMEM((hp, RET_DK, RET_DV), F32)],
        compiler_params=_params(2),
    )(q, k, proj, states, do, *_ret_consts(heads), target)


def _me():
    return (lax.axis_index("x"), lax.axis_index("y"), lax.axis_index("c"))


def _comm(name, ins, out_shapes, n_local, n_remote, plan):
    return run_exchange(name, Exchange(ins, out_shapes, n_local, n_remote, plan))


def run_exchange(name, ex):
    def body(*refs):
        ex.start(*ex.split(refs))
        ex.relay(*ex.split(refs))
        ex.finish(*ex.split(refs))

    return _pcall(
        body, name=name, in_specs=ex.in_specs, out_specs=ex.out_specs, out_shape=ex.out_shapes, scratch_shapes=ex.scratch
    )(*ex.ins)


class Exchange:
    def __init__(self, ins, out_shapes, n_local, n_remote, plan):
        self.ins, self.out_shapes, self.n_local, self.n_remote, self.plan = list(ins), list(out_shapes), n_local, n_remote, plan
        any_spec = pl.BlockSpec(memory_space=pl.ANY)
        self.in_specs, self.out_specs = [any_spec] * len(self.ins), [any_spec] * len(self.out_shapes)
        self.scratch = [
            pltpu.SemaphoreType.DMA((n_remote,)),
            pltpu.SemaphoreType.DMA((n_remote,)),
            pltpu.SemaphoreType.DMA((max(n_local, 1),)),
        ]

    def split(self, refs):
        n_in, n_out = len(self.ins), len(self.out_shapes)
        return refs[:n_in], refs[n_in : n_in + n_out], refs[n_in + n_out :]

    def _copies(self, in_refs, out_refs, sems):
        send_sems, recv_sems, local_sems = sems
        me = _me()
        local_plan, remote_plan = self.plan(me, in_refs, out_refs)
        assert len(local_plan) == self.n_local and len(remote_plan) == self.n_remote, (len(local_plan), len(remote_plan))
        def local_copy(n):
            src, dst = local_plan[n]
            return pltpu.make_async_copy(src, dst, local_sems.at[n])

        def send(n):
            src, dst, peer, _, _ = remote_plan[n]
            return pltpu.make_async_remote_copy(
                src_ref=src, dst_ref=dst, send_sem=send_sems.at[n], recv_sem=recv_sems.at[n], device_id=peer, device_id_type=MESH
            )

        def arrival(n):
            src, _, _, landing, _ = remote_plan[n]
            return pltpu.make_async_remote_copy(
                src_ref=src, dst_ref=landing, send_sem=send_sems.at[n], recv_sem=recv_sems.at[n], device_id=me, device_id_type=MESH
            )

        after = [a for (_, _, _, _, a) in remote_plan]
        assert all(a is None or a < n for n, a in enumerate(after))
        return local_copy, send, arrival, after

    def start(self, in_refs, out_refs, sems):
        local_copy, send, _, after = self._copies(in_refs, out_refs, sems)
        for n in range(self.n_local):
            local_copy(n).start()
        for n, a in enumerate(after):
            if a is None:
                send(n).start()

    def relay(self, in_refs, out_refs, sems):
        _, send, arrival, after = self._copies(in_refs, out_refs, sems)
        for n in sorted({a for a in after if a is not None}):
            arrival(n).wait_recv()
            for m, a in enumerate(after):
                if a == n:
                    send(m).start()

    def finish(self, in_refs, out_refs, sems):
        local_copy, send, arrival, after = self._copies(in_refs, out_refs, sems)
        relayed = {a for a in after if a is not None}
        for n in range(self.n_remote):
            if n not in relayed:
                arrival(n).wait_recv()
        for n in range(self.n_remote):
            send(n).wait_send()
        for n in range(self.n_local):
            local_copy(n).wait()


_CHIP_FLIPS = ((1, 0), (0, 1), (1, 1))


class Placement:
    def __init__(self, rows, full_rows, cuts=(), offset=None, zero_rows=None):
        edges = [0, *sorted(cuts), rows]
        self.rows, self.full_rows = rows, full_rows
        self.runs = [(a, b - a) for a, b in zip(edges[:-1], edges[1:]) if b > a]
        self.offset = offset if offset is not None else (lambda j, start: j * rows + start)
        self.zero_rows = zero_rows

    def at(self, ref_2d_plus, j, start, size):
        return pl.ds(pl.multiple_of(self.offset(j, start), 16), size)


def gather_layer(prepped, zeros, places, l, name):
    n_w = len(prepped)
    n_chip = sum(len(p.runs) for p in places) * 3
    n_own = sum(len(p.runs) for p in places) * 2 + sum(2 for p in places if p.zero_rows)

    def plan(me, in_refs, out_refs):
        x, y, c = me
        j = 2 * x + y
        sib = (x, y, 1 - c)
        from_chips, own, forwards = [], [], []
        for i, o, p in zip(in_refs[:n_w], out_refs, places):
            for start, size in p.runs:
                for dx, dy in _CHIP_FLIPS:
                    px, py = x ^ dx, y ^ dy
                    jp = 2 * px + py
                    mine_there = o.at[c, p.at(o, j, start, size)]
                    theirs_here = o.at[c, p.at(o, jp, start, size)]
                    n = len(from_chips)
                    from_chips.append((i.at[l, c, pl.ds(start, size)], mine_there, (px, py, c), theirs_here, None))
                    forwards.append((theirs_here, theirs_here, sib, o.at[1 - c, p.at(o, jp, start, size)], n))
                for h in range(2):
                    place = o.at[h, p.at(o, j, start, size)]
                    own.append((i.at[l, h, pl.ds(start, size)], place, sib, place, None))
            if p.zero_rows:
                for h in range(2):
                    place = o.at[h, pl.ds(p.zero_rows[0], p.zero_rows[1])]
                    own.append((in_refs[n_w], place, sib, place, None))
        return [], from_chips + own + forwards

    shapes = [jax.ShapeDtypeStruct((2, p.full_rows, w.shape[3]), w.dtype) for w, p in zip(prepped, places)]
    ins = [*prepped, zeros] if any(p.zero_rows for p in places) else list(prepped)
    ex = Exchange(ins, shapes, 0, 2 * n_chip + n_own, plan)
    return ex if name is None else run_exchange(name, ex)


def sibling_scatter(arrays, name):
    def plan(me, in_refs, out_refs):
        x, y, c = me
        return [], [(i.at[1 - c], o, (x, y, 1 - c), o, None) for i, o in zip(in_refs, out_refs)]

    shapes = [jax.ShapeDtypeStruct(a.shape[1:], a.dtype) for a in arrays]
    ex = Exchange(arrays, shapes, 0, len(arrays), plan)
    return ex if name is None else run_exchange(name, ex)


def sibling_swap(arrays, name):
    def plan(me, in_refs, out_refs):
        x, y, c = me
        return [], [(i, o, (x, y, 1 - c), o, None) for i, o in zip(in_refs, out_refs)]

    shapes = [jax.ShapeDtypeStruct(a.shape, a.dtype) for a in arrays]
    return _comm(name, arrays, shapes, 0, len(arrays), plan)


def scatter_to_chips(arrays, places, name):
    def plan(me, in_refs, out_refs):
        x, y, c = me
        j = 2 * x + y
        local_plan, remote_plan = [], []
        for i, o, p in zip(in_refs, out_refs, places):
            for start, size in p.runs:
                local_plan.append((i.at[p.at(i, j, start, size)], o.at[3, pl.ds(start, size)]))
                for k, (dx, dy) in enumerate(_CHIP_FLIPS):
                    px, py = x ^ dx, y ^ dy
                    landing = o.at[k, pl.ds(start, size)]
                    remote_plan.append((i.at[p.at(i, 2 * px + py, start, size)], landing, (px, py, c), landing, None))
        return local_plan, remote_plan

    n_runs = sum(len(p.runs) for p in places)
    shapes = [jax.ShapeDtypeStruct((4, p.rows, a.shape[1]), a.dtype) for a, p in zip(arrays, places)]
    ex = Exchange(arrays, shapes, n_runs, 3 * n_runs, plan)
    return ex if name is None else run_exchange(name, ex)


def allgather8(block, name):
    def plan(me, in_refs, out_refs):
        x, y, c = me
        (i,), (o,) = in_refs, out_refs
        mine = 4 * x + 2 * y + c
        remote_plan = []
        for flip in range(1, 8):
            px, py, pc = x ^ (flip >> 2), y ^ ((flip >> 1) & 1), c ^ (flip & 1)
            remote_plan.append((i, o.at[mine], (px, py, pc), o.at[4 * px + 2 * py + pc], None))
        return [(i, o.at[mine])], remote_plan

    return _comm(name, [block], [jax.ShapeDtypeStruct((8, *block.shape), block.dtype)], 1, 7, plan)[0]


def transpose_split(a, out_dtype, name):
    L, R, C = a.shape
    rh = R // 2
    rt = _pick(rh, 256)
    n = rh // rt

    def body(x_ref, o_ref):
        o_ref[0, 0] = x_ref[0].T.astype(o_ref.dtype)

    return _pcall(
        body,
        name=name,
        grid=(L, 2, n),
        in_specs=[pl.BlockSpec((1, rt, C), lambda l, h, i: (l, h * n + i, 0))],
        out_specs=pl.BlockSpec((1, 1, C, rt), lambda l, h, i: (l, h, 0, i)),
        out_shape=jax.ShapeDtypeStruct((L, 2, C, rh), out_dtype),
        compiler_params=_params(3),
    )(a)


def cast_split(a, out_dtype, name):
    L, R, C = a.shape
    rt = _pick(R, 512, 16)

    def body(x_ref, o_ref):
        o_ref[0, 0] = x_ref[0].astype(o_ref.dtype)

    return _pcall(
        body,
        name=name,
        grid=(L, 2, R // rt),
        in_specs=[pl.BlockSpec((1, rt, C // 2), lambda l, h, i: (l, i, h))],
        out_specs=pl.BlockSpec((1, 1, rt, C // 2), lambda l, h, i: (l, h, i, 0)),
        out_shape=jax.ShapeDtypeStruct((L, 2, R, C // 2), out_dtype),
        compiler_params=_params(3),
    )(a)


def untranspose_halves(mine, theirs, my_c, stack, l, n_layers, name):
    C, rh = mine.shape
    n_in = 2 if stack is None else 3
    rt = _pick(rh, 256)
    n = rh // rt

    def body(c_ref, a_ref, b_ref, *rest):
        o_ref = rest[-1]
        h = pl.program_id(0)
        o_ref[0] = jnp.where(h == c_ref[0], a_ref[...], b_ref[...]).T

    grid_spec = pltpu.PrefetchScalarGridSpec(
        num_scalar_prefetch=1,
        grid=(2, n),
        in_specs=[pl.BlockSpec((C, rt), lambda h, i, c: (0, i))] * 2
        + ([] if stack is None else [pl.BlockSpec(memory_space=pl.ANY)]),
        out_specs=pl.BlockSpec((1, rt, C), lambda h, i, c: (l, h * n + i, 0)),
    )
    return _pcall(
        body,
        name=name,
        grid_spec=grid_spec,
        out_shape=jax.ShapeDtypeStruct((n_layers, 2 * rh, C), F32),
        input_output_aliases={} if stack is None else {n_in: 0},
        compiler_params=_params(2),
    )(my_c.reshape(1), mine, theirs, *([] if stack is None else [stack]))


def merge_halves(mine, theirs, my_c, stack, l, n_layers, name):
    R, ch = mine.shape
    n_in = 2 if stack is None else 3
    rt = _pick(R, 512, 8)

    def body(c_ref, a_ref, b_ref, *rest):
        o_ref = rest[-1]
        o_ref[0] = jnp.where(pl.program_id(0) == c_ref[0], a_ref[...], b_ref[...])

    grid_spec = pltpu.PrefetchScalarGridSpec(
        num_scalar_prefetch=1,
        grid=(2, R // rt),
        in_specs=[pl.BlockSpec((rt, ch), lambda h, i, c: (i, 0))] * 2
        + ([] if stack is None else [pl.BlockSpec(memory_space=pl.ANY)]),
        out_specs=pl.BlockSpec((1, rt, ch), lambda h, i, c: (l, i, h)),
    )
    return _pcall(
        body,
        name=name,
        grid_spec=grid_spec,
        out_shape=jax.ShapeDtypeStruct((n_layers, R, 2 * ch), F32),
        input_output_aliases={} if stack is None else {n_in: 0},
        compiler_params=_params(2),
    )(my_c.reshape(1), mine, theirs, *([] if stack is None else [stack]))


def add_pair(g, r, my_c, name):
    _, rows, w = g.shape
    tm = _pick(rows, max(16, min(1024, (1 << 20) // w // 16 * 16)), 16)

    def body(c_ref, g_ref, r_ref, o_ref):
        o_ref[...] = (g_ref[0].astype(F32) + r_ref[...].astype(F32)).astype(o_ref.dtype)

    grid_spec = pltpu.PrefetchScalarGridSpec(
        num_scalar_prefetch=1,
        grid=(rows // tm,),
        in_specs=[pl.BlockSpec((1, tm, w), lambda i, c: (c[0], i, 0)), pl.BlockSpec((tm, w), lambda i, c: (i, 0))],
        out_specs=pl.BlockSpec((tm, w), lambda i, c: (i, 0)),
    )
    return _pcall(
        body, name=name, grid_spec=grid_spec, out_shape=jax.ShapeDtypeStruct((rows, w), BF16), compiler_params=_params(1)
    )(my_c.reshape(1), g, r)


class Dims:
    def __init__(self, S, D, L, ret_heads, mla_heads, q_rank, kv_rank):
        self.S, self.D, self.L, self.HR, self.HM, self.QR, self.KR = S, D, L, ret_heads, mla_heads, q_rank, kv_rank
        self.RQ, self.RV, self.MV = ret_heads * RET_DK, ret_heads * RET_DV, mla_heads * MLA_DV
        self.D_IN = 2 * self.RQ + 2 * self.RV + q_rank + kv_rank + MLA_ROPE + self.MV + 2 * D
        self.lo = 2 * self.RQ + 2 * self.RV
        self.mid = q_rank + kv_rank + MLA_ROPE
        self.DP = self.D_IN + LANE - MLA_ROPE
        self.o_rq, self.o_rk, self.o_rv, self.o_rg = 0, self.RQ, 2 * self.RQ, 2 * self.RQ + self.RV
        self.o_mg = self.lo
        self.o_bga = self.o_mg + self.MV
        self.o_bgb = self.o_bga + D
        self.o_cq = self.o_bgb + D
        self.o_ckv = self.o_cq + q_rank
        self.o_kr = self.o_ckv + kv_rank

    def pad_uq_rows(self, w):
        heads = w.shape[-2] // (MLA_NOPE + MLA_ROPE)
        w = w.reshape(*w.shape[:-2], heads, MLA_NOPE + MLA_ROPE, w.shape[-1])
        w = jnp.pad(w, [(0, 0)] * (w.ndim - 2) + [(0, MLA_QW - MLA_NOPE - MLA_ROPE), (0, 0)])
        return w.reshape(*w.shape[:-3], heads * MLA_QW, w.shape[-1])

    def unpad_uq_rows(self, w):
        heads = w.shape[-2] // MLA_QW
        w = w.reshape(*w.shape[:-2], heads, MLA_QW, w.shape[-1])[..., : MLA_NOPE + MLA_ROPE, :]
        return w.reshape(*w.shape[:-3], heads * (MLA_NOPE + MLA_ROPE), w.shape[-1])

    def placements(self):
        rows_in = self.D_IN // 4
        lo, mid, n_hi = self.lo, self.mid, self.D_IN - self.lo - self.mid
        cuts = {b % rows_in for b in (lo, lo + mid)} - {0}

        def offset_in(j, start):
            g = j * rows_in + start
            return jnp.where(g < lo, g, jnp.where(g < lo + mid, g + n_hi, g - mid))

        return [
            Placement(rows_in, self.DP, cuts, offset_in, zero_rows=(self.D_IN, self.DP - self.D_IN)),
            Placement(self.HM * MLA_QW // 4, self.HM * MLA_QW),
            Placement(self.HM * (MLA_NOPE + MLA_DV) // 4, self.HM * (MLA_NOPE + MLA_DV)),
            Placement(self.RV // 4, self.RV),
            Placement(self.MV // 4, self.MV),
            Placement(self.D // 4, self.D),
        ]


def _rope_tables(positions, dim, width):
    inv = 1.0 / (ROPE_BASE ** (jnp.arange(0, dim, 2, dtype=F32) / dim))
    ang = positions.astype(F32)[:, None] * inv
    cos, sin = jnp.cos(ang), jnp.sin(ang)
    pad = jnp.zeros((positions.shape[0], width - dim), F32)
    return jnp.concatenate([cos, cos, pad], axis=1), jnp.concatenate([-sin, sin, pad], axis=1)


def _tiles(x):
    return [x[:, t * LANE : (t + 1) * LANE] for t in range(x.shape[1] // LANE)]


def _cat(parts):
    return parts[0] if len(parts) == 1 else jnp.concatenate(parts, axis=1)


def _rms(x, eps=EPS):
    return lax.rsqrt(jnp.mean(x * x, axis=1, keepdims=True) + eps)


def layer_fwd(dm, l, x, shift, scale, gate, g_norm, g_cq, g_ckv, w, tabs, host=None, host_in=None):
    cos_r, sin_r, cos_m, sin_m = tabs
    nm = lambda s: f"l{l}_{s}"

    def f_norm(x, g, scale, shift):
        return [x * _rms(x) * g * (1.0 + scale) + shift], []

    (h,) = rowwise(f_norm, [x], [g_norm, scale, shift], [(dm.D, BF16)], tm=256, name=nm("norm"))
    proj, *late = _as_list(matmul(h, w["in"], tb=True, b_split=True, name=nm("mm_in"), tn=1920, host=host_in))
    if host_in is not None:
        w = dict(w, **dict(zip(("uq", "ukv", "ret", "mla", "out"), late)))

    def f_rope_ret(rq, rk, cos, sin):
        rq, rk = rq.astype(F32), rk.astype(F32)
        q = _cat([_rope_tile(t, cos, sin, RET_DK // 2) for t in _tiles(rq)])
        k = _cat([_rope_tile(t, cos, sin, RET_DK // 2) * (RET_DK**-0.5) for t in _tiles(rk)])
        return [q, k], []

    rq, rk = rowwise(
        f_rope_ret,
        [win(proj, dm.o_rq, dm.RQ), win(proj, dm.o_rk, dm.RQ), cos_r, sin_r],
        [],
        [(dm.RQ, BF16)] * 2,
        tm=512,
        name=nm("rope_ret"),
    )
    o_ret, a_ret, states = ret_fwd(rq, rk, proj, dm.o_rv, dm.o_rg, heads=dm.HR, tb=RET_BLOCK, name=nm("ret_fwd"))

    def f_prep(cq, ckv, kr, cos, sin, g_cq, g_ckv):
        cq, ckv, kr = cq.astype(F32), ckv.astype(F32), kr.astype(F32)
        return [cq * _rms(cq) * g_cq, ckv * _rms(ckv) * g_ckv, _rope_tile(kr, cos, sin, MLA_ROPE // 2)], []

    cqn, ckvn, krr = rowwise(
        f_prep,
        [win(proj, dm.o_cq, dm.QR), win(proj, dm.o_ckv, dm.KR), win(proj, dm.o_kr, LANE), cos_m, sin_m],
        [g_cq, g_ckv],
        [(dm.QR, BF16), (dm.KR, BF16), (LANE, BF16)],
        tm=512,
        name=nm("mla_prep"),
    )
    q_raw = matmul(cqn, w["uq"], tb=True, b_split=True, name=nm("mm_uq"), tn=4096)
    kv = matmul(ckvn, w["ukv"], tb=True, b_split=True, name=nm("mm_ukv"), tn=4096)

    def f_rope_q(q, cos, sin):
        t = _tiles(q.astype(F32))
        rot = [t[n] if n % 2 == 0 else _rope_tile(t[n], cos, sin, MLA_ROPE // 2) for n in range(len(t))]
        return [_cat([r * (MLA_SCALE * LOG2E) for r in rot])], []

    (q,) = rowwise(f_rope_q, [q_raw, cos_m, sin_m], [], [(dm.HM * MLA_QW, BF16)], tm=256, name=nm("rope_q"))
    o_mla, a_mla, lse, *hosted = attn_fwd(
        q, kv, krr, proj, dm.o_mg, heads=dm.HM, tq=ATTN_BLOCK, name=nm("attn_fwd"), host=host
    )

    y_ret = matmul(a_ret, w["ret"], b_split=True, out_dtype=F32, name=nm("mm_ret"))
    y_mla = matmul(a_mla, w["mla"], b_split=True, out_dtype=F32, name=nm("mm_mla"))

    def f_merge(y_ret, y_mla, bga, bgb):
        return [_sigmoid(bga.astype(F32)) * y_ret + _sigmoid(bgb.astype(F32)) * y_mla], []

    (merged,) = rowwise(
        f_merge, [y_ret, y_mla, win(proj, dm.o_bga, dm.D), win(proj, dm.o_bgb, dm.D)], [], [(dm.D, BF16)], tm=256, name=nm("merge")
    )
    out = matmul(merged, w["out"], b_split=True, out_dtype=F32, name=nm("mm_out"))

    def f_resid(x, out, gate):
        return [x + gate * out], []

    (x_new,) = rowwise(f_resid, [x, out], [gate], [(dm.D, F32)], tm=256, name=nm("resid"))
    saved = dict(
        x=x, h=h, proj=proj, rq=rq, rk=rk, o_ret=o_ret, a_ret=a_ret, states=states, cqn=cqn, ckvn=ckvn, krr=krr, q=q, kv=kv,
        o_mla=o_mla, a_mla=a_mla, lse=lse, y_ret=y_ret, y_mla=y_mla, merged=merged, out=out,
    )
    return x_new, saved, hosted, w


def _as_list(v):
    return list(v) if isinstance(v, (list, tuple)) else [v]


def layer_bwd(dm, l, dx_out, sv, shift, scale, gate, g_norm, g_cq, g_ckv, w, tabs, host=None, exchange_dh=None):
    cos_r, sin_r, cos_m, sin_m = tabs
    nm = lambda s: f"l{l}_{s}"
    proj = sv["proj"]

    def b_resid(dx, out, gate):
        return [dx * gate], [_sum0(dx * out)]

    dout, d_gate = rowwise(b_resid, [dx_out, sv["out"]], [gate], [(dm.D, BF16)], [dm.D], tm=256, name=nm("resid_bwd"))
    dmerged = matmul(dout, w["out"], tb=True, b_split=True, name=nm("mm_dmerged"))
    dw_out = matmul(sv["merged"], dout, ta=True, out_split=True, tn=2048, name=nm("mm_dw_out"))

    def b_merge(dmg, y_ret, y_mla, bga, bgb):
        dmg = dmg.astype(F32)
        ga, gb = _sigmoid(bga.astype(F32)), _sigmoid(bgb.astype(F32))
        d_gates = jnp.concatenate([dmg * y_ret * ga * (1.0 - ga), dmg * y_mla * gb * (1.0 - gb)], axis=1)
        return [d_gates, dmg * ga, dmg * gb], []

    dproj, dy_ret, dy_mla = rowwise(
        b_merge,
        [dmerged, sv["y_ret"], sv["y_mla"], win(proj, dm.o_bga, dm.D), win(proj, dm.o_bgb, dm.D)],
        [],
        [(2 * dm.D, BF16), (dm.D, BF16), (dm.D, BF16)],
        tm=256,
        name=nm("merge_bwd"),
        into=(dm.DP, dm.o_bga),
    )
    da_ret = matmul(dy_ret, w["ret"], tb=True, b_split=True, name=nm("mm_da_ret"))
    dw_ret = matmul(sv["a_ret"], dy_ret, ta=True, out_split=True, tn=2048, name=nm("mm_dw_ret"))
    da_mla = matmul(dy_mla, w["mla"], tb=True, b_split=True, name=nm("mm_da_mla"))
    dw_mla = matmul(sv["a_mla"], dy_mla, ta=True, out_split=True, tn=2048, name=nm("mm_dw_mla"))

    def b_ret_gate(da, rg, o):
        da, rg = da.astype(F32), rg.astype(F32)
        do_parts, drg_parts = [], []
        for hh in range(dm.HR):
            sl = slice(hh * RET_DV, (hh + 1) * RET_DV)
            oh, dah, rgh = o[:, sl], da[:, sl], rg[:, sl]
            mu = jnp.mean(oh, axis=1, keepdims=True)
            d = oh - mu
            r = lax.rsqrt(jnp.mean(d * d, axis=1, keepdims=True) + EPS)
            n = d * r
            dn = dah * _silu(rgh)
            drg_parts.append(dah * n * _dsilu(rgh))
            do_parts.append(r * (dn - jnp.mean(dn, axis=1, keepdims=True) - n * jnp.mean(dn * n, axis=1, keepdims=True)))
        return [_cat(drg_parts), _cat(do_parts)], []

    dproj, do_ret = rowwise(
        b_ret_gate, [da_ret, win(proj, dm.o_rg, dm.RV), sv["o_ret"]], [], [(dm.RV, BF16)] * 2, tm=256, name=nm("ret_gate_bwd"),
        into=(dproj, dm.o_rg),
    )
    dq_rot, dk_rot, dproj = ret_bwd(
        sv["rq"], sv["rk"], proj, dm.o_rv, sv["states"], do_ret, heads=dm.HR, tb=RET_BLOCK, name=nm("ret_bwd"), into=(dproj, dm.o_rv)
    )

    def b_rope_ret(dq, dk, cos, sin):
        dq, dk = dq.astype(F32), dk.astype(F32)
        q = [_rope_tile(t, cos, sin, RET_DK // 2, inverse=True) for t in _tiles(dq)]
        k = [_rope_tile(t, cos, sin, RET_DK // 2, inverse=True) * (RET_DK**-0.5) for t in _tiles(dk)]
        return [_cat(q + k)], []

    (dproj,) = rowwise(
        b_rope_ret, [dq_rot, dk_rot, cos_r, sin_r], [], [(2 * dm.RQ, BF16)], tm=512, name=nm("rope_ret_bwd"), into=(dproj, dm.o_rq)
    )

    def b_mla_gate(da, mg, o):
        da, mg = da.astype(F32), mg.astype(F32)
        return [da * o * _dsilu(mg), da * _silu(mg)], []

    dproj, do_mla = rowwise(
        b_mla_gate, [da_mla, win(proj, dm.o_mg, dm.MV), sv["o_mla"]], [], [(dm.MV, BF16)] * 2, tm=256, name=nm("mla_gate_bwd"),
        into=(dproj, dm.o_mg),
    )
    dq_att, dkv, dkr_heads, *hosted = attn_bwd(
        sv["q"], sv["kv"], sv["krr"], do_mla, sv["o_mla"], sv["lse"], heads=dm.HM, tq=ATTN_BLOCK, name=nm("attn_bwd"), host=host
    )

    def b_rope_q(dq, cos, sin):
        t = _tiles(dq.astype(F32))
        return [_cat([t[n] if n % 2 == 0 else _rope_tile(t[n], cos, sin, MLA_ROPE // 2, inverse=True) for n in range(len(t))])], []

    (dq_raw,) = rowwise(b_rope_q, [dq_att, cos_m, sin_m], [], [(dm.HM * MLA_QW, BF16)], tm=256, name=nm("rope_q_bwd"))
    dcqn = matmul(dq_raw, w["uq"], b_split=True, name=nm("mm_dcqn"))
    dw_uq = matmul(dq_raw, sv["cqn"], ta=True, out_split=True, tn=2048, name=nm("mm_dw_uq"))
    dckvn = matmul(dkv, w["ukv"], b_split=True, name=nm("mm_dckvn"))
    dw_ukv = matmul(dkv, sv["ckvn"], ta=True, out_split=True, tn=2048, name=nm("mm_dw_ukv"))

    def b_prep(dcqn, dckvn, cq, ckv, cos, sin, dkr_h, g_cq, g_ckv):
        outs, accs = [], []
        for dn, z, g in ((dcqn, cq, g_cq), (dckvn, ckv, g_ckv)):
            dn, z = dn.astype(F32), z.astype(F32)
            n = z * _rms(z)
            dng = dn * g
            outs.append(_rms(z) * (dng - n * jnp.mean(dng * n, axis=1, keepdims=True)))
            accs.append(_sum0(dn * n))
        dkr = dkr_h[0]
        for hh in range(1, dm.HM):
            dkr = dkr + dkr_h[hh]
        return [_cat(outs), _rope_tile(dkr, cos, sin, MLA_ROPE // 2, inverse=True)], accs

    dproj, d_kr, dg_cq, dg_ckv = rowwise(
        b_prep,
        [dcqn, dckvn, win(proj, dm.o_cq, dm.QR), win(proj, dm.o_ckv, dm.KR), cos_m, sin_m, dkr_heads],
        [g_cq, g_ckv],
        [(dm.QR + dm.KR, BF16), (LANE, BF16)],
        [dm.QR, dm.KR],
        tm=256,
        name=nm("mla_prep_bwd"),
        into=(dproj, dm.o_cq),
    )
    (dproj,) = rowwise(lambda a: ([a], []), [d_kr], [], [(LANE, BF16)], tm=512, name=nm("place_dkr"), into=(dproj, dm.o_kr))
    dw_in = matmul(dproj, sv["h"], ta=True, out_split=True, tn=2048, name=nm("mm_dw_in"))
    dws = dict(w_in=dw_in, w_uq=dw_uq, w_ukv=dw_ukv, w_ret_proj=dw_ret, w_mla_proj=dw_mla, w_out=dw_out)
    host_dh = exchange_dh(dws) if exchange_dh else None
    dh, *hosted_dh = _as_list(
        matmul(dproj, w["in"], b_split=True, out_dtype=F32, name=nm("mm_dh"), tn=2048, tk=1920, host=host_dh)
    )

    def b_norm(dh, x, dx_res, g, scale):
        r = _rms(x)
        xn = x * r
        dxn = dh * g * (1.0 + scale)
        dx = dx_res + r * (dxn - xn * jnp.mean(dxn * xn, axis=1, keepdims=True))
        return [dx], [_sum0(dh), _sum0(dh * xn * g), _sum0(dh * (1.0 + scale) * xn)]

    dx, d_shift, d_scale, dg_norm = rowwise(
        b_norm, [dh, sv["x"], dx_out], [g_norm, scale], [(dm.D, F32)], [dm.D] * 3, tm=256, name=nm("norm_bwd")
    )
    dvec = dict(mod=jnp.concatenate([d_shift, d_scale, d_gate], axis=1), g_norm=dg_norm, g_cq=dg_cq, g_ckv=dg_ckv)
    return dx, dws, dvec, hosted, hosted_dh


def adamw(w, g, m, v, name):
    shape = w.shape
    cols = shape[-1]
    view = lambda a: a.reshape(-1, cols)

    def f(w, g, m, v):
        m = ADAM_B1 * m + (1.0 - ADAM_B1) * g
        v = ADAM_B2 * v + (1.0 - ADAM_B2) * (g * g)
        m_hat = m / (1.0 - ADAM_B1**ADAM_STEP)
        v_hat = v / (1.0 - ADAM_B2**ADAM_STEP)
        delta = -ADAM_LR * (m_hat / (jnp.sqrt(v_hat) + ADAM_EPS) + ADAM_WD * w)
        return [delta, m, v], []

    tm = max(8, min(512, (400_000 // cols) // 8 * 8))
    delta, m, v = rowwise(f, [view(w), view(g), view(m), view(v)], [], [(cols, F32)] * 3, tm=tm, name=name)
    return delta.reshape(shape), m.reshape(shape), v.reshape(shape)


def _add_rows(fn, rows, cols, dtype, name):
    tm = max(8, min(512, (400_000 // cols) // 8 * 8))
    return rowwise(lambda *a: ([fn(*a)], []), rows, [], [(cols, dtype)], tm=tm, name=name)[0]


BIG = ("w_in", "w_uq", "w_ukv", "w_ret_proj", "w_mla_proj", "w_out")
COL_SHARDED = ("w_in", "w_uq", "w_ukv")


def kernel(x, c, positions, w_mod, b_mod, g_norm, w_in, g_cq, g_ckv, w_uq, w_ukv, w_ret_proj, w_mla_proj, w_out, g_final, loss_target, m_w_mod, m_b_mod, m_g_norm, m_w_in, m_g_cq, m_g_ckv, m_w_uq, m_w_ukv, m_w_ret_proj, m_w_mla_proj, m_w_out, m_g_final, v_w_mod, v_b_mod, v_g_norm, v_w_in, v_g_cq, v_g_ckv, v_w_uq, v_w_ukv, v_w_ret_proj, v_w_mla_proj, v_w_out, v_g_final):
    weights = dict(w_mod=w_mod, b_mod=b_mod, g_norm=g_norm, w_in=w_in, g_cq=g_cq, g_ckv=g_ckv, w_uq=w_uq, w_ukv=w_ukv,
                   w_ret_proj=w_ret_proj, w_mla_proj=w_mla_proj, w_out=w_out, g_final=g_final)
    m_in = dict(w_mod=m_w_mod, b_mod=m_b_mod, g_norm=m_g_norm, w_in=m_w_in, g_cq=m_g_cq, g_ckv=m_g_ckv, w_uq=m_w_uq,
                w_ukv=m_w_ukv, w_ret_proj=m_w_ret_proj, w_mla_proj=m_w_mla_proj, w_out=m_w_out, g_final=m_g_final)
    v_in = dict(w_mod=v_w_mod, b_mod=v_b_mod, g_norm=v_g_norm, w_in=v_w_in, g_cq=v_g_cq, g_ckv=v_g_ckv, w_uq=v_w_uq,
                w_ukv=v_w_ukv, w_ret_proj=v_w_ret_proj, w_mla_proj=v_w_mla_proj, w_out=v_w_out, g_final=v_g_final)
    order = ("w_mod", "b_mod", "g_norm", "w_in", "g_cq", "g_ckv", "w_uq", "w_ukv", "w_ret_proj", "w_mla_proj", "w_out", "g_final")

    x = x[0]
    target = loss_target[0]
    S, D = x.shape
    L = w_mod.shape[0]
    dm = Dims(S, D, L, w_ret_proj.shape[1] * 4 // RET_DV, w_mla_proj.shape[1] * 4 // MLA_DV, g_cq.shape[1], g_ckv.shape[1])
    my_x, my_y, my_c = _me()
    my_chip = 2 * my_x + my_y
    my_dev = 2 * my_chip + my_c
    C3 = w_mod.shape[2]

    for table in (weights, m_in, v_in):
        table["w_in"] = jnp.swapaxes(table["w_in"], 1, 2)
    prepped = [
        cast_split(weights["w_in"], BF16, "prep_w_in"),
        dm.pad_uq_rows(transpose_split(w_uq, BF16, "prep_w_uq")),
        transpose_split(w_ukv, BF16, "prep_w_ukv"),
        cast_split(w_ret_proj, BF16, "prep_w_ret"),
        cast_split(w_mla_proj, BF16, "prep_w_mla"),
        cast_split(w_out, BF16, "prep_w_out"),
    ]
    places = dm.placements()
    zero_rows = jnp.zeros((places[0].zero_rows[1], D // 2), BF16)
    w_keys = ("in", "uq", "ukv", "ret", "mla", "out")
    layer_w = [dict(zip(w_keys, gather_layer(prepped[:1], zero_rows, places[:1], 0, "l0_gather_w_in")))]
    rest_of_layer0 = gather_layer(prepped[1:], None, places[1:], 0, None)

    c_all = allgather8(c, "gather_c").reshape(8, D)
    (c_act,) = rowwise(lambda z: ([_silu(z)], []), [c_all], [], [(D, BF16)], tm=8, name="silu_c")
    mod_part = jnp.stack([matmul(c_act, w_mod[l], out_dtype=F32, name=f"l{l}_mm_mod", tn=C3) for l in range(L)])
    mod_all = allgather8(mod_part, "gather_mod")
    mod_all = mod_all.reshape(4, 2, L, 8, C3)[:, 0].transpose(1, 2, 0, 3).reshape(L, 8, 3 * D) + b_mod[:, None, :]
    mod = lax.dynamic_index_in_dim(mod_all, my_dev, axis=1, keepdims=False)

    pos = positions[0]
    tabs = (*_rope_tables(pos, RET_DK, LANE), *_rope_tables(pos, MLA_ROPE, LANE))

    def vecs(l):
        return (mod[l : l + 1, :D], mod[l : l + 1, D : 2 * D], mod[l : l + 1, 2 * D :],
                g_norm[l : l + 1], g_cq[l : l + 1], g_ckv[l : l + 1])

    saved = []
    for l in range(L):
        host = gather_layer(prepped, zero_rows, places, l + 1, None) if l + 1 < L else None
        x, sv, hosted, layer_w[l] = layer_fwd(
            dm, l, x, *vecs(l), layer_w[l], tabs, host=host, host_in=rest_of_layer0 if l == 0 else None
        )
        saved.append(sv)
        if host:
            layer_w.append(dict(zip(w_keys, hosted)))

    def f_loss(x, t, g):
        xn = x * _rms(x)
        err = xn * g - t
        dy = err * (1.0 / D)
        dxn = dy * g
        dx = _rms(x) * (dxn - xn * jnp.mean(dxn * xn, axis=1, keepdims=True))
        part = jnp.sum(jnp.sum(err * err, axis=1, keepdims=True), axis=0, keepdims=True) * (0.5 / D)
        return [dx], [jnp.broadcast_to(part, (1, LANE)), _sum0(dy * xn)]

    dx, loss_part, dg_final = rowwise(f_loss, [x, target], [g_final.reshape(1, D)], [(D, F32)], [LANE, D], tm=256, name="loss_head")

    stacks = {n: None for n in BIG}
    dvec = {n: [None] * L for n in ("mod", "g_norm", "g_cq", "g_ckv")}

    def finish_grads(l, from_chips):
        mine = [
            _add_rows(lambda a: ((a[3].astype(F32) + a[0].astype(F32)) + a[1].astype(F32)) + a[2].astype(F32), [r], r.shape[-1],
                      F32, f"l{l}_rs_add4_{n}")
            for n, r in zip(BIG, from_chips)
        ]
        theirs = sibling_swap(mine, f"l{l}_rs_share")
        for n, a, b in zip(BIG, mine, theirs):
            if n == "w_uq":
                a, b = dm.unpad_uq_rows(a), dm.unpad_uq_rows(b)
            finish = untranspose_halves if n in ("w_uq", "w_ukv") else merge_halves
            stacks[n] = finish(a, b, my_c, stacks[n], l, L, f"l{l}_grad_{n}")

    def pair_sums(l, partial, from_sibling):
        return [add_pair(g, r, my_c, f"l{l}_rs_add2_{n}") for n, g, r in zip(BIG, partial, from_sibling)]

    def core_exchange(dws):
        return sibling_scatter([dws[n] for n in BIG], None)

    def last_exchanges(dws):
        partial = [dws[n] for n in BIG]
        return scatter_to_chips(pair_sums(0, partial, sibling_scatter(partial, "l0_rs_cores")), places, None)

    pair = None
    for l in reversed(range(L)):
        host = scatter_to_chips(pair, places, None) if pair is not None else None
        dx, dw_l, dv_l, hosted, hosted_dh = layer_bwd(
            dm, l, dx, saved[l], *vecs(l), layer_w[l], tabs, host=host, exchange_dh=core_exchange if l > 0 else last_exchanges
        )
        if host:
            finish_grads(l + 1, hosted)
        for n in dvec:
            dvec[n][l] = dv_l[n]
        if l > 0:
            pair = pair_sums(l, [dw_l[n] for n in BIG], hosted_dh)
        else:
            finish_grads(0, hosted_dh)
    grad_x = dx[None]

    pieces = [loss_part] + [jnp.concatenate(dvec[n], axis=1) for n in ("mod", "g_norm", "g_cq", "g_ckv")] + [dg_final]
    widths = [p.shape[1] for p in pieces]
    small_all = allgather8(jnp.concatenate(pieces, axis=1), "gather_small").reshape(8, sum(widths))

    def sum8_body(a_ref, o_ref):
        acc = a_ref[0:1, :]
        for d in range(1, 8):
            acc = acc + a_ref[d : d + 1, :]
        o_ref[...] = acc

    small = _pcall(sum8_body, name="sum_small", out_shape=jax.ShapeDtypeStruct((1, sum(widths)), F32))(small_all)
    offs = np.cumsum([0] + widths)
    loss = small[0, 0]
    g_small = {
        "b_mod": small[0, offs[1] : offs[2]].reshape(L, 3 * D),
        "g_norm": small[0, offs[2] : offs[3]].reshape(L, D),
        "g_cq": small[0, offs[3] : offs[4]].reshape(L, dm.QR),
        "g_ckv": small[0, offs[4] : offs[5]].reshape(L, dm.KR),
        "g_final": small[0, offs[5] : offs[6]],
    }

    dmod_all = small_all[:, offs[1] : offs[2]].reshape(8, L, 3 * D)
    dmod_mine = lax.dynamic_slice_in_dim(dmod_all, my_chip * C3, C3, axis=2)
    pad8 = lambda a: jnp.pad(a, ((0, LANE - 8), (0, 0)))
    grads = dict(g_small)
    grads["w_mod"] = jnp.stack(
        [matmul(pad8(c_act), pad8(dmod_mine[:, l]), ta=True, out_dtype=F32, name=f"l{l}_mm_dw_mod", tn=C3) for l in range(L)]
    )

    for n in BIG:
        grads[n] = stacks[n]

    deltas, new_m, new_v = {}, {}, {}
    for n in order:
        wv, gv, mv, vv = weights[n], grads[n], m_in[n], v_in[n]
        if wv.ndim == 1:
            wv, gv, mv, vv = (a.reshape(1, -1) for a in (wv, gv, mv, vv))
        d_, m_, v_ = adamw(wv, gv, mv, vv, f"adamw_{n}")
        deltas[n], new_m[n], new_v[n] = (a.reshape(weights[n].shape) for a in (d_, m_, v_))
        grads[n] = grads[n].reshape(weights[n].shape)
    for table in (grads, deltas, new_m, new_v):
        table["w_in"] = jnp.swapaxes(table["w_in"], 1, 2)

    return (loss, grad_x, *[grads[n] for n in order], *[deltas[n] for n in order], *[new_m[n] for n in order],
            *[new_v[n] for n in order])
```

```python
import functools

import jax
import jax.numpy as jnp
import numpy as np
from jax import lax
from jax.experimental import pallas as pl
from jax.experimental.pallas import tpu as pltpu

F32 = jnp.float32
BF16 = jnp.bfloat16
MESH = pl.DeviceIdType.MESH

VMEM_LIMIT_BYTES = 52 * 1024 * 1024
LANE = 128

CHUNK = 64
EPS = 1e-6
NEG_INF = -1e30
ROPE_BASE = 10000.0
RET_DK = 128
RET_DV = 256
MLA_NOPE = 128
MLA_ROPE = 64
MLA_DV = 128
MLA_QW = 256
MLA_SCALE = float((MLA_NOPE + MLA_ROPE) ** -0.5)
LOG2E = float(np.log2(np.e))
LN2 = float(np.log(2.0))
ATTN_BLOCK = 512
ATTN_FWD_HEADS_PER_STEP = 8
ATTN_HEADS_PER_STEP = 2
RET_BLOCK = 256
RET_HEADS_PER_STEP = 8

ADAM_LR = 0.001
ADAM_B1 = 0.9
ADAM_B2 = 0.999
ADAM_EPS = 1e-08
ADAM_WD = 0.01
ADAM_STEP = 10


def _pcall(body, **kw):
    return pl.pallas_call(body, **kw)


def _params(n_grid):
    return pltpu.CompilerParams(dimension_semantics=("arbitrary",) * n_grid, vmem_limit_bytes=VMEM_LIMIT_BYTES)


def _pick(dim, target, mult=LANE):
    if dim <= target:
        return dim
    best = None
    for t in range(mult, target + 1, mult):
        if dim % t == 0:
            best = t
    assert best is not None, (dim, target, mult)
    return best


def matmul(
    a, b, *, ta=False, tb=False, b_split=False, out_split=False, out_dtype=BF16, name, tm=512, tn=1024, tk=2048, host=None
):
    (M, K) = (a.shape[1], a.shape[0]) if ta else a.shape
    b_rows, b_cols = (b.shape[1], 2 * b.shape[2]) if b_split else b.shape
    (K2, N) = (b_cols, b_rows) if tb else (b_rows, b_cols)
    assert K == K2, (a.shape, b.shape, ta, tb)
    whole_k = b_split and tb and tk >= K and not ta
    whole_n = out_split and tn >= N and not (b_split and not tb)
    whole_bn = b_split and not tb and tn >= N and not out_split
    n_cap = N if (whole_n or whole_bn) else (N // 2 if (out_split or (b_split and not tb)) else N)
    k_cap = K if whole_k else (K // 2 if (b_split and tb) else K)
    tm, tn, tk = _pick(M, tm, 8 if M < LANE else LANE), _pick(n_cap, min(tn, n_cap)), _pick(k_cap, min(tk, k_cap))
    nk = K // tk
    njh, nkh = max((N // 2) // tn, 1), max((K // 2) // tk, 1)
    dn = (((0 if ta else 1,), (1 if tb else 0,)), ((), ()))

    def body(a_ref, b_ref, o_ref, *scratch):
        if whole_k:
            a_blk = a_ref[...].astype(BF16)
            prod = lax.dot_general(a_blk[:, : K // 2], b_ref[0].astype(BF16), dn, preferred_element_type=F32)
            prod += lax.dot_general(a_blk[:, K // 2 :], b_ref[1].astype(BF16), dn, preferred_element_type=F32)
        elif whole_bn:
            a_blk = a_ref[...].astype(BF16)
            halves = [lax.dot_general(a_blk, b_ref[half].astype(BF16), dn, preferred_element_type=F32) for half in range(2)]
            prod = jnp.concatenate(halves, axis=1)
        else:
            b_blk = b_ref[0] if b_split else b_ref[...]
            prod = lax.dot_general(a_ref[...].astype(BF16), b_blk.astype(BF16), dn, preferred_element_type=F32)

        def store(v):
            if whole_n:
                o_ref[0] = v[:, : N // 2].astype(o_ref.dtype)
                o_ref[1] = v[:, N // 2 :].astype(o_ref.dtype)
            elif out_split:
                o_ref[0] = v.astype(o_ref.dtype)
            else:
                o_ref[...] = v.astype(o_ref.dtype)

        if nk == 1:
            store(prod)
            return
        (acc_ref,) = scratch
        k = pl.program_id(2)

        @pl.when(k == 0)
        def _():
            acc_ref[...] = prod

        @pl.when(k > 0)
        def _():
            acc_ref[...] += prod

        @pl.when(k == nk - 1)
        def _():
            store(acc_ref[...])

    a_spec = pl.BlockSpec((tk, tm), lambda i, j, k: (k, i)) if ta else pl.BlockSpec((tm, tk), lambda i, j, k: (i, k))
    if whole_k:
        b_spec = pl.BlockSpec((2, tn, K // 2), lambda i, j, k: (0, j, 0))
    elif whole_bn:
        b_spec = pl.BlockSpec((2, tk, N // 2), lambda i, j, k: (0, k, 0))
    elif b_split and tb:
        b_spec = pl.BlockSpec((1, tn, tk), lambda i, j, k: (k // nkh, j, k % nkh))
    elif b_split:
        b_spec = pl.BlockSpec((1, tk, tn), lambda i, j, k: (j // njh, k, j % njh))
    elif tb:
        b_spec = pl.BlockSpec((tn, tk), lambda i, j, k: (j, k))
    else:
        b_spec = pl.BlockSpec((tk, tn), lambda i, j, k: (k, j))
    if whole_n:
        out_spec = pl.BlockSpec((2, tm, N // 2), lambda i, j, k: (0, i, 0))
        out_shape = jax.ShapeDtypeStruct((2, M, N // 2), out_dtype)
    elif out_split:
        out_spec = pl.BlockSpec((1, tm, tn), lambda i, j, k: (j // njh, i, j % njh))
        out_shape = jax.ShapeDtypeStruct((2, M, N // 2), out_dtype)
    else:
        out_spec = pl.BlockSpec((tm, tn), lambda i, j, k: (i, j))
        out_shape = jax.ShapeDtypeStruct((M, N), out_dtype)
    grid = (M // tm, N // tn, nk)
    acc_shapes = [] if nk == 1 else [pltpu.VMEM((tm, tn), F32)]
    if host is None:
        return _pcall(
            body, name=name, grid=grid, in_specs=[a_spec, b_spec], out_specs=out_spec, out_shape=out_shape,
            scratch_shapes=acc_shapes, compiler_params=_params(3),
        )(a, b)

    n_hi, n_ho, n_steps = len(host.ins), len(host.out_shapes), grid[0] * grid[1] * grid[2]

    def hosting_body(a_ref, b_ref, *rest):
        host_in, rest = rest[:n_hi], rest[n_hi:]
        o_ref, rest = rest[0], rest[1:]
        host_out, rest = rest[:n_ho], rest[n_ho:]
        scratch, host_sems = rest[: len(acc_shapes)], rest[len(acc_shapes) :]
        step = (pl.program_id(0) * grid[1] + pl.program_id(1)) * grid[2] + pl.program_id(2)
        refs = (host_in, host_out, host_sems)

        @pl.when(step == 0)
        def _():
            host.start(*refs)

        @pl.when(step == n_steps // 2)
        def _():
            host.relay(*refs)

        body(a_ref, b_ref, o_ref, *scratch)

        @pl.when(step == n_steps - 1)
        def _():
            host.finish(*refs)

    return _pcall(
        hosting_body,
        name=name,
        grid=grid,
        in_specs=[a_spec, b_spec, *host.in_specs],
        out_specs=[out_spec, *host.out_specs],
        out_shape=[out_shape, *host.out_shapes],
        scratch_shapes=[*acc_shapes, *host.scratch],
        compiler_params=_params(3),
    )(a, b, *host.ins)


def win(arr, off, width):
    assert off % width == 0 and off + width <= arr.shape[1], (arr.shape, off, width)
    return (arr, off // width, width)


def rowwise(fn, rows, vecs, outs, accs=(), *, tm, name, into=None):
    rows = [r if isinstance(r, tuple) else (r, 0, r.shape[-1]) for r in rows]
    S = rows[0][0].shape[-2]
    tm = _pick(S, tm, 8)
    n_rows, n_vecs, n_outs = len(rows), len(vecs), len(outs)
    aliased = into is not None and not isinstance(into[0], int)
    n_alias = 1 if aliased else 0

    def body(*refs):
        ins = [r[...] for r in refs[: n_rows + n_vecs]]
        refs = refs[: n_rows + n_vecs] + refs[n_rows + n_vecs + n_alias :]
        out_refs = refs[n_rows + n_vecs : n_rows + n_vecs + n_outs]
        acc_refs = refs[n_rows + n_vecs + n_outs :]
        res, acc = fn(*ins)
        for r, v in zip(out_refs, res, strict=True):
            r[...] = v.astype(r.dtype)
        if acc_refs:
            i = pl.program_id(0)

            @pl.when(i == 0)
            def _():
                for r, v in zip(acc_refs, acc, strict=True):
                    r[...] = v

            @pl.when(i > 0)
            def _():
                for r, v in zip(acc_refs, acc, strict=True):
                    r[...] += v

    in_specs = []
    for arr, blk, w in rows:
        if arr.ndim == 3:
            in_specs.append(pl.BlockSpec((arr.shape[0], tm, w), lambda i: (0, i, 0)))
        else:
            in_specs.append(pl.BlockSpec((tm, w), functools.partial(lambda i, blk: (i, blk), blk=blk)))
    for v in vecs:
        in_specs.append(pl.BlockSpec(v.shape, functools.partial(lambda i, nd: (0,) * nd, nd=v.ndim)))
    out_specs = [pl.BlockSpec((tm, w), lambda i: (i, 0)) for w, _ in outs]
    out_specs += [pl.BlockSpec((1, w), lambda i: (0, 0)) for w in accs]
    out_shape = [jax.ShapeDtypeStruct((S, w), dt) for w, dt in outs]
    out_shape += [jax.ShapeDtypeStruct((1, w), F32) for w in accs]
    operands = [*[r[0] for r in rows], *vecs]
    aliases = {}
    if into is not None:
        target, off = into
        w0, dt0 = outs[0]
        total = target if isinstance(target, int) else target.shape[1]
        assert off % w0 == 0 and off + w0 <= total
        out_specs[0] = pl.BlockSpec((tm, w0), functools.partial(lambda i, blk: (i, blk), blk=off // w0))
        out_shape[0] = jax.ShapeDtypeStruct((S, total), dt0)
        if aliased:
            in_specs.append(pl.BlockSpec(memory_space=pl.ANY))
            operands.append(target)
            aliases = {len(operands) - 1: 0}
    res = _pcall(
        body,
        name=name,
        grid=(S // tm,),
        in_specs=in_specs,
        out_specs=out_specs,
        out_shape=out_shape,
        input_output_aliases=aliases,
        compiler_params=_params(1),
    )(*operands)
    return res


def _sum0(v):
    return jnp.sum(v, axis=0, keepdims=True)


def _sigmoid(z):
    return 1.0 / (1.0 + jnp.exp(-z))


def _silu(z):
    return z * _sigmoid(z)


def _dsilu(z):
    s = _sigmoid(z)
    return s * (1.0 + z * (1.0 - s))


def _swap_half(x, half):
    if 2 * half == LANE:
        return pltpu.roll(x, half, 1)
    lane = lax.broadcasted_iota(jnp.int32, x.shape, 1)
    return jnp.where((lane % (2 * half)) < half, pltpu.roll(x, LANE - half, 1), pltpu.roll(x, half, 1))


def _rope_tile(x, cosf, sinf, half, inverse=False):
    sw = _swap_half(x, half)
    return x * cosf - sw * sinf if inverse else x * cosf + sw * sinf


def _tri_tables(n, by_key):
    if by_key:
        pairs = [(i, j) for j in range(n) for i in range(j, n)]
    else:
        pairs = [(i, j) for i in range(n) for j in range(i + 1)]
    return (np.array([p[0] for p in pairs], np.int32), np.array([p[1] for p in pairs], np.int32))


def _attn_scores(q, kn, kr, i, j, tq, masked):
    k = jnp.concatenate([kn, kr], axis=1)
    s = lax.dot_general(q, k, (((1,), (1,)), ((), ())), preferred_element_type=F32)
    if masked:
        rc = (i * tq + lax.broadcasted_iota(jnp.int32, s.shape, 0)) // CHUNK
        cc = (j * tq + lax.broadcasted_iota(jnp.int32, s.shape, 1)) // CHUNK
        s = jnp.where(cc <= rc, s, NEG_INF)
    return s, k


def attn_fwd(q, kv, kr, proj, mg_off, *, heads, tq, name, host=None):
    S = q.shape[0]
    tq = _pick(S, tq, CHUNK)
    n = S // tq
    it, jt = _tri_tables(n, by_key=False)
    T = len(it)
    mg_blk = mg_off // MLA_DV
    n_hi, n_ho = (len(host.ins), len(host.out_shapes)) if host else (0, 0)

    hp = min(ATTN_FWD_HEADS_PER_STEP, heads)
    assert heads % hp == 0 and mg_blk % hp == 0
    groups = heads // hp

    def body(it_ref, jt_ref, q_ref, kv_ref, kr_ref, mg_ref, *rest):
        host_in, rest = rest[:n_hi], rest[n_hi:]
        (o_ref, a_ref, lse_ref), rest = rest[:3], rest[3:]
        host_out, rest = rest[:n_ho], rest[n_ho:]
        (m_sc, acc_sc), host_sems = rest[:2], rest[2:]
        h, t = pl.program_id(0), pl.program_id(1)
        i, j = it_ref[t], jt_ref[t]
        if host:
            _host_steps(host, (host_in, host_out, host_sems), h, t, groups, T, before=True)

        @pl.when(j == 0)
        def _():
            m_sc[...] = jnp.full(m_sc.shape, NEG_INF, F32)
            acc_sc[...] = jnp.zeros(acc_sc.shape, F32)

        def step(masked):
            kr = kr_ref[...]
            for u in range(hp):
                q = q_ref[:, u * MLA_QW : (u + 1) * MLA_QW]
                kn = kv_ref[:, u * MLA_QW : u * MLA_QW + MLA_NOPE]
                v = kv_ref[:, u * MLA_QW + MLA_NOPE : (u + 1) * MLA_QW]
                s, _ = _attn_scores(q, kn, kr, i, j, tq, masked)
                m_prev = m_sc[u]
                m_new = jnp.maximum(m_prev, jnp.max(s, axis=1, keepdims=True))
                p = jnp.exp2((s - m_new).astype(BF16))
                alpha = jnp.exp2(m_prev - m_new)
                v_ones = jnp.concatenate([v, jnp.ones(v.shape, BF16)], axis=1)
                acc_sc[u] = alpha * acc_sc[u] + jnp.dot(p, v_ones, preferred_element_type=F32)
                m_sc[u] = m_new

        @pl.when(j < i)
        def _():
            step(False)

        @pl.when(j == i)
        def _():
            step(True)
            for u in range(hp):
                cols = slice(u * MLA_DV, (u + 1) * MLA_DV)
                l = acc_sc[u, :, MLA_DV:]
                o = acc_sc[u, :, :MLA_DV] / l
                o_ref[:, cols] = o
                a_ref[:, cols] = (o * _silu(mg_ref[:, cols].astype(F32))).astype(a_ref.dtype)
                lse_ref[u] = m_sc[u] + jnp.log2(l[:, :1])

        if host:
            _host_steps(host, (host_in, host_out, host_sems), h, t, groups, T, before=False)

    grid_spec = pltpu.PrefetchScalarGridSpec(
        num_scalar_prefetch=2,
        grid=(groups, T),
        in_specs=[
            pl.BlockSpec((tq, hp * MLA_QW), lambda h, t, it, jt: (it[t], h)),
            pl.BlockSpec((tq, hp * MLA_QW), lambda h, t, it, jt: (jt[t], h)),
            pl.BlockSpec((tq, LANE), lambda h, t, it, jt: (jt[t], 0)),
            pl.BlockSpec((tq, hp * MLA_DV), lambda h, t, it, jt: (it[t], mg_blk // hp + h)),
            *(host.in_specs if host else []),
        ],
        out_specs=[
            pl.BlockSpec((tq, hp * MLA_DV), lambda h, t, it, jt: (it[t], h)),
            pl.BlockSpec((tq, hp * MLA_DV), lambda h, t, it, jt: (it[t], h)),
            pl.BlockSpec((hp, tq, 1), lambda h, t, it, jt: (h, it[t], 0)),
            *(host.out_specs if host else []),
        ],
        scratch_shapes=[
            pltpu.VMEM((hp, tq, 1), F32),
            pltpu.VMEM((hp, tq, 2 * MLA_DV), F32),
            *(host.scratch if host else []),
        ],
    )
    return _pcall(
        body,
        name=name,
        grid_spec=grid_spec,
        out_shape=[
            jax.ShapeDtypeStruct((S, heads * MLA_DV), F32),
            jax.ShapeDtypeStruct((S, heads * MLA_DV), BF16),
            jax.ShapeDtypeStruct((heads, S, 1), F32),
            *(host.out_shapes if host else []),
        ],
        compiler_params=_params(2),
    )(jnp.asarray(it), jnp.asarray(jt), q, kv, kr, proj, *(host.ins if host else []))


def attn_bwd(q, kv, kr, do, o, lse, *, heads, tq, name, host=None):
    S = q.shape[0]
    tq = _pick(S, tq, CHUNK)
    n = S // tq
    it, jt = _tri_tables(n, by_key=True)
    T = len(it)
    n_hi, n_ho = (len(host.ins), len(host.out_shapes)) if host else (0, 0)

    hp = ATTN_HEADS_PER_STEP
    assert heads % hp == 0
    groups = heads // hp

    def body(it_ref, jt_ref, q_ref, kv_ref, kr_ref, do_ref, o_ref, lse_ref, *rest):
        host_in, rest = rest[:n_hi], rest[n_hi:]
        (dq_ref, dkv_ref, dkr_ref), rest = rest[:3], rest[3:]
        host_out, rest = rest[:n_ho], rest[n_ho:]
        (dq_acc, dk_sc, dv_sc), host_sems = rest[:3], rest[3:]
        h, t = pl.program_id(0), pl.program_id(1)
        i, j = it_ref[t], jt_ref[t]
        if host:
            _host_steps(host, (host_in, host_out, host_sems), h, t, groups, T, before=True)

        @pl.when(t == 0)
        def _():
            dq_acc[...] = jnp.zeros(dq_acc.shape, F32)

        @pl.when(i == j)
        def _():
            dk_sc[...] = jnp.zeros(dk_sc.shape, F32)
            dv_sc[...] = jnp.zeros(dv_sc.shape, F32)

        def step(masked):
            kr = kr_ref[...]
            rows = pl.ds(pl.multiple_of(i * tq, tq), tq)
            for u in range(hp):
                wide, narrow = slice(u * MLA_QW, (u + 1) * MLA_QW), slice(u * MLA_DV, (u + 1) * MLA_DV)
                q_blk, do_blk = q_ref[:, wide], do_ref[:, narrow]
                kn = kv_ref[:, u * MLA_QW : u * MLA_QW + MLA_NOPE]
                v = kv_ref[:, u * MLA_QW + MLA_NOPE : (u + 1) * MLA_QW]
                s, k = _attn_scores(q_blk, kn, kr, i, j, tq, masked)
                p = jnp.exp2((s - lse_ref[u]).astype(BF16))
                delta = jnp.sum(do_blk.astype(F32) * o_ref[:, narrow], axis=1, keepdims=True)
                dp = lax.dot_general(do_blk, v, _NT, preferred_element_type=F32)
                ds = p * (dp - delta).astype(BF16)
                dv_sc[u] += lax.dot_general(p, do_blk, _TN, preferred_element_type=F32)
                dk_sc[u] += lax.dot_general(ds, q_blk, _TN, preferred_element_type=F32)
                dq_acc[rows, wide] += jnp.dot(ds, k, preferred_element_type=F32)

        @pl.when(i == j)
        def _():
            step(True)

        @pl.when(i > j)
        def _():
            step(False)

        @pl.when(i == n - 1)
        def _():
            for u in range(hp):
                dkv_ref[:, u * MLA_QW : u * MLA_QW + MLA_NOPE] = (dk_sc[u, :, :MLA_NOPE] * LN2).astype(dkv_ref.dtype)
                dkv_ref[:, u * MLA_QW + MLA_NOPE : (u + 1) * MLA_QW] = dv_sc[u].astype(dkv_ref.dtype)
                dkr_ref[u] = dk_sc[u, :, MLA_NOPE:] * LN2

        @pl.when(t == T - 1)
        def _():
            dq_ref[...] = (dq_acc[...] * MLA_SCALE).astype(dq_ref.dtype)

        if host:
            _host_steps(host, (host_in, host_out, host_sems), h, t, groups, T, before=False)

    grid_spec = pltpu.PrefetchScalarGridSpec(
        num_scalar_prefetch=2,
        grid=(groups, T),
        in_specs=[
            pl.BlockSpec((tq, hp * MLA_QW), lambda h, t, it, jt: (it[t], h)),
            pl.BlockSpec((tq, hp * MLA_QW), lambda h, t, it, jt: (jt[t], h)),
            pl.BlockSpec((tq, LANE), lambda h, t, it, jt: (jt[t], 0)),
            pl.BlockSpec((tq, hp * MLA_DV), lambda h, t, it, jt: (it[t], h)),
            pl.BlockSpec((tq, hp * MLA_DV), lambda h, t, it, jt: (it[t], h)),
            pl.BlockSpec((hp, tq, 1), lambda h, t, it, jt: (h, it[t], 0)),
            *(host.in_specs if host else []),
        ],
        out_specs=[
            pl.BlockSpec((S, hp * MLA_QW), lambda h, t, it, jt: (0, h)),
            pl.BlockSpec((tq, hp * MLA_QW), lambda h, t, it, jt: (jt[t], h)),
            pl.BlockSpec((hp, tq, LANE), lambda h, t, it, jt: (h, jt[t], 0)),
            *(host.out_specs if host else []),
        ],
        scratch_shapes=[
            pltpu.VMEM((S, hp * MLA_QW), F32),
            pltpu.VMEM((hp, tq, MLA_QW), F32),
            pltpu.VMEM((hp, tq, MLA_DV), F32),
            *(host.scratch if host else []),
        ],
    )
    return _pcall(
        body,
        name=name,
        grid_spec=grid_spec,
        out_shape=[
            jax.ShapeDtypeStruct((S, heads * MLA_QW), BF16),
            jax.ShapeDtypeStruct((S, heads * (MLA_NOPE + MLA_DV)), BF16),
            jax.ShapeDtypeStruct((heads, S, LANE), F32),
            *(host.out_shapes if host else []),
        ],
        compiler_params=_params(2),
    )(jnp.asarray(it), jnp.asarray(jt), q, kv, kr, do, o, lse, *(host.ins if host else []))


def _host_steps(host, refs, h, t, heads, n_steps, before):
    if before:

        @pl.when((h == 0) & (t == 0))
        def _():
            host.start(*refs)

        @pl.when((h == heads // 2) & (t == n_steps // 2))
        def _():
            host.relay(*refs)

    else:

        @pl.when((h == heads - 1) & (t == n_steps - 1))
        def _():
            host.finish(*refs)


def _ret_consts(heads):
    h = np.arange(heads, dtype=np.float32)
    lg = np.log(np.float32(1.0) - np.float32(2.0) ** (np.float32(-5.0) - h)).astype(np.float32)
    idx = np.arange(CHUNK, dtype=np.float32)
    dmat = np.exp(np.abs(idx[:, None] - idx[None, :])[None] * lg[:, None, None]).astype(np.float32)
    xi = np.exp((idx + 1.0)[None, :] * lg[:, None]).astype(np.float32)
    zeta = np.exp((CHUNK - 1.0 - idx)[None, :] * lg[:, None]).astype(np.float32)
    dec = np.exp(np.float32(CHUNK) * lg).astype(np.float32)
    xi = np.broadcast_to(xi[:, :, None], (heads, CHUNK, RET_DK)).copy()
    zeta = np.broadcast_to(zeta[:, :, None], (heads, CHUNK, RET_DK)).copy()
    dec = np.broadcast_to(dec[:, None, None], (heads, 8, LANE)).copy()
    return jnp.asarray(dmat), jnp.asarray(xi), jnp.asarray(zeta), jnp.asarray(dec)


_NT = (((1,), (1,)), ((), ()))
_TN = (((0,), (0,)), ((), ()))


def _dot(a, b, dn=(((1,), (0,)), ((), ()))):
    return lax.dot_general(a.astype(BF16), b.astype(BF16), dn, preferred_element_type=F32)


def _const_specs(hp):
    return [
        pl.BlockSpec((hp, CHUNK, CHUNK), lambda h, b: (h, 0, 0)),
        pl.BlockSpec((hp, CHUNK, RET_DK), lambda h, b: (h, 0, 0)),
        pl.BlockSpec((hp, CHUNK, RET_DK), lambda h, b: (h, 0, 0)),
        pl.BlockSpec((hp, 8, LANE), lambda h, b: (h, 0, 0)),
    ]


def _ret_group(heads, *blks):
    hp = min(RET_HEADS_PER_STEP, heads)
    assert heads % hp == 0 and all(b % hp == 0 for b in blks)
    return hp


def ret_fwd(q, k, proj, v_off, rg_off, *, heads, tb, name):
    S = q.shape[0]
    tb = _pick(S, tb, CHUNK)
    cpb, nb = tb // CHUNK, S // tb
    v_blk, rg_blk = v_off // RET_DV, rg_off // RET_DV
    hp = _ret_group(heads, v_blk, rg_blk)

    def body(q_ref, k_ref, v_ref, rg_ref, dm_ref, xi_ref, ze_ref, dec_ref, o_ref, a_ref, st_ref, r_sc):
        @pl.when(pl.program_id(1) == 0)
        def _():
            r_sc[...] = jnp.zeros(r_sc.shape, F32)

        for c in range(cpb):
            sl = pl.ds(c * CHUNK, CHUNK)
            for u in range(hp):
                dm, xi, ze, dec = dm_ref[u], xi_ref[u], ze_ref[u], dec_ref[u, 0:1, 0:1]
                narrow, wide = slice(u * RET_DK, (u + 1) * RET_DK), slice(u * RET_DV, (u + 1) * RET_DV)
                qc, kc, vc = q_ref[sl, narrow], k_ref[sl, narrow], v_ref[sl, wide]
                r = r_sc[u]
                st_ref[u, c] = r.astype(BF16)
                s = _dot(qc, kc, _NT) * dm
                o = _dot(s, vc) + _dot(qc.astype(F32) * xi, r)
                r_sc[u] = r * dec + _dot(kc.astype(F32) * ze, vc, _TN)
                mu = jnp.mean(o, axis=1, keepdims=True)
                d = o - mu
                n = d * lax.rsqrt(jnp.mean(d * d, axis=1, keepdims=True) + EPS)
                o_ref[sl, wide] = o
                a_ref[sl, wide] = (n * _silu(rg_ref[sl, wide].astype(F32))).astype(a_ref.dtype)

    return _pcall(
        body,
        name=name,
        grid=(heads // hp, nb),
        in_specs=[
            pl.BlockSpec((tb, hp * RET_DK), lambda h, b: (b, h)),
            pl.BlockSpec((tb, hp * RET_DK), lambda h, b: (b, h)),
            pl.BlockSpec((tb, hp * RET_DV), lambda h, b: (b, v_blk // hp + h)),
            pl.BlockSpec((tb, hp * RET_DV), lambda h, b: (b, rg_blk // hp + h)),
            *_const_specs(hp),
        ],
        out_specs=[
            pl.BlockSpec((tb, hp * RET_DV), lambda h, b: (b, h)),
            pl.BlockSpec((tb, hp * RET_DV), lambda h, b: (b, h)),
            pl.BlockSpec((hp, cpb, RET_DK, RET_DV), lambda h, b: (h, b, 0, 0)),
        ],
        out_shape=[
            jax.ShapeDtypeStruct((S, heads * RET_DV), F32),
            jax.ShapeDtypeStruct((S, heads * RET_DV), BF16),
            jax.ShapeDtypeStruct((heads, S // CHUNK, RET_DK, RET_DV), BF16),
        ],
        scratch_shapes=[pltpu.VMEM((hp, RET_DK, RET_DV), F32)],
        compiler_params=_params(2),
    )(q, k, proj, proj, *_ret_consts(heads))


def ret_bwd(q, k, proj, v_off, states, do, *, heads, tb, name, into):
    S = q.shape[0]
    tb = _pick(S, tb, CHUNK)
    cpb, nb = tb // CHUNK, S // tb
    v_blk = v_off // RET_DV
    hp = _ret_group(heads, v_blk)
    target, dv_off = into
    dv_blk = dv_off // (hp * RET_DV)
    assert dv_off % (hp * RET_DV) == 0

    def body(q_ref, k_ref, v_ref, st_ref, do_ref, dm_ref, xi_ref, ze_ref, dec_ref, _, dq_ref, dk_ref, dv_ref, dr_sc):
        @pl.when(pl.program_id(1) == 0)
        def _():
            dr_sc[...] = jnp.zeros(dr_sc.shape, F32)

        for c in reversed(range(cpb)):
            sl = pl.ds(c * CHUNK, CHUNK)
            for u in range(hp):
                dm, xi, ze, dec = dm_ref[u], xi_ref[u], ze_ref[u], dec_ref[u, 0:1, 0:1]
                narrow, wide = slice(u * RET_DK, (u + 1) * RET_DK), slice(u * RET_DV, (u + 1) * RET_DV)
                qc, kc, vc, doc = q_ref[sl, narrow], k_ref[sl, narrow], v_ref[sl, wide], do_ref[sl, wide]
                r_prev = st_ref[u, c]
                dr = dr_sc[u]
                a = _dot(qc, kc, _NT) * dm
                ds = _dot(doc, vc, _NT) * dm
                kz = kc.astype(F32) * ze
                dq_ref[sl, narrow] = (_dot(ds, kc) + _dot(doc, r_prev, _NT) * xi).astype(dq_ref.dtype)
                dk_ref[sl, narrow] = (_dot(ds, qc, _TN) + _dot(vc, dr, _NT) * ze).astype(dk_ref.dtype)
                dv_ref[sl, wide] = (_dot(a, doc, _TN) + _dot(kz, dr)).astype(dv_ref.dtype)
                dr_sc[u] = dr * dec + _dot(qc.astype(F32) * xi, doc, _TN)

    rev = lambda h, b: (nb - 1 - b, h)
    return _pcall(
        body,
        name=name,
        grid=(heads // hp, nb),
        in_specs=[
            pl.BlockSpec((tb, hp * RET_DK), rev),
            pl.BlockSpec((tb, hp * RET_DK), rev),
            pl.BlockSpec((tb, hp * RET_DV), lambda h, b: (nb - 1 - b, v_blk // hp + h)),
            pl.BlockSpec((hp, cpb, RET_DK, RET_DV), lambda h, b: (h, nb - 1 - b, 0, 0)),
            pl.BlockSpec((tb, hp * RET_DV), rev),
            *_const_specs(hp),
            pl.BlockSpec(memory_space=pl.ANY),
        ],
        out_specs=[
            pl.BlockSpec((tb, hp * RET_DK), rev),
            pl.BlockSpec((tb, hp * RET_DK), rev),
            pl.BlockSpec((tb, hp * RET_DV), lambda h, b: (nb - 1 - b, dv_blk + h)),
        ],
        out_shape=[
            jax.ShapeDtypeStruct((S, heads * RET_DK), BF16),
            jax.ShapeDtypeStruct((S, heads * RET_DK), BF16),
            jax.ShapeDtypeStruct(target.shape, target.dtype),
        ],
        input_output_aliases={9: 2},
        scratch_shapes=[pltpu.VMEM((hp, RET_DK, RET_DV), F32)],
        compiler_params=_params(2),
    )(q, k, proj, states, do, *_ret_consts(heads), target)


def _me():
    return (lax.axis_index("x"), lax.axis_index("y"), lax.axis_index("c"))


def _comm(name, ins, out_shapes, n_local, n_remote, plan):
    return run_exchange(name, Exchange(ins, out_shapes, n_local, n_remote, plan))


def run_exchange(name, ex):
    def body(*refs):
        ex.start(*ex.split(refs))
        ex.relay(*ex.split(refs))
        ex.finish(*ex.split(refs))

    return _pcall(
        body, name=name, in_specs=ex.in_specs, out_specs=ex.out_specs, out_shape=ex.out_shapes, scratch_shapes=ex.scratch
    )(*ex.ins)


class Exchange:
    def __init__(self, ins, out_shapes, n_local, n_remote, plan):
        self.ins, self.out_shapes, self.n_local, self.n_remote, self.plan = list(ins), list(out_shapes), n_local, n_remote, plan
        any_spec = pl.BlockSpec(memory_space=pl.ANY)
        self.in_specs, self.out_specs = [any_spec] * len(self.ins), [any_spec] * len(self.out_shapes)
        self.scratch = [
            pltpu.SemaphoreType.DMA((n_remote,)),
            pltpu.SemaphoreType.DMA((n_remote,)),
            pltpu.SemaphoreType.DMA((max(n_local, 1),)),
        ]

    def split(self, refs):
        n_in, n_out = len(self.ins), len(self.out_shapes)
        return refs[:n_in], refs[n_in : n_in + n_out], refs[n_in + n_out :]

    def _copies(self, in_refs, out_refs, sems):
        send_sems, recv_sems, local_sems = sems
        me = _me()
        local_plan, remote_plan = self.plan(me, in_refs, out_refs)
        assert len(local_plan) == self.n_local and len(remote_plan) == self.n_remote, (len(local_plan), len(remote_plan))
        def local_copy(n):
            src, dst = local_plan[n]
            return pltpu.make_async_copy(src, dst, local_sems.at[n])

        def send(n):
            src, dst, peer, _, _ = remote_plan[n]
            return pltpu.make_async_remote_copy(
                src_ref=src, dst_ref=dst, send_sem=send_sems.at[n], recv_sem=recv_sems.at[n], device_id=peer, device_id_type=MESH
            )

        def arrival(n):
            src, _, _, landing, _ = remote_plan[n]
            return pltpu.make_async_remote_copy(
                src_ref=src, dst_ref=landing, send_sem=send_sems.at[n], recv_sem=recv_sems.at[n], device_id=me, device_id_type=MESH
            )

        after = [a for (_, _, _, _, a) in remote_plan]
        assert all(a is None or a < n for n, a in enumerate(after))
        return local_copy, send, arrival, after

    def start(self, in_refs, out_refs, sems):
        local_copy, send, _, after = self._copies(in_refs, out_refs, sems)
        for n in range(self.n_local):
            local_copy(n).start()
        for n, a in enumerate(after):
            if a is None:
                send(n).start()

    def relay(self, in_refs, out_refs, sems):
        _, send, arrival, after = self._copies(in_refs, out_refs, sems)
        for n in sorted({a for a in after if a is not None}):
            arrival(n).wait_recv()
            for m, a in enumerate(after):
                if a == n:
                    send(m).start()

    def finish(self, in_refs, out_refs, sems):
        local_copy, send, arrival, after = self._copies(in_refs, out_refs, sems)
        relayed = {a for a in after if a is not None}
        for n in range(self.n_remote):
            if n not in relayed:
                arrival(n).wait_recv()
        for n in range(self.n_remote):
            send(n).wait_send()
        for n in range(self.n_local):
            local_copy(n).wait()


_CHIP_FLIPS = ((1, 0), (0, 1), (1, 1))


class Placement:
    def __init__(self, rows, full_rows, cuts=(), offset=None, zero_rows=None):
        edges = [0, *sorted(cuts), rows]
        self.rows, self.full_rows = rows, full_rows
        self.runs = [(a, b - a) for a, b in zip(edges[:-1], edges[1:]) if b > a]
        self.offset = offset if offset is not None else (lambda j, start: j * rows + start)
        self.zero_rows = zero_rows

    def at(self, ref_2d_plus, j, start, size):
        return pl.ds(pl.multiple_of(self.offset(j, start), 16), size)


def gather_layer(prepped, zeros, places, l, name):
    n_w = len(prepped)
    n_chip = sum(len(p.runs) for p in places) * 3
    n_own = sum(len(p.runs) for p in places) * 2 + sum(2 for p in places if p.zero_rows)

    def plan(me, in_refs, out_refs):
        x, y, c = me
        j = 2 * x + y
        sib = (x, y, 1 - c)
        from_chips, own, forwards = [], [], []
        for i, o, p in zip(in_refs[:n_w], out_refs, places):
            for start, size in p.runs:
                for dx, dy in _CHIP_FLIPS:
                    px, py = x ^ dx, y ^ dy
                    jp = 2 * px + py
                    mine_there = o.at[c, p.at(o, j, start, size)]
                    theirs_here = o.at[c, p.at(o, jp, start, size)]
                    n = len(from_chips)
                    from_chips.append((i.at[l, c, pl.ds(start, size)], mine_there, (px, py, c), theirs_here, None))
                    forwards.append((theirs_here, theirs_here, sib, o.at[1 - c, p.at(o, jp, start, size)], n))
                for h in range(2):
                    place = o.at[h, p.at(o, j, start, size)]
                    own.append((i.at[l, h, pl.ds(start, size)], place, sib, place, None))
            if p.zero_rows:
                for h in range(2):
                    place = o.at[h, pl.ds(p.zero_rows[0], p.zero_rows[1])]
                    own.append((in_refs[n_w], place, sib, place, None))
        return [], from_chips + own + forwards

    shapes = [jax.ShapeDtypeStruct((2, p.full_rows, w.shape[3]), w.dtype) for w, p in zip(prepped, places)]
    ins = [*prepped, zeros] if any(p.zero_rows for p in places) else list(prepped)
    ex = Exchange(ins, shapes, 0, 2 * n_chip + n_own, plan)
    return ex if name is None else run_exchange(name, ex)


def sibling_scatter(arrays, name):
    def plan(me, in_refs, out_refs):
        x, y, c = me
        return [], [(i.at[1 - c], o, (x, y, 1 - c), o, None) for i, o in zip(in_refs, out_refs)]

    shapes = [jax.ShapeDtypeStruct(a.shape[1:], a.dtype) for a in arrays]
    ex = Exchange(arrays, shapes, 0, len(arrays), plan)
    return ex if name is None else run_exchange(name, ex)


def sibling_swap(arrays, name):
    def plan(me, in_refs, out_refs):
        x, y, c = me
        return [], [(i, o, (x, y, 1 - c), o, None) for i, o in zip(in_refs, out_refs)]

    shapes = [jax.ShapeDtypeStruct(a.shape, a.dtype) for a in arrays]
    return _comm(name, arrays, shapes, 0, len(arrays), plan)


def scatter_to_chips(arrays, places, name):
    def plan(me, in_refs, out_refs):
        x, y, c = me
        j = 2 * x + y
        local_plan, remote_plan = [], []
        for i, o, p in zip(in_refs, out_refs, places):
            for start, size in p.runs:
                local_plan.append((i.at[p.at(i, j, start, size)], o.at[3, pl.ds(start, size)]))
                for k, (dx, dy) in enumerate(_CHIP_FLIPS):
                    px, py = x ^ dx, y ^ dy
                    landing = o.at[k, pl.ds(start, size)]
                    remote_plan.append((i.at[p.at(i, 2 * px + py, start, size)], landing, (px, py, c), landing, None))
        return local_plan, remote_plan

    n_runs = sum(len(p.runs) for p in places)
    shapes = [jax.ShapeDtypeStruct((4, p.rows, a.shape[1]), a.dtype) for a, p in zip(arrays, places)]
    ex = Exchange(arrays, shapes, n_runs, 3 * n_runs, plan)
    return ex if name is None else run_exchange(name, ex)


def allgather8(block, name):
    def plan(me, in_refs, out_refs):
        x, y, c = me
        (i,), (o,) = in_refs, out_refs
        mine = 4 * x + 2 * y + c
        remote_plan = []
        for flip in range(1, 8):
            px, py, pc = x ^ (flip >> 2), y ^ ((flip >> 1) & 1), c ^ (flip & 1)
            remote_plan.append((i, o.at[mine], (px, py, pc), o.at[4 * px + 2 * py + pc], None))
        return [(i, o.at[mine])], remote_plan

    return _comm(name, [block], [jax.ShapeDtypeStruct((8, *block.shape), block.dtype)], 1, 7, plan)[0]


def transpose_split(a, out_dtype, name):
    L, R, C = a.shape
    rh = R // 2
    rt = _pick(rh, 256)
    n = rh // rt

    def body(x_ref, o_ref):
        o_ref[0, 0] = x_ref[0].T.astype(o_ref.dtype)

    return _pcall(
        body,
        name=name,
        grid=(L, 2, n),
        in_specs=[pl.BlockSpec((1, rt, C), lambda l, h, i: (l, h * n + i, 0))],
        out_specs=pl.BlockSpec((1, 1, C, rt), lambda l, h, i: (l, h, 0, i)),
        out_shape=jax.ShapeDtypeStruct((L, 2, C, rh), out_dtype),
        compiler_params=_params(3),
    )(a)


def cast_split(a, out_dtype, name):
    L, R, C = a.shape
    rt = _pick(R, 512, 16)

    def body(x_ref, o_ref):
        o_ref[0, 0] = x_ref[0].astype(o_ref.dtype)

    return _pcall(
        body,
        name=name,
        grid=(L, 2, R // rt),
        in_specs=[pl.BlockSpec((1, rt, C // 2), lambda l, h, i: (l, i, h))],
        out_specs=pl.BlockSpec((1, 1, rt, C // 2), lambda l, h, i: (l, h, i, 0)),
        out_shape=jax.ShapeDtypeStruct((L, 2, R, C // 2), out_dtype),
        compiler_params=_params(3),
    )(a)


def untranspose_halves(mine, theirs, my_c, stack, l, n_layers, name):
    C, rh = mine.shape
    n_in = 2 if stack is None else 3
    rt = _pick(rh, 256)
    n = rh // rt

    def body(c_ref, a_ref, b_ref, *rest):
        o_ref = rest[-1]
        h = pl.program_id(0)
        o_ref[0] = jnp.where(h == c_ref[0], a_ref[...], b_ref[...]).T

    grid_spec = pltpu.PrefetchScalarGridSpec(
        num_scalar_prefetch=1,
        grid=(2, n),
        in_specs=[pl.BlockSpec((C, rt), lambda h, i, c: (0, i))] * 2
        + ([] if stack is None else [pl.BlockSpec(memory_space=pl.ANY)]),
        out_specs=pl.BlockSpec((1, rt, C), lambda h, i, c: (l, h * n + i, 0)),
    )
    return _pcall(
        body,
        name=name,
        grid_spec=grid_spec,
        out_shape=jax.ShapeDtypeStruct((n_layers, 2 * rh, C), F32),
        input_output_aliases={} if stack is None else {n_in: 0},
        compiler_params=_params(2),
    )(my_c.reshape(1), mine, theirs, *([] if stack is None else [stack]))


def merge_halves(mine, theirs, my_c, stack, l, n_layers, name):
    R, ch = mine.shape
    n_in = 2 if stack is None else 3
    rt = _pick(R, 512, 8)

    def body(c_ref, a_ref, b_ref, *rest):
        o_ref = rest[-1]
        o_ref[0] = jnp.where(pl.program_id(0) == c_ref[0], a_ref[...], b_ref[...])

    grid_spec = pltpu.PrefetchScalarGridSpec(
        num_scalar_prefetch=1,
        grid=(2, R // rt),
        in_specs=[pl.BlockSpec((rt, ch), lambda h, i, c: (i, 0))] * 2
        + ([] if stack is None else [pl.BlockSpec(memory_space=pl.ANY)]),
        out_specs=pl.BlockSpec((1, rt, ch), lambda h, i, c: (l, i, h)),
    )
    return _pcall(
        body,
        name=name,
        grid_spec=grid_spec,
        out_shape=jax.ShapeDtypeStruct((n_layers, R, 2 * ch), F32),
        input_output_aliases={} if stack is None else {n_in: 0},
        compiler_params=_params(2),
    )(my_c.reshape(1), mine, theirs, *([] if stack is None else [stack]))


def add_pair(g, r, my_c, name):
    _, rows, w = g.shape
    tm = _pick(rows, max(16, min(1024, (1 << 20) // w // 16 * 16)), 16)

    def body(c_ref, g_ref, r_ref, o_ref):
        o_ref[...] = (g_ref[0].astype(F32) + r_ref[...].astype(F32)).astype(o_ref.dtype)

    grid_spec = pltpu.PrefetchScalarGridSpec(
        num_scalar_prefetch=1,
        grid=(rows // tm,),
        in_specs=[pl.BlockSpec((1, tm, w), lambda i, c: (c[0], i, 0)), pl.BlockSpec((tm, w), lambda i, c: (i, 0))],
        out_specs=pl.BlockSpec((tm, w), lambda i, c: (i, 0)),
    )
    return _pcall(
        body, name=name, grid_spec=grid_spec, out_shape=jax.ShapeDtypeStruct((rows, w), BF16), compiler_params=_params(1)
    )(my_c.reshape(1), g, r)


class Dims:
    def __init__(self, S, D, L, ret_heads, mla_heads, q_rank, kv_rank):
        self.S, self.D, self.L, self.HR, self.HM, self.QR, self.KR = S, D, L, ret_heads, mla_heads, q_rank, kv_rank
        self.RQ, self.RV, self.MV = ret_heads * RET_DK, ret_heads * RET_DV, mla_heads * MLA_DV
        self.D_IN = 2 * self.RQ + 2 * self.RV + q_rank + kv_rank + MLA_ROPE + self.MV + 2 * D
        self.lo = 2 * self.RQ + 2 * self.RV
        self.mid = q_rank + kv_rank + MLA_ROPE
        self.DP = self.D_IN + LANE - MLA_ROPE
        self.o_rq, self.o_rk, self.o_rv, self.o_rg = 0, self.RQ, 2 * self.RQ, 2 * self.RQ + self.RV
        self.o_mg = self.lo
        self.o_bga = self.o_mg + self.MV
        self.o_bgb = self.o_bga + D
        self.o_cq = self.o_bgb + D
        self.o_ckv = self.o_cq + q_rank
        self.o_kr = self.o_ckv + kv_rank

    def pad_uq_rows(self, w):
        heads = w.shape[-2] // (MLA_NOPE + MLA_ROPE)
        w = w.reshape(*w.shape[:-2], heads, MLA_NOPE + MLA_ROPE, w.shape[-1])
        w = jnp.pad(w, [(0, 0)] * (w.ndim - 2) + [(0, MLA_QW - MLA_NOPE - MLA_ROPE), (0, 0)])
        return w.reshape(*w.shape[:-3], heads * MLA_QW, w.shape[-1])

    def unpad_uq_rows(self, w):
        heads = w.shape[-2] // MLA_QW
        w = w.reshape(*w.shape[:-2], heads, MLA_QW, w.shape[-1])[..., : MLA_NOPE + MLA_ROPE, :]
        return w.reshape(*w.shape[:-3], heads * (MLA_NOPE + MLA_ROPE), w.shape[-1])

    def placements(self):
        rows_in = self.D_IN // 4
        lo, mid, n_hi = self.lo, self.mid, self.D_IN - self.lo - self.mid
        cuts = {b % rows_in for b in (lo, lo + mid)} - {0}

        def offset_in(j, start):
            g = j * rows_in + start
            return jnp.where(g < lo, g, jnp.where(g < lo + mid, g + n_hi, g - mid))

        return [
            Placement(rows_in, self.DP, cuts, offset_in, zero_rows=(self.D_IN, self.DP - self.D_IN)),
            Placement(self.HM * MLA_QW // 4, self.HM * MLA_QW),
            Placement(self.HM * (MLA_NOPE + MLA_DV) // 4, self.HM * (MLA_NOPE + MLA_DV)),
            Placement(self.RV // 4, self.RV),
            Placement(self.MV // 4, self.MV),
            Placement(self.D // 4, self.D),
        ]


def _rope_tables(positions, dim, width):
    inv = 1.0 / (ROPE_BASE ** (jnp.arange(0, dim, 2, dtype=F32) / dim))
    ang = positions.astype(F32)[:, None] * inv
    cos, sin = jnp.cos(ang), jnp.sin(ang)
    pad = jnp.zeros((positions.shape[0], width - dim), F32)
    return jnp.concatenate([cos, cos, pad], axis=1), jnp.concatenate([-sin, sin, pad], axis=1)


def _tiles(x):
    return [x[:, t * LANE : (t + 1) * LANE] for t in range(x.shape[1] // LANE)]


def _cat(parts):
    return parts[0] if len(parts) == 1 else jnp.concatenate(parts, axis=1)


def _rms(x, eps=EPS):
    return lax.rsqrt(jnp.mean(x * x, axis=1, keepdims=True) + eps)


def layer_fwd(dm, l, x, shift, scale, gate, g_norm, g_cq, g_ckv, w, tabs, host=None, host_in=None, own_late=False):
    cos_r, sin_r, cos_m, sin_m = tabs
    nm = lambda s: f"l{l}_{s}"

    def f_norm(x, g, scale, shift):
        return [x * _rms(x) * g * (1.0 + scale) + shift], []

    (h,) = rowwise(f_norm, [x], [g_norm, scale, shift], [(dm.D, BF16)], tm=256, name=nm("norm"))
    proj, *late = _as_list(matmul(h, w["in"], tb=True, b_split=True, name=nm("mm_in"), tn=1920, host=host_in))
    if own_late:
        w = dict(w, **dict(zip(("uq", "ukv", "ret", "mla", "out"), late)))
        late = []

    def f_rope_ret(rq, rk, cos, sin):
        rq, rk = rq.astype(F32), rk.astype(F32)
        q = _cat([_rope_tile(t, cos, sin, RET_DK // 2) for t in _tiles(rq)])
        k = _cat([_rope_tile(t, cos, sin, RET_DK // 2) * (RET_DK**-0.5) for t in _tiles(rk)])
        return [q, k], []

    rq, rk = rowwise(
        f_rope_ret,
        [win(proj, dm.o_rq, dm.RQ), win(proj, dm.o_rk, dm.RQ), cos_r, sin_r],
        [],
        [(dm.RQ, BF16)] * 2,
        tm=512,
        name=nm("rope_ret"),
    )
    o_ret, a_ret, states = ret_fwd(rq, rk, proj, dm.o_rv, dm.o_rg, heads=dm.HR, tb=RET_BLOCK, name=nm("ret_fwd"))

    def f_prep(cq, ckv, kr, cos, sin, g_cq, g_ckv):
        cq, ckv, kr = cq.astype(F32), ckv.astype(F32), kr.astype(F32)
        return [cq * _rms(cq) * g_cq, ckv * _rms(ckv) * g_ckv, _rope_tile(kr, cos, sin, MLA_ROPE // 2)], []

    cqn, ckvn, krr = rowwise(
        f_prep,
        [win(proj, dm.o_cq, dm.QR), win(proj, dm.o_ckv, dm.KR), win(proj, dm.o_kr, LANE), cos_m, sin_m],
        [g_cq, g_ckv],
        [(dm.QR, BF16), (dm.KR, BF16), (LANE, BF16)],
        tm=512,
        name=nm("mla_prep"),
    )
    q_raw = matmul(cqn, w["uq"], tb=True, b_split=True, name=nm("mm_uq"), tn=4096)
    kv = matmul(ckvn, w["ukv"], tb=True, b_split=True, name=nm("mm_ukv"), tn=4096)

    def f_rope_q(q, cos, sin):
        t = _tiles(q.astype(F32))
        rot = [t[n] if n % 2 == 0 else _rope_tile(t[n], cos, sin, MLA_ROPE // 2) for n in range(len(t))]
        return [_cat([r * (MLA_SCALE * LOG2E) for r in rot])], []

    (q,) = rowwise(f_rope_q, [q_raw, cos_m, sin_m], [], [(dm.HM * MLA_QW, BF16)], tm=256, name=nm("rope_q"))
    o_mla, a_mla, lse, *hosted = attn_fwd(
        q, kv, krr, proj, dm.o_mg, heads=dm.HM, tq=ATTN_BLOCK, name=nm("attn_fwd"), host=host
    )

    y_ret = matmul(a_ret, w["ret"], b_split=True, name=nm("mm_ret"))
    y_mla = matmul(a_mla, w["mla"], b_split=True, name=nm("mm_mla"))

    def f_merge(y_ret, y_mla, bga, bgb):
        return [_sigmoid(bga.astype(F32)) * y_ret.astype(F32) + _sigmoid(bgb.astype(F32)) * y_mla.astype(F32)], []

    (merged,) = rowwise(
        f_merge, [y_ret, y_mla, win(proj, dm.o_bga, dm.D), win(proj, dm.o_bgb, dm.D)], [], [(dm.D, BF16)], tm=256, name=nm("merge")
    )
    out = matmul(merged, w["out"], b_split=True, out_dtype=F32, name=nm("mm_out"))

    def f_resid(x, out, gate):
        return [x + gate * out], []

    (x_new,) = rowwise(f_resid, [x, out], [gate], [(dm.D, F32)], tm=256, name=nm("resid"))
    saved = dict(
        x=x, h=h, proj=proj, rq=rq, rk=rk, o_ret=o_ret, a_ret=a_ret, states=states, cqn=cqn, ckvn=ckvn, krr=krr, q=q, kv=kv,
        o_mla=o_mla, a_mla=a_mla, lse=lse, y_ret=y_ret, y_mla=y_mla, merged=merged, out=out,
    )
    return x_new, saved, hosted, late, w


def _as_list(v):
    return list(v) if isinstance(v, (list, tuple)) else [v]


def layer_bwd(
    dm, l, dx_out, sv, shift, scale, gate, g_norm, g_cq, g_ckv, w, tabs, host=None, exchange_dh=None, exchange_dw=None
):
    cos_r, sin_r, cos_m, sin_m = tabs
    nm = lambda s: f"l{l}_{s}"
    proj = sv["proj"]

    def b_resid(dx, out, gate):
        return [dx * gate], [_sum0(dx * out)]

    dout, d_gate = rowwise(b_resid, [dx_out, sv["out"]], [gate], [(dm.D, BF16)], [dm.D], tm=256, name=nm("resid_bwd"))
    dmerged = matmul(dout, w["out"], tb=True, b_split=True, name=nm("mm_dmerged"))
    dw_out = matmul(sv["merged"], dout, ta=True, out_split=True, tn=2048, name=nm("mm_dw_out"))

    def b_merge(dmg, y_ret, y_mla, bga, bgb):
        dmg = dmg.astype(F32)
        ga, gb = _sigmoid(bga.astype(F32)), _sigmoid(bgb.astype(F32))
        y_ret, y_mla = y_ret.astype(F32), y_mla.astype(F32)
        d_gates = jnp.concatenate([dmg * y_ret * ga * (1.0 - ga), dmg * y_mla * gb * (1.0 - gb)], axis=1)
        return [d_gates, dmg * ga, dmg * gb], []

    dproj, dy_ret, dy_mla = rowwise(
        b_merge,
        [dmerged, sv["y_ret"], sv["y_mla"], win(proj, dm.o_bga, dm.D), win(proj, dm.o_bgb, dm.D)],
        [],
        [(2 * dm.D, BF16), (dm.D, BF16), (dm.D, BF16)],
        tm=256,
        name=nm("merge_bwd"),
        into=(dm.DP, dm.o_bga),
    )
    da_ret = matmul(dy_ret, w["ret"], tb=True, b_split=True, name=nm("mm_da_ret"))
    dw_ret = matmul(sv["a_ret"], dy_ret, ta=True, out_split=True, tn=2048, name=nm("mm_dw_ret"))
    da_mla = matmul(dy_mla, w["mla"], tb=True, b_split=True, name=nm("mm_da_mla"))
    dw_mla = matmul(sv["a_mla"], dy_mla, ta=True, out_split=True, tn=2048, name=nm("mm_dw_mla"))

    def b_ret_gate(da, rg, o):
        da, rg = da.astype(F32), rg.astype(F32)
        do_parts, drg_parts = [], []
        for hh in range(dm.HR):
            sl = slice(hh * RET_DV, (hh + 1) * RET_DV)
            oh, dah, rgh = o[:, sl], da[:, sl], rg[:, sl]
            mu = jnp.mean(oh, axis=1, keepdims=True)
            d = oh - mu
            r = lax.rsqrt(jnp.mean(d * d, axis=1, keepdims=True) + EPS)
            n = d * r
            dn = dah * _silu(rgh)
            drg_parts.append(dah * n * _dsilu(rgh))
            do_parts.append(r * (dn - jnp.mean(dn, axis=1, keepdims=True) - n * jnp.mean(dn * n, axis=1, keepdims=True)))
        return [_cat(drg_parts), _cat(do_parts)], []

    dproj, do_ret = rowwise(
        b_ret_gate, [da_ret, win(proj, dm.o_rg, dm.RV), sv["o_ret"]], [], [(dm.RV, BF16)] * 2, tm=256, name=nm("ret_gate_bwd"),
        into=(dproj, dm.o_rg),
    )
    dq_rot, dk_rot, dproj = ret_bwd(
        sv["rq"], sv["rk"], proj, dm.o_rv, sv["states"], do_ret, heads=dm.HR, tb=RET_BLOCK, name=nm("ret_bwd"), into=(dproj, dm.o_rv)
    )

    def b_rope_ret(dq, dk, cos, sin):
        dq, dk = dq.astype(F32), dk.astype(F32)
        q = [_rope_tile(t, cos, sin, RET_DK // 2, inverse=True) for t in _tiles(dq)]
        k = [_rope_tile(t, cos, sin, RET_DK // 2, inverse=True) * (RET_DK**-0.5) for t in _tiles(dk)]
        return [_cat(q + k)], []

    (dproj,) = rowwise(
        b_rope_ret, [dq_rot, dk_rot, cos_r, sin_r], [], [(2 * dm.RQ, BF16)], tm=512, name=nm("rope_ret_bwd"), into=(dproj, dm.o_rq)
    )

    def b_mla_gate(da, mg, o):
        da, mg = da.astype(F32), mg.astype(F32)
        return [da * o * _dsilu(mg), da * _silu(mg)], []

    dproj, do_mla = rowwise(
        b_mla_gate, [da_mla, win(proj, dm.o_mg, dm.MV), sv["o_mla"]], [], [(dm.MV, BF16)] * 2, tm=256, name=nm("mla_gate_bwd"),
        into=(dproj, dm.o_mg),
    )
    dq_att, dkv, dkr_heads, *hosted = attn_bwd(
        sv["q"], sv["kv"], sv["krr"], do_mla, sv["o_mla"], sv["lse"], heads=dm.HM, tq=ATTN_BLOCK, name=nm("attn_bwd"), host=host
    )

    def b_rope_q(dq, cos, sin):
        t = _tiles(dq.astype(F32))
        return [_cat([t[n] if n % 2 == 0 else _rope_tile(t[n], cos, sin, MLA_ROPE // 2, inverse=True) for n in range(len(t))])], []

    (dq_raw,) = rowwise(b_rope_q, [dq_att, cos_m, sin_m], [], [(dm.HM * MLA_QW, BF16)], tm=256, name=nm("rope_q_bwd"))
    dcqn = matmul(dq_raw, w["uq"], b_split=True, name=nm("mm_dcqn"))
    dw_uq = matmul(dq_raw, sv["cqn"], ta=True, out_split=True, tn=2048, name=nm("mm_dw_uq"))
    dckvn = matmul(dkv, w["ukv"], b_split=True, name=nm("mm_dckvn"))
    dw_ukv = matmul(dkv, sv["ckvn"], ta=True, out_split=True, tn=2048, name=nm("mm_dw_ukv"))

    def b_prep(dcqn, dckvn, cq, ckv, cos, sin, dkr_h, g_cq, g_ckv):
        outs, accs = [], []
        for dn, z, g in ((dcqn, cq, g_cq), (dckvn, ckv, g_ckv)):
            dn, z = dn.astype(F32), z.astype(F32)
            n = z * _rms(z)
            dng = dn * g
            outs.append(_rms(z) * (dng - n * jnp.mean(dng * n, axis=1, keepdims=True)))
            accs.append(_sum0(dn * n))
        dkr = dkr_h[0]
        for hh in range(1, dm.HM):
            dkr = dkr + dkr_h[hh]
        return [_cat(outs), _rope_tile(dkr, cos, sin, MLA_ROPE // 2, inverse=True)], accs

    dproj, d_kr, dg_cq, dg_ckv = rowwise(
        b_prep,
        [dcqn, dckvn, win(proj, dm.o_cq, dm.QR), win(proj, dm.o_ckv, dm.KR), cos_m, sin_m, dkr_heads],
        [g_cq, g_ckv],
        [(dm.QR + dm.KR, BF16), (LANE, BF16)],
        [dm.QR, dm.KR],
        tm=256,
        name=nm("mla_prep_bwd"),
        into=(dproj, dm.o_cq),
    )
    (dproj,) = rowwise(lambda a: ([a], []), [d_kr], [], [(LANE, BF16)], tm=512, name=nm("place_dkr"), into=(dproj, dm.o_kr))
    dws = dict(w_uq=dw_uq, w_ukv=dw_ukv, w_ret_proj=dw_ret, w_mla_proj=dw_mla, w_out=dw_out)
    host_dw = exchange_dw(dws) if exchange_dw else None
    dw_in, *hosted_dw = _as_list(
        matmul(dproj, sv["h"], ta=True, out_split=True, tn=2048, name=nm("mm_dw_in"), host=host_dw)
    )
    dws["w_in"] = dw_in
    host_dh = exchange_dh(dws) if exchange_dh else None
    dh, *hosted_dh = _as_list(
        matmul(dproj, w["in"], b_split=True, out_dtype=F32, name=nm("mm_dh"), tn=2048, tk=1920, host=host_dh)
    )

    def b_norm(dh, x, dx_res, g, scale):
        r = _rms(x)
        xn = x * r
        dxn = dh * g * (1.0 + scale)
        dx = dx_res + r * (dxn - xn * jnp.mean(dxn * xn, axis=1, keepdims=True))
        return [dx], [_sum0(dh), _sum0(dh * xn * g), _sum0(dh * (1.0 + scale) * xn)]

    dx, d_shift, d_scale, dg_norm = rowwise(
        b_norm, [dh, sv["x"], dx_out], [g_norm, scale], [(dm.D, F32)], [dm.D] * 3, tm=256, name=nm("norm_bwd")
    )
    dvec = dict(mod=jnp.concatenate([d_shift, d_scale, d_gate], axis=1), g_norm=dg_norm, g_cq=dg_cq, g_ckv=dg_ckv)
    return dx, dws, dvec, hosted, hosted_dh, hosted_dw


def adamw(w, g, m, v, name):
    shape = w.shape
    cols = shape[-1]
    view = lambda a: a.reshape(-1, cols)

    def f(w, g, m, v):
        m = ADAM_B1 * m + (1.0 - ADAM_B1) * g
        v = ADAM_B2 * v + (1.0 - ADAM_B2) * (g * g)
        m_hat = m / (1.0 - ADAM_B1**ADAM_STEP)
        v_hat = v / (1.0 - ADAM_B2**ADAM_STEP)
        delta = -ADAM_LR * (m_hat / (jnp.sqrt(v_hat) + ADAM_EPS) + ADAM_WD * w)
        return [delta, m, v], []

    tm = max(8, min(512, (400_000 // cols) // 8 * 8))
    delta, m, v = rowwise(f, [view(w), view(g), view(m), view(v)], [], [(cols, F32)] * 3, tm=tm, name=name)
    return delta.reshape(shape), m.reshape(shape), v.reshape(shape)


def _add_rows(fn, rows, cols, dtype, name):
    tm = max(8, min(512, (400_000 // cols) // 8 * 8))
    return rowwise(lambda *a: ([fn(*a)], []), rows, [], [(cols, dtype)], tm=tm, name=name)[0]


BIG = ("w_in", "w_uq", "w_ukv", "w_ret_proj", "w_mla_proj", "w_out")
COL_SHARDED = ("w_in", "w_uq", "w_ukv")


def kernel(x, c, positions, w_mod, b_mod, g_norm, w_in, g_cq, g_ckv, w_uq, w_ukv, w_ret_proj, w_mla_proj, w_out, g_final, loss_target, m_w_mod, m_b_mod, m_g_norm, m_w_in, m_g_cq, m_g_ckv, m_w_uq, m_w_ukv, m_w_ret_proj, m_w_mla_proj, m_w_out, m_g_final, v_w_mod, v_b_mod, v_g_norm, v_w_in, v_g_cq, v_g_ckv, v_w_uq, v_w_ukv, v_w_ret_proj, v_w_mla_proj, v_w_out, v_g_final):
    weights = dict(w_mod=w_mod, b_mod=b_mod, g_norm=g_norm, w_in=w_in, g_cq=g_cq, g_ckv=g_ckv, w_uq=w_uq, w_ukv=w_ukv,
                   w_ret_proj=w_ret_proj, w_mla_proj=w_mla_proj, w_out=w_out, g_final=g_final)
    m_in = dict(w_mod=m_w_mod, b_mod=m_b_mod, g_norm=m_g_norm, w_in=m_w_in, g_cq=m_g_cq, g_ckv=m_g_ckv, w_uq=m_w_uq,
                w_ukv=m_w_ukv, w_ret_proj=m_w_ret_proj, w_mla_proj=m_w_mla_proj, w_out=m_w_out, g_final=m_g_final)
    v_in = dict(w_mod=v_w_mod, b_mod=v_b_mod, g_norm=v_g_norm, w_in=v_w_in, g_cq=v_g_cq, g_ckv=v_g_ckv, w_uq=v_w_uq,
                w_ukv=v_w_ukv, w_ret_proj=v_w_ret_proj, w_mla_proj=v_w_mla_proj, w_out=v_w_out, g_final=v_g_final)
    order = ("w_mod", "b_mod", "g_norm", "w_in", "g_cq", "g_ckv", "w_uq", "w_ukv", "w_ret_proj", "w_mla_proj", "w_out", "g_final")

    x = x[0]
    target = loss_target[0]
    S, D = x.shape
    L = w_mod.shape[0]
    dm = Dims(S, D, L, w_ret_proj.shape[1] * 4 // RET_DV, w_mla_proj.shape[1] * 4 // MLA_DV, g_cq.shape[1], g_ckv.shape[1])
    my_x, my_y, my_c = _me()
    my_chip = 2 * my_x + my_y
    my_dev = 2 * my_chip + my_c
    C3 = w_mod.shape[2]

    for table in (weights, m_in, v_in):
        table["w_in"] = jnp.swapaxes(table["w_in"], 1, 2)
    prepped = [
        cast_split(weights["w_in"], BF16, "prep_w_in"),
        dm.pad_uq_rows(transpose_split(w_uq, BF16, "prep_w_uq")),
        transpose_split(w_ukv, BF16, "prep_w_ukv"),
        cast_split(w_ret_proj, BF16, "prep_w_ret"),
        cast_split(w_mla_proj, BF16, "prep_w_mla"),
        cast_split(w_out, BF16, "prep_w_out"),
    ]
    places = dm.placements()
    zero_rows = jnp.zeros((places[0].zero_rows[1], D // 2), BF16)
    w_keys = ("in", "uq", "ukv", "ret", "mla", "out")
    layer_w = [dict(zip(w_keys, gather_layer(prepped[:1], zero_rows, places[:1], 0, "l0_gather_w_in")))]
    rest_of_layer0 = gather_layer(prepped[1:], None, places[1:], 0, None)

    c_all = allgather8(c, "gather_c").reshape(8, D)
    (c_act,) = rowwise(lambda z: ([_silu(z)], []), [c_all], [], [(D, BF16)], tm=8, name="silu_c")
    mod_part = jnp.stack([matmul(c_act, w_mod[l], out_dtype=F32, name=f"l{l}_mm_mod", tn=C3) for l in range(L)])
    mod_all = allgather8(mod_part, "gather_mod")
    mod_all = mod_all.reshape(4, 2, L, 8, C3)[:, 0].transpose(1, 2, 0, 3).reshape(L, 8, 3 * D) + b_mod[:, None, :]
    mod = lax.dynamic_index_in_dim(mod_all, my_dev, axis=1, keepdims=False)

    pos = positions[0]
    tabs = (*_rope_tables(pos, RET_DK, LANE), *_rope_tables(pos, MLA_ROPE, LANE))

    def vecs(l):
        return (mod[l : l + 1, :D], mod[l : l + 1, D : 2 * D], mod[l : l + 1, 2 * D :],
                g_norm[l : l + 1], g_cq[l : l + 1], g_ckv[l : l + 1])

    saved = []
    for l in range(L):
        host = host_in = None
        if l == 0:
            host_in = rest_of_layer0
            host = gather_layer(prepped, zero_rows, places, 1, None) if L > 1 else None
        elif l + 1 < L:
            host_in = gather_layer(prepped[:1], zero_rows, places[:1], l + 1, None)
            host = gather_layer(prepped[1:], None, places[1:], l + 1, None)
        x, sv, hosted, late, layer_w[l] = layer_fwd(
            dm, l, x, *vecs(l), layer_w[l], tabs, host=host, host_in=host_in, own_late=(l == 0)
        )
        saved.append(sv)
        if host:
            layer_w.append(dict(zip(w_keys, [*late, *hosted])))

    def f_loss(x, t, g):
        xn = x * _rms(x)
        err = xn * g - t
        dy = err * (1.0 / D)
        dxn = dy * g
        dx = _rms(x) * (dxn - xn * jnp.mean(dxn * xn, axis=1, keepdims=True))
        part = jnp.sum(jnp.sum(err * err, axis=1, keepdims=True), axis=0, keepdims=True) * (0.5 / D)
        return [dx], [jnp.broadcast_to(part, (1, LANE)), _sum0(dy * xn)]

    dx, loss_part, dg_final = rowwise(f_loss, [x, target], [g_final.reshape(1, D)], [(D, F32)], [LANE, D], tm=256, name="loss_head")

    stacks = {n: None for n in BIG}
    dvec = {n: [None] * L for n in ("mod", "g_norm", "g_cq", "g_ckv")}

    def finish_grads(l, from_chips):
        mine = [
            _add_rows(lambda a: ((a[3].astype(F32) + a[0].astype(F32)) + a[1].astype(F32)) + a[2].astype(F32), [r], r.shape[-1],
                      F32, f"l{l}_rs_add4_{n}")
            for n, r in zip(BIG, from_chips)
        ]
        theirs = sibling_swap(mine, f"l{l}_rs_share")
        for n, a, b in zip(BIG, mine, theirs):
            if n == "w_uq":
                a, b = dm.unpad_uq_rows(a), dm.unpad_uq_rows(b)
            finish = untranspose_halves if n in ("w_uq", "w_ukv") else merge_halves
            stacks[n] = finish(a, b, my_c, stacks[n], l, L, f"l{l}_grad_{n}")

    def pair_sums(l, names, partial, from_sibling):
        return [add_pair(g, r, my_c, f"l{l}_rs_add2_{n}") for n, g, r in zip(names, partial, from_sibling)]

    def core_exchange(dws):
        return sibling_scatter([dws[n] for n in BIG], None)

    def last_exchange(names, which, tag):
        def build(dws):
            partial = [dws[n] for n in names]
            pairs = pair_sums(0, names, partial, sibling_scatter(partial, f"l0_rs_cores_{tag}"))
            return scatter_to_chips(pairs, which, None)

        return build

    pair = None
    for l in reversed(range(L)):
        host = scatter_to_chips(pair, places, None) if pair is not None else None
        hooks = dict(exchange_dh=core_exchange)
        if l == 0:
            hooks = dict(exchange_dw=last_exchange(BIG[1:], places[1:], "rest"), exchange_dh=last_exchange(BIG[:1], places[:1], "in"))
        dx, dw_l, dv_l, hosted, hosted_dh, hosted_dw = layer_bwd(
            dm, l, dx, saved[l], *vecs(l), layer_w[l], tabs, host=host, **hooks
        )
        if host:
            finish_grads(l + 1, hosted)
        for n in dvec:
            dvec[n][l] = dv_l[n]
        if l > 0:
            pair = pair_sums(l, BIG, [dw_l[n] for n in BIG], hosted_dh)
        else:
            finish_grads(0, [*hosted_dh, *hosted_dw])
    grad_x = dx[None]

    pieces = [loss_part] + [jnp.concatenate(dvec[n], axis=1) for n in ("mod", "g_norm", "g_cq", "g_ckv")] + [dg_final]
    widths = [p.shape[1] for p in pieces]
    small_all = allgather8(jnp.concatenate(pieces, axis=1), "gather_small").reshape(8, sum(widths))

    def sum8_body(a_ref, o_ref):
        acc = a_ref[0:1, :]
        for d in range(1, 8):
            acc = acc + a_ref[d : d + 1, :]
        o_ref[...] = acc

    small = _pcall(sum8_body, name="sum_small", out_shape=jax.ShapeDtypeStruct((1, sum(widths)), F32))(small_all)
    offs = np.cumsum([0] + widths)
    loss = small[0, 0]
    g_small = {
        "b_mod": small[0, offs[1] : offs[2]].reshape(L, 3 * D),
        "g_norm": small[0, offs[2] : offs[3]].reshape(L, D),
        "g_cq": small[0, offs[3] : offs[4]].reshape(L, dm.QR),
        "g_ckv": small[0, offs[4] : offs[5]].reshape(L, dm.KR),
        "g_final": small[0, offs[5] : offs[6]],
    }

    dmod_all = small_all[:, offs[1] : offs[2]].reshape(8, L, 3 * D)
    dmod_mine = lax.dynamic_slice_in_dim(dmod_all, my_chip * C3, C3, axis=2)
    pad8 = lambda a: jnp.pad(a, ((0, LANE - 8), (0, 0)))
    grads = dict(g_small)
    grads["w_mod"] = jnp.stack(
        [matmul(pad8(c_act), pad8(dmod_mine[:, l]), ta=True, out_dtype=F32, name=f"l{l}_mm_dw_mod", tn=C3) for l in range(L)]
    )

    for n in BIG:
        grads[n] = stacks[n]

    deltas, new_m, new_v = {}, {}, {}
    for n in order:
        wv, gv, mv, vv = weights[n], grads[n], m_in[n], v_in[n]
        if wv.ndim == 1:
            wv, gv, mv, vv = (a.reshape(1, -1) for a in (wv, gv, mv, vv))
        d_, m_, v_ = adamw(wv, gv, mv, vv, f"adamw_{n}")
        deltas[n], new_m[n], new_v[n] = (a.reshape(weights[n].shape) for a in (d_, m_, v_))
        grads[n] = grads[n].reshape(weights[n].shape)
    for table in (grads, deltas, new_m, new_v):
        table["w_in"] = jnp.swapaxes(table["w_in"], 1, 2)

    return (loss, grad_x, *[grads[n] for n in order], *[deltas[n] for n in order], *[new_m[n] for n in order],
            *[new_v[n] for n in order])
```

```python
import functools

import jax
import jax.numpy as jnp
import numpy as np
from jax import lax
from jax.experimental import pallas as pl
from jax.experimental.pallas import tpu as pltpu

F32 = jnp.float32
BF16 = jnp.bfloat16
MESH = pl.DeviceIdType.MESH

VMEM_LIMIT_BYTES = 52 * 1024 * 1024
LANE = 128

CHUNK = 64
EPS = 1e-6
NEG_INF = -1e30
ROPE_BASE = 10000.0
RET_DK = 128
RET_DV = 256
MLA_NOPE = 128
MLA_ROPE = 64
MLA_DV = 128
MLA_QW = 256
MLA_SCALE = float((MLA_NOPE + MLA_ROPE) ** -0.5)
LOG2E = float(np.log2(np.e))
LN2 = float(np.log(2.0))
ATTN_BLOCK = 512
ATTN_FWD_HEADS_PER_STEP = 8
ATTN_HEADS_PER_STEP = 2
RET_BLOCK = 256
RET_HEADS_PER_STEP = 8

ADAM_LR = 0.001
ADAM_B1 = 0.9
ADAM_B2 = 0.999
ADAM_EPS = 1e-08
ADAM_WD = 0.01
ADAM_STEP = 10


def _pcall(body, **kw):
    return pl.pallas_call(body, **kw)


def _params(n_grid):
    return pltpu.CompilerParams(dimension_semantics=("arbitrary",) * n_grid, vmem_limit_bytes=VMEM_LIMIT_BYTES)


def _pick(dim, target, mult=LANE):
    if dim <= target:
        return dim
    best = None
    for t in range(mult, target + 1, mult):
        if dim % t == 0:
            best = t
    assert best is not None, (dim, target, mult)
    return best


def matmul(
    a, b, *, ta=False, tb=False, b_split=False, out_split=False, out_dtype=BF16, name, tm=512, tn=1024, tk=2048, host=None
):
    (M, K) = (a.shape[1], a.shape[0]) if ta else a.shape
    b_rows, b_cols = (b.shape[1], 2 * b.shape[2]) if b_split else b.shape
    (K2, N) = (b_cols, b_rows) if tb else (b_rows, b_cols)
    assert K == K2, (a.shape, b.shape, ta, tb)
    whole_k = b_split and tb and tk >= K and not ta
    whole_n = out_split and tn >= N and not (b_split and not tb)
    whole_bn = b_split and not tb and tn >= N and not out_split
    n_cap = N if (whole_n or whole_bn) else (N // 2 if (out_split or (b_split and not tb)) else N)
    k_cap = K if whole_k else (K // 2 if (b_split and tb) else K)
    tm, tn, tk = _pick(M, tm, 8 if M < LANE else LANE), _pick(n_cap, min(tn, n_cap)), _pick(k_cap, min(tk, k_cap))
    nk = K // tk
    njh, nkh = max((N // 2) // tn, 1), max((K // 2) // tk, 1)
    dn = (((0 if ta else 1,), (1 if tb else 0,)), ((), ()))

    def body(a_ref, b_ref, o_ref, *scratch):
        if whole_k:
            a_blk = a_ref[...].astype(BF16)
            prod = lax.dot_general(a_blk[:, : K // 2], b_ref[0].astype(BF16), dn, preferred_element_type=F32)
            prod += lax.dot_general(a_blk[:, K // 2 :], b_ref[1].astype(BF16), dn, preferred_element_type=F32)
        elif whole_bn:
            a_blk = a_ref[...].astype(BF16)
            halves = [lax.dot_general(a_blk, b_ref[half].astype(BF16), dn, preferred_element_type=F32) for half in range(2)]
            prod = jnp.concatenate(halves, axis=1)
        else:
            b_blk = b_ref[0] if b_split else b_ref[...]
            prod = lax.dot_general(a_ref[...].astype(BF16), b_blk.astype(BF16), dn, preferred_element_type=F32)

        def store(v):
            if whole_n:
                o_ref[0] = v[:, : N // 2].astype(o_ref.dtype)
                o_ref[1] = v[:, N // 2 :].astype(o_ref.dtype)
            elif out_split:
                o_ref[0] = v.astype(o_ref.dtype)
            else:
                o_ref[...] = v.astype(o_ref.dtype)

        if nk == 1:
            store(prod)
            return
        (acc_ref,) = scratch
        k = pl.program_id(2)

        @pl.when(k == 0)
        def _():
            acc_ref[...] = prod

        @pl.when(k > 0)
        def _():
            acc_ref[...] += prod

        @pl.when(k == nk - 1)
        def _():
            store(acc_ref[...])

    a_spec = pl.BlockSpec((tk, tm), lambda i, j, k: (k, i)) if ta else pl.BlockSpec((tm, tk), lambda i, j, k: (i, k))
    if whole_k:
        b_spec = pl.BlockSpec((2, tn, K // 2), lambda i, j, k: (0, j, 0))
    elif whole_bn:
        b_spec = pl.BlockSpec((2, tk, N // 2), lambda i, j, k: (0, k, 0))
    elif b_split and tb:
        b_spec = pl.BlockSpec((1, tn, tk), lambda i, j, k: (k // nkh, j, k % nkh))
    elif b_split:
        b_spec = pl.BlockSpec((1, tk, tn), lambda i, j, k: (j // njh, k, j % njh))
    elif tb:
        b_spec = pl.BlockSpec((tn, tk), lambda i, j, k: (j, k))
    else:
        b_spec = pl.BlockSpec((tk, tn), lambda i, j, k: (k, j))
    if whole_n:
        out_spec = pl.BlockSpec((2, tm, N // 2), lambda i, j, k: (0, i, 0))
        out_shape = jax.ShapeDtypeStruct((2, M, N // 2), out_dtype)
    elif out_split:
        out_spec = pl.BlockSpec((1, tm, tn), lambda i, j, k: (j // njh, i, j % njh))
        out_shape = jax.ShapeDtypeStruct((2, M, N // 2), out_dtype)
    else:
        out_spec = pl.BlockSpec((tm, tn), lambda i, j, k: (i, j))
        out_shape = jax.ShapeDtypeStruct((M, N), out_dtype)
    grid = (M // tm, N // tn, nk)
    acc_shapes = [] if nk == 1 else [pltpu.VMEM((tm, tn), F32)]
    if host is None:
        return _pcall(
            body, name=name, grid=grid, in_specs=[a_spec, b_spec], out_specs=out_spec, out_shape=out_shape,
            scratch_shapes=acc_shapes, compiler_params=_params(3),
        )(a, b)

    n_hi, n_ho, n_steps = len(host.ins), len(host.out_shapes), grid[0] * grid[1] * grid[2]

    def hosting_body(a_ref, b_ref, *rest):
        host_in, rest = rest[:n_hi], rest[n_hi:]
        o_ref, rest = rest[0], rest[1:]
        host_out, rest = rest[:n_ho], rest[n_ho:]
        scratch, host_sems = rest[: len(acc_shapes)], rest[len(acc_shapes) :]
        step = (pl.program_id(0) * grid[1] + pl.program_id(1)) * grid[2] + pl.program_id(2)
        refs = (host_in, host_out, host_sems)

        @pl.when(step == 0)
        def _():
            host.start(*refs)

        @pl.when(step == n_steps // 2)
        def _():
            host.relay(*refs)

        body(a_ref, b_ref, o_ref, *scratch)

        @pl.when(step == n_steps - 1)
        def _():
            host.finish(*refs)

    return _pcall(
        hosting_body,
        name=name,
        grid=grid,
        in_specs=[a_spec, b_spec, *host.in_specs],
        out_specs=[out_spec, *host.out_specs],
        out_shape=[out_shape, *host.out_shapes],
        scratch_shapes=[*acc_shapes, *host.scratch],
        compiler_params=_params(3),
    )(a, b, *host.ins)


def win(arr, off, width):
    assert off % width == 0 and off + width <= arr.shape[1], (arr.shape, off, width)
    return (arr, off // width, width)


def rowwise(fn, rows, vecs, outs, accs=(), *, tm, name, into=None):
    rows = [r if isinstance(r, tuple) else (r, 0, r.shape[-1]) for r in rows]
    S = rows[0][0].shape[-2]
    tm = _pick(S, tm, 8)
    n_rows, n_vecs, n_outs = len(rows), len(vecs), len(outs)
    aliased = into is not None and not isinstance(into[0], int)
    n_alias = 1 if aliased else 0

    def body(*refs):
        ins = [r[...] for r in refs[: n_rows + n_vecs]]
        refs = refs[: n_rows + n_vecs] + refs[n_rows + n_vecs + n_alias :]
        out_refs = refs[n_rows + n_vecs : n_rows + n_vecs + n_outs]
        acc_refs = refs[n_rows + n_vecs + n_outs :]
        res, acc = fn(*ins)
        for r, v in zip(out_refs, res, strict=True):
            r[...] = v.astype(r.dtype)
        if acc_refs:
            i = pl.program_id(0)

            @pl.when(i == 0)
            def _():
                for r, v in zip(acc_refs, acc, strict=True):
                    r[...] = v

            @pl.when(i > 0)
            def _():
                for r, v in zip(acc_refs, acc, strict=True):
                    r[...] += v

    in_specs = []
    for arr, blk, w in rows:
        if arr.ndim == 3:
            in_specs.append(pl.BlockSpec((arr.shape[0], tm, w), lambda i: (0, i, 0)))
        else:
            in_specs.append(pl.BlockSpec((tm, w), functools.partial(lambda i, blk: (i, blk), blk=blk)))
    for v in vecs:
        in_specs.append(pl.BlockSpec(v.shape, functools.partial(lambda i, nd: (0,) * nd, nd=v.ndim)))
    out_specs = [pl.BlockSpec((tm, w), lambda i: (i, 0)) for w, _ in outs]
    out_specs += [pl.BlockSpec((1, w), lambda i: (0, 0)) for w in accs]
    out_shape = [jax.ShapeDtypeStruct((S, w), dt) for w, dt in outs]
    out_shape += [jax.ShapeDtypeStruct((1, w), F32) for w in accs]
    operands = [*[r[0] for r in rows], *vecs]
    aliases = {}
    if into is not None:
        target, off = into
        w0, dt0 = outs[0]
        total = target if isinstance(target, int) else target.shape[1]
        assert off % w0 == 0 and off + w0 <= total
        out_specs[0] = pl.BlockSpec((tm, w0), functools.partial(lambda i, blk: (i, blk), blk=off // w0))
        out_shape[0] = jax.ShapeDtypeStruct((S, total), dt0)
        if aliased:
            in_specs.append(pl.BlockSpec(memory_space=pl.ANY))
            operands.append(target)
            aliases = {len(operands) - 1: 0}
    res = _pcall(
        body,
        name=name,
        grid=(S // tm,),
        in_specs=in_specs,
        out_specs=out_specs,
        out_shape=out_shape,
        input_output_aliases=aliases,
        compiler_params=_params(1),
    )(*operands)
    return res


def _sum0(v):
    return jnp.sum(v, axis=0, keepdims=True)


def _sigmoid(z):
    return 1.0 / (1.0 + jnp.exp(-z))


def _silu(z):
    return z * _sigmoid(z)


def _dsilu(z):
    s = _sigmoid(z)
    return s * (1.0 + z * (1.0 - s))


def _swap_half(x, half):
    if 2 * half == LANE:
        return pltpu.roll(x, half, 1)
    lane = lax.broadcasted_iota(jnp.int32, x.shape, 1)
    return jnp.where((lane % (2 * half)) < half, pltpu.roll(x, LANE - half, 1), pltpu.roll(x, half, 1))


def _rope_tile(x, cosf, sinf, half, inverse=False):
    sw = _swap_half(x, half)
    return x * cosf - sw * sinf if inverse else x * cosf + sw * sinf


def _tri_tables(n, by_key):
    if by_key:
        pairs = [(i, j) for j in range(n) for i in range(j, n)]
    else:
        pairs = [(i, j) for i in range(n) for j in range(i + 1)]
    return (np.array([p[0] for p in pairs], np.int32), np.array([p[1] for p in pairs], np.int32))


def _attn_scores(q, kn, kr, i, j, tq, masked):
    k = jnp.concatenate([kn, kr], axis=1)
    s = lax.dot_general(q, k, (((1,), (1,)), ((), ())), preferred_element_type=F32)
    if masked:
        rc = (i * tq + lax.broadcasted_iota(jnp.int32, s.shape, 0)) // CHUNK
        cc = (j * tq + lax.broadcasted_iota(jnp.int32, s.shape, 1)) // CHUNK
        s = jnp.where(cc <= rc, s, NEG_INF)
    return s, k


def attn_fwd(q, kv, kr, proj, mg_off, *, heads, tq, name, host=None):
    S = q.shape[0]
    tq = _pick(S, tq, CHUNK)
    n = S // tq
    it, jt = _tri_tables(n, by_key=False)
    T = len(it)
    mg_blk = mg_off // MLA_DV
    n_hi, n_ho = (len(host.ins), len(host.out_shapes)) if host else (0, 0)

    hp = min(ATTN_FWD_HEADS_PER_STEP, heads)
    assert heads % hp == 0 and mg_blk % hp == 0
    groups = heads // hp

    def body(it_ref, jt_ref, q_ref, kv_ref, kr_ref, mg_ref, *rest):
        host_in, rest = rest[:n_hi], rest[n_hi:]
        (o_ref, a_ref, lse_ref), rest = rest[:3], rest[3:]
        host_out, rest = rest[:n_ho], rest[n_ho:]
        (m_sc, acc_sc), host_sems = rest[:2], rest[2:]
        h, t = pl.program_id(0), pl.program_id(1)
        i, j = it_ref[t], jt_ref[t]
        if host:
            _host_steps(host, (host_in, host_out, host_sems), h, t, groups, T, before=True)

        @pl.when(j == 0)
        def _():
            m_sc[...] = jnp.full(m_sc.shape, NEG_INF, F32)
            acc_sc[...] = jnp.zeros(acc_sc.shape, F32)

        def step(masked):
            kr = kr_ref[...]
            for u in range(hp):
                q = q_ref[:, u * MLA_QW : (u + 1) * MLA_QW]
                kn = kv_ref[:, u * MLA_QW : u * MLA_QW + MLA_NOPE]
                v = kv_ref[:, u * MLA_QW + MLA_NOPE : (u + 1) * MLA_QW]
                s, _ = _attn_scores(q, kn, kr, i, j, tq, masked)
                m_prev = m_sc[u]
                m_new = jnp.maximum(m_prev, jnp.max(s, axis=1, keepdims=True))
                p = jnp.exp2((s - m_new).astype(BF16))
                alpha = jnp.exp2(m_prev - m_new)
                v_ones = jnp.concatenate([v, jnp.ones(v.shape, BF16)], axis=1)
                acc_sc[u] = alpha * acc_sc[u] + jnp.dot(p, v_ones, preferred_element_type=F32)
                m_sc[u] = m_new

        @pl.when(j < i)
        def _():
            step(False)

        @pl.when(j == i)
        def _():
            step(True)
            for u in range(hp):
                cols = slice(u * MLA_DV, (u + 1) * MLA_DV)
                l = acc_sc[u, :, MLA_DV:]
                o = acc_sc[u, :, :MLA_DV] / l
                o_ref[:, cols] = o
                a_ref[:, cols] = (o * _silu(mg_ref[:, cols].astype(F32))).astype(a_ref.dtype)
                lse_ref[u] = m_sc[u] + jnp.log2(l[:, :1])

        if host:
            _host_steps(host, (host_in, host_out, host_sems), h, t, groups, T, before=False)

    grid_spec = pltpu.PrefetchScalarGridSpec(
        num_scalar_prefetch=2,
        grid=(groups, T),
        in_specs=[
            pl.BlockSpec((tq, hp * MLA_QW), lambda h, t, it, jt: (it[t], h)),
            pl.BlockSpec((tq, hp * MLA_QW), lambda h, t, it, jt: (jt[t], h)),
            pl.BlockSpec((tq, LANE), lambda h, t, it, jt: (jt[t], 0)),
            pl.BlockSpec((tq, hp * MLA_DV), lambda h, t, it, jt: (it[t], mg_blk // hp + h)),
            *(host.in_specs if host else []),
        ],
        out_specs=[
            pl.BlockSpec((tq, hp * MLA_DV), lambda h, t, it, jt: (it[t], h)),
            pl.BlockSpec((tq, hp * MLA_DV), lambda h, t, it, jt: (it[t], h)),
            pl.BlockSpec((hp, tq, 1), lambda h, t, it, jt: (h, it[t], 0)),
            *(host.out_specs if host else []),
        ],
        scratch_shapes=[
            pltpu.VMEM((hp, tq, 1), F32),
            pltpu.VMEM((hp, tq, 2 * MLA_DV), F32),
            *(host.scratch if host else []),
        ],
    )
    return _pcall(
        body,
        name=name,
        grid_spec=grid_spec,
        out_shape=[
            jax.ShapeDtypeStruct((S, heads * MLA_DV), F32),
            jax.ShapeDtypeStruct((S, heads * MLA_DV), BF16),
            jax.ShapeDtypeStruct((heads, S, 1), F32),
            *(host.out_shapes if host else []),
        ],
        compiler_params=_params(2),
    )(jnp.asarray(it), jnp.asarray(jt), q, kv, kr, proj, *(host.ins if host else []))


def attn_bwd(q, kv, kr, do, o, lse, *, heads, tq, name, host=None):
    S = q.shape[0]
    tq = _pick(S, tq, CHUNK)
    n = S // tq
    it, jt = _tri_tables(n, by_key=True)
    T = len(it)
    n_hi, n_ho = (len(host.ins), len(host.out_shapes)) if host else (0, 0)

    hp = ATTN_HEADS_PER_STEP
    assert heads % hp == 0
    groups = heads // hp

    def body(it_ref, jt_ref, q_ref, kv_ref, kr_ref, do_ref, o_ref, lse_ref, *rest):
        host_in, rest = rest[:n_hi], rest[n_hi:]
        (dq_ref, dkv_ref, dkr_ref), rest = rest[:3], rest[3:]
        host_out, rest = rest[:n_ho], rest[n_ho:]
        (dq_acc, dk_sc, dv_sc), host_sems = rest[:3], rest[3:]
        h, t = pl.program_id(0), pl.program_id(1)
        i, j = it_ref[t], jt_ref[t]
        if host:
            _host_steps(host, (host_in, host_out, host_sems), h, t, groups, T, before=True)

        @pl.when(t == 0)
        def _():
            dq_acc[...] = jnp.zeros(dq_acc.shape, F32)

        @pl.when(i == j)
        def _():
            dk_sc[...] = jnp.zeros(dk_sc.shape, F32)
            dv_sc[...] = jnp.zeros(dv_sc.shape, F32)

        def step(masked):
            kr = kr_ref[...]
            rows = pl.ds(pl.multiple_of(i * tq, tq), tq)
            for u in range(hp):
                wide, narrow = slice(u * MLA_QW, (u + 1) * MLA_QW), slice(u * MLA_DV, (u + 1) * MLA_DV)
                q_blk, do_blk = q_ref[:, wide], do_ref[:, narrow]
                kn = kv_ref[:, u * MLA_QW : u * MLA_QW + MLA_NOPE]
                v = kv_ref[:, u * MLA_QW + MLA_NOPE : (u + 1) * MLA_QW]
                s, k = _attn_scores(q_blk, kn, kr, i, j, tq, masked)
                p = jnp.exp2((s - lse_ref[u]).astype(BF16))
                delta = jnp.sum(do_blk.astype(F32) * o_ref[:, narrow], axis=1, keepdims=True)
                dp = lax.dot_general(do_blk, v, _NT, preferred_element_type=F32)
                ds = p * (dp - delta).astype(BF16)
                dv_sc[u] += lax.dot_general(p, do_blk, _TN, preferred_element_type=F32)
                dk_sc[u] += lax.dot_general(ds, q_blk, _TN, preferred_element_type=F32)
                dq_acc[rows, wide] += jnp.dot(ds, k, preferred_element_type=F32)

        @pl.when(i == j)
        def _():
            step(True)

        @pl.when(i > j)
        def _():
            step(False)

        @pl.when(i == n - 1)
        def _():
            for u in range(hp):
                dkv_ref[:, u * MLA_QW : u * MLA_QW + MLA_NOPE] = (dk_sc[u, :, :MLA_NOPE] * LN2).astype(dkv_ref.dtype)
                dkv_ref[:, u * MLA_QW + MLA_NOPE : (u + 1) * MLA_QW] = dv_sc[u].astype(dkv_ref.dtype)
                dkr_ref[u] = dk_sc[u, :, MLA_NOPE:] * LN2

        @pl.when(t == T - 1)
        def _():
            dq_ref[...] = (dq_acc[...] * MLA_SCALE).astype(dq_ref.dtype)

        if host:
            _host_steps(host, (host_in, host_out, host_sems), h, t, groups, T, before=False)

    grid_spec = pltpu.PrefetchScalarGridSpec(
        num_scalar_prefetch=2,
        grid=(groups, T),
        in_specs=[
            pl.BlockSpec((tq, hp * MLA_QW), lambda h, t, it, jt: (it[t], h)),
            pl.BlockSpec((tq, hp * MLA_QW), lambda h, t, it, jt: (jt[t], h)),
            pl.BlockSpec((tq, LANE), lambda h, t, it, jt: (jt[t], 0)),
            pl.BlockSpec((tq, hp * MLA_DV), lambda h, t, it, jt: (it[t], h)),
            pl.BlockSpec((tq, hp * MLA_DV), lambda h, t, it, jt: (it[t], h)),
            pl.BlockSpec((hp, tq, 1), lambda h, t, it, jt: (h, it[t], 0)),
            *(host.in_specs if host else []),
        ],
        out_specs=[
            pl.BlockSpec((S, hp * MLA_QW), lambda h, t, it, jt: (0, h)),
            pl.BlockSpec((tq, hp * MLA_QW), lambda h, t, it, jt: (jt[t], h)),
            pl.BlockSpec((hp, tq, LANE), lambda h, t, it, jt: (h, jt[t], 0)),
            *(host.out_specs if host else []),
        ],
        scratch_shapes=[
            pltpu.VMEM((S, hp * MLA_QW), F32),
            pltpu.VMEM((hp, tq, MLA_QW), F32),
            pltpu.VMEM((hp, tq, MLA_DV), F32),
            *(host.scratch if host else []),
        ],
    )
    return _pcall(
        body,
        name=name,
        grid_spec=grid_spec,
        out_shape=[
            jax.ShapeDtypeStruct((S, heads * MLA_QW), BF16),
            jax.ShapeDtypeStruct((S, heads * (MLA_NOPE + MLA_DV)), BF16),
            jax.ShapeDtypeStruct((heads, S, LANE), F32),
            *(host.out_shapes if host else []),
        ],
        compiler_params=_params(2),
    )(jnp.asarray(it), jnp.asarray(jt), q, kv, kr, do, o, lse, *(host.ins if host else []))


def _host_steps(host, refs, h, t, heads, n_steps, before):
    if before:

        @pl.when((h == 0) & (t == 0))
        def _():
            host.start(*refs)

        @pl.when((h == heads // 2) & (t == n_steps // 2))
        def _():
            host.relay(*refs)

    else:

        @pl.when((h == heads - 1) & (t == n_steps - 1))
        def _():
            host.finish(*refs)


def _ret_consts(heads):
    h = np.arange(heads, dtype=np.float32)
    lg = np.log(np.float32(1.0) - np.float32(2.0) ** (np.float32(-5.0) - h)).astype(np.float32)
    idx = np.arange(CHUNK, dtype=np.float32)
    dmat = np.exp(np.abs(idx[:, None] - idx[None, :])[None] * lg[:, None, None]).astype(np.float32)
    xi = np.exp((idx + 1.0)[None, :] * lg[:, None]).astype(np.float32)
    zeta = np.exp((CHUNK - 1.0 - idx)[None, :] * lg[:, None]).astype(np.float32)
    dec = np.exp(np.float32(CHUNK) * lg).astype(np.float32)
    xi = np.broadcast_to(xi[:, :, None], (heads, CHUNK, RET_DK)).copy()
    zeta = np.broadcast_to(zeta[:, :, None], (heads, CHUNK, RET_DK)).copy()
    dec = np.broadcast_to(dec[:, None, None], (heads, 8, LANE)).copy()
    return jnp.asarray(dmat), jnp.asarray(xi), jnp.asarray(zeta), jnp.asarray(dec)


_NT = (((1,), (1,)), ((), ()))
_TN = (((0,), (0,)), ((), ()))


def _dot(a, b, dn=(((1,), (0,)), ((), ()))):
    return lax.dot_general(a.astype(BF16), b.astype(BF16), dn, preferred_element_type=F32)


def _const_specs(hp):
    return [
        pl.BlockSpec((hp, CHUNK, CHUNK), lambda h, b: (h, 0, 0)),
        pl.BlockSpec((hp, CHUNK, RET_DK), lambda h, b: (h, 0, 0)),
        pl.BlockSpec((hp, CHUNK, RET_DK), lambda h, b: (h, 0, 0)),
        pl.BlockSpec((hp, 8, LANE), lambda h, b: (h, 0, 0)),
    ]


def _ret_group(heads, *blks):
    hp = min(RET_HEADS_PER_STEP, heads)
    assert heads % hp == 0 and all(b % hp == 0 for b in blks)
    return hp


def ret_fwd(q, k, proj, v_off, rg_off, *, heads, tb, name):
    S = q.shape[0]
    tb = _pick(S, tb, CHUNK)
    cpb, nb = tb // CHUNK, S // tb
    v_blk, rg_blk = v_off // RET_DV, rg_off // RET_DV
    hp = _ret_group(heads, v_blk, rg_blk)

    def body(q_ref, k_ref, v_ref, rg_ref, dm_ref, xi_ref, ze_ref, dec_ref, o_ref, a_ref, st_ref, r_sc):
        @pl.when(pl.program_id(1) == 0)
        def _():
            r_sc[...] = jnp.zeros(r_sc.shape, F32)

        for c in range(cpb):
            sl = pl.ds(c * CHUNK, CHUNK)
            for u in range(hp):
                dm, xi, ze, dec = dm_ref[u], xi_ref[u], ze_ref[u], dec_ref[u, 0:1, 0:1]
                narrow, wide = slice(u * RET_DK, (u + 1) * RET_DK), slice(u * RET_DV, (u + 1) * RET_DV)
                qc, kc, vc = q_ref[sl, narrow], k_ref[sl, narrow], v_ref[sl, wide]
                r = r_sc[u]
                st_ref[u, c] = r.astype(BF16)
                s = _dot(qc, kc, _NT) * dm
                o = _dot(s, vc) + _dot(qc.astype(F32) * xi, r)
                r_sc[u] = r * dec + _dot(kc.astype(F32) * ze, vc, _TN)
                mu = jnp.mean(o, axis=1, keepdims=True)
                d = o - mu
                n = d * lax.rsqrt(jnp.mean(d * d, axis=1, keepdims=True) + EPS)
                o_ref[sl, wide] = o
                a_ref[sl, wide] = (n * _silu(rg_ref[sl, wide].astype(F32))).astype(a_ref.dtype)

    return _pcall(
        body,
        name=name,
        grid=(heads // hp, nb),
        in_specs=[
            pl.BlockSpec((tb, hp * RET_DK), lambda h, b: (b, h)),
            pl.BlockSpec((tb, hp * RET_DK), lambda h, b: (b, h)),
            pl.BlockSpec((tb, hp * RET_DV), lambda h, b: (b, v_blk // hp + h)),
            pl.BlockSpec((tb, hp * RET_DV), lambda h, b: (b, rg_blk // hp + h)),
            *_const_specs(hp),
        ],
        out_specs=[
            pl.BlockSpec((tb, hp * RET_DV), lambda h, b: (b, h)),
            pl.BlockSpec((tb, hp * RET_DV), lambda h, b: (b, h)),
            pl.BlockSpec((hp, cpb, RET_DK, RET_DV), lambda h, b: (h, b, 0, 0)),
        ],
        out_shape=[
            jax.ShapeDtypeStruct((S, heads * RET_DV), F32),
            jax.ShapeDtypeStruct((S, heads * RET_DV), BF16),
            jax.ShapeDtypeStruct((heads, S // CHUNK, RET_DK, RET_DV), BF16),
        ],
        scratch_shapes=[pltpu.VMEM((hp, RET_DK, RET_DV), F32)],
        compiler_params=_params(2),
    )(q, k, proj, proj, *_ret_consts(heads))


def ret_bwd(q, k, proj, v_off, states, do, *, heads, tb, name, into):
    S = q.shape[0]
    tb = _pick(S, tb, CHUNK)
    cpb, nb = tb // CHUNK, S // tb
    v_blk = v_off // RET_DV
    hp = _ret_group(heads, v_blk)
    target, dv_off = into
    dv_blk = dv_off // (hp * RET_DV)
    assert dv_off % (hp * RET_DV) == 0

    def body(q_ref, k_ref, v_ref, st_ref, do_ref, dm_ref, xi_ref, ze_ref, dec_ref, _, dq_ref, dk_ref, dv_ref, dr_sc):
        @pl.when(pl.program_id(1) == 0)
        def _():
            dr_sc[...] = jnp.zeros(dr_sc.shape, F32)

        for c in reversed(range(cpb)):
            sl = pl.ds(c * CHUNK, CHUNK)
            for u in range(hp):
                dm, xi, ze, dec = dm_ref[u], xi_ref[u], ze_ref[u], dec_ref[u, 0:1, 0:1]
                narrow, wide = slice(u * RET_DK, (u + 1) * RET_DK), slice(u * RET_DV, (u + 1) * RET_DV)
                qc, kc, vc, doc = q_ref[sl, narrow], k_ref[sl, narrow], v_ref[sl, wide], do_ref[sl, wide]
                r_prev = st_ref[u, c]
                dr = dr_sc[u]
                a = _dot(qc, kc, _NT) * dm
                ds = _dot(doc, vc, _NT) * dm
                kz = kc.astype(F32) * ze
                dq_ref[sl, narrow] = (_dot(ds, kc) + _dot(doc, r_prev, _NT) * xi).astype(dq_ref.dtype)
                dk_ref[sl, narrow] = (_dot(ds, qc, _TN) + _dot(vc, dr, _NT) * ze).astype(dk_ref.dtype)
                dv_ref[sl, wide] = (_dot(a, doc, _TN) + _dot(kz, dr)).astype(dv_ref.dtype)
                dr_sc[u] = dr * dec + _dot(qc.astype(F32) * xi, doc, _TN)

    rev = lambda h, b: (nb - 1 - b, h)
    return _pcall(
        body,
        name=name,
        grid=(heads // hp, nb),
        in_specs=[
            pl.BlockSpec((tb, hp * RET_DK), rev),
            pl.BlockSpec((tb, hp * RET_DK), rev),
            pl.BlockSpec((tb, hp * RET_DV), lambda h, b: (nb - 1 - b, v_blk // hp + h)),
            pl.BlockSpec((hp, cpb, RET_DK, RET_DV), lambda h, b: (h, nb - 1 - b, 0, 0)),
            pl.BlockSpec((tb, hp * RET_DV), rev),
            *_const_specs(hp),
            pl.BlockSpec(memory_space=pl.ANY),
        ],
        out_specs=[
            pl.BlockSpec((tb, hp * RET_DK), rev),
            pl.BlockSpec((tb, hp * RET_DK), rev),
            pl.BlockSpec((tb, hp * RET_DV), lambda h, b: (nb - 1 - b, dv_blk + h)),
        ],
        out_shape=[
            jax.ShapeDtypeStruct((S, heads * RET_DK), BF16),
            jax.ShapeDtypeStruct((S, heads * RET_DK), BF16),
            jax.ShapeDtypeStruct(target.shape, target.dtype),
        ],
        input_output_aliases={9: 2},
        scratch_shapes=[pltpu.VMEM((hp, RET_DK, RET_DV), F32)],
        compiler_params=_params(2),
    )(q, k, proj, states, do, *_ret_consts(heads), target)


def _me():
    return (lax.axis_index("x"), lax.axis_index("y"), lax.axis_index("c"))


def _comm(name, ins, out_shapes, n_local, n_remote, plan):
    return run_exchange(name, Exchange(ins, out_shapes, n_local, n_remote, plan))


def run_exchange(name, ex):
    def body(*refs):
        ex.start(*ex.split(refs))
        ex.relay(*ex.split(refs))
        ex.finish(*ex.split(refs))

    return _pcall(
        body, name=name, in_specs=ex.in_specs, out_specs=ex.out_specs, out_shape=ex.out_shapes, scratch_shapes=ex.scratch
    )(*ex.ins)


class Exchange:
    def __init__(self, ins, out_shapes, n_local, n_remote, plan):
        self.ins, self.out_shapes, self.n_local, self.n_remote, self.plan = list(ins), list(out_shapes), n_local, n_remote, plan
        any_spec = pl.BlockSpec(memory_space=pl.ANY)
        self.in_specs, self.out_specs = [any_spec] * len(self.ins), [any_spec] * len(self.out_shapes)
        self.scratch = [
            pltpu.SemaphoreType.DMA((n_remote,)),
            pltpu.SemaphoreType.DMA((n_remote,)),
            pltpu.SemaphoreType.DMA((max(n_local, 1),)),
        ]

    def split(self, refs):
        n_in, n_out = len(self.ins), len(self.out_shapes)
        return refs[:n_in], refs[n_in : n_in + n_out], refs[n_in + n_out :]

    def _copies(self, in_refs, out_refs, sems):
        send_sems, recv_sems, local_sems = sems
        me = _me()
        local_plan, remote_plan = self.plan(me, in_refs, out_refs)
        assert len(local_plan) == self.n_local and len(remote_plan) == self.n_remote, (len(local_plan), len(remote_plan))
        def local_copy(n):
            src, dst = local_plan[n]
            return pltpu.make_async_copy(src, dst, local_sems.at[n])

        def send(n):
            src, dst, peer, _, _ = remote_plan[n]
            return pltpu.make_async_remote_copy(
                src_ref=src, dst_ref=dst, send_sem=send_sems.at[n], recv_sem=recv_sems.at[n], device_id=peer, device_id_type=MESH
            )

        def arrival(n):
            src, _, _, landing, _ = remote_plan[n]
            return pltpu.make_async_remote_copy(
                src_ref=src, dst_ref=landing, send_sem=send_sems.at[n], recv_sem=recv_sems.at[n], device_id=me, device_id_type=MESH
            )

        after = [a for (_, _, _, _, a) in remote_plan]
        assert all(a is None or a < n for n, a in enumerate(after))
        return local_copy, send, arrival, after

    def start(self, in_refs, out_refs, sems):
        local_copy, send, _, after = self._copies(in_refs, out_refs, sems)
        for n in range(self.n_local):
            local_copy(n).start()
        for n, a in enumerate(after):
            if a is None:
                send(n).start()

    def relay(self, in_refs, out_refs, sems):
        _, send, arrival, after = self._copies(in_refs, out_refs, sems)
        for n in sorted({a for a in after if a is not None}):
            arrival(n).wait_recv()
            for m, a in enumerate(after):
                if a == n:
                    send(m).start()

    def finish(self, in_refs, out_refs, sems):
        local_copy, send, arrival, after = self._copies(in_refs, out_refs, sems)
        relayed = {a for a in after if a is not None}
        for n in range(self.n_remote):
            if n not in relayed:
                arrival(n).wait_recv()
        for n in range(self.n_remote):
            send(n).wait_send()
        for n in range(self.n_local):
            local_copy(n).wait()


_CHIP_FLIPS = ((1, 0), (0, 1), (1, 1))


class Placement:
    def __init__(self, rows, full_rows, cuts=(), offset=None, zero_rows=None):
        edges = [0, *sorted(cuts), rows]
        self.rows, self.full_rows = rows, full_rows
        self.runs = [(a, b - a) for a, b in zip(edges[:-1], edges[1:]) if b > a]
        self.offset = offset if offset is not None else (lambda j, start: j * rows + start)
        self.zero_rows = zero_rows

    def at(self, ref_2d_plus, j, start, size):
        return pl.ds(pl.multiple_of(self.offset(j, start), 16), size)


def gather_layer(prepped, zeros, places, l, name):
    n_w = len(prepped)
    n_chip = sum(len(p.runs) for p in places) * 3
    n_own = sum(len(p.runs) for p in places) * 2 + sum(2 for p in places if p.zero_rows)

    def plan(me, in_refs, out_refs):
        x, y, c = me
        j = 2 * x + y
        sib = (x, y, 1 - c)
        from_chips, own, forwards = [], [], []
        for i, o, p in zip(in_refs[:n_w], out_refs, places):
            for start, size in p.runs:
                for dx, dy in _CHIP_FLIPS:
                    px, py = x ^ dx, y ^ dy
                    jp = 2 * px + py
                    mine_there = o.at[c, p.at(o, j, start, size)]
                    theirs_here = o.at[c, p.at(o, jp, start, size)]
                    n = len(from_chips)
                    from_chips.append((i.at[l, c, pl.ds(start, size)], mine_there, (px, py, c), theirs_here, None))
                    forwards.append((theirs_here, theirs_here, sib, o.at[1 - c, p.at(o, jp, start, size)], n))
                for h in range(2):
                    place = o.at[h, p.at(o, j, start, size)]
                    own.append((i.at[l, h, pl.ds(start, size)], place, sib, place, None))
            if p.zero_rows:
                for h in range(2):
                    place = o.at[h, pl.ds(p.zero_rows[0], p.zero_rows[1])]
                    own.append((in_refs[n_w], place, sib, place, None))
        return [], from_chips + own + forwards

    shapes = [jax.ShapeDtypeStruct((2, p.full_rows, w.shape[3]), w.dtype) for w, p in zip(prepped, places)]
    ins = [*prepped, zeros] if any(p.zero_rows for p in places) else list(prepped)
    ex = Exchange(ins, shapes, 0, 2 * n_chip + n_own, plan)
    return ex if name is None else run_exchange(name, ex)


def sibling_scatter(arrays, name):
    def plan(me, in_refs, out_refs):
        x, y, c = me
        return [], [(i.at[1 - c], o, (x, y, 1 - c), o, None) for i, o in zip(in_refs, out_refs)]

    shapes = [jax.ShapeDtypeStruct(a.shape[1:], a.dtype) for a in arrays]
    ex = Exchange(arrays, shapes, 0, len(arrays), plan)
    return ex if name is None else run_exchange(name, ex)


def sibling_swap(arrays, name):
    def plan(me, in_refs, out_refs):
        x, y, c = me
        return [], [(i, o, (x, y, 1 - c), o, None) for i, o in zip(in_refs, out_refs)]

    shapes = [jax.ShapeDtypeStruct(a.shape, a.dtype) for a in arrays]
    ex = Exchange(arrays, shapes, 0, len(arrays), plan)
    return ex if name is None else run_exchange(name, ex)


def scatter_to_chips(arrays, places, name):
    def plan(me, in_refs, out_refs):
        x, y, c = me
        j = 2 * x + y
        local_plan, remote_plan = [], []
        for i, o, p in zip(in_refs, out_refs, places):
            for start, size in p.runs:
                local_plan.append((i.at[p.at(i, j, start, size)], o.at[3, pl.ds(start, size)]))
                for k, (dx, dy) in enumerate(_CHIP_FLIPS):
                    px, py = x ^ dx, y ^ dy
                    landing = o.at[k, pl.ds(start, size)]
                    remote_plan.append((i.at[p.at(i, 2 * px + py, start, size)], landing, (px, py, c), landing, None))
        return local_plan, remote_plan

    n_runs = sum(len(p.runs) for p in places)
    shapes = [jax.ShapeDtypeStruct((4, p.rows, a.shape[1]), a.dtype) for a, p in zip(arrays, places)]
    ex = Exchange(arrays, shapes, n_runs, 3 * n_runs, plan)
    return ex if name is None else run_exchange(name, ex)


def allgather8(block, name):
    def plan(me, in_refs, out_refs):
        x, y, c = me
        (i,), (o,) = in_refs, out_refs
        mine = 4 * x + 2 * y + c
        remote_plan = []
        for flip in range(1, 8):
            px, py, pc = x ^ (flip >> 2), y ^ ((flip >> 1) & 1), c ^ (flip & 1)
            remote_plan.append((i, o.at[mine], (px, py, pc), o.at[4 * px + 2 * py + pc], None))
        return [(i, o.at[mine])], remote_plan

    return _comm(name, [block], [jax.ShapeDtypeStruct((8, *block.shape), block.dtype)], 1, 7, plan)[0]


def transpose_split(a, out_dtype, name):
    L, R, C = a.shape
    rh = R // 2
    rt = _pick(rh, 256)
    n = rh // rt

    def body(x_ref, o_ref):
        o_ref[0, 0] = x_ref[0].T.astype(o_ref.dtype)

    return _pcall(
        body,
        name=name,
        grid=(L, 2, n),
        in_specs=[pl.BlockSpec((1, rt, C), lambda l, h, i: (l, h * n + i, 0))],
        out_specs=pl.BlockSpec((1, 1, C, rt), lambda l, h, i: (l, h, 0, i)),
        out_shape=jax.ShapeDtypeStruct((L, 2, C, rh), out_dtype),
        compiler_params=_params(3),
    )(a)


def cast_split(a, out_dtype, name):
    L, R, C = a.shape
    rt = _pick(R, 512, 16)

    def body(x_ref, o_ref):
        o_ref[0, 0] = x_ref[0].astype(o_ref.dtype)

    return _pcall(
        body,
        name=name,
        grid=(L, 2, R // rt),
        in_specs=[pl.BlockSpec((1, rt, C // 2), lambda l, h, i: (l, i, h))],
        out_specs=pl.BlockSpec((1, 1, rt, C // 2), lambda l, h, i: (l, h, i, 0)),
        out_shape=jax.ShapeDtypeStruct((L, 2, R, C // 2), out_dtype),
        compiler_params=_params(3),
    )(a)


def untranspose_halves(mine, theirs, my_c, stack, l, n_layers, name):
    C, rh = mine.shape
    n_in = 2 if stack is None else 3
    rt = _pick(rh, 256)
    n = rh // rt

    def body(c_ref, a_ref, b_ref, *rest):
        o_ref = rest[-1]
        h = pl.program_id(0)
        o_ref[0] = jnp.where(h == c_ref[0], a_ref[...], b_ref[...]).T

    grid_spec = pltpu.PrefetchScalarGridSpec(
        num_scalar_prefetch=1,
        grid=(2, n),
        in_specs=[pl.BlockSpec((C, rt), lambda h, i, c: (0, i))] * 2
        + ([] if stack is None else [pl.BlockSpec(memory_space=pl.ANY)]),
        out_specs=pl.BlockSpec((1, rt, C), lambda h, i, c: (l, h * n + i, 0)),
    )
    return _pcall(
        body,
        name=name,
        grid_spec=grid_spec,
        out_shape=jax.ShapeDtypeStruct((n_layers, 2 * rh, C), F32),
        input_output_aliases={} if stack is None else {n_in: 0},
        compiler_params=_params(2),
    )(my_c.reshape(1), mine, theirs, *([] if stack is None else [stack]))


def merge_halves(mine, theirs, my_c, stack, l, n_layers, name):
    R, ch = mine.shape
    n_in = 2 if stack is None else 3
    rt = _pick(R, 512, 8)

    def body(c_ref, a_ref, b_ref, *rest):
        o_ref = rest[-1]
        o_ref[0] = jnp.where(pl.program_id(0) == c_ref[0], a_ref[...], b_ref[...])

    grid_spec = pltpu.PrefetchScalarGridSpec(
        num_scalar_prefetch=1,
        grid=(2, R // rt),
        in_specs=[pl.BlockSpec((rt, ch), lambda h, i, c: (i, 0))] * 2
        + ([] if stack is None else [pl.BlockSpec(memory_space=pl.ANY)]),
        out_specs=pl.BlockSpec((1, rt, ch), lambda h, i, c: (l, i, h)),
    )
    return _pcall(
        body,
        name=name,
        grid_spec=grid_spec,
        out_shape=jax.ShapeDtypeStruct((n_layers, R, 2 * ch), F32),
        input_output_aliases={} if stack is None else {n_in: 0},
        compiler_params=_params(2),
    )(my_c.reshape(1), mine, theirs, *([] if stack is None else [stack]))


def add_pair(g, r, my_c, name):
    _, rows, w = g.shape
    tm = _pick(rows, max(16, min(1024, (1 << 20) // w // 16 * 16)), 16)

    def body(c_ref, g_ref, r_ref, o_ref):
        o_ref[...] = (g_ref[0].astype(F32) + r_ref[...].astype(F32)).astype(o_ref.dtype)

    grid_spec = pltpu.PrefetchScalarGridSpec(
        num_scalar_prefetch=1,
        grid=(rows // tm,),
        in_specs=[pl.BlockSpec((1, tm, w), lambda i, c: (c[0], i, 0)), pl.BlockSpec((tm, w), lambda i, c: (i, 0))],
        out_specs=pl.BlockSpec((tm, w), lambda i, c: (i, 0)),
    )
    return _pcall(
        body, name=name, grid_spec=grid_spec, out_shape=jax.ShapeDtypeStruct((rows, w), BF16), compiler_params=_params(1)
    )(my_c.reshape(1), g, r)


class Dims:
    def __init__(self, S, D, L, ret_heads, mla_heads, q_rank, kv_rank):
        self.S, self.D, self.L, self.HR, self.HM, self.QR, self.KR = S, D, L, ret_heads, mla_heads, q_rank, kv_rank
        self.RQ, self.RV, self.MV = ret_heads * RET_DK, ret_heads * RET_DV, mla_heads * MLA_DV
        self.D_IN = 2 * self.RQ + 2 * self.RV + q_rank + kv_rank + MLA_ROPE + self.MV + 2 * D
        self.lo = 2 * self.RQ + 2 * self.RV
        self.mid = q_rank + kv_rank + MLA_ROPE
        self.DP = self.D_IN + LANE - MLA_ROPE
        self.o_rq, self.o_rk, self.o_rv, self.o_rg = 0, self.RQ, 2 * self.RQ, 2 * self.RQ + self.RV
        self.o_mg = self.lo
        self.o_bga = self.o_mg + self.MV
        self.o_bgb = self.o_bga + D
        self.o_cq = self.o_bgb + D
        self.o_ckv = self.o_cq + q_rank
        self.o_kr = self.o_ckv + kv_rank

    def pad_uq_rows(self, w):
        heads = w.shape[-2] // (MLA_NOPE + MLA_ROPE)
        w = w.reshape(*w.shape[:-2], heads, MLA_NOPE + MLA_ROPE, w.shape[-1])
        w = jnp.pad(w, [(0, 0)] * (w.ndim - 2) + [(0, MLA_QW - MLA_NOPE - MLA_ROPE), (0, 0)])
        return w.reshape(*w.shape[:-3], heads * MLA_QW, w.shape[-1])

    def unpad_uq_rows(self, w):
        heads = w.shape[-2] // MLA_QW
        w = w.reshape(*w.shape[:-2], heads, MLA_QW, w.shape[-1])[..., : MLA_NOPE + MLA_ROPE, :]
        return w.reshape(*w.shape[:-3], heads * (MLA_NOPE + MLA_ROPE), w.shape[-1])

    def placements(self):
        rows_in = self.D_IN // 4
        lo, mid, n_hi = self.lo, self.mid, self.D_IN - self.lo - self.mid
        cuts = {b % rows_in for b in (lo, lo + mid)} - {0}

        def offset_in(j, start):
            g = j * rows_in + start
            return jnp.where(g < lo, g, jnp.where(g < lo + mid, g + n_hi, g - mid))

        return [
            Placement(rows_in, self.DP, cuts, offset_in, zero_rows=(self.D_IN, self.DP - self.D_IN)),
            Placement(self.HM * MLA_QW // 4, self.HM * MLA_QW),
            Placement(self.HM * (MLA_NOPE + MLA_DV) // 4, self.HM * (MLA_NOPE + MLA_DV)),
            Placement(self.RV // 4, self.RV),
            Placement(self.MV // 4, self.MV),
            Placement(self.D // 4, self.D),
        ]


def _rope_tables(positions, dim, width):
    inv = 1.0 / (ROPE_BASE ** (jnp.arange(0, dim, 2, dtype=F32) / dim))
    ang = positions.astype(F32)[:, None] * inv
    cos, sin = jnp.cos(ang), jnp.sin(ang)
    pad = jnp.zeros((positions.shape[0], width - dim), F32)
    return jnp.concatenate([cos, cos, pad], axis=1), jnp.concatenate([-sin, sin, pad], axis=1)


def _tiles(x):
    return [x[:, t * LANE : (t + 1) * LANE] for t in range(x.shape[1] // LANE)]


def _cat(parts):
    return parts[0] if len(parts) == 1 else jnp.concatenate(parts, axis=1)


def _rms(x, eps=EPS):
    return lax.rsqrt(jnp.mean(x * x, axis=1, keepdims=True) + eps)


def layer_fwd(dm, l, x, shift, scale, gate, g_norm, g_cq, g_ckv, w, tabs, host=None, host_in=None, own_late=False):
    cos_r, sin_r, cos_m, sin_m = tabs
    nm = lambda s: f"l{l}_{s}"

    def f_norm(x, g, scale, shift):
        return [x * _rms(x) * g * (1.0 + scale) + shift], []

    (h,) = rowwise(f_norm, [x], [g_norm, scale, shift], [(dm.D, BF16)], tm=256, name=nm("norm"))
    proj, *late = _as_list(matmul(h, w["in"], tb=True, b_split=True, name=nm("mm_in"), tn=1920, host=host_in))
    if own_late:
        w = dict(w, **dict(zip(("uq", "ukv", "ret", "mla", "out"), late)))
        late = []

    def f_rope_ret(rq, rk, cos, sin):
        rq, rk = rq.astype(F32), rk.astype(F32)
        q = _cat([_rope_tile(t, cos, sin, RET_DK // 2) for t in _tiles(rq)])
        k = _cat([_rope_tile(t, cos, sin, RET_DK // 2) * (RET_DK**-0.5) for t in _tiles(rk)])
        return [q, k], []

    rq, rk = rowwise(
        f_rope_ret,
        [win(proj, dm.o_rq, dm.RQ), win(proj, dm.o_rk, dm.RQ), cos_r, sin_r],
        [],
        [(dm.RQ, BF16)] * 2,
        tm=512,
        name=nm("rope_ret"),
    )
    o_ret, a_ret, states = ret_fwd(rq, rk, proj, dm.o_rv, dm.o_rg, heads=dm.HR, tb=RET_BLOCK, name=nm("ret_fwd"))

    def f_prep(cq, ckv, kr, cos, sin, g_cq, g_ckv):
        cq, ckv, kr = cq.astype(F32), ckv.astype(F32), kr.astype(F32)
        return [cq * _rms(cq) * g_cq, ckv * _rms(ckv) * g_ckv, _rope_tile(kr, cos, sin, MLA_ROPE // 2)], []

    cqn, ckvn, krr = rowwise(
        f_prep,
        [win(proj, dm.o_cq, dm.QR), win(proj, dm.o_ckv, dm.KR), win(proj, dm.o_kr, LANE), cos_m, sin_m],
        [g_cq, g_ckv],
        [(dm.QR, BF16), (dm.KR, BF16), (LANE, BF16)],
        tm=512,
        name=nm("mla_prep"),
    )
    q_raw = matmul(cqn, w["uq"], tb=True, b_split=True, name=nm("mm_uq"), tn=4096)
    kv = matmul(ckvn, w["ukv"], tb=True, b_split=True, name=nm("mm_ukv"), tn=4096)

    def f_rope_q(q, cos, sin):
        t = _tiles(q.astype(F32))
        rot = [t[n] if n % 2 == 0 else _rope_tile(t[n], cos, sin, MLA_ROPE // 2) for n in range(len(t))]
        return [_cat([r * (MLA_SCALE * LOG2E) for r in rot])], []

    (q,) = rowwise(f_rope_q, [q_raw, cos_m, sin_m], [], [(dm.HM * MLA_QW, BF16)], tm=256, name=nm("rope_q"))
    o_mla, a_mla, lse, *hosted = attn_fwd(
        q, kv, krr, proj, dm.o_mg, heads=dm.HM, tq=ATTN_BLOCK, name=nm("attn_fwd"), host=host
    )

    y_ret = matmul(a_ret, w["ret"], b_split=True, name=nm("mm_ret"))
    y_mla = matmul(a_mla, w["mla"], b_split=True, name=nm("mm_mla"))

    def f_merge(y_ret, y_mla, bga, bgb):
        return [_sigmoid(bga.astype(F32)) * y_ret.astype(F32) + _sigmoid(bgb.astype(F32)) * y_mla.astype(F32)], []

    (merged,) = rowwise(
        f_merge, [y_ret, y_mla, win(proj, dm.o_bga, dm.D), win(proj, dm.o_bgb, dm.D)], [], [(dm.D, BF16)], tm=256, name=nm("merge")
    )
    out = matmul(merged, w["out"], b_split=True, out_dtype=F32, name=nm("mm_out"))

    def f_resid(x, out, gate):
        return [x + gate * out], []

    (x_new,) = rowwise(f_resid, [x, out], [gate], [(dm.D, F32)], tm=256, name=nm("resid"))
    saved = dict(
        x=x, h=h, proj=proj, rq=rq, rk=rk, o_ret=o_ret, a_ret=a_ret, states=states, cqn=cqn, ckvn=ckvn, krr=krr, q=q, kv=kv,
        o_mla=o_mla, a_mla=a_mla, lse=lse, y_ret=y_ret, y_mla=y_mla, merged=merged, out=out,
    )
    return x_new, saved, hosted, late, w


def _as_list(v):
    return list(v) if isinstance(v, (list, tuple)) else [v]


def layer_bwd(
    dm, l, dx_out, sv, shift, scale, gate, g_norm, g_cq, g_ckv, w, tabs, host=None, exchange_dh=None, exchange_dw=None
):
    cos_r, sin_r, cos_m, sin_m = tabs
    nm = lambda s: f"l{l}_{s}"
    proj = sv["proj"]

    def b_resid(dx, out, gate):
        return [dx * gate], [_sum0(dx * out)]

    dout, d_gate = rowwise(b_resid, [dx_out, sv["out"]], [gate], [(dm.D, BF16)], [dm.D], tm=256, name=nm("resid_bwd"))
    dmerged = matmul(dout, w["out"], tb=True, b_split=True, name=nm("mm_dmerged"))
    dw_out = matmul(sv["merged"], dout, ta=True, out_split=True, tn=2048, name=nm("mm_dw_out"))

    def b_merge(dmg, y_ret, y_mla, bga, bgb):
        dmg = dmg.astype(F32)
        ga, gb = _sigmoid(bga.astype(F32)), _sigmoid(bgb.astype(F32))
        y_ret, y_mla = y_ret.astype(F32), y_mla.astype(F32)
        d_gates = jnp.concatenate([dmg * y_ret * ga * (1.0 - ga), dmg * y_mla * gb * (1.0 - gb)], axis=1)
        return [d_gates, dmg * ga, dmg * gb], []

    dproj, dy_ret, dy_mla = rowwise(
        b_merge,
        [dmerged, sv["y_ret"], sv["y_mla"], win(proj, dm.o_bga, dm.D), win(proj, dm.o_bgb, dm.D)],
        [],
        [(2 * dm.D, BF16), (dm.D, BF16), (dm.D, BF16)],
        tm=256,
        name=nm("merge_bwd"),
        into=(dm.DP, dm.o_bga),
    )
    da_ret = matmul(dy_ret, w["ret"], tb=True, b_split=True, name=nm("mm_da_ret"))
    dw_ret = matmul(sv["a_ret"], dy_ret, ta=True, out_split=True, tn=2048, name=nm("mm_dw_ret"))
    da_mla = matmul(dy_mla, w["mla"], tb=True, b_split=True, name=nm("mm_da_mla"))
    dw_mla = matmul(sv["a_mla"], dy_mla, ta=True, out_split=True, tn=2048, name=nm("mm_dw_mla"))

    def b_ret_gate(da, rg, o):
        da, rg = da.astype(F32), rg.astype(F32)
        do_parts, drg_parts = [], []
        for hh in range(dm.HR):
            sl = slice(hh * RET_DV, (hh + 1) * RET_DV)
            oh, dah, rgh = o[:, sl], da[:, sl], rg[:, sl]
            mu = jnp.mean(oh, axis=1, keepdims=True)
            d = oh - mu
            r = lax.rsqrt(jnp.mean(d * d, axis=1, keepdims=True) + EPS)
            n = d * r
            dn = dah * _silu(rgh)
            drg_parts.append(dah * n * _dsilu(rgh))
            do_parts.append(r * (dn - jnp.mean(dn, axis=1, keepdims=True) - n * jnp.mean(dn * n, axis=1, keepdims=True)))
        return [_cat(drg_parts), _cat(do_parts)], []

    dproj, do_ret = rowwise(
        b_ret_gate, [da_ret, win(proj, dm.o_rg, dm.RV), sv["o_ret"]], [], [(dm.RV, BF16)] * 2, tm=256, name=nm("ret_gate_bwd"),
        into=(dproj, dm.o_rg),
    )
    dq_rot, dk_rot, dproj = ret_bwd(
        sv["rq"], sv["rk"], proj, dm.o_rv, sv["states"], do_ret, heads=dm.HR, tb=RET_BLOCK, name=nm("ret_bwd"), into=(dproj, dm.o_rv)
    )

    def b_rope_ret(dq, dk, cos, sin):
        dq, dk = dq.astype(F32), dk.astype(F32)
        q = [_rope_tile(t, cos, sin, RET_DK // 2, inverse=True) for t in _tiles(dq)]
        k = [_rope_tile(t, cos, sin, RET_DK // 2, inverse=True) * (RET_DK**-0.5) for t in _tiles(dk)]
        return [_cat(q + k)], []

    (dproj,) = rowwise(
        b_rope_ret, [dq_rot, dk_rot, cos_r, sin_r], [], [(2 * dm.RQ, BF16)], tm=512, name=nm("rope_ret_bwd"), into=(dproj, dm.o_rq)
    )

    def b_mla_gate(da, mg, o):
        da, mg = da.astype(F32), mg.astype(F32)
        return [da * o * _dsilu(mg), da * _silu(mg)], []

    dproj, do_mla = rowwise(
        b_mla_gate, [da_mla, win(proj, dm.o_mg, dm.MV), sv["o_mla"]], [], [(dm.MV, BF16)] * 2, tm=256, name=nm("mla_gate_bwd"),
        into=(dproj, dm.o_mg),
    )
    dq_att, dkv, dkr_heads, *hosted = attn_bwd(
        sv["q"], sv["kv"], sv["krr"], do_mla, sv["o_mla"], sv["lse"], heads=dm.HM, tq=ATTN_BLOCK, name=nm("attn_bwd"), host=host
    )

    def b_rope_q(dq, cos, sin):
        t = _tiles(dq.astype(F32))
        return [_cat([t[n] if n % 2 == 0 else _rope_tile(t[n], cos, sin, MLA_ROPE // 2, inverse=True) for n in range(len(t))])], []

    (dq_raw,) = rowwise(b_rope_q, [dq_att, cos_m, sin_m], [], [(dm.HM * MLA_QW, BF16)], tm=256, name=nm("rope_q_bwd"))
    dcqn = matmul(dq_raw, w["uq"], b_split=True, name=nm("mm_dcqn"))
    dw_uq = matmul(dq_raw, sv["cqn"], ta=True, out_split=True, tn=2048, name=nm("mm_dw_uq"))
    dckvn = matmul(dkv, w["ukv"], b_split=True, name=nm("mm_dckvn"))
    dw_ukv = matmul(dkv, sv["ckvn"], ta=True, out_split=True, tn=2048, name=nm("mm_dw_ukv"))

    def b_prep(dcqn, dckvn, cq, ckv, cos, sin, dkr_h, g_cq, g_ckv):
        outs, accs = [], []
        for dn, z, g in ((dcqn, cq, g_cq), (dckvn, ckv, g_ckv)):
            dn, z = dn.astype(F32), z.astype(F32)
            n = z * _rms(z)
            dng = dn * g
            outs.append(_rms(z) * (dng - n * jnp.mean(dng * n, axis=1, keepdims=True)))
            accs.append(_sum0(dn * n))
        dkr = dkr_h[0]
        for hh in range(1, dm.HM):
            dkr = dkr + dkr_h[hh]
        return [_cat(outs), _rope_tile(dkr, cos, sin, MLA_ROPE // 2, inverse=True)], accs

    dproj, d_kr, dg_cq, dg_ckv = rowwise(
        b_prep,
        [dcqn, dckvn, win(proj, dm.o_cq, dm.QR), win(proj, dm.o_ckv, dm.KR), cos_m, sin_m, dkr_heads],
        [g_cq, g_ckv],
        [(dm.QR + dm.KR, BF16), (LANE, BF16)],
        [dm.QR, dm.KR],
        tm=256,
        name=nm("mla_prep_bwd"),
        into=(dproj, dm.o_cq),
    )
    (dproj,) = rowwise(lambda a: ([a], []), [d_kr], [], [(LANE, BF16)], tm=512, name=nm("place_dkr"), into=(dproj, dm.o_kr))
    dws = dict(w_uq=dw_uq, w_ukv=dw_ukv, w_ret_proj=dw_ret, w_mla_proj=dw_mla, w_out=dw_out)
    host_dw = exchange_dw(dws, hosted) if exchange_dw else None
    dw_in, *hosted_dw = _as_list(
        matmul(dproj, sv["h"], ta=True, out_split=True, tn=2048, name=nm("mm_dw_in"), host=host_dw)
    )
    dws["w_in"] = dw_in
    host_dh = exchange_dh(dws) if exchange_dh else None
    dh, *hosted_dh = _as_list(
        matmul(dproj, w["in"], b_split=True, out_dtype=F32, name=nm("mm_dh"), tn=2048, tk=1920, host=host_dh)
    )

    def b_norm(dh, x, dx_res, g, scale):
        r = _rms(x)
        xn = x * r
        dxn = dh * g * (1.0 + scale)
        dx = dx_res + r * (dxn - xn * jnp.mean(dxn * xn, axis=1, keepdims=True))
        return [dx], [_sum0(dh), _sum0(dh * xn * g), _sum0(dh * (1.0 + scale) * xn)]

    dx, d_shift, d_scale, dg_norm = rowwise(
        b_norm, [dh, sv["x"], dx_out], [g_norm, scale], [(dm.D, F32)], [dm.D] * 3, tm=256, name=nm("norm_bwd")
    )
    dvec = dict(mod=jnp.concatenate([d_shift, d_scale, d_gate], axis=1), g_norm=dg_norm, g_cq=dg_cq, g_ckv=dg_ckv)
    return dx, dws, dvec, hosted, hosted_dh, hosted_dw


def adamw(w, g, m, v, name):
    shape = w.shape
    cols = shape[-1]
    view = lambda a: a.reshape(-1, cols)

    def f(w, g, m, v):
        m = ADAM_B1 * m + (1.0 - ADAM_B1) * g
        v = ADAM_B2 * v + (1.0 - ADAM_B2) * (g * g)
        m_hat = m / (1.0 - ADAM_B1**ADAM_STEP)
        v_hat = v / (1.0 - ADAM_B2**ADAM_STEP)
        delta = -ADAM_LR * (m_hat / (jnp.sqrt(v_hat) + ADAM_EPS) + ADAM_WD * w)
        return [delta, m, v], []

    tm = max(8, min(512, (400_000 // cols) // 8 * 8))
    delta, m, v = rowwise(f, [view(w), view(g), view(m), view(v)], [], [(cols, F32)] * 3, tm=tm, name=name)
    return delta.reshape(shape), m.reshape(shape), v.reshape(shape)


def _add_rows(fn, rows, cols, dtype, name):
    tm = max(8, min(512, (400_000 // cols) // 8 * 8))
    return rowwise(lambda *a: ([fn(*a)], []), rows, [], [(cols, dtype)], tm=tm, name=name)[0]


BIG = ("w_in", "w_uq", "w_ukv", "w_ret_proj", "w_mla_proj", "w_out")
COL_SHARDED = ("w_in", "w_uq", "w_ukv")


def kernel(x, c, positions, w_mod, b_mod, g_norm, w_in, g_cq, g_ckv, w_uq, w_ukv, w_ret_proj, w_mla_proj, w_out, g_final, loss_target, m_w_mod, m_b_mod, m_g_norm, m_w_in, m_g_cq, m_g_ckv, m_w_uq, m_w_ukv, m_w_ret_proj, m_w_mla_proj, m_w_out, m_g_final, v_w_mod, v_b_mod, v_g_norm, v_w_in, v_g_cq, v_g_ckv, v_w_uq, v_w_ukv, v_w_ret_proj, v_w_mla_proj, v_w_out, v_g_final):
    weights = dict(w_mod=w_mod, b_mod=b_mod, g_norm=g_norm, w_in=w_in, g_cq=g_cq, g_ckv=g_ckv, w_uq=w_uq, w_ukv=w_ukv,
                   w_ret_proj=w_ret_proj, w_mla_proj=w_mla_proj, w_out=w_out, g_final=g_final)
    m_in = dict(w_mod=m_w_mod, b_mod=m_b_mod, g_norm=m_g_norm, w_in=m_w_in, g_cq=m_g_cq, g_ckv=m_g_ckv, w_uq=m_w_uq,
                w_ukv=m_w_ukv, w_ret_proj=m_w_ret_proj, w_mla_proj=m_w_mla_proj, w_out=m_w_out, g_final=m_g_final)
    v_in = dict(w_mod=v_w_mod, b_mod=v_b_mod, g_norm=v_g_norm, w_in=v_w_in, g_cq=v_g_cq, g_ckv=v_g_ckv, w_uq=v_w_uq,
                w_ukv=v_w_ukv, w_ret_proj=v_w_ret_proj, w_mla_proj=v_w_mla_proj, w_out=v_w_out, g_final=v_g_final)
    order = ("w_mod", "b_mod", "g_norm", "w_in", "g_cq", "g_ckv", "w_uq", "w_ukv", "w_ret_proj", "w_mla_proj", "w_out", "g_final")

    x = x[0]
    target = loss_target[0]
    S, D = x.shape
    L = w_mod.shape[0]
    dm = Dims(S, D, L, w_ret_proj.shape[1] * 4 // RET_DV, w_mla_proj.shape[1] * 4 // MLA_DV, g_cq.shape[1], g_ckv.shape[1])
    my_x, my_y, my_c = _me()
    my_chip = 2 * my_x + my_y
    my_dev = 2 * my_chip + my_c
    C3 = w_mod.shape[2]

    for table in (weights, m_in, v_in):
        table["w_in"] = jnp.swapaxes(table["w_in"], 1, 2)
    prepped = [
        cast_split(weights["w_in"], BF16, "prep_w_in"),
        dm.pad_uq_rows(transpose_split(w_uq, BF16, "prep_w_uq")),
        transpose_split(w_ukv, BF16, "prep_w_ukv"),
        cast_split(w_ret_proj, BF16, "prep_w_ret"),
        cast_split(w_mla_proj, BF16, "prep_w_mla"),
        cast_split(w_out, BF16, "prep_w_out"),
    ]
    places = dm.placements()
    zero_rows = jnp.zeros((places[0].zero_rows[1], D // 2), BF16)
    w_keys = ("in", "uq", "ukv", "ret", "mla", "out")
    layer_w = [dict(zip(w_keys, gather_layer(prepped[:1], zero_rows, places[:1], 0, "l0_gather_w_in")))]
    rest_of_layer0 = gather_layer(prepped[1:], None, places[1:], 0, None)

    c_all = allgather8(c, "gather_c").reshape(8, D)
    (c_act,) = rowwise(lambda z: ([_silu(z)], []), [c_all], [], [(D, BF16)], tm=8, name="silu_c")
    mod_part = jnp.stack([matmul(c_act, w_mod[l], out_dtype=F32, name=f"l{l}_mm_mod", tn=C3) for l in range(L)])
    mod_all = allgather8(mod_part, "gather_mod")
    mod_all = mod_all.reshape(4, 2, L, 8, C3)[:, 0].transpose(1, 2, 0, 3).reshape(L, 8, 3 * D) + b_mod[:, None, :]
    mod = lax.dynamic_index_in_dim(mod_all, my_dev, axis=1, keepdims=False)

    pos = positions[0]
    tabs = (*_rope_tables(pos, RET_DK, LANE), *_rope_tables(pos, MLA_ROPE, LANE))

    def vecs(l):
        return (mod[l : l + 1, :D], mod[l : l + 1, D : 2 * D], mod[l : l + 1, 2 * D :],
                g_norm[l : l + 1], g_cq[l : l + 1], g_ckv[l : l + 1])

    saved = []
    for l in range(L):
        host = host_in = None
        if l == 0:
            host_in = rest_of_layer0
            host = gather_layer(prepped, zero_rows, places, 1, None) if L > 1 else None
        elif l + 1 < L:
            host_in = gather_layer(prepped[1:], None, places[1:], l + 1, None)
            host = gather_layer(prepped[:1], zero_rows, places[:1], l + 1, None)
        x, sv, hosted, late, layer_w[l] = layer_fwd(
            dm, l, x, *vecs(l), layer_w[l], tabs, host=host, host_in=host_in, own_late=(l == 0)
        )
        saved.append(sv)
        if host:
            layer_w.append(dict(zip(w_keys, [*hosted, *late])))

    def f_loss(x, t, g):
        xn = x * _rms(x)
        err = xn * g - t
        dy = err * (1.0 / D)
        dxn = dy * g
        dx = _rms(x) * (dxn - xn * jnp.mean(dxn * xn, axis=1, keepdims=True))
        part = jnp.sum(jnp.sum(err * err, axis=1, keepdims=True), axis=0, keepdims=True) * (0.5 / D)
        return [dx], [jnp.broadcast_to(part, (1, LANE)), _sum0(dy * xn)]

    dx, loss_part, dg_final = rowwise(f_loss, [x, target], [g_final.reshape(1, D)], [(D, F32)], [LANE, D], tm=256, name="loss_head")

    stacks = {n: None for n in BIG}
    dvec = {n: [None] * L for n in ("mod", "g_norm", "g_cq", "g_ckv")}

    def reduce_chips(l, from_chips):
        return [
            _add_rows(lambda a: ((a[3].astype(F32) + a[0].astype(F32)) + a[1].astype(F32)) + a[2].astype(F32), [r], r.shape[-1],
                      F32, f"l{l}_rs_add4_{n}")
            for n, r in zip(BIG, from_chips)
        ]

    def finish_grads(l, from_chips):
        mine = reduce_chips(l, from_chips)
        store_grads(l, mine, sibling_swap(mine, f"l{l}_rs_share"))

    def store_grads(l, mine, theirs):
        for n, a, b in zip(BIG, mine, theirs):
            if n == "w_uq":
                a, b = dm.unpad_uq_rows(a), dm.unpad_uq_rows(b)
            finish = untranspose_halves if n in ("w_uq", "w_ukv") else merge_halves
            stacks[n] = finish(a, b, my_c, stacks[n], l, L, f"l{l}_grad_{n}")

    def pair_sums(l, names, partial, from_sibling):
        return [add_pair(g, r, my_c, f"l{l}_rs_add2_{n}") for n, g, r in zip(names, partial, from_sibling)]

    def core_exchange(dws):
        return sibling_scatter([dws[n] for n in BIG], None)

    def last_exchange(names, which, tag):
        def build(dws, _=None):
            partial = [dws[n] for n in names]
            pairs = pair_sums(0, names, partial, sibling_scatter(partial, f"l0_rs_cores_{tag}"))
            return scatter_to_chips(pairs, which, None)

        return build

    pair = None
    for l in reversed(range(L)):
        host = scatter_to_chips(pair, places, None) if pair is not None else None
        halves = {}

        def share_halves(dws, from_chips, l=l, halves=halves):
            halves["mine"] = reduce_chips(l + 1, from_chips)
            return sibling_swap(halves["mine"], None)

        hooks = dict(exchange_dh=core_exchange, exchange_dw=share_halves if host else None)
        if l == 0:
            hooks = dict(exchange_dw=last_exchange(BIG[1:], places[1:], "rest"), exchange_dh=last_exchange(BIG[:1], places[:1], "in"))
        dx, dw_l, dv_l, hosted, hosted_dh, hosted_dw = layer_bwd(
            dm, l, dx, saved[l], *vecs(l), layer_w[l], tabs, host=host, **hooks
        )
        if host and l > 0:
            store_grads(l + 1, halves["mine"], hosted_dw)
        elif host:
            finish_grads(l + 1, hosted)
        for n in dvec:
            dvec[n][l] = dv_l[n]
        if l > 0:
            pair = pair_sums(l, BIG, [dw_l[n] for n in BIG], hosted_dh)
        else:
            finish_grads(0, [*hosted_dh, *hosted_dw])
    grad_x = dx[None]

    pieces = [loss_part] + [jnp.concatenate(dvec[n], axis=1) for n in ("mod", "g_norm", "g_cq", "g_ckv")] + [dg_final]
    widths = [p.shape[1] for p in pieces]
    small_all = allgather8(jnp.concatenate(pieces, axis=1), "gather_small").reshape(8, sum(widths))

    def sum8_body(a_ref, o_ref):
        acc = a_ref[0:1, :]
        for d in range(1, 8):
            acc = acc + a_ref[d : d + 1, :]
        o_ref[...] = acc

    small = _pcall(sum8_body, name="sum_small", out_shape=jax.ShapeDtypeStruct((1, sum(widths)), F32))(small_all)
    offs = np.cumsum([0] + widths)
    loss = small[0, 0]
    g_small = {
        "b_mod": small[0, offs[1] : offs[2]].reshape(L, 3 * D),
        "g_norm": small[0, offs[2] : offs[3]].reshape(L, D),
        "g_cq": small[0, offs[3] : offs[4]].reshape(L, dm.QR),
        "g_ckv": small[0, offs[4] : offs[5]].reshape(L, dm.KR),
        "g_final": small[0, offs[5] : offs[6]],
    }

    dmod_all = small_all[:, offs[1] : offs[2]].reshape(8, L, 3 * D)
    dmod_mine = lax.dynamic_slice_in_dim(dmod_all, my_chip * C3, C3, axis=2)
    pad8 = lambda a: jnp.pad(a, ((0, LANE - 8), (0, 0)))
    grads = dict(g_small)
    grads["w_mod"] = jnp.stack(
        [matmul(pad8(c_act), pad8(dmod_mine[:, l]), ta=True, out_dtype=F32, name=f"l{l}_mm_dw_mod", tn=C3) for l in range(L)]
    )

    for n in BIG:
        grads[n] = stacks[n]

    deltas, new_m, new_v = {}, {}, {}
    for n in order:
        wv, gv, mv, vv = weights[n], grads[n], m_in[n], v_in[n]
        if wv.ndim == 1:
            wv, gv, mv, vv = (a.reshape(1, -1) for a in (wv, gv, mv, vv))
        d_, m_, v_ = adamw(wv, gv, mv, vv, f"adamw_{n}")
        deltas[n], new_m[n], new_v[n] = (a.reshape(weights[n].shape) for a in (d_, m_, v_))
        grads[n] = grads[n].reshape(weights[n].shape)
    for table in (grads, deltas, new_m, new_v):
        table["w_in"] = jnp.swapaxes(table["w_in"], 1, 2)

    return (loss, grad_x, *[grads[n] for n in order], *[deltas[n] for n in order], *[new_m[n] for n in order],
            *[new_v[n] for n in order])
```

```python
import functools

import jax
import jax.numpy as jnp
import numpy as np
from jax import lax
from jax.experimental import pallas as pl
from jax.experimental.pallas import tpu as pltpu

F32 = jnp.float32
BF16 = jnp.bfloat16
MESH = pl.DeviceIdType.MESH

VMEM_LIMIT_BYTES = 52 * 1024 * 1024
LANE = 128

CHUNK = 64
EPS = 1e-6
NEG_INF = -1e30
ROPE_BASE = 10000.0
RET_DK = 128
RET_DV = 256
MLA_NOPE = 128
MLA_ROPE = 64
MLA_DV = 128
MLA_QW = 256
MLA_SCALE = float((MLA_NOPE + MLA_ROPE) ** -0.5)
LOG2E = float(np.log2(np.e))
LN2 = float(np.log(2.0))
ATTN_BLOCK = 512
ATTN_FWD_HEADS_PER_STEP = 8
ATTN_HEADS_PER_STEP = 2
RET_BLOCK = 256
RET_HEADS_PER_STEP = 8

ADAM_LR = 0.001
ADAM_B1 = 0.9
ADAM_B2 = 0.999
ADAM_EPS = 1e-08
ADAM_WD = 0.01
ADAM_STEP = 10


def _pcall(body, **kw):
    return pl.pallas_call(body, **kw)


def _params(n_grid):
    return pltpu.CompilerParams(dimension_semantics=("arbitrary",) * n_grid, vmem_limit_bytes=VMEM_LIMIT_BYTES)


def _pick(dim, target, mult=LANE):
    if dim <= target:
        return dim
    best = None
    for t in range(mult, target + 1, mult):
        if dim % t == 0:
            best = t
    assert best is not None, (dim, target, mult)
    return best


def matmul(
    a, b, *, ta=False, tb=False, b_split=False, out_split=False, out_dtype=BF16, name, tm=512, tn=1024, tk=2048, host=None
):
    (M, K) = (a.shape[1], a.shape[0]) if ta else a.shape
    b_rows, b_cols = (b.shape[1], 2 * b.shape[2]) if b_split else b.shape
    (K2, N) = (b_cols, b_rows) if tb else (b_rows, b_cols)
    assert K == K2, (a.shape, b.shape, ta, tb)
    whole_k = b_split and tb and tk >= K and not ta
    whole_n = out_split and tn >= N and not (b_split and not tb)
    whole_bn = b_split and not tb and tn >= N and not out_split
    n_cap = N if (whole_n or whole_bn) else (N // 2 if (out_split or (b_split and not tb)) else N)
    k_cap = K if whole_k else (K // 2 if (b_split and tb) else K)
    tm, tn, tk = _pick(M, tm, 8 if M < LANE else LANE), _pick(n_cap, min(tn, n_cap)), _pick(k_cap, min(tk, k_cap))
    nk = K // tk
    njh, nkh = max((N // 2) // tn, 1), max((K // 2) // tk, 1)
    dn = (((0 if ta else 1,), (1 if tb else 0,)), ((), ()))

    def body(a_ref, b_ref, o_ref, *scratch):
        if whole_k:
            a_blk = a_ref[...].astype(BF16)
            prod = lax.dot_general(a_blk[:, : K // 2], b_ref[0].astype(BF16), dn, preferred_element_type=F32)
            prod += lax.dot_general(a_blk[:, K // 2 :], b_ref[1].astype(BF16), dn, preferred_element_type=F32)
        elif whole_bn:
            a_blk = a_ref[...].astype(BF16)
            halves = [lax.dot_general(a_blk, b_ref[half].astype(BF16), dn, preferred_element_type=F32) for half in range(2)]
            prod = jnp.concatenate(halves, axis=1)
        else:
            b_blk = b_ref[0] if b_split else b_ref[...]
            prod = lax.dot_general(a_ref[...].astype(BF16), b_blk.astype(BF16), dn, preferred_element_type=F32)

        def store(v):
            if whole_n:
                o_ref[0] = v[:, : N // 2].astype(o_ref.dtype)
                o_ref[1] = v[:, N // 2 :].astype(o_ref.dtype)
            elif out_split:
                o_ref[0] = v.astype(o_ref.dtype)
            else:
                o_ref[...] = v.astype(o_ref.dtype)

        if nk == 1:
            store(prod)
            return
        (acc_ref,) = scratch
        k = pl.program_id(2)

        @pl.when(k == 0)
        def _():
            acc_ref[...] = prod

        @pl.when(k > 0)
        def _():
            acc_ref[...] += prod

        @pl.when(k == nk - 1)
        def _():
            store(acc_ref[...])

    a_spec = pl.BlockSpec((tk, tm), lambda i, j, k: (k, i)) if ta else pl.BlockSpec((tm, tk), lambda i, j, k: (i, k))
    if whole_k:
        b_spec = pl.BlockSpec((2, tn, K // 2), lambda i, j, k: (0, j, 0))
    elif whole_bn:
        b_spec = pl.BlockSpec((2, tk, N // 2), lambda i, j, k: (0, k, 0))
    elif b_split and tb:
        b_spec = pl.BlockSpec((1, tn, tk), lambda i, j, k: (k // nkh, j, k % nkh))
    elif b_split:
        b_spec = pl.BlockSpec((1, tk, tn), lambda i, j, k: (j // njh, k, j % njh))
    elif tb:
        b_spec = pl.BlockSpec((tn, tk), lambda i, j, k: (j, k))
    else:
        b_spec = pl.BlockSpec((tk, tn), lambda i, j, k: (k, j))
    if whole_n:
        out_spec = pl.BlockSpec((2, tm, N // 2), lambda i, j, k: (0, i, 0))
        out_shape = jax.ShapeDtypeStruct((2, M, N // 2), out_dtype)
    elif out_split:
        out_spec = pl.BlockSpec((1, tm, tn), lambda i, j, k: (j // njh, i, j % njh))
        out_shape = jax.ShapeDtypeStruct((2, M, N // 2), out_dtype)
    else:
        out_spec = pl.BlockSpec((tm, tn), lambda i, j, k: (i, j))
        out_shape = jax.ShapeDtypeStruct((M, N), out_dtype)
    grid = (M // tm, N // tn, nk)
    acc_shapes = [] if nk == 1 else [pltpu.VMEM((tm, tn), F32)]
    if host is None:
        return _pcall(
            body, name=name, grid=grid, in_specs=[a_spec, b_spec], out_specs=out_spec, out_shape=out_shape,
            scratch_shapes=acc_shapes, compiler_params=_params(3),
        )(a, b)

    n_hi, n_ho, n_steps = len(host.ins), len(host.out_shapes), grid[0] * grid[1] * grid[2]

    def hosting_body(a_ref, b_ref, *rest):
        host_in, rest = rest[:n_hi], rest[n_hi:]
        o_ref, rest = rest[0], rest[1:]
        host_out, rest = rest[:n_ho], rest[n_ho:]
        scratch, host_sems = rest[: len(acc_shapes)], rest[len(acc_shapes) :]
        step = (pl.program_id(0) * grid[1] + pl.program_id(1)) * grid[2] + pl.program_id(2)
        refs = (host_in, host_out, host_sems)

        @pl.when(step == 0)
        def _():
            host.start(*refs)

        @pl.when(step == n_steps // 2)
        def _():
            host.relay(*refs)

        body(a_ref, b_ref, o_ref, *scratch)

        @pl.when(step == n_steps - 1)
        def _():
            host.finish(*refs)

    return _pcall(
        hosting_body,
        name=name,
        grid=grid,
        in_specs=[a_spec, b_spec, *host.in_specs],
        out_specs=[out_spec, *host.out_specs],
        out_shape=[out_shape, *host.out_shapes],
        scratch_shapes=[*acc_shapes, *host.scratch],
        compiler_params=_params(3),
    )(a, b, *host.ins)


def win(arr, off, width):
    assert off % width == 0 and off + width <= arr.shape[1], (arr.shape, off, width)
    return (arr, off // width, width)


def rowwise(fn, rows, vecs, outs, accs=(), *, tm, name, into=None):
    rows = [r if isinstance(r, tuple) else (r, 0, r.shape[-1]) for r in rows]
    S = rows[0][0].shape[-2]
    tm = _pick(S, tm, 8)
    n_rows, n_vecs, n_outs = len(rows), len(vecs), len(outs)
    aliased = into is not None and not isinstance(into[0], int)
    n_alias = 1 if aliased else 0

    def body(*refs):
        ins = [r[...] for r in refs[: n_rows + n_vecs]]
        refs = refs[: n_rows + n_vecs] + refs[n_rows + n_vecs + n_alias :]
        out_refs = refs[n_rows + n_vecs : n_rows + n_vecs + n_outs]
        acc_refs = refs[n_rows + n_vecs + n_outs :]
        res, acc = fn(*ins)
        for r, v in zip(out_refs, res, strict=True):
            r[...] = v.astype(r.dtype)
        if acc_refs:
            i = pl.program_id(0)

            @pl.when(i == 0)
            def _():
                for r, v in zip(acc_refs, acc, strict=True):
                    r[...] = v

            @pl.when(i > 0)
            def _():
                for r, v in zip(acc_refs, acc, strict=True):
                    r[...] += v

    in_specs = []
    for arr, blk, w in rows:
        if arr.ndim == 3:
            in_specs.append(pl.BlockSpec((arr.shape[0], tm, w), lambda i: (0, i, 0)))
        else:
            in_specs.append(pl.BlockSpec((tm, w), functools.partial(lambda i, blk: (i, blk), blk=blk)))
    for v in vecs:
        in_specs.append(pl.BlockSpec(v.shape, functools.partial(lambda i, nd: (0,) * nd, nd=v.ndim)))
    out_specs = [pl.BlockSpec((tm, w), lambda i: (i, 0)) for w, _ in outs]
    out_specs += [pl.BlockSpec((1, w), lambda i: (0, 0)) for w in accs]
    out_shape = [jax.ShapeDtypeStruct((S, w), dt) for w, dt in outs]
    out_shape += [jax.ShapeDtypeStruct((1, w), F32) for w in accs]
    operands = [*[r[0] for r in rows], *vecs]
    aliases = {}
    if into is not None:
        target, off = into
        w0, dt0 = outs[0]
        total = target if isinstance(target, int) else target.shape[1]
        assert off % w0 == 0 and off + w0 <= total
        out_specs[0] = pl.BlockSpec((tm, w0), functools.partial(lambda i, blk: (i, blk), blk=off // w0))
        out_shape[0] = jax.ShapeDtypeStruct((S, total), dt0)
        if aliased:
            in_specs.append(pl.BlockSpec(memory_space=pl.ANY))
            operands.append(target)
            aliases = {len(operands) - 1: 0}
    res = _pcall(
        body,
        name=name,
        grid=(S // tm,),
        in_specs=in_specs,
        out_specs=out_specs,
        out_shape=out_shape,
        input_output_aliases=aliases,
        compiler_params=_params(1),
    )(*operands)
    return res


def _sum0(v):
    return jnp.sum(v, axis=0, keepdims=True)


def _sigmoid(z):
    return 1.0 / (1.0 + jnp.exp(-z))


def _silu(z):
    return z * _sigmoid(z)


def _dsilu(z):
    s = _sigmoid(z)
    return s * (1.0 + z * (1.0 - s))


def _swap_half(x, half):
    if 2 * half == LANE:
        return pltpu.roll(x, half, 1)
    lane = lax.broadcasted_iota(jnp.int32, x.shape, 1)
    return jnp.where((lane % (2 * half)) < half, pltpu.roll(x, LANE - half, 1), pltpu.roll(x, half, 1))


def _rope_tile(x, cosf, sinf, half, inverse=False):
    sw = _swap_half(x, half)
    return x * cosf - sw * sinf if inverse else x * cosf + sw * sinf


def _tri_tables(n, by_key):
    if by_key:
        pairs = [(i, j) for j in range(n) for i in range(j, n)]
    else:
        pairs = [(i, j) for i in range(n) for j in range(i + 1)]
    return (np.array([p[0] for p in pairs], np.int32), np.array([p[1] for p in pairs], np.int32))


def _attn_scores(q, kn, kr, i, j, tq, masked):
    k = jnp.concatenate([kn, kr], axis=1)
    s = lax.dot_general(q, k, (((1,), (1,)), ((), ())), preferred_element_type=F32)
    if masked:
        rc = (i * tq + lax.broadcasted_iota(jnp.int32, s.shape, 0)) // CHUNK
        cc = (j * tq + lax.broadcasted_iota(jnp.int32, s.shape, 1)) // CHUNK
        s = jnp.where(cc <= rc, s, NEG_INF)
    return s, k


def attn_fwd(q, kv, kr, proj, mg_off, *, heads, tq, name, host=None):
    S = q.shape[0]
    tq = _pick(S, tq, CHUNK)
    n = S // tq
    it, jt = _tri_tables(n, by_key=False)
    T = len(it)
    mg_blk = mg_off // MLA_DV
    n_hi, n_ho = (len(host.ins), len(host.out_shapes)) if host else (0, 0)

    hp = min(ATTN_FWD_HEADS_PER_STEP, heads)
    assert heads % hp == 0 and mg_blk % hp == 0
    groups = heads // hp

    def body(it_ref, jt_ref, q_ref, kv_ref, kr_ref, mg_ref, *rest):
        host_in, rest = rest[:n_hi], rest[n_hi:]
        (o_ref, a_ref, lse_ref), rest = rest[:3], rest[3:]
        host_out, rest = rest[:n_ho], rest[n_ho:]
        (m_sc, acc_sc), host_sems = rest[:2], rest[2:]
        h, t = pl.program_id(0), pl.program_id(1)
        i, j = it_ref[t], jt_ref[t]
        if host:
            _host_steps(host, (host_in, host_out, host_sems), h, t, groups, T, before=True)

        @pl.when(j == 0)
        def _():
            m_sc[...] = jnp.full(m_sc.shape, NEG_INF, F32)
            acc_sc[...] = jnp.zeros(acc_sc.shape, F32)

        def step(masked):
            kr = kr_ref[...]
            for u in range(hp):
                q = q_ref[:, u * MLA_QW : (u + 1) * MLA_QW]
                kn = kv_ref[:, u * MLA_QW : u * MLA_QW + MLA_NOPE]
                v = kv_ref[:, u * MLA_QW + MLA_NOPE : (u + 1) * MLA_QW]
                s, _ = _attn_scores(q, kn, kr, i, j, tq, masked)
                m_prev = m_sc[u]
                m_new = jnp.maximum(m_prev, jnp.max(s, axis=1, keepdims=True))
                p = jnp.exp2((s - m_new).astype(BF16))
                alpha = jnp.exp2(m_prev - m_new)
                v_ones = jnp.concatenate([v, jnp.ones(v.shape, BF16)], axis=1)
                acc_sc[u] = alpha * acc_sc[u] + jnp.dot(p, v_ones, preferred_element_type=F32)
                m_sc[u] = m_new

        @pl.when(j < i)
        def _():
            step(False)

        @pl.when(j == i)
        def _():
            step(True)
            for u in range(hp):
                cols = slice(u * MLA_DV, (u + 1) * MLA_DV)
                l = acc_sc[u, :, MLA_DV:]
                o = acc_sc[u, :, :MLA_DV] / l
                o_ref[:, cols] = o
                a_ref[:, cols] = (o * _silu(mg_ref[:, cols].astype(F32))).astype(a_ref.dtype)
                lse_ref[u] = m_sc[u] + jnp.log2(l[:, :1])

        if host:
            _host_steps(host, (host_in, host_out, host_sems), h, t, groups, T, before=False)

    grid_spec = pltpu.PrefetchScalarGridSpec(
        num_scalar_prefetch=2,
        grid=(groups, T),
        in_specs=[
            pl.BlockSpec((tq, hp * MLA_QW), lambda h, t, it, jt: (it[t], h)),
            pl.BlockSpec((tq, hp * MLA_QW), lambda h, t, it, jt: (jt[t], h)),
            pl.BlockSpec((tq, LANE), lambda h, t, it, jt: (jt[t], 0)),
            pl.BlockSpec((tq, hp * MLA_DV), lambda h, t, it, jt: (it[t], mg_blk // hp + h)),
            *(host.in_specs if host else []),
        ],
        out_specs=[
            pl.BlockSpec((tq, hp * MLA_DV), lambda h, t, it, jt: (it[t], h)),
            pl.BlockSpec((tq, hp * MLA_DV), lambda h, t, it, jt: (it[t], h)),
            pl.BlockSpec((hp, tq, 1), lambda h, t, it, jt: (h, it[t], 0)),
            *(host.out_specs if host else []),
        ],
        scratch_shapes=[
            pltpu.VMEM((hp, tq, 1), F32),
            pltpu.VMEM((hp, tq, 2 * MLA_DV), F32),
            *(host.scratch if host else []),
        ],
    )
    return _pcall(
        body,
        name=name,
        grid_spec=grid_spec,
        out_shape=[
            jax.ShapeDtypeStruct((S, heads * MLA_DV), F32),
            jax.ShapeDtypeStruct((S, heads * MLA_DV), BF16),
            jax.ShapeDtypeStruct((heads, S, 1), F32),
            *(host.out_shapes if host else []),
        ],
        compiler_params=_params(2),
    )(jnp.asarray(it), jnp.asarray(jt), q, kv, kr, proj, *(host.ins if host else []))


def attn_bwd(q, kv, kr, do, o, lse, *, heads, tq, name, host=None):
    S = q.shape[0]
    tq = _pick(S, tq, CHUNK)
    n = S // tq
    it, jt = _tri_tables(n, by_key=True)
    T = len(it)
    n_hi, n_ho = (len(host.ins), len(host.out_shapes)) if host else (0, 0)

    hp = ATTN_HEADS_PER_STEP
    assert heads % hp == 0
    groups = heads // hp

    def body(it_ref, jt_ref, q_ref, kv_ref, kr_ref, do_ref, o_ref, lse_ref, *rest):
        host_in, rest = rest[:n_hi], rest[n_hi:]
        (dq_ref, dkv_ref, dkr_ref), rest = rest[:3], rest[3:]
        host_out, rest = rest[:n_ho], rest[n_ho:]
        (dq_acc, dk_sc, dv_sc), host_sems = rest[:3], rest[3:]
        h, t = pl.program_id(0), pl.program_id(1)
        i, j = it_ref[t], jt_ref[t]
        if host:
            _host_steps(host, (host_in, host_out, host_sems), h, t, groups, T, before=True)

        @pl.when(t == 0)
        def _():
            dq_acc[...] = jnp.zeros(dq_acc.shape, F32)

        @pl.when(i == j)
        def _():
            dk_sc[...] = jnp.zeros(dk_sc.shape, F32)
            dv_sc[...] = jnp.zeros(dv_sc.shape, F32)

        def step(masked):
            kr = kr_ref[...]
            rows = pl.ds(pl.multiple_of(i * tq, tq), tq)
            for u in range(hp):
                wide, narrow = slice(u * MLA_QW, (u + 1) * MLA_QW), slice(u * MLA_DV, (u + 1) * MLA_DV)
                q_blk, do_blk = q_ref[:, wide], do_ref[:, narrow]
                kn = kv_ref[:, u * MLA_QW : u * MLA_QW + MLA_NOPE]
                v = kv_ref[:, u * MLA_QW + MLA_NOPE : (u + 1) * MLA_QW]
                s, k = _attn_scores(q_blk, kn, kr, i, j, tq, masked)
                p = jnp.exp2((s - lse_ref[u]).astype(BF16))
                delta = jnp.sum(do_blk.astype(F32) * o_ref[:, narrow], axis=1, keepdims=True)
                dp = lax.dot_general(do_blk, v, _NT, preferred_element_type=F32)
                ds = p * (dp - delta).astype(BF16)
                dv_sc[u] += lax.dot_general(p, do_blk, _TN, preferred_element_type=F32)
                dk_sc[u] += lax.dot_general(ds, q_blk, _TN, preferred_element_type=F32)
                dq_acc[rows, wide] += jnp.dot(ds, k, preferred_element_type=F32)

        @pl.when(i == j)
        def _():
            step(True)

        @pl.when(i > j)
        def _():
            step(False)

        @pl.when(i == n - 1)
        def _():
            for u in range(hp):
                dkv_ref[:, u * MLA_QW : u * MLA_QW + MLA_NOPE] = (dk_sc[u, :, :MLA_NOPE] * LN2).astype(dkv_ref.dtype)
                dkv_ref[:, u * MLA_QW + MLA_NOPE : (u + 1) * MLA_QW] = dv_sc[u].astype(dkv_ref.dtype)
                dkr_ref[u] = dk_sc[u, :, MLA_NOPE:] * LN2

        @pl.when(t == T - 1)
        def _():
            dq_ref[...] = (dq_acc[...] * MLA_SCALE).astype(dq_ref.dtype)

        if host:
            _host_steps(host, (host_in, host_out, host_sems), h, t, groups, T, before=False)

    grid_spec = pltpu.PrefetchScalarGridSpec(
        num_scalar_prefetch=2,
        grid=(groups, T),
        in_specs=[
            pl.BlockSpec((tq, hp * MLA_QW), lambda h, t, it, jt: (it[t], h)),
            pl.BlockSpec((tq, hp * MLA_QW), lambda h, t, it, jt: (jt[t], h)),
            pl.BlockSpec((tq, LANE), lambda h, t, it, jt: (jt[t], 0)),
            pl.BlockSpec((tq, hp * MLA_DV), lambda h, t, it, jt: (it[t], h)),
            pl.BlockSpec((tq, hp * MLA_DV), lambda h, t, it, jt: (it[t], h)),
            pl.BlockSpec((hp, tq, 1), lambda h, t, it, jt: (h, it[t], 0)),
            *(host.in_specs if host else []),
        ],
        out_specs=[
            pl.BlockSpec((S, hp * MLA_QW), lambda h, t, it, jt: (0, h)),
            pl.BlockSpec((tq, hp * MLA_QW), lambda h, t, it, jt: (jt[t], h)),
            pl.BlockSpec((hp, tq, LANE), lambda h, t, it, jt: (h, jt[t], 0)),
            *(host.out_specs if host else []),
        ],
        scratch_shapes=[
            pltpu.VMEM((S, hp * MLA_QW), F32),
            pltpu.VMEM((hp, tq, MLA_QW), F32),
            pltpu.VMEM((hp, tq, MLA_DV), F32),
            *(host.scratch if host else []),
        ],
    )
    return _pcall(
        body,
        name=name,
        grid_spec=grid_spec,
        out_shape=[
            jax.ShapeDtypeStruct((S, heads * MLA_QW), BF16),
            jax.ShapeDtypeStruct((S, heads * (MLA_NOPE + MLA_DV)), BF16),
            jax.ShapeDtypeStruct((heads, S, LANE), F32),
            *(host.out_shapes if host else []),
        ],
        compiler_params=_params(2),
    )(jnp.asarray(it), jnp.asarray(jt), q, kv, kr, do, o, lse, *(host.ins if host else []))


def _host_steps(host, refs, h, t, heads, n_steps, before):
    if before:

        @pl.when((h == 0) & (t == 0))
        def _():
            host.start(*refs)

        @pl.when((h == heads // 2) & (t == n_steps // 2))
        def _():
            host.relay(*refs)

    else:

        @pl.when((h == heads - 1) & (t == n_steps - 1))
        def _():
            host.finish(*refs)


def _ret_consts(heads):
    h = np.arange(heads, dtype=np.float32)
    lg = np.log(np.float32(1.0) - np.float32(2.0) ** (np.float32(-5.0) - h)).astype(np.float32)
    idx = np.arange(CHUNK, dtype=np.float32)
    dmat = np.exp(np.abs(idx[:, None] - idx[None, :])[None] * lg[:, None, None]).astype(np.float32)
    xi = np.exp((idx + 1.0)[None, :] * lg[:, None]).astype(np.float32)
    zeta = np.exp((CHUNK - 1.0 - idx)[None, :] * lg[:, None]).astype(np.float32)
    dec = np.exp(np.float32(CHUNK) * lg).astype(np.float32)
    xi = np.broadcast_to(xi[:, :, None], (heads, CHUNK, RET_DK)).copy()
    zeta = np.broadcast_to(zeta[:, :, None], (heads, CHUNK, RET_DK)).copy()
    dec = np.broadcast_to(dec[:, None, None], (heads, 8, LANE)).copy()
    return jnp.asarray(dmat), jnp.asarray(xi), jnp.asarray(zeta), jnp.asarray(dec)


_NT = (((1,), (1,)), ((), ()))
_TN = (((0,), (0,)), ((), ()))


def _dot(a, b, dn=(((1,), (0,)), ((), ()))):
    return lax.dot_general(a.astype(BF16), b.astype(BF16), dn, preferred_element_type=F32)


def _const_specs(hp):
    return [
        pl.BlockSpec((hp, CHUNK, CHUNK), lambda h, b: (h, 0, 0)),
        pl.BlockSpec((hp, CHUNK, RET_DK), lambda h, b: (h, 0, 0)),
        pl.BlockSpec((hp, CHUNK, RET_DK), lambda h, b: (h, 0, 0)),
        pl.BlockSpec((hp, 8, LANE), lambda h, b: (h, 0, 0)),
    ]


def _ret_group(heads, *blks):
    hp = min(RET_HEADS_PER_STEP, heads)
    assert heads % hp == 0 and all(b % hp == 0 for b in blks)
    return hp


def ret_fwd(q, k, proj, v_off, rg_off, *, heads, tb, name):
    S = q.shape[0]
    tb = _pick(S, tb, CHUNK)
    cpb, nb = tb // CHUNK, S // tb
    v_blk, rg_blk = v_off // RET_DV, rg_off // RET_DV
    hp = _ret_group(heads, v_blk, rg_blk)

    def body(q_ref, k_ref, v_ref, rg_ref, dm_ref, xi_ref, ze_ref, dec_ref, o_ref, a_ref, st_ref, r_sc):
        @pl.when(pl.program_id(1) == 0)
        def _():
            r_sc[...] = jnp.zeros(r_sc.shape, F32)

        for c in range(cpb):
            sl = pl.ds(c * CHUNK, CHUNK)
            for u in range(hp):
                dm, xi, ze, dec = dm_ref[u], xi_ref[u], ze_ref[u], dec_ref[u, 0:1, 0:1]
                narrow, wide = slice(u * RET_DK, (u + 1) * RET_DK), slice(u * RET_DV, (u + 1) * RET_DV)
                qc, kc, vc = q_ref[sl, narrow], k_ref[sl, narrow], v_ref[sl, wide]
                r = r_sc[u]
                st_ref[u, c] = r.astype(BF16)
                s = _dot(qc, kc, _NT) * dm
                o = _dot(s, vc) + _dot(qc.astype(F32) * xi, r)
                r_sc[u] = r * dec + _dot(kc.astype(F32) * ze, vc, _TN)
                mu = jnp.mean(o, axis=1, keepdims=True)
                d = o - mu
                n = d * lax.rsqrt(jnp.mean(d * d, axis=1, keepdims=True) + EPS)
                o_ref[sl, wide] = o
                a_ref[sl, wide] = (n * _silu(rg_ref[sl, wide].astype(F32))).astype(a_ref.dtype)

    return _pcall(
        body,
        name=name,
        grid=(heads // hp, nb),
        in_specs=[
            pl.BlockSpec((tb, hp * RET_DK), lambda h, b: (b, h)),
            pl.BlockSpec((tb, hp * RET_DK), lambda h, b: (b, h)),
            pl.BlockSpec((tb, hp * RET_DV), lambda h, b: (b, v_blk // hp + h)),
            pl.BlockSpec((tb, hp * RET_DV), lambda h, b: (b, rg_blk // hp + h)),
            *_const_specs(hp),
        ],
        out_specs=[
            pl.BlockSpec((tb, hp * RET_DV), lambda h, b: (b, h)),
            pl.BlockSpec((tb, hp * RET_DV), lambda h, b: (b, h)),
            pl.BlockSpec((hp, cpb, RET_DK, RET_DV), lambda h, b: (h, b, 0, 0)),
        ],
        out_shape=[
            jax.ShapeDtypeStruct((S, heads * RET_DV), F32),
            jax.ShapeDtypeStruct((S, heads * RET_DV), BF16),
            jax.ShapeDtypeStruct((heads, S // CHUNK, RET_DK, RET_DV), BF16),
        ],
        scratch_shapes=[pltpu.VMEM((hp, RET_DK, RET_DV), F32)],
        compiler_params=_params(2),
    )(q, k, proj, proj, *_ret_consts(heads))


def ret_bwd(q, k, proj, v_off, states, do, *, heads, tb, name, into):
    S = q.shape[0]
    tb = _pick(S, tb, CHUNK)
    cpb, nb = tb // CHUNK, S // tb
    v_blk = v_off // RET_DV
    hp = _ret_group(heads, v_blk)
    target, dv_off = into
    dv_blk = dv_off // (hp * RET_DV)
    assert dv_off % (hp * RET_DV) == 0

    def body(q_ref, k_ref, v_ref, st_ref, do_ref, dm_ref, xi_ref, ze_ref, dec_ref, _, dq_ref, dk_ref, dv_ref, dr_sc):
        @pl.when(pl.program_id(1) == 0)
        def _():
            dr_sc[...] = jnp.zeros(dr_sc.shape, F32)

        for c in reversed(range(cpb)):
            sl = pl.ds(c * CHUNK, CHUNK)
            for u in range(hp):
                dm, xi, ze, dec = dm_ref[u], xi_ref[u], ze_ref[u], dec_ref[u, 0:1, 0:1]
                narrow, wide = slice(u * RET_DK, (u + 1) * RET_DK), slice(u * RET_DV, (u + 1) * RET_DV)
                qc, kc, vc, doc = q_ref[sl, narrow], k_ref[sl, narrow], v_ref[sl, wide], do_ref[sl, wide]
                r_prev = st_ref[u, c]
                dr = dr_sc[u]
                a = _dot(qc, kc, _NT) * dm
                ds = _dot(doc, vc, _NT) * dm
                kz = kc.astype(F32) * ze
                dq_ref[sl, narrow] = (_dot(ds, kc) + _dot(doc, r_prev, _NT) * xi).astype(dq_ref.dtype)
                dk_ref[sl, narrow] = (_dot(ds, qc, _TN) + _dot(vc, dr, _NT) * ze).astype(dk_ref.dtype)
                dv_ref[sl, wide] = (_dot(a, doc, _TN) + _dot(kz, dr)).astype(dv_ref.dtype)
                dr_sc[u] = dr * dec + _dot(qc.astype(F32) * xi, doc, _TN)

    rev = lambda h, b: (nb - 1 - b, h)
    return _pcall(
        body,
        name=name,
        grid=(heads // hp, nb),
        in_specs=[
            pl.BlockSpec((tb, hp * RET_DK), rev),
            pl.BlockSpec((tb, hp * RET_DK), rev),
            pl.BlockSpec((tb, hp * RET_DV), lambda h, b: (nb - 1 - b, v_blk // hp + h)),
            pl.BlockSpec((hp, cpb, RET_DK, RET_DV), lambda h, b: (h, nb - 1 - b, 0, 0)),
            pl.BlockSpec((tb, hp * RET_DV), rev),
            *_const_specs(hp),
            pl.BlockSpec(memory_space=pl.ANY),
        ],
        out_specs=[
            pl.BlockSpec((tb, hp * RET_DK), rev),
            pl.BlockSpec((tb, hp * RET_DK), rev),
            pl.BlockSpec((tb, hp * RET_DV), lambda h, b: (nb - 1 - b, dv_blk + h)),
        ],
        out_shape=[
            jax.ShapeDtypeStruct((S, heads * RET_DK), BF16),
            jax.ShapeDtypeStruct((S, heads * RET_DK), BF16),
            jax.ShapeDtypeStruct(target.shape, target.dtype),
        ],
        input_output_aliases={9: 2},
        scratch_shapes=[pltpu.VMEM((hp, RET_DK, RET_DV), F32)],
        compiler_params=_params(2),
    )(q, k, proj, states, do, *_ret_consts(heads), target)


def _me():
    return (lax.axis_index("x"), lax.axis_index("y"), lax.axis_index("c"))


def _comm(name, ins, out_shapes, n_local, n_remote, plan):
    return run_exchange(name, Exchange(ins, out_shapes, n_local, n_remote, plan))


def run_exchange(name, ex):
    def body(*refs):
        ex.start(*ex.split(refs))
        ex.relay(*ex.split(refs))
        ex.finish(*ex.split(refs))

    return _pcall(
        body, name=name, in_specs=ex.in_specs, out_specs=ex.out_specs, out_shape=ex.out_shapes, scratch_shapes=ex.scratch
    )(*ex.ins)


class Exchange:
    def __init__(self, ins, out_shapes, n_local, n_remote, plan):
        self.ins, self.out_shapes, self.n_local, self.n_remote, self.plan = list(ins), list(out_shapes), n_local, n_remote, plan
        any_spec = pl.BlockSpec(memory_space=pl.ANY)
        self.in_specs, self.out_specs = [any_spec] * len(self.ins), [any_spec] * len(self.out_shapes)
        self.scratch = [
            pltpu.SemaphoreType.DMA((n_remote,)),
            pltpu.SemaphoreType.DMA((n_remote,)),
            pltpu.SemaphoreType.DMA((max(n_local, 1),)),
        ]

    def split(self, refs):
        n_in, n_out = len(self.ins), len(self.out_shapes)
        return refs[:n_in], refs[n_in : n_in + n_out], refs[n_in + n_out :]

    def _copies(self, in_refs, out_refs, sems):
        send_sems, recv_sems, local_sems = sems
        me = _me()
        local_plan, remote_plan = self.plan(me, in_refs, out_refs)
        assert len(local_plan) == self.n_local and len(remote_plan) == self.n_remote, (len(local_plan), len(remote_plan))
        def local_copy(n):
            src, dst = local_plan[n]
            return pltpu.make_async_copy(src, dst, local_sems.at[n])

        def send(n):
            src, dst, peer, _, _ = remote_plan[n]
            return pltpu.make_async_remote_copy(
                src_ref=src, dst_ref=dst, send_sem=send_sems.at[n], recv_sem=recv_sems.at[n], device_id=peer, device_id_type=MESH
            )

        def arrival(n):
            src, _, _, landing, _ = remote_plan[n]
            return pltpu.make_async_remote_copy(
                src_ref=src, dst_ref=landing, send_sem=send_sems.at[n], recv_sem=recv_sems.at[n], device_id=me, device_id_type=MESH
            )

        after = [a for (_, _, _, _, a) in remote_plan]
        assert all(a is None or a < n for n, a in enumerate(after))
        return local_copy, send, arrival, after

    def start(self, in_refs, out_refs, sems):
        local_copy, send, _, after = self._copies(in_refs, out_refs, sems)
        for n in range(self.n_local):
            local_copy(n).start()
        for n, a in enumerate(after):
            if a is None:
                send(n).start()

    def relay(self, in_refs, out_refs, sems):
        _, send, arrival, after = self._copies(in_refs, out_refs, sems)
        for n in sorted({a for a in after if a is not None}):
            arrival(n).wait_recv()
            for m, a in enumerate(after):
                if a == n:
                    send(m).start()

    def finish(self, in_refs, out_refs, sems):
        local_copy, send, arrival, after = self._copies(in_refs, out_refs, sems)
        relayed = {a for a in after if a is not None}
        for n in range(self.n_remote):
            if n not in relayed:
                arrival(n).wait_recv()
        for n in range(self.n_remote):
            send(n).wait_send()
        for n in range(self.n_local):
            local_copy(n).wait()


_CHIP_FLIPS = ((1, 0), (0, 1), (1, 1))


class Placement:
    def __init__(self, rows, full_rows, cuts=(), offset=None, zero_rows=None):
        edges = [0, *sorted(cuts), rows]
        self.rows, self.full_rows = rows, full_rows
        self.runs = [(a, b - a) for a, b in zip(edges[:-1], edges[1:]) if b > a]
        self.offset = offset if offset is not None else (lambda j, start: j * rows + start)
        self.zero_rows = zero_rows

    def at(self, ref_2d_plus, j, start, size):
        return pl.ds(pl.multiple_of(self.offset(j, start), 16), size)


def gather_layer(prepped, zeros, places, l, name):
    n_w = len(prepped)
    n_chip = sum(len(p.runs) for p in places) * 3
    n_own = sum(len(p.runs) for p in places) * 2 + sum(2 for p in places if p.zero_rows)

    def plan(me, in_refs, out_refs):
        x, y, c = me
        j = 2 * x + y
        sib = (x, y, 1 - c)
        from_chips, own, forwards = [], [], []
        for i, o, p in zip(in_refs[:n_w], out_refs, places):
            for start, size in p.runs:
                for dx, dy in _CHIP_FLIPS:
                    px, py = x ^ dx, y ^ dy
                    jp = 2 * px + py
                    mine_there = o.at[c, p.at(o, j, start, size)]
                    theirs_here = o.at[c, p.at(o, jp, start, size)]
                    n = len(from_chips)
                    from_chips.append((i.at[l, c, pl.ds(start, size)], mine_there, (px, py, c), theirs_here, None))
                    forwards.append((theirs_here, theirs_here, sib, o.at[1 - c, p.at(o, jp, start, size)], n))
                for h in range(2):
                    place = o.at[h, p.at(o, j, start, size)]
                    own.append((i.at[l, h, pl.ds(start, size)], place, sib, place, None))
            if p.zero_rows:
                for h in range(2):
                    place = o.at[h, pl.ds(p.zero_rows[0], p.zero_rows[1])]
                    own.append((in_refs[n_w], place, sib, place, None))
        return [], from_chips + own + forwards

    shapes = [jax.ShapeDtypeStruct((2, p.full_rows, w.shape[3]), w.dtype) for w, p in zip(prepped, places)]
    ins = [*prepped, zeros] if any(p.zero_rows for p in places) else list(prepped)
    ex = Exchange(ins, shapes, 0, 2 * n_chip + n_own, plan)
    return ex if name is None else run_exchange(name, ex)


def sibling_scatter(arrays, name):
    def plan(me, in_refs, out_refs):
        x, y, c = me
        return [], [(i.at[1 - c], o, (x, y, 1 - c), o, None) for i, o in zip(in_refs, out_refs)]

    shapes = [jax.ShapeDtypeStruct(a.shape[1:], a.dtype) for a in arrays]
    ex = Exchange(arrays, shapes, 0, len(arrays), plan)
    return ex if name is None else run_exchange(name, ex)


def sibling_swap(arrays, name):
    def plan(me, in_refs, out_refs):
        x, y, c = me
        return [], [(i, o, (x, y, 1 - c), o, None) for i, o in zip(in_refs, out_refs)]

    shapes = [jax.ShapeDtypeStruct(a.shape, a.dtype) for a in arrays]
    ex = Exchange(arrays, shapes, 0, len(arrays), plan)
    return ex if name is None else run_exchange(name, ex)


def scatter_to_chips(arrays, places, name):
    def plan(me, in_refs, out_refs):
        x, y, c = me
        j = 2 * x + y
        local_plan, remote_plan = [], []
        for i, o, p in zip(in_refs, out_refs, places):
            for start, size in p.runs:
                local_plan.append((i.at[p.at(i, j, start, size)], o.at[3, pl.ds(start, size)]))
                for k, (dx, dy) in enumerate(_CHIP_FLIPS):
                    px, py = x ^ dx, y ^ dy
                    landing = o.at[k, pl.ds(start, size)]
                    remote_plan.append((i.at[p.at(i, 2 * px + py, start, size)], landing, (px, py, c), landing, None))
        return local_plan, remote_plan

    n_runs = sum(len(p.runs) for p in places)
    shapes = [jax.ShapeDtypeStruct((4, p.rows, a.shape[1]), a.dtype) for a, p in zip(arrays, places)]
    ex = Exchange(arrays, shapes, n_runs, 3 * n_runs, plan)
    return ex if name is None else run_exchange(name, ex)


def allgather8(block, name):
    def plan(me, in_refs, out_refs):
        x, y, c = me
        (i,), (o,) = in_refs, out_refs
        mine = 4 * x + 2 * y + c
        remote_plan = []
        for flip in range(1, 8):
            px, py, pc = x ^ (flip >> 2), y ^ ((flip >> 1) & 1), c ^ (flip & 1)
            remote_plan.append((i, o.at[mine], (px, py, pc), o.at[4 * px + 2 * py + pc], None))
        return [(i, o.at[mine])], remote_plan

    return _comm(name, [block], [jax.ShapeDtypeStruct((8, *block.shape), block.dtype)], 1, 7, plan)[0]


def transpose_split(a, out_dtype, name):
    L, R, C = a.shape
    rh = R // 2
    rt = _pick(rh, 256)
    n = rh // rt

    def body(x_ref, o_ref):
        o_ref[0, 0] = x_ref[0].T.astype(o_ref.dtype)

    return _pcall(
        body,
        name=name,
        grid=(L, 2, n),
        in_specs=[pl.BlockSpec((1, rt, C), lambda l, h, i: (l, h * n + i, 0))],
        out_specs=pl.BlockSpec((1, 1, C, rt), lambda l, h, i: (l, h, 0, i)),
        out_shape=jax.ShapeDtypeStruct((L, 2, C, rh), out_dtype),
        compiler_params=_params(3),
    )(a)


def cast_split(a, out_dtype, name):
    L, R, C = a.shape
    rt = _pick(R, 512, 16)

    def body(x_ref, o_ref):
        o_ref[0, 0] = x_ref[0].astype(o_ref.dtype)

    return _pcall(
        body,
        name=name,
        grid=(L, 2, R // rt),
        in_specs=[pl.BlockSpec((1, rt, C // 2), lambda l, h, i: (l, i, h))],
        out_specs=pl.BlockSpec((1, 1, rt, C // 2), lambda l, h, i: (l, h, i, 0)),
        out_shape=jax.ShapeDtypeStruct((L, 2, R, C // 2), out_dtype),
        compiler_params=_params(3),
    )(a)


def untranspose_halves(mine, theirs, my_c, stack, l, n_layers, name):
    C, rh = mine.shape
    n_in = 2 if stack is None else 3
    rt = _pick(rh, 256)
    n = rh // rt

    def body(c_ref, a_ref, b_ref, *rest):
        o_ref = rest[-1]
        h = pl.program_id(0)
        o_ref[0] = jnp.where(h == c_ref[0], a_ref[...], b_ref[...]).T

    grid_spec = pltpu.PrefetchScalarGridSpec(
        num_scalar_prefetch=1,
        grid=(2, n),
        in_specs=[pl.BlockSpec((C, rt), lambda h, i, c: (0, i))] * 2
        + ([] if stack is None else [pl.BlockSpec(memory_space=pl.ANY)]),
        out_specs=pl.BlockSpec((1, rt, C), lambda h, i, c: (l, h * n + i, 0)),
    )
    return _pcall(
        body,
        name=name,
        grid_spec=grid_spec,
        out_shape=jax.ShapeDtypeStruct((n_layers, 2 * rh, C), F32),
        input_output_aliases={} if stack is None else {n_in: 0},
        compiler_params=_params(2),
    )(my_c.reshape(1), mine, theirs, *([] if stack is None else [stack]))


def merge_halves(mine, theirs, my_c, stack, l, n_layers, name):
    R, ch = mine.shape
    n_in = 2 if stack is None else 3
    rt = _pick(R, 512, 8)

    def body(c_ref, a_ref, b_ref, *rest):
        o_ref = rest[-1]
        o_ref[0] = jnp.where(pl.program_id(0) == c_ref[0], a_ref[...], b_ref[...])

    grid_spec = pltpu.PrefetchScalarGridSpec(
        num_scalar_prefetch=1,
        grid=(2, R // rt),
        in_specs=[pl.BlockSpec((rt, ch), lambda h, i, c: (i, 0))] * 2
        + ([] if stack is None else [pl.BlockSpec(memory_space=pl.ANY)]),
        out_specs=pl.BlockSpec((1, rt, ch), lambda h, i, c: (l, i, h)),
    )
    return _pcall(
        body,
        name=name,
        grid_spec=grid_spec,
        out_shape=jax.ShapeDtypeStruct((n_layers, R, 2 * ch), F32),
        input_output_aliases={} if stack is None else {n_in: 0},
        compiler_params=_params(2),
    )(my_c.reshape(1), mine, theirs, *([] if stack is None else [stack]))


def add_pair(g, r, my_c, name):
    _, rows, w = g.shape
    tm = _pick(rows, max(16, min(1024, (1 << 20) // w // 16 * 16)), 16)

    def body(c_ref, g_ref, r_ref, o_ref):
        o_ref[...] = (g_ref[0].astype(F32) + r_ref[...].astype(F32)).astype(o_ref.dtype)

    grid_spec = pltpu.PrefetchScalarGridSpec(
        num_scalar_prefetch=1,
        grid=(rows // tm,),
        in_specs=[pl.BlockSpec((1, tm, w), lambda i, c: (c[0], i, 0)), pl.BlockSpec((tm, w), lambda i, c: (i, 0))],
        out_specs=pl.BlockSpec((tm, w), lambda i, c: (i, 0)),
    )
    return _pcall(
        body, name=name, grid_spec=grid_spec, out_shape=jax.ShapeDtypeStruct((rows, w), BF16), compiler_params=_params(1)
    )(my_c.reshape(1), g, r)


class Dims:
    def __init__(self, S, D, L, ret_heads, mla_heads, q_rank, kv_rank):
        self.S, self.D, self.L, self.HR, self.HM, self.QR, self.KR = S, D, L, ret_heads, mla_heads, q_rank, kv_rank
        self.RQ, self.RV, self.MV = ret_heads * RET_DK, ret_heads * RET_DV, mla_heads * MLA_DV
        self.D_IN = 2 * self.RQ + 2 * self.RV + q_rank + kv_rank + MLA_ROPE + self.MV + 2 * D
        self.lo = 2 * self.RQ + 2 * self.RV
        self.mid = q_rank + kv_rank + MLA_ROPE
        self.DP = self.D_IN + LANE - MLA_ROPE
        self.o_rq, self.o_rk, self.o_rv, self.o_rg = 0, self.RQ, 2 * self.RQ, 2 * self.RQ + self.RV
        self.o_mg = self.lo
        self.o_bga = self.o_mg + self.MV
        self.o_bgb = self.o_bga + D
        self.o_cq = self.o_bgb + D
        self.o_ckv = self.o_cq + q_rank
        self.o_kr = self.o_ckv + kv_rank

    def pad_uq_rows(self, w):
        heads = w.shape[-2] // (MLA_NOPE + MLA_ROPE)
        w = w.reshape(*w.shape[:-2], heads, MLA_NOPE + MLA_ROPE, w.shape[-1])
        w = jnp.pad(w, [(0, 0)] * (w.ndim - 2) + [(0, MLA_QW - MLA_NOPE - MLA_ROPE), (0, 0)])
        return w.reshape(*w.shape[:-3], heads * MLA_QW, w.shape[-1])

    def unpad_uq_rows(self, w):
        heads = w.shape[-2] // MLA_QW
        w = w.reshape(*w.shape[:-2], heads, MLA_QW, w.shape[-1])[..., : MLA_NOPE + MLA_ROPE, :]
        return w.reshape(*w.shape[:-3], heads * (MLA_NOPE + MLA_ROPE), w.shape[-1])

    def placements(self):
        rows_in = self.D_IN // 4
        lo, mid, n_hi = self.lo, self.mid, self.D_IN - self.lo - self.mid
        cuts = {b % rows_in for b in (lo, lo + mid)} - {0}

        def offset_in(j, start):
            g = j * rows_in + start
            return jnp.where(g < lo, g, jnp.where(g < lo + mid, g + n_hi, g - mid))

        return [
            Placement(rows_in, self.DP, cuts, offset_in, zero_rows=(self.D_IN, self.DP - self.D_IN)),
            Placement(self.HM * MLA_QW // 4, self.HM * MLA_QW),
            Placement(self.HM * (MLA_NOPE + MLA_DV) // 4, self.HM * (MLA_NOPE + MLA_DV)),
            Placement(self.RV // 4, self.RV),
            Placement(self.MV // 4, self.MV),
            Placement(self.D // 4, self.D),
        ]


def _rope_tables(positions, dim, width):
    inv = 1.0 / (ROPE_BASE ** (jnp.arange(0, dim, 2, dtype=F32) / dim))
    ang = positions.astype(F32)[:, None] * inv
    cos, sin = jnp.cos(ang), jnp.sin(ang)
    pad = jnp.zeros((positions.shape[0], width - dim), F32)
    return jnp.concatenate([cos, cos, pad], axis=1), jnp.concatenate([-sin, sin, pad], axis=1)


def _tiles(x):
    return [x[:, t * LANE : (t + 1) * LANE] for t in range(x.shape[1] // LANE)]


def _cat(parts):
    return parts[0] if len(parts) == 1 else jnp.concatenate(parts, axis=1)


def _rms(x, eps=EPS):
    return lax.rsqrt(jnp.mean(x * x, axis=1, keepdims=True) + eps)


def layer_fwd(dm, l, x, shift, scale, gate, g_norm, g_cq, g_ckv, w, tabs, host=None, host_in=None, late_keys=()):
    cos_r, sin_r, cos_m, sin_m = tabs
    nm = lambda s: f"l{l}_{s}"

    def f_norm(x, g, scale, shift):
        return [x * _rms(x) * g * (1.0 + scale) + shift], []

    (h,) = rowwise(f_norm, [x], [g_norm, scale, shift], [(dm.D, BF16)], tm=256, name=nm("norm"))
    proj, *late = _as_list(matmul(h, w["in"], tb=True, b_split=True, name=nm("mm_in"), tn=1920, host=host_in))
    w = dict(w, **dict(zip(late_keys, late)))

    def f_rope_ret(rq, rk, cos, sin):
        rq, rk = rq.astype(F32), rk.astype(F32)
        q = _cat([_rope_tile(t, cos, sin, RET_DK // 2) for t in _tiles(rq)])
        k = _cat([_rope_tile(t, cos, sin, RET_DK // 2) * (RET_DK**-0.5) for t in _tiles(rk)])
        return [q, k], []

    rq, rk = rowwise(
        f_rope_ret,
        [win(proj, dm.o_rq, dm.RQ), win(proj, dm.o_rk, dm.RQ), cos_r, sin_r],
        [],
        [(dm.RQ, BF16)] * 2,
        tm=512,
        name=nm("rope_ret"),
    )
    o_ret, a_ret, states = ret_fwd(rq, rk, proj, dm.o_rv, dm.o_rg, heads=dm.HR, tb=RET_BLOCK, name=nm("ret_fwd"))

    def f_prep(cq, ckv, kr, cos, sin, g_cq, g_ckv):
        cq, ckv, kr = cq.astype(F32), ckv.astype(F32), kr.astype(F32)
        return [cq * _rms(cq) * g_cq, ckv * _rms(ckv) * g_ckv, _rope_tile(kr, cos, sin, MLA_ROPE // 2)], []

    cqn, ckvn, krr = rowwise(
        f_prep,
        [win(proj, dm.o_cq, dm.QR), win(proj, dm.o_ckv, dm.KR), win(proj, dm.o_kr, LANE), cos_m, sin_m],
        [g_cq, g_ckv],
        [(dm.QR, BF16), (dm.KR, BF16), (LANE, BF16)],
        tm=512,
        name=nm("mla_prep"),
    )
    q_raw = matmul(cqn, w["uq"], tb=True, b_split=True, name=nm("mm_uq"), tn=4096)
    kv = matmul(ckvn, w["ukv"], tb=True, b_split=True, name=nm("mm_ukv"), tn=4096)

    def f_rope_q(q, cos, sin):
        t = _tiles(q.astype(F32))
        rot = [t[n] if n % 2 == 0 else _rope_tile(t[n], cos, sin, MLA_ROPE // 2) for n in range(len(t))]
        return [_cat([r * (MLA_SCALE * LOG2E) for r in rot])], []

    (q,) = rowwise(f_rope_q, [q_raw, cos_m, sin_m], [], [(dm.HM * MLA_QW, BF16)], tm=256, name=nm("rope_q"))
    o_mla, a_mla, lse, *hosted = attn_fwd(
        q, kv, krr, proj, dm.o_mg, heads=dm.HM, tq=ATTN_BLOCK, name=nm("attn_fwd"), host=host
    )

    y_ret = matmul(a_ret, w["ret"], b_split=True, name=nm("mm_ret"))
    y_mla = matmul(a_mla, w["mla"], b_split=True, name=nm("mm_mla"))

    def f_merge(y_ret, y_mla, bga, bgb):
        return [_sigmoid(bga.astype(F32)) * y_ret.astype(F32) + _sigmoid(bgb.astype(F32)) * y_mla.astype(F32)], []

    (merged,) = rowwise(
        f_merge, [y_ret, y_mla, win(proj, dm.o_bga, dm.D), win(proj, dm.o_bgb, dm.D)], [], [(dm.D, BF16)], tm=256, name=nm("merge")
    )
    out = matmul(merged, w["out"], b_split=True, out_dtype=F32, name=nm("mm_out"))

    def f_resid(x, out, gate):
        return [x + gate * out], []

    (x_new,) = rowwise(f_resid, [x, out], [gate], [(dm.D, F32)], tm=256, name=nm("resid"))
    saved = dict(
        x=x, h=h, proj=proj, rq=rq, rk=rk, o_ret=o_ret, a_ret=a_ret, states=states, cqn=cqn, ckvn=ckvn, krr=krr, q=q, kv=kv,
        o_mla=o_mla, a_mla=a_mla, lse=lse, y_ret=y_ret, y_mla=y_mla, merged=merged, out=out,
    )
    return x_new, saved, hosted, w


def _as_list(v):
    return list(v) if isinstance(v, (list, tuple)) else [v]


def layer_bwd(
    dm, l, dx_out, sv, shift, scale, gate, g_norm, g_cq, g_ckv, w, tabs, host=None, exchange_dh=None, exchange_dw=None
):
    cos_r, sin_r, cos_m, sin_m = tabs
    nm = lambda s: f"l{l}_{s}"
    proj = sv["proj"]

    def b_resid(dx, out, gate):
        return [dx * gate], [_sum0(dx * out)]

    dout, d_gate = rowwise(b_resid, [dx_out, sv["out"]], [gate], [(dm.D, BF16)], [dm.D], tm=256, name=nm("resid_bwd"))
    dmerged = matmul(dout, w["out"], tb=True, b_split=True, name=nm("mm_dmerged"))
    dw_out = matmul(sv["merged"], dout, ta=True, out_split=True, tn=2048, name=nm("mm_dw_out"))

    def b_merge(dmg, y_ret, y_mla, bga, bgb):
        dmg = dmg.astype(F32)
        ga, gb = _sigmoid(bga.astype(F32)), _sigmoid(bgb.astype(F32))
        y_ret, y_mla = y_ret.astype(F32), y_mla.astype(F32)
        d_gates = jnp.concatenate([dmg * y_ret * ga * (1.0 - ga), dmg * y_mla * gb * (1.0 - gb)], axis=1)
        return [d_gates, dmg * ga, dmg * gb], []

    dproj, dy_ret, dy_mla = rowwise(
        b_merge,
        [dmerged, sv["y_ret"], sv["y_mla"], win(proj, dm.o_bga, dm.D), win(proj, dm.o_bgb, dm.D)],
        [],
        [(2 * dm.D, BF16), (dm.D, BF16), (dm.D, BF16)],
        tm=256,
        name=nm("merge_bwd"),
        into=(dm.DP, dm.o_bga),
    )
    da_ret = matmul(dy_ret, w["ret"], tb=True, b_split=True, name=nm("mm_da_ret"))
    dw_ret = matmul(sv["a_ret"], dy_ret, ta=True, out_split=True, tn=2048, name=nm("mm_dw_ret"))
    da_mla = matmul(dy_mla, w["mla"], tb=True, b_split=True, name=nm("mm_da_mla"))
    dw_mla = matmul(sv["a_mla"], dy_mla, ta=True, out_split=True, tn=2048, name=nm("mm_dw_mla"))

    def b_ret_gate(da, rg, o):
        da, rg = da.astype(F32), rg.astype(F32)
        do_parts, drg_parts = [], []
        for hh in range(dm.HR):
            sl = slice(hh * RET_DV, (hh + 1) * RET_DV)
            oh, dah, rgh = o[:, sl], da[:, sl], rg[:, sl]
            mu = jnp.mean(oh, axis=1, keepdims=True)
            d = oh - mu
            r = lax.rsqrt(jnp.mean(d * d, axis=1, keepdims=True) + EPS)
            n = d * r
            dn = dah * _silu(rgh)
            drg_parts.append(dah * n * _dsilu(rgh))
            do_parts.append(r * (dn - jnp.mean(dn, axis=1, keepdims=True) - n * jnp.mean(dn * n, axis=1, keepdims=True)))
        return [_cat(drg_parts), _cat(do_parts)], []

    dproj, do_ret = rowwise(
        b_ret_gate, [da_ret, win(proj, dm.o_rg, dm.RV), sv["o_ret"]], [], [(dm.RV, BF16)] * 2, tm=256, name=nm("ret_gate_bwd"),
        into=(dproj, dm.o_rg),
    )
    dq_rot, dk_rot, dproj = ret_bwd(
        sv["rq"], sv["rk"], proj, dm.o_rv, sv["states"], do_ret, heads=dm.HR, tb=RET_BLOCK, name=nm("ret_bwd"), into=(dproj, dm.o_rv)
    )

    def b_rope_ret(dq, dk, cos, sin):
        dq, dk = dq.astype(F32), dk.astype(F32)
        q = [_rope_tile(t, cos, sin, RET_DK // 2, inverse=True) for t in _tiles(dq)]
        k = [_rope_tile(t, cos, sin, RET_DK // 2, inverse=True) * (RET_DK**-0.5) for t in _tiles(dk)]
        return [_cat(q + k)], []

    (dproj,) = rowwise(
        b_rope_ret, [dq_rot, dk_rot, cos_r, sin_r], [], [(2 * dm.RQ, BF16)], tm=512, name=nm("rope_ret_bwd"), into=(dproj, dm.o_rq)
    )

    def b_mla_gate(da, mg, o):
        da, mg = da.astype(F32), mg.astype(F32)
        return [da * o * _dsilu(mg), da * _silu(mg)], []

    dproj, do_mla = rowwise(
        b_mla_gate, [da_mla, win(proj, dm.o_mg, dm.MV), sv["o_mla"]], [], [(dm.MV, BF16)] * 2, tm=256, name=nm("mla_gate_bwd"),
        into=(dproj, dm.o_mg),
    )
    dq_att, dkv, dkr_heads, *hosted = attn_bwd(
        sv["q"], sv["kv"], sv["krr"], do_mla, sv["o_mla"], sv["lse"], heads=dm.HM, tq=ATTN_BLOCK, name=nm("attn_bwd"), host=host
    )

    def b_rope_q(dq, cos, sin):
        t = _tiles(dq.astype(F32))
        return [_cat([t[n] if n % 2 == 0 else _rope_tile(t[n], cos, sin, MLA_ROPE // 2, inverse=True) for n in range(len(t))])], []

    (dq_raw,) = rowwise(b_rope_q, [dq_att, cos_m, sin_m], [], [(dm.HM * MLA_QW, BF16)], tm=256, name=nm("rope_q_bwd"))
    dcqn = matmul(dq_raw, w["uq"], b_split=True, name=nm("mm_dcqn"))
    dw_uq = matmul(dq_raw, sv["cqn"], ta=True, out_split=True, tn=2048, name=nm("mm_dw_uq"))
    dckvn = matmul(dkv, w["ukv"], b_split=True, name=nm("mm_dckvn"))
    dw_ukv = matmul(dkv, sv["ckvn"], ta=True, out_split=True, tn=2048, name=nm("mm_dw_ukv"))

    def b_prep(dcqn, dckvn, cq, ckv, cos, sin, dkr_h, g_cq, g_ckv):
        outs, accs = [], []
        for dn, z, g in ((dcqn, cq, g_cq), (dckvn, ckv, g_ckv)):
            dn, z = dn.astype(F32), z.astype(F32)
            n = z * _rms(z)
            dng = dn * g
            outs.append(_rms(z) * (dng - n * jnp.mean(dng * n, axis=1, keepdims=True)))
            accs.append(_sum0(dn * n))
        dkr = dkr_h[0]
        for hh in range(1, dm.HM):
            dkr = dkr + dkr_h[hh]
        return [_cat(outs), _rope_tile(dkr, cos, sin, MLA_ROPE // 2, inverse=True)], accs

    dproj, d_kr, dg_cq, dg_ckv = rowwise(
        b_prep,
        [dcqn, dckvn, win(proj, dm.o_cq, dm.QR), win(proj, dm.o_ckv, dm.KR), cos_m, sin_m, dkr_heads],
        [g_cq, g_ckv],
        [(dm.QR + dm.KR, BF16), (LANE, BF16)],
        [dm.QR, dm.KR],
        tm=256,
        name=nm("mla_prep_bwd"),
        into=(dproj, dm.o_cq),
    )
    (dproj,) = rowwise(lambda a: ([a], []), [d_kr], [], [(LANE, BF16)], tm=512, name=nm("place_dkr"), into=(dproj, dm.o_kr))
    dws = dict(w_uq=dw_uq, w_ukv=dw_ukv, w_ret_proj=dw_ret, w_mla_proj=dw_mla, w_out=dw_out)
    host_dw = exchange_dw(dws, hosted) if exchange_dw else None
    dw_in, *hosted_dw = _as_list(
        matmul(dproj, sv["h"], ta=True, out_split=True, tn=2048, name=nm("mm_dw_in"), host=host_dw)
    )
    dws["w_in"] = dw_in
    host_dh = exchange_dh(dws) if exchange_dh else None
    dh, *hosted_dh = _as_list(
        matmul(dproj, w["in"], b_split=True, out_dtype=F32, name=nm("mm_dh"), tn=2048, tk=1920, host=host_dh)
    )

    def b_norm(dh, x, dx_res, g, scale):
        r = _rms(x)
        xn = x * r
        dxn = dh * g * (1.0 + scale)
        dx = dx_res + r * (dxn - xn * jnp.mean(dxn * xn, axis=1, keepdims=True))
        return [dx], [_sum0(dh), _sum0(dh * xn * g), _sum0(dh * (1.0 + scale) * xn)]

    dx, d_shift, d_scale, dg_norm = rowwise(
        b_norm, [dh, sv["x"], dx_out], [g_norm, scale], [(dm.D, F32)], [dm.D] * 3, tm=256, name=nm("norm_bwd")
    )
    dvec = dict(mod=jnp.concatenate([d_shift, d_scale, d_gate], axis=1), g_norm=dg_norm, g_cq=dg_cq, g_ckv=dg_ckv)
    return dx, dws, dvec, hosted, hosted_dh, hosted_dw


def adamw(w, g, m, v, name):
    shape = w.shape
    cols = shape[-1]
    view = lambda a: a.reshape(-1, cols)

    def f(w, g, m, v):
        m = ADAM_B1 * m + (1.0 - ADAM_B1) * g
        v = ADAM_B2 * v + (1.0 - ADAM_B2) * (g * g)
        m_hat = m / (1.0 - ADAM_B1**ADAM_STEP)
        v_hat = v / (1.0 - ADAM_B2**ADAM_STEP)
        delta = -ADAM_LR * (m_hat / (jnp.sqrt(v_hat) + ADAM_EPS) + ADAM_WD * w)
        return [delta, m, v], []

    tm = max(8, min(512, (400_000 // cols) // 8 * 8))
    delta, m, v = rowwise(f, [view(w), view(g), view(m), view(v)], [], [(cols, F32)] * 3, tm=tm, name=name)
    return delta.reshape(shape), m.reshape(shape), v.reshape(shape)


def _add_rows(fn, rows, cols, dtype, name):
    tm = max(8, min(512, (400_000 // cols) // 8 * 8))
    return rowwise(lambda *a: ([fn(*a)], []), rows, [], [(cols, dtype)], tm=tm, name=name)[0]


BIG = ("w_in", "w_uq", "w_ukv", "w_ret_proj", "w_mla_proj", "w_out")
COL_SHARDED = ("w_in", "w_uq", "w_ukv")


def kernel(x, c, positions, w_mod, b_mod, g_norm, w_in, g_cq, g_ckv, w_uq, w_ukv, w_ret_proj, w_mla_proj, w_out, g_final, loss_target, m_w_mod, m_b_mod, m_g_norm, m_w_in, m_g_cq, m_g_ckv, m_w_uq, m_w_ukv, m_w_ret_proj, m_w_mla_proj, m_w_out, m_g_final, v_w_mod, v_b_mod, v_g_norm, v_w_in, v_g_cq, v_g_ckv, v_w_uq, v_w_ukv, v_w_ret_proj, v_w_mla_proj, v_w_out, v_g_final):
    weights = dict(w_mod=w_mod, b_mod=b_mod, g_norm=g_norm, w_in=w_in, g_cq=g_cq, g_ckv=g_ckv, w_uq=w_uq, w_ukv=w_ukv,
                   w_ret_proj=w_ret_proj, w_mla_proj=w_mla_proj, w_out=w_out, g_final=g_final)
    m_in = dict(w_mod=m_w_mod, b_mod=m_b_mod, g_norm=m_g_norm, w_in=m_w_in, g_cq=m_g_cq, g_ckv=m_g_ckv, w_uq=m_w_uq,
                w_ukv=m_w_ukv, w_ret_proj=m_w_ret_proj, w_mla_proj=m_w_mla_proj, w_out=m_w_out, g_final=m_g_final)
    v_in = dict(w_mod=v_w_mod, b_mod=v_b_mod, g_norm=v_g_norm, w_in=v_w_in, g_cq=v_g_cq, g_ckv=v_g_ckv, w_uq=v_w_uq,
                w_ukv=v_w_ukv, w_ret_proj=v_w_ret_proj, w_mla_proj=v_w_mla_proj, w_out=v_w_out, g_final=v_g_final)
    order = ("w_mod", "b_mod", "g_norm", "w_in", "g_cq", "g_ckv", "w_uq", "w_ukv", "w_ret_proj", "w_mla_proj", "w_out", "g_final")

    x = x[0]
    target = loss_target[0]
    S, D = x.shape
    L = w_mod.shape[0]
    dm = Dims(S, D, L, w_ret_proj.shape[1] * 4 // RET_DV, w_mla_proj.shape[1] * 4 // MLA_DV, g_cq.shape[1], g_ckv.shape[1])
    my_x, my_y, my_c = _me()
    my_chip = 2 * my_x + my_y
    my_dev = 2 * my_chip + my_c
    C3 = w_mod.shape[2]

    for table in (weights, m_in, v_in):
        table["w_in"] = jnp.swapaxes(table["w_in"], 1, 2)
    prepped = [
        cast_split(weights["w_in"], BF16, "prep_w_in"),
        dm.pad_uq_rows(transpose_split(w_uq, BF16, "prep_w_uq")),
        transpose_split(w_ukv, BF16, "prep_w_ukv"),
        cast_split(w_ret_proj, BF16, "prep_w_ret"),
        cast_split(w_mla_proj, BF16, "prep_w_mla"),
        cast_split(w_out, BF16, "prep_w_out"),
    ]
    places = dm.placements()
    zero_rows = jnp.zeros((places[0].zero_rows[1], D // 2), BF16)
    w_keys = ("in", "uq", "ukv", "ret", "mla", "out")
    ahead, own = (0, 4, 5), (1, 2, 3)

    def gather_of(which, l):
        return gather_layer([prepped[n] for n in which], zero_rows, [places[n] for n in which], l, None)

    layer_w = [dict(zip(w_keys, gather_layer(prepped[:1], zero_rows, places[:1], 0, "l0_gather_w_in")))]

    c_all = allgather8(c, "gather_c").reshape(8, D)
    (c_act,) = rowwise(lambda z: ([_silu(z)], []), [c_all], [], [(D, BF16)], tm=8, name="silu_c")
    mod_part = jnp.stack([matmul(c_act, w_mod[l], out_dtype=F32, name=f"l{l}_mm_mod", tn=C3) for l in range(L)])
    mod_all = allgather8(mod_part, "gather_mod")
    mod_all = mod_all.reshape(4, 2, L, 8, C3)[:, 0].transpose(1, 2, 0, 3).reshape(L, 8, 3 * D) + b_mod[:, None, :]
    mod = lax.dynamic_index_in_dim(mod_all, my_dev, axis=1, keepdims=False)

    pos = positions[0]
    tabs = (*_rope_tables(pos, RET_DK, LANE), *_rope_tables(pos, MLA_ROPE, LANE))

    def vecs(l):
        return (mod[l : l + 1, :D], mod[l : l + 1, D : 2 * D], mod[l : l + 1, 2 * D :],
                g_norm[l : l + 1], g_cq[l : l + 1], g_ckv[l : l + 1])

    saved = []
    for l in range(L):
        mine = (1, 2, 3, 4, 5) if l == 0 else own
        host = gather_of(ahead, l + 1) if l + 1 < L else None
        x, sv, hosted, layer_w[l] = layer_fwd(
            dm, l, x, *vecs(l), layer_w[l], tabs, host=host, host_in=gather_of(mine, l), late_keys=[w_keys[n] for n in mine]
        )
        saved.append(sv)
        if host:
            layer_w.append(dict(zip([w_keys[n] for n in ahead], hosted)))

    def f_loss(x, t, g):
        xn = x * _rms(x)
        err = xn * g - t
        dy = err * (1.0 / D)
        dxn = dy * g
        dx = _rms(x) * (dxn - xn * jnp.mean(dxn * xn, axis=1, keepdims=True))
        part = jnp.sum(jnp.sum(err * err, axis=1, keepdims=True), axis=0, keepdims=True) * (0.5 / D)
        return [dx], [jnp.broadcast_to(part, (1, LANE)), _sum0(dy * xn)]

    dx, loss_part, dg_final = rowwise(f_loss, [x, target], [g_final.reshape(1, D)], [(D, F32)], [LANE, D], tm=256, name="loss_head")

    stacks = {n: None for n in BIG}
    dvec = {n: [None] * L for n in ("mod", "g_norm", "g_cq", "g_ckv")}

    def reduce_chips(l, from_chips):
        return [
            _add_rows(lambda a: ((a[3].astype(F32) + a[0].astype(F32)) + a[1].astype(F32)) + a[2].astype(F32), [r], r.shape[-1],
                      F32, f"l{l}_rs_add4_{n}")
            for n, r in zip(BIG, from_chips)
        ]

    def finish_grads(l, from_chips):
        mine = reduce_chips(l, from_chips)
        store_grads(l, mine, sibling_swap(mine, f"l{l}_rs_share"))

    def store_grads(l, mine, theirs):
        for n, a, b in zip(BIG, mine, theirs):
            if n == "w_uq":
                a, b = dm.unpad_uq_rows(a), dm.unpad_uq_rows(b)
            finish = untranspose_halves if n in ("w_uq", "w_ukv") else merge_halves
            stacks[n] = finish(a, b, my_c, stacks[n], l, L, f"l{l}_grad_{n}")

    def pair_sums(l, names, partial, from_sibling):
        return [add_pair(g, r, my_c, f"l{l}_rs_add2_{n}") for n, g, r in zip(names, partial, from_sibling)]

    def core_exchange(dws):
        return sibling_scatter([dws[n] for n in BIG], None)

    def last_exchange(names, which, tag):
        def build(dws, _=None):
            partial = [dws[n] for n in names]
            pairs = pair_sums(0, names, partial, sibling_scatter(partial, f"l0_rs_cores_{tag}"))
            return scatter_to_chips(pairs, which, None)

        return build

    pair = None
    for l in reversed(range(L)):
        host = scatter_to_chips(pair, places, None) if pair is not None else None
        halves = {}

        def share_halves(dws, from_chips, l=l, halves=halves):
            halves["mine"] = reduce_chips(l + 1, from_chips)
            return sibling_swap(halves["mine"], None)

        hooks = dict(exchange_dh=core_exchange, exchange_dw=share_halves if host else None)
        if l == 0:
            hooks = dict(exchange_dw=last_exchange(BIG[1:], places[1:], "rest"), exchange_dh=last_exchange(BIG[:1], places[:1], "in"))
        dx, dw_l, dv_l, hosted, hosted_dh, hosted_dw = layer_bwd(
            dm, l, dx, saved[l], *vecs(l), layer_w[l], tabs, host=host, **hooks
        )
        if host and l > 0:
            store_grads(l + 1, halves["mine"], hosted_dw)
        elif host:
            finish_grads(l + 1, hosted)
        for n in dvec:
            dvec[n][l] = dv_l[n]
        if l > 0:
            pair = pair_sums(l, BIG, [dw_l[n] for n in BIG], hosted_dh)
        else:
            finish_grads(0, [*hosted_dh, *hosted_dw])
    grad_x = dx[None]

    pieces = [loss_part] + [jnp.concatenate(dvec[n], axis=1) for n in ("mod", "g_norm", "g_cq", "g_ckv")] + [dg_final]
    widths = [p.shape[1] for p in pieces]
    small_all = allgather8(jnp.concatenate(pieces, axis=1), "gather_small").reshape(8, sum(widths))

    def sum8_body(a_ref, o_ref):
        acc = a_ref[0:1, :]
        for d in range(1, 8):
            acc = acc + a_ref[d : d + 1, :]
        o_ref[...] = acc

    small = _pcall(sum8_body, name="sum_small", out_shape=jax.ShapeDtypeStruct((1, sum(widths)), F32))(small_all)
    offs = np.cumsum([0] + widths)
    loss = small[0, 0]
    g_small = {
        "b_mod": small[0, offs[1] : offs[2]].reshape(L, 3 * D),
        "g_norm": small[0, offs[2] : offs[3]].reshape(L, D),
        "g_cq": small[0, offs[3] : offs[4]].reshape(L, dm.QR),
        "g_ckv": small[0, offs[4] : offs[5]].reshape(L, dm.KR),
        "g_final": small[0, offs[5] : offs[6]],
    }

    dmod_all = small_all[:, offs[1] : offs[2]].reshape(8, L, 3 * D)
    dmod_mine = lax.dynamic_slice_in_dim(dmod_all, my_chip * C3, C3, axis=2)
    pad8 = lambda a: jnp.pad(a, ((0, LANE - 8), (0, 0)))
    grads = dict(g_small)
    grads["w_mod"] = jnp.stack(
        [matmul(pad8(c_act), pad8(dmod_mine[:, l]), ta=True, out_dtype=F32, name=f"l{l}_mm_dw_mod", tn=C3) for l in range(L)]
    )

    for n in BIG:
        grads[n] = stacks[n]

    deltas, new_m, new_v = {}, {}, {}
    for n in order:
        wv, gv, mv, vv = weights[n], grads[n], m_in[n], v_in[n]
        if wv.ndim == 1:
            wv, gv, mv, vv = (a.reshape(1, -1) for a in (wv, gv, mv, vv))
        d_, m_, v_ = adamw(wv, gv, mv, vv, f"adamw_{n}")
        deltas[n], new_m[n], new_v[n] = (a.reshape(weights[n].shape) for a in (d_, m_, v_))
        grads[n] = grads[n].reshape(weights[n].shape)
    for table in (grads, deltas, new_m, new_v):
        table["w_in"] = jnp.swapaxes(table["w_in"], 1, 2)

    return (loss, grad_x, *[grads[n] for n in order], *[deltas[n] for n in order], *[new_m[n] for n in order],
            *[new_v[n] for n in order])
```

```python
import functools

import jax
import jax.numpy as jnp
import numpy as np
from jax import lax
from jax.experimental import pallas as pl
from jax.experimental.pallas import tpu as pltpu

F32 = jnp.float32
BF16 = jnp.bfloat16
MESH = pl.DeviceIdType.MESH

VMEM_LIMIT_BYTES = 52 * 1024 * 1024
LANE = 128

CHUNK = 64
EPS = 1e-6
NEG_INF = -1e30
ROPE_BASE = 10000.0
RET_DK = 128
RET_DV = 256
MLA_NOPE = 128
MLA_ROPE = 64
MLA_DV = 128
MLA_QW = 256
MLA_SCALE = float((MLA_NOPE + MLA_ROPE) ** -0.5)
LOG2E = float(np.log2(np.e))
LN2 = float(np.log(2.0))
ATTN_BLOCK = 512
ATTN_FWD_HEADS_PER_STEP = 8
ATTN_HEADS_PER_STEP = 2
RET_BLOCK = 256
RET_HEADS_PER_STEP = 8

ADAM_LR = 0.001
ADAM_B1 = 0.9
ADAM_B2 = 0.999
ADAM_EPS = 1e-08
ADAM_WD = 0.01
ADAM_STEP = 10


def _pcall(body, **kw):
    return pl.pallas_call(body, **kw)


def _params(n_grid):
    return pltpu.CompilerParams(dimension_semantics=("arbitrary",) * n_grid, vmem_limit_bytes=VMEM_LIMIT_BYTES)


def _pick(dim, target, mult=LANE):
    if dim <= target:
        return dim
    best = None
    for t in range(mult, target + 1, mult):
        if dim % t == 0:
            best = t
    assert best is not None, (dim, target, mult)
    return best


def matmul(
    a, b, *, ta=False, tb=False, b_split=False, out_split=False, out_dtype=BF16, name, tm=512, tn=1024, tk=2048, host=None
):
    (M, K) = (a.shape[1], a.shape[0]) if ta else a.shape
    b_rows, b_cols = (b.shape[1], 2 * b.shape[2]) if b_split else b.shape
    (K2, N) = (b_cols, b_rows) if tb else (b_rows, b_cols)
    assert K == K2, (a.shape, b.shape, ta, tb)
    whole_k = b_split and tb and tk >= K and not ta
    whole_n = out_split and tn >= N and not (b_split and not tb)
    whole_bn = b_split and not tb and tn >= N and not out_split
    n_cap = N if (whole_n or whole_bn) else (N // 2 if (out_split or (b_split and not tb)) else N)
    k_cap = K if whole_k else (K // 2 if (b_split and tb) else K)
    tm, tn, tk = _pick(M, tm, 8 if M < LANE else LANE), _pick(n_cap, min(tn, n_cap)), _pick(k_cap, min(tk, k_cap))
    nk = K // tk
    njh, nkh = max((N // 2) // tn, 1), max((K // 2) // tk, 1)
    dn = (((0 if ta else 1,), (1 if tb else 0,)), ((), ()))

    def body(a_ref, b_ref, o_ref, *scratch):
        if whole_k:
            a_blk = a_ref[...].astype(BF16)
            prod = lax.dot_general(a_blk[:, : K // 2], b_ref[0].astype(BF16), dn, preferred_element_type=F32)
            prod += lax.dot_general(a_blk[:, K // 2 :], b_ref[1].astype(BF16), dn, preferred_element_type=F32)
        elif whole_bn:
            a_blk = a_ref[...].astype(BF16)
            halves = [lax.dot_general(a_blk, b_ref[half].astype(BF16), dn, preferred_element_type=F32) for half in range(2)]
            prod = jnp.concatenate(halves, axis=1)
        else:
            b_blk = b_ref[0] if b_split else b_ref[...]
            prod = lax.dot_general(a_ref[...].astype(BF16), b_blk.astype(BF16), dn, preferred_element_type=F32)

        def store(v):
            if whole_n:
                o_ref[0] = v[:, : N // 2].astype(o_ref.dtype)
                o_ref[1] = v[:, N // 2 :].astype(o_ref.dtype)
            elif out_split:
                o_ref[0] = v.astype(o_ref.dtype)
            else:
                o_ref[...] = v.astype(o_ref.dtype)

        if nk == 1:
            store(prod)
            return
        (acc_ref,) = scratch
        k = pl.program_id(2)

        @pl.when(k == 0)
        def _():
            acc_ref[...] = prod

        @pl.when(k > 0)
        def _():
            acc_ref[...] += prod

        @pl.when(k == nk - 1)
        def _():
            store(acc_ref[...])

    a_spec = pl.BlockSpec((tk, tm), lambda i, j, k: (k, i)) if ta else pl.BlockSpec((tm, tk), lambda i, j, k: (i, k))
    if whole_k:
        b_spec = pl.BlockSpec((2, tn, K // 2), lambda i, j, k: (0, j, 0))
    elif whole_bn:
        b_spec = pl.BlockSpec((2, tk, N // 2), lambda i, j, k: (0, k, 0))
    elif b_split and tb:
        b_spec = pl.BlockSpec((1, tn, tk), lambda i, j, k: (k // nkh, j, k % nkh))
    elif b_split:
        b_spec = pl.BlockSpec((1, tk, tn), lambda i, j, k: (j // njh, k, j % njh))
    elif tb:
        b_spec = pl.BlockSpec((tn, tk), lambda i, j, k: (j, k))
    else:
        b_spec = pl.BlockSpec((tk, tn), lambda i, j, k: (k, j))
    if whole_n:
        out_spec = pl.BlockSpec((2, tm, N // 2), lambda i, j, k: (0, i, 0))
        out_shape = jax.ShapeDtypeStruct((2, M, N // 2), out_dtype)
    elif out_split:
        out_spec = pl.BlockSpec((1, tm, tn), lambda i, j, k: (j // njh, i, j % njh))
        out_shape = jax.ShapeDtypeStruct((2, M, N // 2), out_dtype)
    else:
        out_spec = pl.BlockSpec((tm, tn), lambda i, j, k: (i, j))
        out_shape = jax.ShapeDtypeStruct((M, N), out_dtype)
    grid = (M // tm, N // tn, nk)
    acc_shapes = [] if nk == 1 else [pltpu.VMEM((tm, tn), F32)]
    if host is None:
        return _pcall(
            body, name=name, grid=grid, in_specs=[a_spec, b_spec], out_specs=out_spec, out_shape=out_shape,
            scratch_shapes=acc_shapes, compiler_params=_params(3),
        )(a, b)

    n_hi, n_ho, n_steps = len(host.ins), len(host.out_shapes), grid[0] * grid[1] * grid[2]

    def hosting_body(a_ref, b_ref, *rest):
        host_in, rest = rest[:n_hi], rest[n_hi:]
        o_ref, rest = rest[0], rest[1:]
        host_out, rest = rest[:n_ho], rest[n_ho:]
        scratch, host_sems = rest[: len(acc_shapes)], rest[len(acc_shapes) :]
        step = (pl.program_id(0) * grid[1] + pl.program_id(1)) * grid[2] + pl.program_id(2)
        refs = (host_in, host_out, host_sems)

        @pl.when(step == 0)
        def _():
            host.start(*refs)

        @pl.when(step == n_steps // 2)
        def _():
            host.relay(*refs)

        body(a_ref, b_ref, o_ref, *scratch)

        @pl.when(step == n_steps - 1)
        def _():
            host.finish(*refs)

    return _pcall(
        hosting_body,
        name=name,
        grid=grid,
        in_specs=[a_spec, b_spec, *host.in_specs],
        out_specs=[out_spec, *host.out_specs],
        out_shape=[out_shape, *host.out_shapes],
        scratch_shapes=[*acc_shapes, *host.scratch],
        compiler_params=_params(3),
    )(a, b, *host.ins)


def win(arr, off, width):
    assert off % width == 0 and off + width <= arr.shape[1], (arr.shape, off, width)
    return (arr, off // width, width)


def rowwise(fn, rows, vecs, outs, accs=(), *, tm, name, into=None):
    rows = [r if isinstance(r, tuple) else (r, 0, r.shape[-1]) for r in rows]
    S = rows[0][0].shape[-2]
    tm = _pick(S, tm, 8)
    n_rows, n_vecs, n_outs = len(rows), len(vecs), len(outs)
    aliased = into is not None and not isinstance(into[0], int)
    n_alias = 1 if aliased else 0

    def body(*refs):
        ins = [r[...] for r in refs[: n_rows + n_vecs]]
        refs = refs[: n_rows + n_vecs] + refs[n_rows + n_vecs + n_alias :]
        out_refs = refs[n_rows + n_vecs : n_rows + n_vecs + n_outs]
        acc_refs = refs[n_rows + n_vecs + n_outs :]
        res, acc = fn(*ins)
        for r, v in zip(out_refs, res, strict=True):
            r[...] = v.astype(r.dtype)
        if acc_refs:
            i = pl.program_id(0)

            @pl.when(i == 0)
            def _():
                for r, v in zip(acc_refs, acc, strict=True):
                    r[...] = v

            @pl.when(i > 0)
            def _():
                for r, v in zip(acc_refs, acc, strict=True):
                    r[...] += v

    in_specs = []
    for arr, blk, w in rows:
        if arr.ndim == 3:
            in_specs.append(pl.BlockSpec((arr.shape[0], tm, w), lambda i: (0, i, 0)))
        else:
            in_specs.append(pl.BlockSpec((tm, w), functools.partial(lambda i, blk: (i, blk), blk=blk)))
    for v in vecs:
        in_specs.append(pl.BlockSpec(v.shape, functools.partial(lambda i, nd: (0,) * nd, nd=v.ndim)))
    out_specs = [pl.BlockSpec((tm, w), lambda i: (i, 0)) for w, _ in outs]
    out_specs += [pl.BlockSpec((1, w), lambda i: (0, 0)) for w in accs]
    out_shape = [jax.ShapeDtypeStruct((S, w), dt) for w, dt in outs]
    out_shape += [jax.ShapeDtypeStruct((1, w), F32) for w in accs]
    operands = [*[r[0] for r in rows], *vecs]
    aliases = {}
    if into is not None:
        target, off = into
        w0, dt0 = outs[0]
        total = target if isinstance(target, int) else target.shape[1]
        assert off % w0 == 0 and off + w0 <= total
        out_specs[0] = pl.BlockSpec((tm, w0), functools.partial(lambda i, blk: (i, blk), blk=off // w0))
        out_shape[0] = jax.ShapeDtypeStruct((S, total), dt0)
        if aliased:
            in_specs.append(pl.BlockSpec(memory_space=pl.ANY))
            operands.append(target)
            aliases = {len(operands) - 1: 0}
    res = _pcall(
        body,
        name=name,
        grid=(S // tm,),
        in_specs=in_specs,
        out_specs=out_specs,
        out_shape=out_shape,
        input_output_aliases=aliases,
        compiler_params=_params(1),
    )(*operands)
    return res


def _sum0(v):
    return jnp.sum(v, axis=0, keepdims=True)


def _sigmoid(z):
    return 1.0 / (1.0 + jnp.exp(-z))


def _silu(z):
    return z * _sigmoid(z)


def _dsilu(z):
    s = _sigmoid(z)
    return s * (1.0 + z * (1.0 - s))


def _swap_half(x, half):
    if 2 * half == LANE:
        return pltpu.roll(x, half, 1)
    lane = lax.broadcasted_iota(jnp.int32, x.shape, 1)
    return jnp.where((lane % (2 * half)) < half, pltpu.roll(x, LANE - half, 1), pltpu.roll(x, half, 1))


def _rope_tile(x, cosf, sinf, half, inverse=False):
    sw = _swap_half(x, half)
    return x * cosf - sw * sinf if inverse else x * cosf + sw * sinf


def _tri_tables(n, by_key):
    if by_key:
        pairs = [(i, j) for j in range(n) for i in range(j, n)]
    else:
        pairs = [(i, j) for i in range(n) for j in range(i + 1)]
    return (np.array([p[0] for p in pairs], np.int32), np.array([p[1] for p in pairs], np.int32))


def _attn_scores(q, kn, kr, i, j, tq, masked):
    k = jnp.concatenate([kn, kr], axis=1)
    s = lax.dot_general(q, k, (((1,), (1,)), ((), ())), preferred_element_type=F32)
    if masked:
        rc = (i * tq + lax.broadcasted_iota(jnp.int32, s.shape, 0)) // CHUNK
        cc = (j * tq + lax.broadcasted_iota(jnp.int32, s.shape, 1)) // CHUNK
        s = jnp.where(cc <= rc, s, NEG_INF)
    return s, k


def attn_fwd(q, kv, kr, proj, mg_off, *, heads, tq, name, host=None):
    S = q.shape[0]
    tq = _pick(S, tq, CHUNK)
    n = S // tq
    it, jt = _tri_tables(n, by_key=False)
    T = len(it)
    mg_blk = mg_off // MLA_DV
    n_hi, n_ho = (len(host.ins), len(host.out_shapes)) if host else (0, 0)

    hp = min(ATTN_FWD_HEADS_PER_STEP, heads)
    assert heads % hp == 0 and mg_blk % hp == 0
    groups = heads // hp

    def body(it_ref, jt_ref, q_ref, kv_ref, kr_ref, mg_ref, *rest):
        host_in, rest = rest[:n_hi], rest[n_hi:]
        (o_ref, a_ref, lse_ref), rest = rest[:3], rest[3:]
        host_out, rest = rest[:n_ho], rest[n_ho:]
        (m_sc, acc_sc), host_sems = rest[:2], rest[2:]
        h, t = pl.program_id(0), pl.program_id(1)
        i, j = it_ref[t], jt_ref[t]
        if host:
            _host_steps(host, (host_in, host_out, host_sems), h, t, groups, T, before=True)

        @pl.when(j == 0)
        def _():
            m_sc[...] = jnp.full(m_sc.shape, NEG_INF, F32)
            acc_sc[...] = jnp.zeros(acc_sc.shape, F32)

        def step(masked):
            kr = kr_ref[...]
            for u in range(hp):
                q = q_ref[:, u * MLA_QW : (u + 1) * MLA_QW]
                kn = kv_ref[:, u * MLA_QW : u * MLA_QW + MLA_NOPE]
                v = kv_ref[:, u * MLA_QW + MLA_NOPE : (u + 1) * MLA_QW]
                s, _ = _attn_scores(q, kn, kr, i, j, tq, masked)
                m_prev = m_sc[u]
                m_new = jnp.maximum(m_prev, jnp.max(s, axis=1, keepdims=True))
                p = jnp.exp2((s - m_new).astype(BF16))
                alpha = jnp.exp2(m_prev - m_new)
                v_ones = jnp.concatenate([v, jnp.ones(v.shape, BF16)], axis=1)
                acc_sc[u] = alpha * acc_sc[u] + jnp.dot(p, v_ones, preferred_element_type=F32)
                m_sc[u] = m_new

        @pl.when(j < i)
        def _():
            step(False)

        @pl.when(j == i)
        def _():
            step(True)
            for u in range(hp):
                cols = slice(u * MLA_DV, (u + 1) * MLA_DV)
                l = acc_sc[u, :, MLA_DV:]
                o = acc_sc[u, :, :MLA_DV] / l
                o_ref[:, cols] = o
                a_ref[:, cols] = (o * _silu(mg_ref[:, cols].astype(F32))).astype(a_ref.dtype)
                lse_ref[u] = m_sc[u] + jnp.log2(l[:, :1])

        if host:
            _host_steps(host, (host_in, host_out, host_sems), h, t, groups, T, before=False)

    grid_spec = pltpu.PrefetchScalarGridSpec(
        num_scalar_prefetch=2,
        grid=(groups, T),
        in_specs=[
            pl.BlockSpec((tq, hp * MLA_QW), lambda h, t, it, jt: (it[t], h)),
            pl.BlockSpec((tq, hp * MLA_QW), lambda h, t, it, jt: (jt[t], h)),
            pl.BlockSpec((tq, LANE), lambda h, t, it, jt: (jt[t], 0)),
            pl.BlockSpec((tq, hp * MLA_DV), lambda h, t, it, jt: (it[t], mg_blk // hp + h)),
            *(host.in_specs if host else []),
        ],
        out_specs=[
            pl.BlockSpec((tq, hp * MLA_DV), lambda h, t, it, jt: (it[t], h)),
            pl.BlockSpec((tq, hp * MLA_DV), lambda h, t, it, jt: (it[t], h)),
            pl.BlockSpec((hp, tq, 1), lambda h, t, it, jt: (h, it[t], 0)),
            *(host.out_specs if host else []),
        ],
        scratch_shapes=[
            pltpu.VMEM((hp, tq, 1), F32),
            pltpu.VMEM((hp, tq, 2 * MLA_DV), F32),
            *(host.scratch if host else []),
        ],
    )
    return _pcall(
        body,
        name=name,
        grid_spec=grid_spec,
        out_shape=[
            jax.ShapeDtypeStruct((S, heads * MLA_DV), F32),
            jax.ShapeDtypeStruct((S, heads * MLA_DV), BF16),
            jax.ShapeDtypeStruct((heads, S, 1), F32),
            *(host.out_shapes if host else []),
        ],
        compiler_params=_params(2),
    )(jnp.asarray(it), jnp.asarray(jt), q, kv, kr, proj, *(host.ins if host else []))


def attn_bwd(q, kv, kr, do, o, lse, *, heads, tq, name, host=None):
    S = q.shape[0]
    tq = _pick(S, tq, CHUNK)
    n = S // tq
    it, jt = _tri_tables(n, by_key=True)
    T = len(it)
    n_hi, n_ho = (len(host.ins), len(host.out_shapes)) if host else (0, 0)

    hp = ATTN_HEADS_PER_STEP
    assert heads % hp == 0
    groups = heads // hp

    def body(it_ref, jt_ref, q_ref, kv_ref, kr_ref, do_ref, o_ref, lse_ref, *rest):
        host_in, rest = rest[:n_hi], rest[n_hi:]
        (dq_ref, dkv_ref, dkr_ref), rest = rest[:3], rest[3:]
        host_out, rest = rest[:n_ho], rest[n_ho:]
        (dq_acc, dk_sc, dv_sc), host_sems = rest[:3], rest[3:]
        h, t = pl.program_id(0), pl.program_id(1)
        i, j = it_ref[t], jt_ref[t]
        if host:
            _host_steps(host, (host_in, host_out, host_sems), h, t, groups, T, before=True)

        @pl.when(t == 0)
        def _():
            dq_acc[...] = jnp.zeros(dq_acc.shape, F32)

        @pl.when(i == j)
        def _():
            dk_sc[...] = jnp.zeros(dk_sc.shape, F32)
            dv_sc[...] = jnp.zeros(dv_sc.shape, F32)

        def step(masked):
            kr = kr_ref[...]
            rows = pl.ds(pl.multiple_of(i * tq, tq), tq)
            for u in range(hp):
                wide, narrow = slice(u * MLA_QW, (u + 1) * MLA_QW), slice(u * MLA_DV, (u + 1) * MLA_DV)
                q_blk, do_blk = q_ref[:, wide], do_ref[:, narrow]
                kn = kv_ref[:, u * MLA_QW : u * MLA_QW + MLA_NOPE]
                v = kv_ref[:, u * MLA_QW + MLA_NOPE : (u + 1) * MLA_QW]
                s, k = _attn_scores(q_blk, kn, kr, i, j, tq, masked)
                p = jnp.exp2((s - lse_ref[u]).astype(BF16))
                delta = jnp.sum(do_blk.astype(F32) * o_ref[:, narrow], axis=1, keepdims=True)
                dp = lax.dot_general(do_blk, v, _NT, preferred_element_type=F32)
                ds = p * (dp - delta).astype(BF16)
                dv_sc[u] += lax.dot_general(p, do_blk, _TN, preferred_element_type=F32)
                dk_sc[u] += lax.dot_general(ds, q_blk, _TN, preferred_element_type=F32)
                dq_acc[rows, wide] += jnp.dot(ds, k, preferred_element_type=F32)

        @pl.when(i == j)
        def _():
            step(True)

        @pl.when(i > j)
        def _():
            step(False)

        @pl.when(i == n - 1)
        def _():
            for u in range(hp):
                dkv_ref[:, u * MLA_QW : u * MLA_QW + MLA_NOPE] = (dk_sc[u, :, :MLA_NOPE] * LN2).astype(dkv_ref.dtype)
                dkv_ref[:, u * MLA_QW + MLA_NOPE : (u + 1) * MLA_QW] = dv_sc[u].astype(dkv_ref.dtype)
                dkr_ref[u] = dk_sc[u, :, MLA_NOPE:] * LN2

        @pl.when(t == T - 1)
        def _():
            dq_ref[...] = (dq_acc[...] * MLA_SCALE).astype(dq_ref.dtype)

        if host:
            _host_steps(host, (host_in, host_out, host_sems), h, t, groups, T, before=False)

    grid_spec = pltpu.PrefetchScalarGridSpec(
        num_scalar_prefetch=2,
        grid=(groups, T),
        in_specs=[
            pl.BlockSpec((tq, hp * MLA_QW), lambda h, t, it, jt: (it[t], h)),
            pl.BlockSpec((tq, hp * MLA_QW), lambda h, t, it, jt: (jt[t], h)),
            pl.BlockSpec((tq, LANE), lambda h, t, it, jt: (jt[t], 0)),
            pl.BlockSpec((tq, hp * MLA_DV), lambda h, t, it, jt: (it[t], h)),
            pl.BlockSpec((tq, hp * MLA_DV), lambda h, t, it, jt: (it[t], h)),
            pl.BlockSpec((hp, tq, 1), lambda h, t, it, jt: (h, it[t], 0)),
            *(host.in_specs if host else []),
        ],
        out_specs=[
            pl.BlockSpec((S, hp * MLA_QW), lambda h, t, it, jt: (0, h)),
            pl.BlockSpec((tq, hp * MLA_QW), lambda h, t, it, jt: (jt[t], h)),
            pl.BlockSpec((hp, tq, LANE), lambda h, t, it, jt: (h, jt[t], 0)),
            *(host.out_specs if host else []),
        ],
        scratch_shapes=[
            pltpu.VMEM((S, hp * MLA_QW), F32),
            pltpu.VMEM((hp, tq, MLA_QW), F32),
            pltpu.VMEM((hp, tq, MLA_DV), F32),
            *(host.scratch if host else []),
        ],
    )
    return _pcall(
        body,
        name=name,
        grid_spec=grid_spec,
        out_shape=[
            jax.ShapeDtypeStruct((S, heads * MLA_QW), BF16),
            jax.ShapeDtypeStruct((S, heads * (MLA_NOPE + MLA_DV)), BF16),
            jax.ShapeDtypeStruct((heads, S, LANE), F32),
            *(host.out_shapes if host else []),
        ],
        compiler_params=_params(2),
    )(jnp.asarray(it), jnp.asarray(jt), q, kv, kr, do, o, lse, *(host.ins if host else []))


def _host_steps(host, refs, h, t, heads, n_steps, before):
    if before:

        @pl.when((h == 0) & (t == 0))
        def _():
            host.start(*refs)

        @pl.when((h == heads // 2) & (t == n_steps // 2))
        def _():
            host.relay(*refs)

    else:

        @pl.when((h == heads - 1) & (t == n_steps - 1))
        def _():
            host.finish(*refs)


def _ret_consts(heads):
    h = np.arange(heads, dtype=np.float32)
    lg = np.log(np.float32(1.0) - np.float32(2.0) ** (np.float32(-5.0) - h)).astype(np.float32)
    idx = np.arange(CHUNK, dtype=np.float32)
    dmat = np.exp(np.abs(idx[:, None] - idx[None, :])[None] * lg[:, None, None]).astype(np.float32)
    xi = np.exp((idx + 1.0)[None, :] * lg[:, None]).astype(np.float32)
    zeta = np.exp((CHUNK - 1.0 - idx)[None, :] * lg[:, None]).astype(np.float32)
    dec = np.exp(np.float32(CHUNK) * lg).astype(np.float32)
    xi = np.broadcast_to(xi[:, :, None], (heads, CHUNK, RET_DK)).copy()
    zeta = np.broadcast_to(zeta[:, :, None], (heads, CHUNK, RET_DK)).copy()
    dec = np.broadcast_to(dec[:, None, None], (heads, 8, LANE)).copy()
    return jnp.asarray(dmat), jnp.asarray(xi), jnp.asarray(zeta), jnp.asarray(dec)


_NT = (((1,), (1,)), ((), ()))
_TN = (((0,), (0,)), ((), ()))


def _dot(a, b, dn=(((1,), (0,)), ((), ()))):
    return lax.dot_general(a.astype(BF16), b.astype(BF16), dn, preferred_element_type=F32)


def _const_specs(hp):
    return [
        pl.BlockSpec((hp, CHUNK, CHUNK), lambda h, b: (h, 0, 0)),
        pl.BlockSpec((hp, CHUNK, RET_DK), lambda h, b: (h, 0, 0)),
        pl.BlockSpec((hp, CHUNK, RET_DK), lambda h, b: (h, 0, 0)),
        pl.BlockSpec((hp, 8, LANE), lambda h, b: (h, 0, 0)),
    ]


def _ret_group(heads, *blks):
    hp = min(RET_HEADS_PER_STEP, heads)
    assert heads % hp == 0 and all(b % hp == 0 for b in blks)
    return hp


def ret_fwd(q, k, proj, v_off, rg_off, *, heads, tb, name):
    S = q.shape[0]
    tb = _pick(S, tb, CHUNK)
    cpb, nb = tb // CHUNK, S // tb
    v_blk, rg_blk = v_off // RET_DV, rg_off // RET_DV
    hp = _ret_group(heads, v_blk, rg_blk)

    def body(q_ref, k_ref, v_ref, rg_ref, dm_ref, xi_ref, ze_ref, dec_ref, o_ref, a_ref, st_ref, r_sc):
        @pl.when(pl.program_id(1) == 0)
        def _():
            r_sc[...] = jnp.zeros(r_sc.shape, F32)

        for c in range(cpb):
            sl = pl.ds(c * CHUNK, CHUNK)
            for u in range(hp):
                dm, xi, ze, dec = dm_ref[u], xi_ref[u], ze_ref[u], dec_ref[u, 0:1, 0:1]
                narrow, wide = slice(u * RET_DK, (u + 1) * RET_DK), slice(u * RET_DV, (u + 1) * RET_DV)
                qc, kc, vc = q_ref[sl, narrow], k_ref[sl, narrow], v_ref[sl, wide]
                r = r_sc[u]
                st_ref[u, c] = r.astype(BF16)
                s = _dot(qc, kc, _NT) * dm
                o = _dot(s, vc) + _dot(qc.astype(F32) * xi, r)
                r_sc[u] = r * dec + _dot(kc.astype(F32) * ze, vc, _TN)
                mu = jnp.mean(o, axis=1, keepdims=True)
                d = o - mu
                n = d * lax.rsqrt(jnp.mean(d * d, axis=1, keepdims=True) + EPS)
                o_ref[sl, wide] = o
                a_ref[sl, wide] = (n * _silu(rg_ref[sl, wide].astype(F32))).astype(a_ref.dtype)

    return _pcall(
        body,
        name=name,
        grid=(heads // hp, nb),
        in_specs=[
            pl.BlockSpec((tb, hp * RET_DK), lambda h, b: (b, h)),
            pl.BlockSpec((tb, hp * RET_DK), lambda h, b: (b, h)),
            pl.BlockSpec((tb, hp * RET_DV), lambda h, b: (b, v_blk // hp + h)),
            pl.BlockSpec((tb, hp * RET_DV), lambda h, b: (b, rg_blk // hp + h)),
            *_const_specs(hp),
        ],
        out_specs=[
            pl.BlockSpec((tb, hp * RET_DV), lambda h, b: (b, h)),
            pl.BlockSpec((tb, hp * RET_DV), lambda h, b: (b, h)),
            pl.BlockSpec((hp, cpb, RET_DK, RET_DV), lambda h, b: (h, b, 0, 0)),
        ],
        out_shape=[
            jax.ShapeDtypeStruct((S, heads * RET_DV), F32),
            jax.ShapeDtypeStruct((S, heads * RET_DV), BF16),
            jax.ShapeDtypeStruct((heads, S // CHUNK, RET_DK, RET_DV), BF16),
        ],
        scratch_shapes=[pltpu.VMEM((hp, RET_DK, RET_DV), F32)],
        compiler_params=_params(2),
    )(q, k, proj, proj, *_ret_consts(heads))


def ret_bwd(q, k, proj, v_off, states, do, *, heads, tb, name, into):
    S = q.shape[0]
    tb = _pick(S, tb, CHUNK)
    cpb, nb = tb // CHUNK, S // tb
    v_blk = v_off // RET_DV
    hp = _ret_group(heads, v_blk)
    target, dv_off = into
    dv_blk = dv_off // (hp * RET_DV)
    assert dv_off % (hp * RET_DV) == 0

    def body(q_ref, k_ref, v_ref, st_ref, do_ref, dm_ref, xi_ref, ze_ref, dec_ref, _, dq_ref, dk_ref, dv_ref, dr_sc):
        @pl.when(pl.program_id(1) == 0)
        def _():
            dr_sc[...] = jnp.zeros(dr_sc.shape, F32)

        for c in reversed(range(cpb)):
            sl = pl.ds(c * CHUNK, CHUNK)
            for u in range(hp):
                dm, xi, ze, dec = dm_ref[u], xi_ref[u], ze_ref[u], dec_ref[u, 0:1, 0:1]
                narrow, wide = slice(u * RET_DK, (u + 1) * RET_DK), slice(u * RET_DV, (u + 1) * RET_DV)
                qc, kc, vc, doc = q_ref[sl, narrow], k_ref[sl, narrow], v_ref[sl, wide], do_ref[sl, wide]
                r_prev = st_ref[u, c]
                dr = dr_sc[u]
                a = _dot(qc, kc, _NT) * dm
                ds = _dot(doc, vc, _NT) * dm
                kz = kc.astype(F32) * ze
                dq_ref[sl, narrow] = (_dot(ds, kc) + _dot(doc, r_prev, _NT) * xi).astype(dq_ref.dtype)
                dk_ref[sl, narrow] = (_dot(ds, qc, _TN) + _dot(vc, dr, _NT) * ze).astype(dk_ref.dtype)
                dv_ref[sl, wide] = (_dot(a, doc, _TN) + _dot(kz, dr)).astype(dv_ref.dtype)
                dr_sc[u] = dr * dec + _dot(qc.astype(F32) * xi, doc, _TN)

    rev = lambda h, b: (nb - 1 - b, h)
    return _pcall(
        body,
        name=name,
        grid=(heads // hp, nb),
        in_specs=[
            pl.BlockSpec((tb, hp * RET_DK), rev),
            pl.BlockSpec((tb, hp * RET_DK), rev),
            pl.BlockSpec((tb, hp * RET_DV), lambda h, b: (nb - 1 - b, v_blk // hp + h)),
            pl.BlockSpec((hp, cpb, RET_DK, RET_DV), lambda h, b: (h, nb - 1 - b, 0, 0)),
            pl.BlockSpec((tb, hp * RET_DV), rev),
            *_const_specs(hp),
            pl.BlockSpec(memory_space=pl.ANY),
        ],
        out_specs=[
            pl.BlockSpec((tb, hp * RET_DK), rev),
            pl.BlockSpec((tb, hp * RET_DK), rev),
            pl.BlockSpec((tb, hp * RET_DV), lambda h, b: (nb - 1 - b, dv_blk + h)),
        ],
        out_shape=[
            jax.ShapeDtypeStruct((S, heads * RET_DK), BF16),
            jax.ShapeDtypeStruct((S, heads * RET_DK), BF16),
            jax.ShapeDtypeStruct(target.shape, target.dtype),
        ],
        input_output_aliases={9: 2},
        scratch_shapes=[pltpu.VMEM((hp, RET_DK, RET_DV), F32)],
        compiler_params=_params(2),
    )(q, k, proj, states, do, *_ret_consts(heads), target)


def _me():
    return (lax.axis_index("x"), lax.axis_index("y"), lax.axis_index("c"))


def _comm(name, ins, out_shapes, n_local, n_remote, plan):
    return run_exchange(name, Exchange(ins, out_shapes, n_local, n_remote, plan))


def run_exchange(name, ex):
    def body(*refs):
        ex.start(*ex.split(refs))
        ex.relay(*ex.split(refs))
        ex.finish(*ex.split(refs))

    return _pcall(
        body, name=name, in_specs=ex.in_specs, out_specs=ex.out_specs, out_shape=ex.out_shapes, scratch_shapes=ex.scratch
    )(*ex.ins)


class Exchange:
    def __init__(self, ins, out_shapes, n_local, n_remote, plan):
        self.ins, self.out_shapes, self.n_local, self.n_remote, self.plan = list(ins), list(out_shapes), n_local, n_remote, plan
        any_spec = pl.BlockSpec(memory_space=pl.ANY)
        self.in_specs, self.out_specs = [any_spec] * len(self.ins), [any_spec] * len(self.out_shapes)
        self.scratch = [
            pltpu.SemaphoreType.DMA((n_remote,)),
            pltpu.SemaphoreType.DMA((n_remote,)),
            pltpu.SemaphoreType.DMA((max(n_local, 1),)),
        ]

    def split(self, refs):
        n_in, n_out = len(self.ins), len(self.out_shapes)
        return refs[:n_in], refs[n_in : n_in + n_out], refs[n_in + n_out :]

    def _copies(self, in_refs, out_refs, sems):
        send_sems, recv_sems, local_sems = sems
        me = _me()
        local_plan, remote_plan = self.plan(me, in_refs, out_refs)
        assert len(local_plan) == self.n_local and len(remote_plan) == self.n_remote, (len(local_plan), len(remote_plan))
        def local_copy(n):
            src, dst = local_plan[n]
            return pltpu.make_async_copy(src, dst, local_sems.at[n])

        def send(n):
            src, dst, peer, _, _ = remote_plan[n]
            return pltpu.make_async_remote_copy(
                src_ref=src, dst_ref=dst, send_sem=send_sems.at[n], recv_sem=recv_sems.at[n], device_id=peer, device_id_type=MESH
            )

        def arrival(n):
            src, _, _, landing, _ = remote_plan[n]
            return pltpu.make_async_remote_copy(
                src_ref=src, dst_ref=landing, send_sem=send_sems.at[n], recv_sem=recv_sems.at[n], device_id=me, device_id_type=MESH
            )

        after = [a for (_, _, _, _, a) in remote_plan]
        assert all(a is None or a < n for n, a in enumerate(after))
        return local_copy, send, arrival, after

    def start(self, in_refs, out_refs, sems):
        local_copy, send, _, after = self._copies(in_refs, out_refs, sems)
        for n in range(self.n_local):
            local_copy(n).start()
        for n, a in enumerate(after):
            if a is None:
                send(n).start()

    def relay(self, in_refs, out_refs, sems):
        _, send, arrival, after = self._copies(in_refs, out_refs, sems)
        for n in sorted({a for a in after if a is not None}):
            arrival(n).wait_recv()
            for m, a in enumerate(after):
                if a == n:
                    send(m).start()

    def finish(self, in_refs, out_refs, sems):
        local_copy, send, arrival, after = self._copies(in_refs, out_refs, sems)
        relayed = {a for a in after if a is not None}
        for n in range(self.n_remote):
            if n not in relayed:
                arrival(n).wait_recv()
        for n in range(self.n_remote):
            send(n).wait_send()
        for n in range(self.n_local):
            local_copy(n).wait()


_CHIP_FLIPS = ((1, 0), (0, 1), (1, 1))


class Placement:
    def __init__(self, rows, full_rows, cuts=(), offset=None, zero_rows=None):
        edges = [0, *sorted(cuts), rows]
        self.rows, self.full_rows = rows, full_rows
        self.runs = [(a, b - a) for a, b in zip(edges[:-1], edges[1:]) if b > a]
        self.offset = offset if offset is not None else (lambda j, start: j * rows + start)
        self.zero_rows = zero_rows

    def at(self, ref_2d_plus, j, start, size):
        return pl.ds(pl.multiple_of(self.offset(j, start), 16), size)


def gather_layer(prepped, zeros, places, l, name):
    n_w = len(prepped)
    n_chip = sum(len(p.runs) for p in places) * 3
    n_own = sum(len(p.runs) for p in places) * 2 + sum(2 for p in places if p.zero_rows)

    def plan(me, in_refs, out_refs):
        x, y, c = me
        j = 2 * x + y
        sib = (x, y, 1 - c)
        from_chips, own, forwards = [], [], []
        for i, o, p in zip(in_refs[:n_w], out_refs, places):
            for start, size in p.runs:
                for dx, dy in _CHIP_FLIPS:
                    px, py = x ^ dx, y ^ dy
                    jp = 2 * px + py
                    mine_there = o.at[c, p.at(o, j, start, size)]
                    theirs_here = o.at[c, p.at(o, jp, start, size)]
                    n = len(from_chips)
                    from_chips.append((i.at[l, c, pl.ds(start, size)], mine_there, (px, py, c), theirs_here, None))
                    forwards.append((theirs_here, theirs_here, sib, o.at[1 - c, p.at(o, jp, start, size)], n))
                for h in range(2):
                    place = o.at[h, p.at(o, j, start, size)]
                    own.append((i.at[l, h, pl.ds(start, size)], place, sib, place, None))
            if p.zero_rows:
                for h in range(2):
                    place = o.at[h, pl.ds(p.zero_rows[0], p.zero_rows[1])]
                    own.append((in_refs[n_w], place, sib, place, None))
        return [], from_chips + own + forwards

    shapes = [jax.ShapeDtypeStruct((2, p.full_rows, w.shape[3]), w.dtype) for w, p in zip(prepped, places)]
    ins = [*prepped, zeros] if any(p.zero_rows for p in places) else list(prepped)
    ex = Exchange(ins, shapes, 0, 2 * n_chip + n_own, plan)
    return ex if name is None else run_exchange(name, ex)


def sibling_scatter(arrays, name):
    def plan(me, in_refs, out_refs):
        x, y, c = me
        return [], [(i.at[1 - c], o, (x, y, 1 - c), o, None) for i, o in zip(in_refs, out_refs)]

    shapes = [jax.ShapeDtypeStruct(a.shape[1:], a.dtype) for a in arrays]
    ex = Exchange(arrays, shapes, 0, len(arrays), plan)
    return ex if name is None else run_exchange(name, ex)


def sibling_swap(arrays, name):
    def plan(me, in_refs, out_refs):
        x, y, c = me
        return [], [(i, o, (x, y, 1 - c), o, None) for i, o in zip(in_refs, out_refs)]

    shapes = [jax.ShapeDtypeStruct(a.shape, a.dtype) for a in arrays]
    ex = Exchange(arrays, shapes, 0, len(arrays), plan)
    return ex if name is None else run_exchange(name, ex)


def scatter_to_chips(arrays, places, name):
    def plan(me, in_refs, out_refs):
        x, y, c = me
        j = 2 * x + y
        local_plan, remote_plan = [], []
        for i, o, p in zip(in_refs, out_refs, places):
            for start, size in p.runs:
                local_plan.append((i.at[p.at(i, j, start, size)], o.at[3, pl.ds(start, size)]))
                for k, (dx, dy) in enumerate(_CHIP_FLIPS):
                    px, py = x ^ dx, y ^ dy
                    landing = o.at[k, pl.ds(start, size)]
                    remote_plan.append((i.at[p.at(i, 2 * px + py, start, size)], landing, (px, py, c), landing, None))
        return local_plan, remote_plan

    n_runs = sum(len(p.runs) for p in places)
    shapes = [jax.ShapeDtypeStruct((4, p.rows, a.shape[1]), a.dtype) for a, p in zip(arrays, places)]
    ex = Exchange(arrays, shapes, n_runs, 3 * n_runs, plan)
    return ex if name is None else run_exchange(name, ex)


def allgather8(block, name):
    def plan(me, in_refs, out_refs):
        x, y, c = me
        (i,), (o,) = in_refs, out_refs
        mine = 4 * x + 2 * y + c
        remote_plan = []
        for flip in range(1, 8):
            px, py, pc = x ^ (flip >> 2), y ^ ((flip >> 1) & 1), c ^ (flip & 1)
            remote_plan.append((i, o.at[mine], (px, py, pc), o.at[4 * px + 2 * py + pc], None))
        return [(i, o.at[mine])], remote_plan

    return _comm(name, [block], [jax.ShapeDtypeStruct((8, *block.shape), block.dtype)], 1, 7, plan)[0]


def transpose_split(a, out_dtype, name):
    L, R, C = a.shape
    rh = R // 2
    rt = _pick(rh, 256)
    n = rh // rt

    def body(x_ref, o_ref):
        o_ref[0, 0] = x_ref[0].T.astype(o_ref.dtype)

    return _pcall(
        body,
        name=name,
        grid=(L, 2, n),
        in_specs=[pl.BlockSpec((1, rt, C), lambda l, h, i: (l, h * n + i, 0))],
        out_specs=pl.BlockSpec((1, 1, C, rt), lambda l, h, i: (l, h, 0, i)),
        out_shape=jax.ShapeDtypeStruct((L, 2, C, rh), out_dtype),
        compiler_params=_params(3),
    )(a)


def cast_split(a, out_dtype, name):
    L, R, C = a.shape
    rt = _pick(R, 512, 16)

    def body(x_ref, o_ref):
        o_ref[0, 0] = x_ref[0].astype(o_ref.dtype)

    return _pcall(
        body,
        name=name,
        grid=(L, 2, R // rt),
        in_specs=[pl.BlockSpec((1, rt, C // 2), lambda l, h, i: (l, i, h))],
        out_specs=pl.BlockSpec((1, 1, rt, C // 2), lambda l, h, i: (l, h, i, 0)),
        out_shape=jax.ShapeDtypeStruct((L, 2, R, C // 2), out_dtype),
        compiler_params=_params(3),
    )(a)


def untranspose_halves(mine, theirs, my_c, stack, l, n_layers, name):
    C, rh = mine.shape
    n_in = 2 if stack is None else 3
    rt = _pick(rh, 256)
    n = rh // rt

    def body(c_ref, a_ref, b_ref, *rest):
        o_ref = rest[-1]
        h = pl.program_id(0)
        o_ref[0] = jnp.where(h == c_ref[0], a_ref[...], b_ref[...]).T

    grid_spec = pltpu.PrefetchScalarGridSpec(
        num_scalar_prefetch=1,
        grid=(2, n),
        in_specs=[pl.BlockSpec((C, rt), lambda h, i, c: (0, i))] * 2
        + ([] if stack is None else [pl.BlockSpec(memory_space=pl.ANY)]),
        out_specs=pl.BlockSpec((1, rt, C), lambda h, i, c: (l, h * n + i, 0)),
    )
    return _pcall(
        body,
        name=name,
        grid_spec=grid_spec,
        out_shape=jax.ShapeDtypeStruct((n_layers, 2 * rh, C), F32),
        input_output_aliases={} if stack is None else {n_in: 0},
        compiler_params=_params(2),
    )(my_c.reshape(1), mine, theirs, *([] if stack is None else [stack]))


def merge_halves(mine, theirs, my_c, stack, l, n_layers, name):
    R, ch = mine.shape
    n_in = 2 if stack is None else 3
    rt = _pick(R, 512, 8)

    def body(c_ref, a_ref, b_ref, *rest):
        o_ref = rest[-1]
        o_ref[0] = jnp.where(pl.program_id(0) == c_ref[0], a_ref[...], b_ref[...])

    grid_spec = pltpu.PrefetchScalarGridSpec(
        num_scalar_prefetch=1,
        grid=(2, R // rt),
        in_specs=[pl.BlockSpec((rt, ch), lambda h, i, c: (i, 0))] * 2
        + ([] if stack is None else [pl.BlockSpec(memory_space=pl.ANY)]),
        out_specs=pl.BlockSpec((1, rt, ch), lambda h, i, c: (l, i, h)),
    )
    return _pcall(
        body,
        name=name,
        grid_spec=grid_spec,
        out_shape=jax.ShapeDtypeStruct((n_layers, R, 2 * ch), F32),
        input_output_aliases={} if stack is None else {n_in: 0},
        compiler_params=_params(2),
    )(my_c.reshape(1), mine, theirs, *([] if stack is None else [stack]))


def add_pair(g, r, my_c, name):
    _, rows, w = g.shape
    tm = _pick(rows, max(16, min(1024, (1 << 20) // w // 16 * 16)), 16)

    def body(c_ref, g_ref, r_ref, o_ref):
        o_ref[...] = (g_ref[0].astype(F32) + r_ref[...].astype(F32)).astype(o_ref.dtype)

    grid_spec = pltpu.PrefetchScalarGridSpec(
        num_scalar_prefetch=1,
        grid=(rows // tm,),
        in_specs=[pl.BlockSpec((1, tm, w), lambda i, c: (c[0], i, 0)), pl.BlockSpec((tm, w), lambda i, c: (i, 0))],
        out_specs=pl.BlockSpec((tm, w), lambda i, c: (i, 0)),
    )
    return _pcall(
        body, name=name, grid_spec=grid_spec, out_shape=jax.ShapeDtypeStruct((rows, w), BF16), compiler_params=_params(1)
    )(my_c.reshape(1), g, r)


class Dims:
    def __init__(self, S, D, L, ret_heads, mla_heads, q_rank, kv_rank):
        self.S, self.D, self.L, self.HR, self.HM, self.QR, self.KR = S, D, L, ret_heads, mla_heads, q_rank, kv_rank
        self.RQ, self.RV, self.MV = ret_heads * RET_DK, ret_heads * RET_DV, mla_heads * MLA_DV
        self.D_IN = 2 * self.RQ + 2 * self.RV + q_rank + kv_rank + MLA_ROPE + self.MV + 2 * D
        self.lo = 2 * self.RQ + 2 * self.RV
        self.mid = q_rank + kv_rank + MLA_ROPE
        self.DP = self.D_IN + LANE - MLA_ROPE
        self.o_rq, self.o_rk, self.o_rv, self.o_rg = 0, self.RQ, 2 * self.RQ, 2 * self.RQ + self.RV
        self.o_mg = self.lo
        self.o_bga = self.o_mg + self.MV
        self.o_bgb = self.o_bga + D
        self.o_cq = self.o_bgb + D
        self.o_ckv = self.o_cq + q_rank
        self.o_kr = self.o_ckv + kv_rank

    def pad_uq_rows(self, w):
        heads = w.shape[-2] // (MLA_NOPE + MLA_ROPE)
        w = w.reshape(*w.shape[:-2], heads, MLA_NOPE + MLA_ROPE, w.shape[-1])
        w = jnp.pad(w, [(0, 0)] * (w.ndim - 2) + [(0, MLA_QW - MLA_NOPE - MLA_ROPE), (0, 0)])
        return w.reshape(*w.shape[:-3], heads * MLA_QW, w.shape[-1])

    def unpad_uq_rows(self, w):
        heads = w.shape[-2] // MLA_QW
        w = w.reshape(*w.shape[:-2], heads, MLA_QW, w.shape[-1])[..., : MLA_NOPE + MLA_ROPE, :]
        return w.reshape(*w.shape[:-3], heads * (MLA_NOPE + MLA_ROPE), w.shape[-1])

    def placements(self):
        rows_in = self.D_IN // 4
        lo, mid, n_hi = self.lo, self.mid, self.D_IN - self.lo - self.mid
        cuts = {b % rows_in for b in (lo, lo + mid)} - {0}

        def offset_in(j, start):
            g = j * rows_in + start
            return jnp.where(g < lo, g, jnp.where(g < lo + mid, g + n_hi, g - mid))

        return [
            Placement(rows_in, self.DP, cuts, offset_in, zero_rows=(self.D_IN, self.DP - self.D_IN)),
            Placement(self.HM * MLA_QW // 4, self.HM * MLA_QW),
            Placement(self.HM * (MLA_NOPE + MLA_DV) // 4, self.HM * (MLA_NOPE + MLA_DV)),
            Placement(self.RV // 4, self.RV),
            Placement(self.MV // 4, self.MV),
            Placement(self.D // 4, self.D),
        ]


def _rope_tables(positions, dim, width):
    inv = 1.0 / (ROPE_BASE ** (jnp.arange(0, dim, 2, dtype=F32) / dim))
    ang = positions.astype(F32)[:, None] * inv
    cos, sin = jnp.cos(ang), jnp.sin(ang)
    pad = jnp.zeros((positions.shape[0], width - dim), F32)
    return jnp.concatenate([cos, cos, pad], axis=1), jnp.concatenate([-sin, sin, pad], axis=1)


def _tiles(x):
    return [x[:, t * LANE : (t + 1) * LANE] for t in range(x.shape[1] // LANE)]


def _cat(parts):
    return parts[0] if len(parts) == 1 else jnp.concatenate(parts, axis=1)


def _rms(x, eps=EPS):
    return lax.rsqrt(jnp.mean(x * x, axis=1, keepdims=True) + eps)


def layer_fwd(dm, l, x, shift, scale, gate, g_norm, g_cq, g_ckv, w, tabs, host=None, host_in=None, late_keys=()):
    cos_r, sin_r, cos_m, sin_m = tabs
    nm = lambda s: f"l{l}_{s}"

    def f_norm(x, g, scale, shift):
        return [x * _rms(x) * g * (1.0 + scale) + shift], []

    (h,) = rowwise(f_norm, [x], [g_norm, scale, shift], [(dm.D, BF16)], tm=256, name=nm("norm"))
    proj, *late = _as_list(matmul(h, w["in"], tb=True, b_split=True, name=nm("mm_in"), tn=1920, host=host_in))
    w = dict(w, **dict(zip(late_keys, late)))

    def f_rope_ret(rq, rk, cos, sin):
        rq, rk = rq.astype(F32), rk.astype(F32)
        q = _cat([_rope_tile(t, cos, sin, RET_DK // 2) for t in _tiles(rq)])
        k = _cat([_rope_tile(t, cos, sin, RET_DK // 2) * (RET_DK**-0.5) for t in _tiles(rk)])
        return [q, k], []

    rq, rk = rowwise(
        f_rope_ret,
        [win(proj, dm.o_rq, dm.RQ), win(proj, dm.o_rk, dm.RQ), cos_r, sin_r],
        [],
        [(dm.RQ, BF16)] * 2,
        tm=512,
        name=nm("rope_ret"),
    )
    o_ret, a_ret, states = ret_fwd(rq, rk, proj, dm.o_rv, dm.o_rg, heads=dm.HR, tb=RET_BLOCK, name=nm("ret_fwd"))

    def f_prep(cq, ckv, kr, cos, sin, g_cq, g_ckv):
        cq, ckv, kr = cq.astype(F32), ckv.astype(F32), kr.astype(F32)
        return [cq * _rms(cq) * g_cq, ckv * _rms(ckv) * g_ckv, _rope_tile(kr, cos, sin, MLA_ROPE // 2)], []

    cqn, ckvn, krr = rowwise(
        f_prep,
        [win(proj, dm.o_cq, dm.QR), win(proj, dm.o_ckv, dm.KR), win(proj, dm.o_kr, LANE), cos_m, sin_m],
        [g_cq, g_ckv],
        [(dm.QR, BF16), (dm.KR, BF16), (LANE, BF16)],
        tm=512,
        name=nm("mla_prep"),
    )
    q_raw = matmul(cqn, w["uq"], tb=True, b_split=True, name=nm("mm_uq"), tn=4096)
    kv = matmul(ckvn, w["ukv"], tb=True, b_split=True, name=nm("mm_ukv"), tn=4096)

    def f_rope_q(q, cos, sin):
        t = _tiles(q.astype(F32))
        rot = [t[n] if n % 2 == 0 else _rope_tile(t[n], cos, sin, MLA_ROPE // 2) for n in range(len(t))]
        return [_cat([r * (MLA_SCALE * LOG2E) for r in rot])], []

    (q,) = rowwise(f_rope_q, [q_raw, cos_m, sin_m], [], [(dm.HM * MLA_QW, BF16)], tm=256, name=nm("rope_q"))
    o_mla, a_mla, lse, *hosted = attn_fwd(
        q, kv, krr, proj, dm.o_mg, heads=dm.HM, tq=ATTN_BLOCK, name=nm("attn_fwd"), host=host
    )

    y_ret = matmul(a_ret, w["ret"], b_split=True, name=nm("mm_ret"))
    y_mla = matmul(a_mla, w["mla"], b_split=True, name=nm("mm_mla"))

    def f_merge(y_ret, y_mla, bga, bgb):
        return [_sigmoid(bga.astype(F32)) * y_ret.astype(F32) + _sigmoid(bgb.astype(F32)) * y_mla.astype(F32)], []

    (merged,) = rowwise(
        f_merge, [y_ret, y_mla, win(proj, dm.o_bga, dm.D), win(proj, dm.o_bgb, dm.D)], [], [(dm.D, BF16)], tm=256, name=nm("merge")
    )
    out = matmul(merged, w["out"], b_split=True, out_dtype=F32, name=nm("mm_out"))

    def f_resid(x, out, gate):
        return [x + gate * out], []

    (x_new,) = rowwise(f_resid, [x, out], [gate], [(dm.D, F32)], tm=256, name=nm("resid"))
    saved = dict(
        x=x, h=h, proj=proj, rq=rq, rk=rk, o_ret=o_ret, a_ret=a_ret, states=states, cqn=cqn, ckvn=ckvn, krr=krr, q=q, kv=kv,
        o_mla=o_mla, a_mla=a_mla, lse=lse, y_ret=y_ret, y_mla=y_mla, merged=merged, out=out,
    )
    return x_new, saved, hosted, w


def _as_list(v):
    return list(v) if isinstance(v, (list, tuple)) else [v]


def layer_bwd(
    dm, l, dx_out, sv, shift, scale, gate, g_norm, g_cq, g_ckv, w, tabs, host=None, exchange_dh=None, exchange_dw=None
):
    cos_r, sin_r, cos_m, sin_m = tabs
    nm = lambda s: f"l{l}_{s}"
    proj = sv["proj"]

    def b_resid(dx, out, gate):
        return [dx * gate], [_sum0(dx * out)]

    dout, d_gate = rowwise(b_resid, [dx_out, sv["out"]], [gate], [(dm.D, BF16)], [dm.D], tm=256, name=nm("resid_bwd"))
    dmerged = matmul(dout, w["out"], tb=True, b_split=True, name=nm("mm_dmerged"))
    dw_out = matmul(sv["merged"], dout, ta=True, out_split=True, tn=2048, name=nm("mm_dw_out"))

    def b_merge(dmg, y_ret, y_mla, bga, bgb):
        dmg = dmg.astype(F32)
        ga, gb = _sigmoid(bga.astype(F32)), _sigmoid(bgb.astype(F32))
        y_ret, y_mla = y_ret.astype(F32), y_mla.astype(F32)
        d_gates = jnp.concatenate([dmg * y_ret * ga * (1.0 - ga), dmg * y_mla * gb * (1.0 - gb)], axis=1)
        return [d_gates, dmg * ga, dmg * gb], []

    dproj, dy_ret, dy_mla = rowwise(
        b_merge,
        [dmerged, sv["y_ret"], sv["y_mla"], win(proj, dm.o_bga, dm.D), win(proj, dm.o_bgb, dm.D)],
        [],
        [(2 * dm.D, BF16), (dm.D, BF16), (dm.D, BF16)],
        tm=256,
        name=nm("merge_bwd"),
        into=(dm.DP, dm.o_bga),
    )
    da_ret = matmul(dy_ret, w["ret"], tb=True, b_split=True, name=nm("mm_da_ret"))
    dw_ret = matmul(sv["a_ret"], dy_ret, ta=True, out_split=True, tn=2048, name=nm("mm_dw_ret"))
    da_mla = matmul(dy_mla, w["mla"], tb=True, b_split=True, name=nm("mm_da_mla"))
    dw_mla = matmul(sv["a_mla"], dy_mla, ta=True, out_split=True, tn=2048, name=nm("mm_dw_mla"))

    def b_ret_gate(da, rg, o):
        da, rg = da.astype(F32), rg.astype(F32)
        do_parts, drg_parts = [], []
        for hh in range(dm.HR):
            sl = slice(hh * RET_DV, (hh + 1) * RET_DV)
            oh, dah, rgh = o[:, sl], da[:, sl], rg[:, sl]
            mu = jnp.mean(oh, axis=1, keepdims=True)
            d = oh - mu
            r = lax.rsqrt(jnp.mean(d * d, axis=1, keepdims=True) + EPS)
            n = d * r
            dn = dah * _silu(rgh)
            drg_parts.append(dah * n * _dsilu(rgh))
            do_parts.append(r * (dn - jnp.mean(dn, axis=1, keepdims=True) - n * jnp.mean(dn * n, axis=1, keepdims=True)))
        return [_cat(drg_parts), _cat(do_parts)], []

    dproj, do_ret = rowwise(
        b_ret_gate, [da_ret, win(proj, dm.o_rg, dm.RV), sv["o_ret"]], [], [(dm.RV, BF16)] * 2, tm=256, name=nm("ret_gate_bwd"),
        into=(dproj, dm.o_rg),
    )
    dq_rot, dk_rot, dproj = ret_bwd(
        sv["rq"], sv["rk"], proj, dm.o_rv, sv["states"], do_ret, heads=dm.HR, tb=RET_BLOCK, name=nm("ret_bwd"), into=(dproj, dm.o_rv)
    )

    def b_rope_ret(dq, dk, cos, sin):
        dq, dk = dq.astype(F32), dk.astype(F32)
        q = [_rope_tile(t, cos, sin, RET_DK // 2, inverse=True) for t in _tiles(dq)]
        k = [_rope_tile(t, cos, sin, RET_DK // 2, inverse=True) * (RET_DK**-0.5) for t in _tiles(dk)]
        return [_cat(q + k)], []

    (dproj,) = rowwise(
        b_rope_ret, [dq_rot, dk_rot, cos_r, sin_r], [], [(2 * dm.RQ, BF16)], tm=512, name=nm("rope_ret_bwd"), into=(dproj, dm.o_rq)
    )

    def b_mla_gate(da, mg, o):
        da, mg = da.astype(F32), mg.astype(F32)
        return [da * o * _dsilu(mg), da * _silu(mg)], []

    dproj, do_mla = rowwise(
        b_mla_gate, [da_mla, win(proj, dm.o_mg, dm.MV), sv["o_mla"]], [], [(dm.MV, BF16)] * 2, tm=256, name=nm("mla_gate_bwd"),
        into=(dproj, dm.o_mg),
    )
    dq_att, dkv, dkr_heads, *hosted = attn_bwd(
        sv["q"], sv["kv"], sv["krr"], do_mla, sv["o_mla"], sv["lse"], heads=dm.HM, tq=ATTN_BLOCK, name=nm("attn_bwd"), host=host
    )

    def b_rope_q(dq, cos, sin):
        t = _tiles(dq.astype(F32))
        return [_cat([t[n] if n % 2 == 0 else _rope_tile(t[n], cos, sin, MLA_ROPE // 2, inverse=True) for n in range(len(t))])], []

    (dq_raw,) = rowwise(b_rope_q, [dq_att, cos_m, sin_m], [], [(dm.HM * MLA_QW, BF16)], tm=256, name=nm("rope_q_bwd"))
    dcqn = matmul(dq_raw, w["uq"], b_split=True, name=nm("mm_dcqn"))
    dw_uq = matmul(dq_raw, sv["cqn"], ta=True, out_split=True, tn=2048, name=nm("mm_dw_uq"))
    dckvn = matmul(dkv, w["ukv"], b_split=True, name=nm("mm_dckvn"))
    dw_ukv = matmul(dkv, sv["ckvn"], ta=True, out_split=True, tn=2048, name=nm("mm_dw_ukv"))

    def b_prep(dcqn, dckvn, cq, ckv, cos, sin, dkr_h, g_cq, g_ckv):
        outs, accs = [], []
        for dn, z, g in ((dcqn, cq, g_cq), (dckvn, ckv, g_ckv)):
            dn, z = dn.astype(F32), z.astype(F32)
            n = z * _rms(z)
            dng = dn * g
            outs.append(_rms(z) * (dng - n * jnp.mean(dng * n, axis=1, keepdims=True)))
            accs.append(_sum0(dn * n))
        dkr = dkr_h[0]
        for hh in range(1, dm.HM):
            dkr = dkr + dkr_h[hh]
        return [_cat(outs), _rope_tile(dkr, cos, sin, MLA_ROPE // 2, inverse=True)], accs

    dproj, d_kr, dg_cq, dg_ckv = rowwise(
        b_prep,
        [dcqn, dckvn, win(proj, dm.o_cq, dm.QR), win(proj, dm.o_ckv, dm.KR), cos_m, sin_m, dkr_heads],
        [g_cq, g_ckv],
        [(dm.QR + dm.KR, BF16), (LANE, BF16)],
        [dm.QR, dm.KR],
        tm=256,
        name=nm("mla_prep_bwd"),
        into=(dproj, dm.o_cq),
    )
    (dproj,) = rowwise(lambda a: ([a], []), [d_kr], [], [(LANE, BF16)], tm=512, name=nm("place_dkr"), into=(dproj, dm.o_kr))
    dws = dict(w_uq=dw_uq, w_ukv=dw_ukv, w_ret_proj=dw_ret, w_mla_proj=dw_mla, w_out=dw_out)
    host_dw = exchange_dw(dws, hosted) if exchange_dw else None
    dw_in, *hosted_dw = _as_list(
        matmul(dproj, sv["h"], ta=True, out_split=True, tn=2048, name=nm("mm_dw_in"), host=host_dw)
    )
    dws["w_in"] = dw_in
    host_dh = exchange_dh(dws) if exchange_dh else None
    dh, *hosted_dh = _as_list(
        matmul(dproj, w["in"], b_split=True, out_dtype=F32, name=nm("mm_dh"), tn=2048, tk=1920, host=host_dh)
    )

    def b_norm(dh, x, dx_res, g, scale):
        r = _rms(x)
        xn = x * r
        dxn = dh * g * (1.0 + scale)
        dx = dx_res + r * (dxn - xn * jnp.mean(dxn * xn, axis=1, keepdims=True))
        return [dx], [_sum0(dh), _sum0(dh * xn * g), _sum0(dh * (1.0 + scale) * xn)]

    dx, d_shift, d_scale, dg_norm = rowwise(
        b_norm, [dh, sv["x"], dx_out], [g_norm, scale], [(dm.D, F32)], [dm.D] * 3, tm=256, name=nm("norm_bwd")
    )
    dvec = dict(mod=jnp.concatenate([d_shift, d_scale, d_gate], axis=1), g_norm=dg_norm, g_cq=dg_cq, g_ckv=dg_ckv)
    return dx, dws, dvec, hosted, hosted_dh, hosted_dw


def adamw(w, g, m, v, name):
    shape = w.shape
    cols = shape[-1]
    view = lambda a: a.reshape(-1, cols)

    def f(w, g, m, v):
        m = ADAM_B1 * m + (1.0 - ADAM_B1) * g
        v = ADAM_B2 * v + (1.0 - ADAM_B2) * (g * g)
        m_hat = m / (1.0 - ADAM_B1**ADAM_STEP)
        v_hat = v / (1.0 - ADAM_B2**ADAM_STEP)
        delta = -ADAM_LR * (m_hat / (jnp.sqrt(v_hat) + ADAM_EPS) + ADAM_WD * w)
        return [delta, m, v], []

    tm = max(8, min(512, (400_000 // cols) // 8 * 8))
    delta, m, v = rowwise(f, [view(w), view(g), view(m), view(v)], [], [(cols, F32)] * 3, tm=tm, name=name)
    return delta.reshape(shape), m.reshape(shape), v.reshape(shape)


def _add_rows(fn, rows, cols, dtype, name):
    tm = max(8, min(512, (400_000 // cols) // 8 * 8))
    return rowwise(lambda *a: ([fn(*a)], []), rows, [], [(cols, dtype)], tm=tm, name=name)[0]


BIG = ("w_in", "w_uq", "w_ukv", "w_ret_proj", "w_mla_proj", "w_out")
COL_SHARDED = ("w_in", "w_uq", "w_ukv")


def kernel(x, c, positions, w_mod, b_mod, g_norm, w_in, g_cq, g_ckv, w_uq, w_ukv, w_ret_proj, w_mla_proj, w_out, g_final, loss_target, m_w_mod, m_b_mod, m_g_norm, m_w_in, m_g_cq, m_g_ckv, m_w_uq, m_w_ukv, m_w_ret_proj, m_w_mla_proj, m_w_out, m_g_final, v_w_mod, v_b_mod, v_g_norm, v_w_in, v_g_cq, v_g_ckv, v_w_uq, v_w_ukv, v_w_ret_proj, v_w_mla_proj, v_w_out, v_g_final):
    weights = dict(w_mod=w_mod, b_mod=b_mod, g_norm=g_norm, w_in=w_in, g_cq=g_cq, g_ckv=g_ckv, w_uq=w_uq, w_ukv=w_ukv,
                   w_ret_proj=w_ret_proj, w_mla_proj=w_mla_proj, w_out=w_out, g_final=g_final)
    m_in = dict(w_mod=m_w_mod, b_mod=m_b_mod, g_norm=m_g_norm, w_in=m_w_in, g_cq=m_g_cq, g_ckv=m_g_ckv, w_uq=m_w_uq,
                w_ukv=m_w_ukv, w_ret_proj=m_w_ret_proj, w_mla_proj=m_w_mla_proj, w_out=m_w_out, g_final=m_g_final)
    v_in = dict(w_mod=v_w_mod, b_mod=v_b_mod, g_norm=v_g_norm, w_in=v_w_in, g_cq=v_g_cq, g_ckv=v_g_ckv, w_uq=v_w_uq,
                w_ukv=v_w_ukv, w_ret_proj=v_w_ret_proj, w_mla_proj=v_w_mla_proj, w_out=v_w_out, g_final=v_g_final)
    order = ("w_mod", "b_mod", "g_norm", "w_in", "g_cq", "g_ckv", "w_uq", "w_ukv", "w_ret_proj", "w_mla_proj", "w_out", "g_final")

    x = x[0]
    target = loss_target[0]
    S, D = x.shape
    L = w_mod.shape[0]
    dm = Dims(S, D, L, w_ret_proj.shape[1] * 4 // RET_DV, w_mla_proj.shape[1] * 4 // MLA_DV, g_cq.shape[1], g_ckv.shape[1])
    my_x, my_y, my_c = _me()
    my_chip = 2 * my_x + my_y
    my_dev = 2 * my_chip + my_c
    C3 = w_mod.shape[2]

    for table in (weights, m_in, v_in):
        table["w_in"] = jnp.swapaxes(table["w_in"], 1, 2)
    prepped = [
        cast_split(weights["w_in"], BF16, "prep_w_in"),
        dm.pad_uq_rows(transpose_split(w_uq, BF16, "prep_w_uq")),
        transpose_split(w_ukv, BF16, "prep_w_ukv"),
        cast_split(w_ret_proj, BF16, "prep_w_ret"),
        cast_split(w_mla_proj, BF16, "prep_w_mla"),
        cast_split(w_out, BF16, "prep_w_out"),
    ]
    places = dm.placements()
    zero_rows = jnp.zeros((places[0].zero_rows[1], D // 2), BF16)
    w_keys = ("in", "uq", "ukv", "ret", "mla", "out")
    ahead, own = (0, 4), (1, 2, 3, 5)

    def gather_of(which, l):
        return gather_layer([prepped[n] for n in which], zero_rows, [places[n] for n in which], l, None)

    layer_w = [dict(zip(w_keys, gather_layer(prepped[:1], zero_rows, places[:1], 0, "l0_gather_w_in")))]

    c_all = allgather8(c, "gather_c").reshape(8, D)
    (c_act,) = rowwise(lambda z: ([_silu(z)], []), [c_all], [], [(D, BF16)], tm=8, name="silu_c")
    mod_part = jnp.stack([matmul(c_act, w_mod[l], out_dtype=F32, name=f"l{l}_mm_mod", tn=C3) for l in range(L)])
    mod_all = allgather8(mod_part, "gather_mod")
    mod_all = mod_all.reshape(4, 2, L, 8, C3)[:, 0].transpose(1, 2, 0, 3).reshape(L, 8, 3 * D) + b_mod[:, None, :]
    mod = lax.dynamic_index_in_dim(mod_all, my_dev, axis=1, keepdims=False)

    pos = positions[0]
    tabs = (*_rope_tables(pos, RET_DK, LANE), *_rope_tables(pos, MLA_ROPE, LANE))

    def vecs(l):
        return (mod[l : l + 1, :D], mod[l : l + 1, D : 2 * D], mod[l : l + 1, 2 * D :],
                g_norm[l : l + 1], g_cq[l : l + 1], g_ckv[l : l + 1])

    saved = []
    for l in range(L):
        mine = (1, 2, 3, 4, 5) if l == 0 else own
        host = gather_of(ahead, l + 1) if l + 1 < L else None
        x, sv, hosted, layer_w[l] = layer_fwd(
            dm, l, x, *vecs(l), layer_w[l], tabs, host=host, host_in=gather_of(mine, l), late_keys=[w_keys[n] for n in mine]
        )
        saved.append(sv)
        if host:
            layer_w.append(dict(zip([w_keys[n] for n in ahead], hosted)))

    def f_loss(x, t, g):
        xn = x * _rms(x)
        err = xn * g - t
        dy = err * (1.0 / D)
        dxn = dy * g
        dx = _rms(x) * (dxn - xn * jnp.mean(dxn * xn, axis=1, keepdims=True))
        part = jnp.sum(jnp.sum(err * err, axis=1, keepdims=True), axis=0, keepdims=True) * (0.5 / D)
        return [dx], [jnp.broadcast_to(part, (1, LANE)), _sum0(dy * xn)]

    dx, loss_part, dg_final = rowwise(f_loss, [x, target], [g_final.reshape(1, D)], [(D, F32)], [LANE, D], tm=256, name="loss_head")

    stacks = {n: None for n in BIG}
    dvec = {n: [None] * L for n in ("mod", "g_norm", "g_cq", "g_ckv")}

    def reduce_chips(l, from_chips):
        return [
            _add_rows(lambda a: ((a[3].astype(F32) + a[0].astype(F32)) + a[1].astype(F32)) + a[2].astype(F32), [r], r.shape[-1],
                      F32, f"l{l}_rs_add4_{n}")
            for n, r in zip(BIG, from_chips)
        ]

    def finish_grads(l, from_chips):
        mine = reduce_chips(l, from_chips)
        store_grads(l, mine, sibling_swap(mine, f"l{l}_rs_share"))

    def store_grads(l, mine, theirs):
        for n, a, b in zip(BIG, mine, theirs):
            if n == "w_uq":
                a, b = dm.unpad_uq_rows(a), dm.unpad_uq_rows(b)
            finish = untranspose_halves if n in ("w_uq", "w_ukv") else merge_halves
            stacks[n] = finish(a, b, my_c, stacks[n], l, L, f"l{l}_grad_{n}")

    def pair_sums(l, names, partial, from_sibling):
        return [add_pair(g, r, my_c, f"l{l}_rs_add2_{n}") for n, g, r in zip(names, partial, from_sibling)]

    def core_exchange(dws):
        return sibling_scatter([dws[n] for n in BIG], None)

    def last_exchange(names, which, tag):
        def build(dws, _=None):
            partial = [dws[n] for n in names]
            pairs = pair_sums(0, names, partial, sibling_scatter(partial, f"l0_rs_cores_{tag}"))
            return scatter_to_chips(pairs, which, None)

        return build

    pair = None
    for l in reversed(range(L)):
        host = scatter_to_chips(pair, places, None) if pair is not None else None
        halves = {}

        def share_halves(dws, from_chips, l=l, halves=halves):
            halves["mine"] = reduce_chips(l + 1, from_chips)
            return sibling_swap(halves["mine"], None)

        hooks = dict(exchange_dh=core_exchange, exchange_dw=share_halves if host else None)
        if l == 0:
            hooks = dict(exchange_dw=last_exchange(BIG[1:], places[1:], "rest"), exchange_dh=last_exchange(BIG[:1], places[:1], "in"))
        dx, dw_l, dv_l, hosted, hosted_dh, hosted_dw = layer_bwd(
            dm, l, dx, saved[l], *vecs(l), layer_w[l], tabs, host=host, **hooks
        )
        if host and l > 0:
            store_grads(l + 1, halves["mine"], hosted_dw)
        elif host:
            finish_grads(l + 1, hosted)
        for n in dvec:
            dvec[n][l] = dv_l[n]
        if l > 0:
            pair = pair_sums(l, BIG, [dw_l[n] for n in BIG], hosted_dh)
        else:
            finish_grads(0, [*hosted_dh, *hosted_dw])
    grad_x = dx[None]

    pieces = [loss_part] + [jnp.concatenate(dvec[n], axis=1) for n in ("mod", "g_norm", "g_cq", "g_ckv")] + [dg_final]
    widths = [p.shape[1] for p in pieces]
    small_all = allgather8(jnp.concatenate(pieces, axis=1), "gather_small").reshape(8, sum(widths))

    def sum8_body(a_ref, o_ref):
        acc = a_ref[0:1, :]
        for d in range(1, 8):
            acc = acc + a_ref[d : d + 1, :]
        o_ref[...] = acc

    small = _pcall(sum8_body, name="sum_small", out_shape=jax.ShapeDtypeStruct((1, sum(widths)), F32))(small_all)
    offs = np.cumsum([0] + widths)
    loss = small[0, 0]
    g_small = {
        "b_mod": small[0, offs[1] : offs[2]].reshape(L, 3 * D),
        "g_norm": small[0, offs[2] : offs[3]].reshape(L, D),
        "g_cq": small[0, offs[3] : offs[4]].reshape(L, dm.QR),
        "g_ckv": small[0, offs[4] : offs[5]].reshape(L, dm.KR),
        "g_final": small[0, offs[5] : offs[6]],
    }

    dmod_all = small_all[:, offs[1] : offs[2]].reshape(8, L, 3 * D)
    dmod_mine = lax.dynamic_slice_in_dim(dmod_all, my_chip * C3, C3, axis=2)
    pad8 = lambda a: jnp.pad(a, ((0, LANE - 8), (0, 0)))
    grads = dict(g_small)
    grads["w_mod"] = jnp.stack(
        [matmul(pad8(c_act), pad8(dmod_mine[:, l]), ta=True, out_dtype=F32, name=f"l{l}_mm_dw_mod", tn=C3) for l in range(L)]
    )

    for n in BIG:
        grads[n] = stacks[n]

    deltas, new_m, new_v = {}, {}, {}
    for n in order:
        wv, gv, mv, vv = weights[n], grads[n], m_in[n], v_in[n]
        if wv.ndim == 1:
            wv, gv, mv, vv = (a.reshape(1, -1) for a in (wv, gv, mv, vv))
        d_, m_, v_ = adamw(wv, gv, mv, vv, f"adamw_{n}")
        deltas[n], new_m[n], new_v[n] = (a.reshape(weights[n].shape) for a in (d_, m_, v_))
        grads[n] = grads[n].reshape(weights[n].shape)
    for table in (grads, deltas, new_m, new_v):
        table["w_in"] = jnp.swapaxes(table["w_in"], 1, 2)

    return (loss, grad_x, *[grads[n] for n in order], *[deltas[n] for n in order], *[new_m[n] for n in order],
            *[new_v[n] for n in order])
```

```python
import functools

import jax
import jax.numpy as jnp
import numpy as np
from jax import lax
from jax.experimental import pallas as pl
from jax.experimental.pallas import tpu as pltpu

F32 = jnp.float32
BF16 = jnp.bfloat16
MESH = pl.DeviceIdType.MESH

VMEM_LIMIT_BYTES = 52 * 1024 * 1024
LANE = 128

CHUNK = 64
EPS = 1e-6
NEG_INF = -1e30
ROPE_BASE = 10000.0
RET_DK = 128
RET_DV = 256
MLA_NOPE = 128
MLA_ROPE = 64
MLA_DV = 128
MLA_QW = 256
MLA_SCALE = float((MLA_NOPE + MLA_ROPE) ** -0.5)
LOG2E = float(np.log2(np.e))
LN2 = float(np.log(2.0))
ATTN_BLOCK = 512
ATTN_FWD_HEADS_PER_STEP = 8
ATTN_FWD_ROW_SPLIT = 2
ATTN_HEADS_PER_STEP = 2
RET_BLOCK = 256
RET_HEADS_PER_STEP = 8

ADAM_LR = 0.001
ADAM_B1 = 0.9
ADAM_B2 = 0.999
ADAM_EPS = 1e-08
ADAM_WD = 0.01
ADAM_STEP = 10


def _pcall(body, **kw):
    return pl.pallas_call(body, **kw)


def _params(n_grid):
    return pltpu.CompilerParams(dimension_semantics=("arbitrary",) * n_grid, vmem_limit_bytes=VMEM_LIMIT_BYTES)


def _pick(dim, target, mult=LANE):
    if dim <= target:
        return dim
    best = None
    for t in range(mult, target + 1, mult):
        if dim % t == 0:
            best = t
    assert best is not None, (dim, target, mult)
    return best


def matmul(
    a, b, *, ta=False, tb=False, b_split=False, out_split=False, out_dtype=BF16, name, tm=512, tn=1024, tk=2048, host=None
):
    (M, K) = (a.shape[1], a.shape[0]) if ta else a.shape
    b_rows, b_cols = (b.shape[1], 2 * b.shape[2]) if b_split else b.shape
    (K2, N) = (b_cols, b_rows) if tb else (b_rows, b_cols)
    assert K == K2, (a.shape, b.shape, ta, tb)
    whole_k = b_split and tb and tk >= K and not ta
    whole_n = out_split and tn >= N and not (b_split and not tb)
    whole_bn = b_split and not tb and tn >= N and not out_split
    n_cap = N if (whole_n or whole_bn) else (N // 2 if (out_split or (b_split and not tb)) else N)
    k_cap = K if whole_k else (K // 2 if (b_split and tb) else K)
    tm, tn, tk = _pick(M, tm, 8 if M < LANE else LANE), _pick(n_cap, min(tn, n_cap)), _pick(k_cap, min(tk, k_cap))
    nk = K // tk
    njh, nkh = max((N // 2) // tn, 1), max((K // 2) // tk, 1)
    dn = (((0 if ta else 1,), (1 if tb else 0,)), ((), ()))

    def body(a_ref, b_ref, o_ref, *scratch):
        if whole_k:
            a_blk = a_ref[...].astype(BF16)
            prod = lax.dot_general(a_blk[:, : K // 2], b_ref[0].astype(BF16), dn, preferred_element_type=F32)
            prod += lax.dot_general(a_blk[:, K // 2 :], b_ref[1].astype(BF16), dn, preferred_element_type=F32)
        elif whole_bn:
            a_blk = a_ref[...].astype(BF16)
            halves = [lax.dot_general(a_blk, b_ref[half].astype(BF16), dn, preferred_element_type=F32) for half in range(2)]
            prod = jnp.concatenate(halves, axis=1)
        else:
            b_blk = b_ref[0] if b_split else b_ref[...]
            prod = lax.dot_general(a_ref[...].astype(BF16), b_blk.astype(BF16), dn, preferred_element_type=F32)

        def store(v):
            if whole_n:
                o_ref[0] = v[:, : N // 2].astype(o_ref.dtype)
                o_ref[1] = v[:, N // 2 :].astype(o_ref.dtype)
            elif out_split:
                o_ref[0] = v.astype(o_ref.dtype)
            else:
                o_ref[...] = v.astype(o_ref.dtype)

        if nk == 1:
            store(prod)
            return
        (acc_ref,) = scratch
        k = pl.program_id(2)

        @pl.when(k == 0)
        def _():
            acc_ref[...] = prod

        @pl.when(k > 0)
        def _():
            acc_ref[...] += prod

        @pl.when(k == nk - 1)
        def _():
            store(acc_ref[...])

    a_spec = pl.BlockSpec((tk, tm), lambda i, j, k: (k, i)) if ta else pl.BlockSpec((tm, tk), lambda i, j, k: (i, k))
    if whole_k:
        b_spec = pl.BlockSpec((2, tn, K // 2), lambda i, j, k: (0, j, 0))
    elif whole_bn:
        b_spec = pl.BlockSpec((2, tk, N // 2), lambda i, j, k: (0, k, 0))
    elif b_split and tb:
        b_spec = pl.BlockSpec((1, tn, tk), lambda i, j, k: (k // nkh, j, k % nkh))
    elif b_split:
        b_spec = pl.BlockSpec((1, tk, tn), lambda i, j, k: (j // njh, k, j % njh))
    elif tb:
        b_spec = pl.BlockSpec((tn, tk), lambda i, j, k: (j, k))
    else:
        b_spec = pl.BlockSpec((tk, tn), lambda i, j, k: (k, j))
    if whole_n:
        out_spec = pl.BlockSpec((2, tm, N // 2), lambda i, j, k: (0, i, 0))
        out_shape = jax.ShapeDtypeStruct((2, M, N // 2), out_dtype)
    elif out_split:
        out_spec = pl.BlockSpec((1, tm, tn), lambda i, j, k: (j // njh, i, j % njh))
        out_shape = jax.ShapeDtypeStruct((2, M, N // 2), out_dtype)
    else:
        out_spec = pl.BlockSpec((tm, tn), lambda i, j, k: (i, j))
        out_shape = jax.ShapeDtypeStruct((M, N), out_dtype)
    grid = (M // tm, N // tn, nk)
    acc_shapes = [] if nk == 1 else [pltpu.VMEM((tm, tn), F32)]
    if host is None:
        return _pcall(
            body, name=name, grid=grid, in_specs=[a_spec, b_spec], out_specs=out_spec, out_shape=out_shape,
            scratch_shapes=acc_shapes, compiler_params=_params(3),
        )(a, b)

    n_hi, n_ho, n_steps = len(host.ins), len(host.out_shapes), grid[0] * grid[1] * grid[2]

    def hosting_body(a_ref, b_ref, *rest):
        host_in, rest = rest[:n_hi], rest[n_hi:]
        o_ref, rest = rest[0], rest[1:]
        host_out, rest = rest[:n_ho], rest[n_ho:]
        scratch, host_sems = rest[: len(acc_shapes)], rest[len(acc_shapes) :]
        step = (pl.program_id(0) * grid[1] + pl.program_id(1)) * grid[2] + pl.program_id(2)
        refs = (host_in, host_out, host_sems)

        @pl.when(step == 0)
        def _():
            host.start(*refs)

        @pl.when(step == n_steps // 2)
        def _():
            host.relay(*refs)

        body(a_ref, b_ref, o_ref, *scratch)

        @pl.when(step == n_steps - 1)
        def _():
            host.finish(*refs)

    return _pcall(
        hosting_body,
        name=name,
        grid=grid,
        in_specs=[a_spec, b_spec, *host.in_specs],
        out_specs=[out_spec, *host.out_specs],
        out_shape=[out_shape, *host.out_shapes],
        scratch_shapes=[*acc_shapes, *host.scratch],
        compiler_params=_params(3),
    )(a, b, *host.ins)


def win(arr, off, width):
    assert off % width == 0 and off + width <= arr.shape[1], (arr.shape, off, width)
    return (arr, off // width, width)


def rowwise(fn, rows, vecs, outs, accs=(), *, tm, name, into=None):
    rows = [r if isinstance(r, tuple) else (r, 0, r.shape[-1]) for r in rows]
    S = rows[0][0].shape[-2]
    tm = _pick(S, tm, 8)
    n_rows, n_vecs, n_outs = len(rows), len(vecs), len(outs)
    aliased = into is not None and not isinstance(into[0], int)
    n_alias = 1 if aliased else 0

    def body(*refs):
        ins = [r[...] for r in refs[: n_rows + n_vecs]]
        refs = refs[: n_rows + n_vecs] + refs[n_rows + n_vecs + n_alias :]
        out_refs = refs[n_rows + n_vecs : n_rows + n_vecs + n_outs]
        acc_refs = refs[n_rows + n_vecs + n_outs :]
        res, acc = fn(*ins)
        for r, v in zip(out_refs, res, strict=True):
            r[...] = v.astype(r.dtype)
        if acc_refs:
            i = pl.program_id(0)

            @pl.when(i == 0)
            def _():
                for r, v in zip(acc_refs, acc, strict=True):
                    r[...] = v

            @pl.when(i > 0)
            def _():
                for r, v in zip(acc_refs, acc, strict=True):
                    r[...] += v

    in_specs = []
    for arr, blk, w in rows:
        if arr.ndim == 3:
            in_specs.append(pl.BlockSpec((arr.shape[0], tm, w), lambda i: (0, i, 0)))
        else:
            in_specs.append(pl.BlockSpec((tm, w), functools.partial(lambda i, blk: (i, blk), blk=blk)))
    for v in vecs:
        in_specs.append(pl.BlockSpec(v.shape, functools.partial(lambda i, nd: (0,) * nd, nd=v.ndim)))
    out_specs = [pl.BlockSpec((tm, w), lambda i: (i, 0)) for w, _ in outs]
    out_specs += [pl.BlockSpec((1, w), lambda i: (0, 0)) for w in accs]
    out_shape = [jax.ShapeDtypeStruct((S, w), dt) for w, dt in outs]
    out_shape += [jax.ShapeDtypeStruct((1, w), F32) for w in accs]
    operands = [*[r[0] for r in rows], *vecs]
    aliases = {}
    if into is not None:
        target, off = into
        w0, dt0 = outs[0]
        total = target if isinstance(target, int) else target.shape[1]
        assert off % w0 == 0 and off + w0 <= total
        out_specs[0] = pl.BlockSpec((tm, w0), functools.partial(lambda i, blk: (i, blk), blk=off // w0))
        out_shape[0] = jax.ShapeDtypeStruct((S, total), dt0)
        if aliased:
            in_specs.append(pl.BlockSpec(memory_space=pl.ANY))
            operands.append(target)
            aliases = {len(operands) - 1: 0}
    res = _pcall(
        body,
        name=name,
        grid=(S // tm,),
        in_specs=in_specs,
        out_specs=out_specs,
        out_shape=out_shape,
        input_output_aliases=aliases,
        compiler_params=_params(1),
    )(*operands)
    return res


def _sum0(v):
    return jnp.sum(v, axis=0, keepdims=True)


def _sigmoid(z):
    return 1.0 / (1.0 + jnp.exp(-z))


def _silu(z):
    return z * _sigmoid(z)


def _dsilu(z):
    s = _sigmoid(z)
    return s * (1.0 + z * (1.0 - s))


def _swap_half(x, half):
    if 2 * half == LANE:
        return pltpu.roll(x, half, 1)
    lane = lax.broadcasted_iota(jnp.int32, x.shape, 1)
    return jnp.where((lane % (2 * half)) < half, pltpu.roll(x, LANE - half, 1), pltpu.roll(x, half, 1))


def _rope_tile(x, cosf, sinf, half, inverse=False):
    sw = _swap_half(x, half)
    return x * cosf - sw * sinf if inverse else x * cosf + sw * sinf


def _tri_tables(n, by_key):
    if by_key:
        pairs = [(i, j) for j in range(n) for i in range(j, n)]
    else:
        pairs = [(i, j) for i in range(n) for j in range(i + 1)]
    return (np.array([p[0] for p in pairs], np.int32), np.array([p[1] for p in pairs], np.int32))


def _attn_scores(q, kn, kr, i, j, tq, masked, row0=0):
    k = jnp.concatenate([kn, kr], axis=1)
    s = lax.dot_general(q, k, (((1,), (1,)), ((), ())), preferred_element_type=F32)
    if masked:
        rc = (i * tq + row0 + lax.broadcasted_iota(jnp.int32, s.shape, 0)) // CHUNK
        cc = (j * tq + lax.broadcasted_iota(jnp.int32, s.shape, 1)) // CHUNK
        s = jnp.where(cc <= rc, s, NEG_INF)
    return s, k


def attn_fwd(q, kv, kr, proj, mg_off, *, heads, tq, name, host=None):
    S = q.shape[0]
    tq = _pick(S, tq, CHUNK)
    n = S // tq
    it, jt = _tri_tables(n, by_key=False)
    T = len(it)
    mg_blk = mg_off // MLA_DV
    n_hi, n_ho = (len(host.ins), len(host.out_shapes)) if host else (0, 0)

    hp = min(ATTN_FWD_HEADS_PER_STEP, heads)
    assert heads % hp == 0 and mg_blk % hp == 0
    groups = heads // hp

    def body(it_ref, jt_ref, q_ref, kv_ref, kr_ref, mg_ref, *rest):
        host_in, rest = rest[:n_hi], rest[n_hi:]
        (o_ref, a_ref, lse_ref), rest = rest[:3], rest[3:]
        host_out, rest = rest[:n_ho], rest[n_ho:]
        (m_sc, acc_sc), host_sems = rest[:2], rest[2:]
        h, t = pl.program_id(0), pl.program_id(1)
        i, j = it_ref[t], jt_ref[t]
        if host:
            _host_steps(host, (host_in, host_out, host_sems), h, t, groups, T, before=True)

        @pl.when(j == 0)
        def _():
            m_sc[...] = jnp.full(m_sc.shape, NEG_INF, F32)
            acc_sc[...] = jnp.zeros(acc_sc.shape, F32)

        def step(masked):
            kr = kr_ref[...]
            sub = tq // ATTN_FWD_ROW_SPLIT
            for u in range(hp):
                kn = kv_ref[:, u * MLA_QW : u * MLA_QW + MLA_NOPE]
                v = kv_ref[:, u * MLA_QW + MLA_NOPE : (u + 1) * MLA_QW]
                v_ones = jnp.concatenate([v, jnp.ones(v.shape, BF16)], axis=1)
                for r in range(ATTN_FWD_ROW_SPLIT):
                    rows = slice(r * sub, (r + 1) * sub)
                    q = q_ref[rows, u * MLA_QW : (u + 1) * MLA_QW]
                    s, _ = _attn_scores(q, kn, kr, i, j, tq, masked, row0=r * sub)
                    m_prev = m_sc[u, rows]
                    m_new = jnp.maximum(m_prev, jnp.max(s, axis=1, keepdims=True))
                    p = jnp.exp2((s - m_new).astype(BF16))
                    alpha = jnp.exp2(m_prev - m_new)
                    acc_sc[u, rows] = alpha * acc_sc[u, rows] + jnp.dot(p, v_ones, preferred_element_type=F32)
                    m_sc[u, rows] = m_new

        @pl.when(j < i)
        def _():
            step(False)

        @pl.when(j == i)
        def _():
            step(True)
            for u in range(hp):
                cols = slice(u * MLA_DV, (u + 1) * MLA_DV)
                l = acc_sc[u, :, MLA_DV:]
                o = acc_sc[u, :, :MLA_DV] / l
                o_ref[:, cols] = o
                a_ref[:, cols] = (o * _silu(mg_ref[:, cols].astype(F32))).astype(a_ref.dtype)
                lse_ref[u] = m_sc[u] + jnp.log2(l[:, :1])

        if host:
            _host_steps(host, (host_in, host_out, host_sems), h, t, groups, T, before=False)

    grid_spec = pltpu.PrefetchScalarGridSpec(
        num_scalar_prefetch=2,
        grid=(groups, T),
        in_specs=[
            pl.BlockSpec((tq, hp * MLA_QW), lambda h, t, it, jt: (it[t], h)),
            pl.BlockSpec((tq, hp * MLA_QW), lambda h, t, it, jt: (jt[t], h)),
            pl.BlockSpec((tq, LANE), lambda h, t, it, jt: (jt[t], 0)),
            pl.BlockSpec((tq, hp * MLA_DV), lambda h, t, it, jt: (it[t], mg_blk // hp + h)),
            *(host.in_specs if host else []),
        ],
        out_specs=[
            pl.BlockSpec((tq, hp * MLA_DV), lambda h, t, it, jt: (it[t], h)),
            pl.BlockSpec((tq, hp * MLA_DV), lambda h, t, it, jt: (it[t], h)),
            pl.BlockSpec((hp, tq, 1), lambda h, t, it, jt: (h, it[t], 0)),
            *(host.out_specs if host else []),
        ],
        scratch_shapes=[
            pltpu.VMEM((hp, tq, 1), F32),
            pltpu.VMEM((hp, tq, 2 * MLA_DV), F32),
            *(host.scratch if host else []),
        ],
    )
    return _pcall(
        body,
        name=name,
        grid_spec=grid_spec,
        out_shape=[
            jax.ShapeDtypeStruct((S, heads * MLA_DV), F32),
            jax.ShapeDtypeStruct((S, heads * MLA_DV), BF16),
            jax.ShapeDtypeStruct((heads, S, 1), F32),
            *(host.out_shapes if host else []),
        ],
        compiler_params=_params(2),
    )(jnp.asarray(it), jnp.asarray(jt), q, kv, kr, proj, *(host.ins if host else []))


def attn_bwd(q, kv, kr, do, o, lse, *, heads, tq, name, host=None):
    S = q.shape[0]
    tq = _pick(S, tq, CHUNK)
    n = S // tq
    it, jt = _tri_tables(n, by_key=True)
    T = len(it)
    n_hi, n_ho = (len(host.ins), len(host.out_shapes)) if host else (0, 0)

    hp = ATTN_HEADS_PER_STEP
    assert heads % hp == 0
    groups = heads // hp

    def body(it_ref, jt_ref, q_ref, kv_ref, kr_ref, do_ref, o_ref, lse_ref, *rest):
        host_in, rest = rest[:n_hi], rest[n_hi:]
        (dq_ref, dkv_ref, dkr_ref), rest = rest[:3], rest[3:]
        host_out, rest = rest[:n_ho], rest[n_ho:]
        (dq_acc, dk_sc, dv_sc), host_sems = rest[:3], rest[3:]
        h, t = pl.program_id(0), pl.program_id(1)
        i, j = it_ref[t], jt_ref[t]
        if host:
            _host_steps(host, (host_in, host_out, host_sems), h, t, groups, T, before=True)

        @pl.when(t == 0)
        def _():
            dq_acc[...] = jnp.zeros(dq_acc.shape, F32)

        @pl.when(i == j)
        def _():
            dk_sc[...] = jnp.zeros(dk_sc.shape, F32)
            dv_sc[...] = jnp.zeros(dv_sc.shape, F32)

        def step(masked):
            kr = kr_ref[...]
            rows = pl.ds(pl.multiple_of(i * tq, tq), tq)
            for u in range(hp):
                wide, narrow = slice(u * MLA_QW, (u + 1) * MLA_QW), slice(u * MLA_DV, (u + 1) * MLA_DV)
                q_blk, do_blk = q_ref[:, wide], do_ref[:, narrow]
                kn = kv_ref[:, u * MLA_QW : u * MLA_QW + MLA_NOPE]
                v = kv_ref[:, u * MLA_QW + MLA_NOPE : (u + 1) * MLA_QW]
                s, k = _attn_scores(q_blk, kn, kr, i, j, tq, masked)
                p = jnp.exp2((s - lse_ref[u]).astype(BF16))
                delta = jnp.sum(do_blk.astype(F32) * o_ref[:, narrow], axis=1, keepdims=True)
                dp = lax.dot_general(do_blk, v, _NT, preferred_element_type=F32)
                ds = p * (dp - delta).astype(BF16)
                dv_sc[u] += lax.dot_general(p, do_blk, _TN, preferred_element_type=F32)
                dk_sc[u] += lax.dot_general(ds, q_blk, _TN, preferred_element_type=F32)
                dq_acc[rows, wide] += jnp.dot(ds, k, preferred_element_type=F32)

        @pl.when(i == j)
        def _():
            step(True)

        @pl.when(i > j)
        def _():
            step(False)

        @pl.when(i == n - 1)
        def _():
            for u in range(hp):
                dkv_ref[:, u * MLA_QW : u * MLA_QW + MLA_NOPE] = (dk_sc[u, :, :MLA_NOPE] * LN2).astype(dkv_ref.dtype)
                dkv_ref[:, u * MLA_QW + MLA_NOPE : (u + 1) * MLA_QW] = dv_sc[u].astype(dkv_ref.dtype)
                dkr_ref[u] = dk_sc[u, :, MLA_NOPE:] * LN2

        @pl.when(t == T - 1)
        def _():
            dq_ref[...] = (dq_acc[...] * MLA_SCALE).astype(dq_ref.dtype)

        if host:
            _host_steps(host, (host_in, host_out, host_sems), h, t, groups, T, before=False)

    grid_spec = pltpu.PrefetchScalarGridSpec(
        num_scalar_prefetch=2,
        grid=(groups, T),
        in_specs=[
            pl.BlockSpec((tq, hp * MLA_QW), lambda h, t, it, jt: (it[t], h)),
            pl.BlockSpec((tq, hp * MLA_QW), lambda h, t, it, jt: (jt[t], h)),
            pl.BlockSpec((tq, LANE), lambda h, t, it, jt: (jt[t], 0)),
            pl.BlockSpec((tq, hp * MLA_DV), lambda h, t, it, jt: (it[t], h)),
            pl.BlockSpec((tq, hp * MLA_DV), lambda h, t, it, jt: (it[t], h)),
            pl.BlockSpec((hp, tq, 1), lambda h, t, it, jt: (h, it[t], 0)),
            *(host.in_specs if host else []),
        ],
        out_specs=[
            pl.BlockSpec((S, hp * MLA_QW), lambda h, t, it, jt: (0, h)),
            pl.BlockSpec((tq, hp * MLA_QW), lambda h, t, it, jt: (jt[t], h)),
            pl.BlockSpec((hp, tq, LANE), lambda h, t, it, jt: (h, jt[t], 0)),
            *(host.out_specs if host else []),
        ],
        scratch_shapes=[
            pltpu.VMEM((S, hp * MLA_QW), F32),
            pltpu.VMEM((hp, tq, MLA_QW), F32),
            pltpu.VMEM((hp, tq, MLA_DV), F32),
            *(host.scratch if host else []),
        ],
    )
    return _pcall(
        body,
        name=name,
        grid_spec=grid_spec,
        out_shape=[
            jax.ShapeDtypeStruct((S, heads * MLA_QW), BF16),
            jax.ShapeDtypeStruct((S, heads * (MLA_NOPE + MLA_DV)), BF16),
            jax.ShapeDtypeStruct((heads, S, LANE), F32),
            *(host.out_shapes if host else []),
        ],
        compiler_params=_params(2),
    )(jnp.asarray(it), jnp.asarray(jt), q, kv, kr, do, o, lse, *(host.ins if host else []))


def _host_steps(host, refs, h, t, heads, n_steps, before):
    if before:

        @pl.when((h == 0) & (t == 0))
        def _():
            host.start(*refs)

        @pl.when((h == heads // 2) & (t == n_steps // 2))
        def _():
            host.relay(*refs)

    else:

        @pl.when((h == heads - 1) & (t == n_steps - 1))
        def _():
            host.finish(*refs)


def _ret_consts(heads):
    h = np.arange(heads, dtype=np.float32)
    lg = np.log(np.float32(1.0) - np.float32(2.0) ** (np.float32(-5.0) - h)).astype(np.float32)
    idx = np.arange(CHUNK, dtype=np.float32)
    dmat = np.exp(np.abs(idx[:, None] - idx[None, :])[None] * lg[:, None, None]).astype(np.float32)
    xi = np.exp((idx + 1.0)[None, :] * lg[:, None]).astype(np.float32)
    zeta = np.exp((CHUNK - 1.0 - idx)[None, :] * lg[:, None]).astype(np.float32)
    dec = np.exp(np.float32(CHUNK) * lg).astype(np.float32)
    xi = np.broadcast_to(xi[:, :, None], (heads, CHUNK, RET_DK)).copy()
    zeta = np.broadcast_to(zeta[:, :, None], (heads, CHUNK, RET_DK)).copy()
    dec = np.broadcast_to(dec[:, None, None], (heads, 8, LANE)).copy()
    return jnp.asarray(dmat), jnp.asarray(xi), jnp.asarray(zeta), jnp.asarray(dec)


_NT = (((1,), (1,)), ((), ()))
_TN = (((0,), (0,)), ((), ()))


def _dot(a, b, dn=(((1,), (0,)), ((), ()))):
    return lax.dot_general(a.astype(BF16), b.astype(BF16), dn, preferred_element_type=F32)


def _const_specs(hp):
    return [
        pl.BlockSpec((hp, CHUNK, CHUNK), lambda h, b: (h, 0, 0)),
        pl.BlockSpec((hp, CHUNK, RET_DK), lambda h, b: (h, 0, 0)),
        pl.BlockSpec((hp, CHUNK, RET_DK), lambda h, b: (h, 0, 0)),
        pl.BlockSpec((hp, 8, LANE), lambda h, b: (h, 0, 0)),
    ]


def _ret_group(heads, *blks):
    hp = min(RET_HEADS_PER_STEP, heads)
    assert heads % hp == 0 and all(b % hp == 0 for b in blks)
    return hp


def ret_fwd(q, k, proj, v_off, rg_off, *, heads, tb, name):
    S = q.shape[0]
    tb = _pick(S, tb, CHUNK)
    cpb, nb = tb // CHUNK, S // tb
    v_blk, rg_blk = v_off // RET_DV, rg_off // RET_DV
    hp = _ret_group(heads, v_blk, rg_blk)

    def body(q_ref, k_ref, v_ref, rg_ref, dm_ref, xi_ref, ze_ref, dec_ref, o_ref, a_ref, st_ref, r_sc):
        @pl.when(pl.program_id(1) == 0)
        def _():
            r_sc[...] = jnp.zeros(r_sc.shape, F32)

        for c in range(cpb):
            sl = pl.ds(c * CHUNK, CHUNK)
            for u in range(hp):
                dm, xi, ze, dec = dm_ref[u], xi_ref[u], ze_ref[u], dec_ref[u, 0:1, 0:1]
                narrow, wide = slice(u * RET_DK, (u + 1) * RET_DK), slice(u * RET_DV, (u + 1) * RET_DV)
                qc, kc, vc = q_ref[sl, narrow], k_ref[sl, narrow], v_ref[sl, wide]
                r = r_sc[u]
                st_ref[u, c] = r.astype(BF16)
                s = _dot(qc, kc, _NT) * dm
                o = _dot(s, vc) + _dot(qc.astype(F32) * xi, r)
                r_sc[u] = r * dec + _dot(kc.astype(F32) * ze, vc, _TN)
                mu = jnp.mean(o, axis=1, keepdims=True)
                d = o - mu
                n = d * lax.rsqrt(jnp.mean(d * d, axis=1, keepdims=True) + EPS)
                o_ref[sl, wide] = o
                a_ref[sl, wide] = (n * _silu(rg_ref[sl, wide].astype(F32))).astype(a_ref.dtype)

    return _pcall(
        body,
        name=name,
        grid=(heads // hp, nb),
        in_specs=[
            pl.BlockSpec((tb, hp * RET_DK), lambda h, b: (b, h)),
            pl.BlockSpec((tb, hp * RET_DK), lambda h, b: (b, h)),
            pl.BlockSpec((tb, hp * RET_DV), lambda h, b: (b, v_blk // hp + h)),
            pl.BlockSpec((tb, hp * RET_DV), lambda h, b: (b, rg_blk // hp + h)),
            *_const_specs(hp),
        ],
        out_specs=[
            pl.BlockSpec((tb, hp * RET_DV), lambda h, b: (b, h)),
            pl.BlockSpec((tb, hp * RET_DV), lambda h, b: (b, h)),
            pl.BlockSpec((hp, cpb, RET_DK, RET_DV), lambda h, b: (h, b, 0, 0)),
        ],
        out_shape=[
            jax.ShapeDtypeStruct((S, heads * RET_DV), F32),
            jax.ShapeDtypeStruct((S, heads * RET_DV), BF16),
            jax.ShapeDtypeStruct((heads, S // CHUNK, RET_DK, RET_DV), BF16),
        ],
        scratch_shapes=[pltpu.VMEM((hp, RET_DK, RET_DV), F32)],
        compiler_params=_params(2),
    )(q, k, proj, proj, *_ret_consts(heads))


def ret_bwd(q, k, proj, v_off, states, do, *, heads, tb, name, into):
    S = q.shape[0]
    tb = _pick(S, tb, CHUNK)
    cpb, nb = tb // CHUNK, S // tb
    v_blk = v_off // RET_DV
    hp = _ret_group(heads, v_blk)
    target, dv_off = into
    dv_blk = dv_off // (hp * RET_DV)
    assert dv_off % (hp * RET_DV) == 0

    def body(q_ref, k_ref, v_ref, st_ref, do_ref, dm_ref, xi_ref, ze_ref, dec_ref, _, dq_ref, dk_ref, dv_ref, dr_sc):
        @pl.when(pl.program_id(1) == 0)
        def _():
            dr_sc[...] = jnp.zeros(dr_sc.shape, F32)

        for c in reversed(range(cpb)):
            sl = pl.ds(c * CHUNK, CHUNK)
            for u in range(hp):
                dm, xi, ze, dec = dm_ref[u], xi_ref[u], ze_ref[u], dec_ref[u, 0:1, 0:1]
                narrow, wide = slice(u * RET_DK, (u + 1) * RET_DK), slice(u * RET_DV, (u + 1) * RET_DV)
                qc, kc, vc, doc = q_ref[sl, narrow], k_ref[sl, narrow], v_ref[sl, wide], do_ref[sl, wide]
                r_prev = st_ref[u, c]
                dr = dr_sc[u]
                a = _dot(qc, kc, _NT) * dm
                ds = _dot(doc, vc, _NT) * dm
                kz = kc.astype(F32) * ze
                dq_ref[sl, narrow] = (_dot(ds, kc) + _dot(doc, r_prev, _NT) * xi).astype(dq_ref.dtype)
                dk_ref[sl, narrow] = (_dot(ds, qc, _TN) + _dot(vc, dr, _NT) * ze).astype(dk_ref.dtype)
                dv_ref[sl, wide] = (_dot(a, doc, _TN) + _dot(kz, dr)).astype(dv_ref.dtype)
                dr_sc[u] = dr * dec + _dot(qc.astype(F32) * xi, doc, _TN)

    rev = lambda h, b: (nb - 1 - b, h)
    return _pcall(
        body,
        name=name,
        grid=(heads // hp, nb),
        in_specs=[
            pl.BlockSpec((tb, hp * RET_DK), rev),
            pl.BlockSpec((tb, hp * RET_DK), rev),
            pl.BlockSpec((tb, hp * RET_DV), lambda h, b: (nb - 1 - b, v_blk // hp + h)),
            pl.BlockSpec((hp, cpb, RET_DK, RET_DV), lambda h, b: (h, nb - 1 - b, 0, 0)),
            pl.BlockSpec((tb, hp * RET_DV), rev),
            *_const_specs(hp),
            pl.BlockSpec(memory_space=pl.ANY),
        ],
        out_specs=[
            pl.BlockSpec((tb, hp * RET_DK), rev),
            pl.BlockSpec((tb, hp * RET_DK), rev),
            pl.BlockSpec((tb, hp * RET_DV), lambda h, b: (nb - 1 - b, dv_blk + h)),
        ],
        out_shape=[
            jax.ShapeDtypeStruct((S, heads * RET_DK), BF16),
            jax.ShapeDtypeStruct((S, heads * RET_DK), BF16),
            jax.ShapeDtypeStruct(target.shape, target.dtype),
        ],
        input_output_aliases={9: 2},
        scratch_shapes=[pltpu.VMEM((hp, RET_DK, RET_DV), F32)],
        compiler_params=_params(2),
    )(q, k, proj, states, do, *_ret_consts(heads), target)


def _me():
    return (lax.axis_index("x"), lax.axis_index("y"), lax.axis_index("c"))


def _comm(name, ins, out_shapes, n_local, n_remote, plan):
    return run_exchange(name, Exchange(ins, out_shapes, n_local, n_remote, plan))


def run_exchange(name, ex):
    def body(*refs):
        ex.start(*ex.split(refs))
        ex.relay(*ex.split(refs))
        ex.finish(*ex.split(refs))

    return _pcall(
        body, name=name, in_specs=ex.in_specs, out_specs=ex.out_specs, out_shape=ex.out_shapes, scratch_shapes=ex.scratch
    )(*ex.ins)


class Exchange:
    def __init__(self, ins, out_shapes, n_local, n_remote, plan):
        self.ins, self.out_shapes, self.n_local, self.n_remote, self.plan = list(ins), list(out_shapes), n_local, n_remote, plan
        any_spec = pl.BlockSpec(memory_space=pl.ANY)
        self.in_specs, self.out_specs = [any_spec] * len(self.ins), [any_spec] * len(self.out_shapes)
        self.scratch = [
            pltpu.SemaphoreType.DMA((n_remote,)),
            pltpu.SemaphoreType.DMA((n_remote,)),
            pltpu.SemaphoreType.DMA((max(n_local, 1),)),
        ]

    def split(self, refs):
        n_in, n_out = len(self.ins), len(self.out_shapes)
        return refs[:n_in], refs[n_in : n_in + n_out], refs[n_in + n_out :]

    def _copies(self, in_refs, out_refs, sems):
        send_sems, recv_sems, local_sems = sems
        me = _me()
        local_plan, remote_plan = self.plan(me, in_refs, out_refs)
        assert len(local_plan) == self.n_local and len(remote_plan) == self.n_remote, (len(local_plan), len(remote_plan))
        def local_copy(n):
            src, dst = local_plan[n]
            return pltpu.make_async_copy(src, dst, local_sems.at[n])

        def send(n):
            src, dst, peer, _, _ = remote_plan[n]
            return pltpu.make_async_remote_copy(
                src_ref=src, dst_ref=dst, send_sem=send_sems.at[n], recv_sem=recv_sems.at[n], device_id=peer, device_id_type=MESH
            )

        def arrival(n):
            src, _, _, landing, _ = remote_plan[n]
            return pltpu.make_async_remote_copy(
                src_ref=src, dst_ref=landing, send_sem=send_sems.at[n], recv_sem=recv_sems.at[n], device_id=me, device_id_type=MESH
            )

        after = [a for (_, _, _, _, a) in remote_plan]
        assert all(a is None or a < n for n, a in enumerate(after))
        return local_copy, send, arrival, after

    def start(self, in_refs, out_refs, sems):
        local_copy, send, _, after = self._copies(in_refs, out_refs, sems)
        for n in range(self.n_local):
            local_copy(n).start()
        for n, a in enumerate(after):
            if a is None:
                send(n).start()

    def relay(self, in_refs, out_refs, sems):
        _, send, arrival, after = self._copies(in_refs, out_refs, sems)
        for n in sorted({a for a in after if a is not None}):
            arrival(n).wait_recv()
            for m, a in enumerate(after):
                if a == n:
                    send(m).start()

    def finish(self, in_refs, out_refs, sems):
        local_copy, send, arrival, after = self._copies(in_refs, out_refs, sems)
        relayed = {a for a in after if a is not None}
        for n in range(self.n_remote):
            if n not in relayed:
                arrival(n).wait_recv()
        for n in range(self.n_remote):
            send(n).wait_send()
        for n in range(self.n_local):
            local_copy(n).wait()


_CHIP_FLIPS = ((1, 0), (0, 1), (1, 1))


class Placement:
    def __init__(self, rows, full_rows, cuts=(), offset=None, zero_rows=None):
        edges = [0, *sorted(cuts), rows]
        self.rows, self.full_rows = rows, full_rows
        self.runs = [(a, b - a) for a, b in zip(edges[:-1], edges[1:]) if b > a]
        self.offset = offset if offset is not None else (lambda j, start: j * rows + start)
        self.zero_rows = zero_rows

    def at(self, ref_2d_plus, j, start, size):
        return pl.ds(pl.multiple_of(self.offset(j, start), 16), size)


def gather_layer(prepped, zeros, places, l, name):
    n_w = len(prepped)
    n_chip = sum(len(p.runs) for p in places) * 3
    n_own = sum(len(p.runs) for p in places) * 2 + sum(2 for p in places if p.zero_rows)

    def plan(me, in_refs, out_refs):
        x, y, c = me
        j = 2 * x + y
        sib = (x, y, 1 - c)
        from_chips, own, forwards = [], [], []
        for i, o, p in zip(in_refs[:n_w], out_refs, places):
            for start, size in p.runs:
                for dx, dy in _CHIP_FLIPS:
                    px, py = x ^ dx, y ^ dy
                    jp = 2 * px + py
                    mine_there = o.at[c, p.at(o, j, start, size)]
                    theirs_here = o.at[c, p.at(o, jp, start, size)]
                    n = len(from_chips)
                    from_chips.append((i.at[l, c, pl.ds(start, size)], mine_there, (px, py, c), theirs_here, None))
                    forwards.append((theirs_here, theirs_here, sib, o.at[1 - c, p.at(o, jp, start, size)], n))
                for h in range(2):
                    place = o.at[h, p.at(o, j, start, size)]
                    own.append((i.at[l, h, pl.ds(start, size)], place, sib, place, None))
            if p.zero_rows:
                for h in range(2):
                    place = o.at[h, pl.ds(p.zero_rows[0], p.zero_rows[1])]
                    own.append((in_refs[n_w], place, sib, place, None))
        return [], from_chips + own + forwards

    shapes = [jax.ShapeDtypeStruct((2, p.full_rows, w.shape[3]), w.dtype) for w, p in zip(prepped, places)]
    ins = [*prepped, zeros] if any(p.zero_rows for p in places) else list(prepped)
    ex = Exchange(ins, shapes, 0, 2 * n_chip + n_own, plan)
    return ex if name is None else run_exchange(name, ex)


def sibling_scatter(arrays, name):
    def plan(me, in_refs, out_refs):
        x, y, c = me
        return [], [(i.at[1 - c], o, (x, y, 1 - c), o, None) for i, o in zip(in_refs, out_refs)]

    shapes = [jax.ShapeDtypeStruct(a.shape[1:], a.dtype) for a in arrays]
    ex = Exchange(arrays, shapes, 0, len(arrays), plan)
    return ex if name is None else run_exchange(name, ex)


def sibling_swap(arrays, name):
    def plan(me, in_refs, out_refs):
        x, y, c = me
        return [], [(i, o, (x, y, 1 - c), o, None) for i, o in zip(in_refs, out_refs)]

    shapes = [jax.ShapeDtypeStruct(a.shape, a.dtype) for a in arrays]
    ex = Exchange(arrays, shapes, 0, len(arrays), plan)
    return ex if name is None else run_exchange(name, ex)


def scatter_to_chips(arrays, places, name):
    def plan(me, in_refs, out_refs):
        x, y, c = me
        j = 2 * x + y
        local_plan, remote_plan = [], []
        for i, o, p in zip(in_refs, out_refs, places):
            for start, size in p.runs:
                local_plan.append((i.at[p.at(i, j, start, size)], o.at[3, pl.ds(start, size)]))
                for k, (dx, dy) in enumerate(_CHIP_FLIPS):
                    px, py = x ^ dx, y ^ dy
                    landing = o.at[k, pl.ds(start, size)]
                    remote_plan.append((i.at[p.at(i, 2 * px + py, start, size)], landing, (px, py, c), landing, None))
        return local_plan, remote_plan

    n_runs = sum(len(p.runs) for p in places)
    shapes = [jax.ShapeDtypeStruct((4, p.rows, a.shape[1]), a.dtype) for a, p in zip(arrays, places)]
    ex = Exchange(arrays, shapes, n_runs, 3 * n_runs, plan)
    return ex if name is None else run_exchange(name, ex)


def allgather8(block, name):
    def plan(me, in_refs, out_refs):
        x, y, c = me
        (i,), (o,) = in_refs, out_refs
        mine = 4 * x + 2 * y + c
        remote_plan = []
        for flip in range(1, 8):
            px, py, pc = x ^ (flip >> 2), y ^ ((flip >> 1) & 1), c ^ (flip & 1)
            remote_plan.append((i, o.at[mine], (px, py, pc), o.at[4 * px + 2 * py + pc], None))
        return [(i, o.at[mine])], remote_plan

    return _comm(name, [block], [jax.ShapeDtypeStruct((8, *block.shape), block.dtype)], 1, 7, plan)[0]


def transpose_split(a, out_dtype, name):
    L, R, C = a.shape
    rh = R // 2
    rt = _pick(rh, 256)
    n = rh // rt

    def body(x_ref, o_ref):
        o_ref[0, 0] = x_ref[0].T.astype(o_ref.dtype)

    return _pcall(
        body,
        name=name,
        grid=(L, 2, n),
        in_specs=[pl.BlockSpec((1, rt, C), lambda l, h, i: (l, h * n + i, 0))],
        out_specs=pl.BlockSpec((1, 1, C, rt), lambda l, h, i: (l, h, 0, i)),
        out_shape=jax.ShapeDtypeStruct((L, 2, C, rh), out_dtype),
        compiler_params=_params(3),
    )(a)


def cast_split(a, out_dtype, name):
    L, R, C = a.shape
    rt = _pick(R, 512, 16)

    def body(x_ref, o_ref):
        o_ref[0, 0] = x_ref[0].astype(o_ref.dtype)

    return _pcall(
        body,
        name=name,
        grid=(L, 2, R // rt),
        in_specs=[pl.BlockSpec((1, rt, C // 2), lambda l, h, i: (l, i, h))],
        out_specs=pl.BlockSpec((1, 1, rt, C // 2), lambda l, h, i: (l, h, i, 0)),
        out_shape=jax.ShapeDtypeStruct((L, 2, R, C // 2), out_dtype),
        compiler_params=_params(3),
    )(a)


def untranspose_halves(mine, theirs, my_c, stack, l, n_layers, name):
    C, rh = mine.shape
    n_in = 2 if stack is None else 3
    rt = _pick(rh, 256)
    n = rh // rt

    def body(c_ref, a_ref, b_ref, *rest):
        o_ref = rest[-1]
        h = pl.program_id(0)
        o_ref[0] = jnp.where(h == c_ref[0], a_ref[...], b_ref[...]).T

    grid_spec = pltpu.PrefetchScalarGridSpec(
        num_scalar_prefetch=1,
        grid=(2, n),
        in_specs=[pl.BlockSpec((C, rt), lambda h, i, c: (0, i))] * 2
        + ([] if stack is None else [pl.BlockSpec(memory_space=pl.ANY)]),
        out_specs=pl.BlockSpec((1, rt, C), lambda h, i, c: (l, h * n + i, 0)),
    )
    return _pcall(
        body,
        name=name,
        grid_spec=grid_spec,
        out_shape=jax.ShapeDtypeStruct((n_layers, 2 * rh, C), F32),
        input_output_aliases={} if stack is None else {n_in: 0},
        compiler_params=_params(2),
    )(my_c.reshape(1), mine, theirs, *([] if stack is None else [stack]))


def merge_halves(mine, theirs, my_c, stack, l, n_layers, name):
    R, ch = mine.shape
    n_in = 2 if stack is None else 3
    rt = _pick(R, 512, 8)

    def body(c_ref, a_ref, b_ref, *rest):
        o_ref = rest[-1]
        o_ref[0] = jnp.where(pl.program_id(0) == c_ref[0], a_ref[...], b_ref[...])

    grid_spec = pltpu.PrefetchScalarGridSpec(
        num_scalar_prefetch=1,
        grid=(2, R // rt),
        in_specs=[pl.BlockSpec((rt, ch), lambda h, i, c: (i, 0))] * 2
        + ([] if stack is None else [pl.BlockSpec(memory_space=pl.ANY)]),
        out_specs=pl.BlockSpec((1, rt, ch), lambda h, i, c: (l, i, h)),
    )
    return _pcall(
        body,
        name=name,
        grid_spec=grid_spec,
        out_shape=jax.ShapeDtypeStruct((n_layers, R, 2 * ch), F32),
        input_output_aliases={} if stack is None else {n_in: 0},
        compiler_params=_params(2),
    )(my_c.reshape(1), mine, theirs, *([] if stack is None else [stack]))


def add_pair(g, r, my_c, name):
    _, rows, w = g.shape
    tm = _pick(rows, max(16, min(1024, (1 << 20) // w // 16 * 16)), 16)

    def body(c_ref, g_ref, r_ref, o_ref):
        o_ref[...] = (g_ref[0].astype(F32) + r_ref[...].astype(F32)).astype(o_ref.dtype)

    grid_spec = pltpu.PrefetchScalarGridSpec(
        num_scalar_prefetch=1,
        grid=(rows // tm,),
        in_specs=[pl.BlockSpec((1, tm, w), lambda i, c: (c[0], i, 0)), pl.BlockSpec((tm, w), lambda i, c: (i, 0))],
        out_specs=pl.BlockSpec((tm, w), lambda i, c: (i, 0)),
    )
    return _pcall(
        body, name=name, grid_spec=grid_spec, out_shape=jax.ShapeDtypeStruct((rows, w), BF16), compiler_params=_params(1)
    )(my_c.reshape(1), g, r)


class Dims:
    def __init__(self, S, D, L, ret_heads, mla_heads, q_rank, kv_rank):
        self.S, self.D, self.L, self.HR, self.HM, self.QR, self.KR = S, D, L, ret_heads, mla_heads, q_rank, kv_rank
        self.RQ, self.RV, self.MV = ret_heads * RET_DK, ret_heads * RET_DV, mla_heads * MLA_DV
        self.D_IN = 2 * self.RQ + 2 * self.RV + q_rank + kv_rank + MLA_ROPE + self.MV + 2 * D
        self.lo = 2 * self.RQ + 2 * self.RV
        self.mid = q_rank + kv_rank + MLA_ROPE
        self.DP = self.D_IN + LANE - MLA_ROPE
        self.o_rq, self.o_rk, self.o_rv, self.o_rg = 0, self.RQ, 2 * self.RQ, 2 * self.RQ + self.RV
        self.o_mg = self.lo
        self.o_bga = self.o_mg + self.MV
        self.o_bgb = self.o_bga + D
        self.o_cq = self.o_bgb + D
        self.o_ckv = self.o_cq + q_rank
        self.o_kr = self.o_ckv + kv_rank

    def pad_uq_rows(self, w):
        heads = w.shape[-2] // (MLA_NOPE + MLA_ROPE)
        w = w.reshape(*w.shape[:-2], heads, MLA_NOPE + MLA_ROPE, w.shape[-1])
        w = jnp.pad(w, [(0, 0)] * (w.ndim - 2) + [(0, MLA_QW - MLA_NOPE - MLA_ROPE), (0, 0)])
        return w.reshape(*w.shape[:-3], heads * MLA_QW, w.shape[-1])

    def unpad_uq_rows(self, w):
        heads = w.shape[-2] // MLA_QW
        w = w.reshape(*w.shape[:-2], heads, MLA_QW, w.shape[-1])[..., : MLA_NOPE + MLA_ROPE, :]
        return w.reshape(*w.shape[:-3], heads * (MLA_NOPE + MLA_ROPE), w.shape[-1])

    def placements(self):
        rows_in = self.D_IN // 4
        lo, mid, n_hi = self.lo, self.mid, self.D_IN - self.lo - self.mid
        cuts = {b % rows_in for b in (lo, lo + mid)} - {0}

        def offset_in(j, start):
            g = j * rows_in + start
            return jnp.where(g < lo, g, jnp.where(g < lo + mid, g + n_hi, g - mid))

        return [
            Placement(rows_in, self.DP, cuts, offset_in, zero_rows=(self.D_IN, self.DP - self.D_IN)),
            Placement(self.HM * MLA_QW // 4, self.HM * MLA_QW),
            Placement(self.HM * (MLA_NOPE + MLA_DV) // 4, self.HM * (MLA_NOPE + MLA_DV)),
            Placement(self.RV // 4, self.RV),
            Placement(self.MV // 4, self.MV),
            Placement(self.D // 4, self.D),
        ]


def _rope_tables(positions, dim, width):
    inv = 1.0 / (ROPE_BASE ** (jnp.arange(0, dim, 2, dtype=F32) / dim))
    ang = positions.astype(F32)[:, None] * inv
    cos, sin = jnp.cos(ang), jnp.sin(ang)
    pad = jnp.zeros((positions.shape[0], width - dim), F32)
    return jnp.concatenate([cos, cos, pad], axis=1), jnp.concatenate([-sin, sin, pad], axis=1)


def _tiles(x):
    return [x[:, t * LANE : (t + 1) * LANE] for t in range(x.shape[1] // LANE)]


def _cat(parts):
    return parts[0] if len(parts) == 1 else jnp.concatenate(parts, axis=1)


def _rms(x, eps=EPS):
    return lax.rsqrt(jnp.mean(x * x, axis=1, keepdims=True) + eps)


def layer_fwd(dm, l, x, shift, scale, gate, g_norm, g_cq, g_ckv, w, tabs, host=None, host_in=None, late_keys=()):
    cos_r, sin_r, cos_m, sin_m = tabs
    nm = lambda s: f"l{l}_{s}"

    def f_norm(x, g, scale, shift):
        return [x * _rms(x) * g * (1.0 + scale) + shift], []

    (h,) = rowwise(f_norm, [x], [g_norm, scale, shift], [(dm.D, BF16)], tm=256, name=nm("norm"))
    proj, *late = _as_list(matmul(h, w["in"], tb=True, b_split=True, name=nm("mm_in"), tn=1920, host=host_in))
    w = dict(w, **dict(zip(late_keys, late)))

    def f_rope_ret(rq, rk, cos, sin):
        rq, rk = rq.astype(F32), rk.astype(F32)
        q = _cat([_rope_tile(t, cos, sin, RET_DK // 2) for t in _tiles(rq)])
        k = _cat([_rope_tile(t, cos, sin, RET_DK // 2) * (RET_DK**-0.5) for t in _tiles(rk)])
        return [q, k], []

    rq, rk = rowwise(
        f_rope_ret,
        [win(proj, dm.o_rq, dm.RQ), win(proj, dm.o_rk, dm.RQ), cos_r, sin_r],
        [],
        [(dm.RQ, BF16)] * 2,
        tm=512,
        name=nm("rope_ret"),
    )
    o_ret, a_ret, states = ret_fwd(rq, rk, proj, dm.o_rv, dm.o_rg, heads=dm.HR, tb=RET_BLOCK, name=nm("ret_fwd"))

    def f_prep(cq, ckv, kr, cos, sin, g_cq, g_ckv):
        cq, ckv, kr = cq.astype(F32), ckv.astype(F32), kr.astype(F32)
        return [cq * _rms(cq) * g_cq, ckv * _rms(ckv) * g_ckv, _rope_tile(kr, cos, sin, MLA_ROPE // 2)], []

    cqn, ckvn, krr = rowwise(
        f_prep,
        [win(proj, dm.o_cq, dm.QR), win(proj, dm.o_ckv, dm.KR), win(proj, dm.o_kr, LANE), cos_m, sin_m],
        [g_cq, g_ckv],
        [(dm.QR, BF16), (dm.KR, BF16), (LANE, BF16)],
        tm=512,
        name=nm("mla_prep"),
    )
    q_raw = matmul(cqn, w["uq"], tb=True, b_split=True, name=nm("mm_uq"), tn=4096)
    kv = matmul(ckvn, w["ukv"], tb=True, b_split=True, name=nm("mm_ukv"), tn=4096)

    def f_rope_q(q, cos, sin):
        t = _tiles(q.astype(F32))
        rot = [t[n] if n % 2 == 0 else _rope_tile(t[n], cos, sin, MLA_ROPE // 2) for n in range(len(t))]
        return [_cat([r * (MLA_SCALE * LOG2E) for r in rot])], []

    (q,) = rowwise(f_rope_q, [q_raw, cos_m, sin_m], [], [(dm.HM * MLA_QW, BF16)], tm=256, name=nm("rope_q"))
    o_mla, a_mla, lse, *hosted = attn_fwd(
        q, kv, krr, proj, dm.o_mg, heads=dm.HM, tq=ATTN_BLOCK, name=nm("attn_fwd"), host=host
    )

    y_ret = matmul(a_ret, w["ret"], b_split=True, name=nm("mm_ret"))
    y_mla = matmul(a_mla, w["mla"], b_split=True, name=nm("mm_mla"))

    def f_merge(y_ret, y_mla, bga, bgb):
        return [_sigmoid(bga.astype(F32)) * y_ret.astype(F32) + _sigmoid(bgb.astype(F32)) * y_mla.astype(F32)], []

    (merged,) = rowwise(
        f_merge, [y_ret, y_mla, win(proj, dm.o_bga, dm.D), win(proj, dm.o_bgb, dm.D)], [], [(dm.D, BF16)], tm=256, name=nm("merge")
    )
    out = matmul(merged, w["out"], b_split=True, out_dtype=F32, name=nm("mm_out"))

    def f_resid(x, out, gate):
        return [x + gate * out], []

    (x_new,) = rowwise(f_resid, [x, out], [gate], [(dm.D, F32)], tm=256, name=nm("resid"))
    saved = dict(
        x=x, h=h, proj=proj, rq=rq, rk=rk, o_ret=o_ret, a_ret=a_ret, states=states, cqn=cqn, ckvn=ckvn, krr=krr, q=q, kv=kv,
        o_mla=o_mla, a_mla=a_mla, lse=lse, y_ret=y_ret, y_mla=y_mla, merged=merged, out=out,
    )
    return x_new, saved, hosted, w


def _as_list(v):
    return list(v) if isinstance(v, (list, tuple)) else [v]


def layer_bwd(
    dm, l, dx_out, sv, shift, scale, gate, g_norm, g_cq, g_ckv, w, tabs, host=None, exchange_dh=None, exchange_dw=None
):
    cos_r, sin_r, cos_m, sin_m = tabs
    nm = lambda s: f"l{l}_{s}"
    proj = sv["proj"]

    def b_resid(dx, out, gate):
        return [dx * gate], [_sum0(dx * out)]

    dout, d_gate = rowwise(b_resid, [dx_out, sv["out"]], [gate], [(dm.D, BF16)], [dm.D], tm=256, name=nm("resid_bwd"))
    dmerged = matmul(dout, w["out"], tb=True, b_split=True, name=nm("mm_dmerged"))
    dw_out = matmul(sv["merged"], dout, ta=True, out_split=True, tn=2048, name=nm("mm_dw_out"))

    def b_merge(dmg, y_ret, y_mla, bga, bgb):
        dmg = dmg.astype(F32)
        ga, gb = _sigmoid(bga.astype(F32)), _sigmoid(bgb.astype(F32))
        y_ret, y_mla = y_ret.astype(F32), y_mla.astype(F32)
        d_gates = jnp.concatenate([dmg * y_ret * ga * (1.0 - ga), dmg * y_mla * gb * (1.0 - gb)], axis=1)
        return [d_gates, dmg * ga, dmg * gb], []

    dproj, dy_ret, dy_mla = rowwise(
        b_merge,
        [dmerged, sv["y_ret"], sv["y_mla"], win(proj, dm.o_bga, dm.D), win(proj, dm.o_bgb, dm.D)],
        [],
        [(2 * dm.D, BF16), (dm.D, BF16), (dm.D, BF16)],
        tm=256,
        name=nm("merge_bwd"),
        into=(dm.DP, dm.o_bga),
    )
    da_ret = matmul(dy_ret, w["ret"], tb=True, b_split=True, name=nm("mm_da_ret"))
    dw_ret = matmul(sv["a_ret"], dy_ret, ta=True, out_split=True, tn=2048, name=nm("mm_dw_ret"))
    da_mla = matmul(dy_mla, w["mla"], tb=True, b_split=True, name=nm("mm_da_mla"))
    dw_mla = matmul(sv["a_mla"], dy_mla, ta=True, out_split=True, tn=2048, name=nm("mm_dw_mla"))

    def b_ret_gate(da, rg, o):
        da, rg = da.astype(F32), rg.astype(F32)
        do_parts, drg_parts = [], []
        for hh in range(dm.HR):
            sl = slice(hh * RET_DV, (hh + 1) * RET_DV)
            oh, dah, rgh = o[:, sl], da[:, sl], rg[:, sl]
            mu = jnp.mean(oh, axis=1, keepdims=True)
            d = oh - mu
            r = lax.rsqrt(jnp.mean(d * d, axis=1, keepdims=True) + EPS)
            n = d * r
            dn = dah * _silu(rgh)
            drg_parts.append(dah * n * _dsilu(rgh))
            do_parts.append(r * (dn - jnp.mean(dn, axis=1, keepdims=True) - n * jnp.mean(dn * n, axis=1, keepdims=True)))
        return [_cat(drg_parts), _cat(do_parts)], []

    dproj, do_ret = rowwise(
        b_ret_gate, [da_ret, win(proj, dm.o_rg, dm.RV), sv["o_ret"]], [], [(dm.RV, BF16)] * 2, tm=256, name=nm("ret_gate_bwd"),
        into=(dproj, dm.o_rg),
    )
    dq_rot, dk_rot, dproj = ret_bwd(
        sv["rq"], sv["rk"], proj, dm.o_rv, sv["states"], do_ret, heads=dm.HR, tb=RET_BLOCK, name=nm("ret_bwd"), into=(dproj, dm.o_rv)
    )

    def b_rope_ret(dq, dk, cos, sin):
        dq, dk = dq.astype(F32), dk.astype(F32)
        q = [_rope_tile(t, cos, sin, RET_DK // 2, inverse=True) for t in _tiles(dq)]
        k = [_rope_tile(t, cos, sin, RET_DK // 2, inverse=True) * (RET_DK**-0.5) for t in _tiles(dk)]
        return [_cat(q + k)], []

    (dproj,) = rowwise(
        b_rope_ret, [dq_rot, dk_rot, cos_r, sin_r], [], [(2 * dm.RQ, BF16)], tm=512, name=nm("rope_ret_bwd"), into=(dproj, dm.o_rq)
    )

    def b_mla_gate(da, mg, o):
        da, mg = da.astype(F32), mg.astype(F32)
        return [da * o * _dsilu(mg), da * _silu(mg)], []

    dproj, do_mla = rowwise(
        b_mla_gate, [da_mla, win(proj, dm.o_mg, dm.MV), sv["o_mla"]], [], [(dm.MV, BF16)] * 2, tm=256, name=nm("mla_gate_bwd"),
        into=(dproj, dm.o_mg),
    )
    dq_att, dkv, dkr_heads, *hosted = attn_bwd(
        sv["q"], sv["kv"], sv["krr"], do_mla, sv["o_mla"], sv["lse"], heads=dm.HM, tq=ATTN_BLOCK, name=nm("attn_bwd"), host=host
    )

    def b_rope_q(dq, cos, sin):
        t = _tiles(dq.astype(F32))
        return [_cat([t[n] if n % 2 == 0 else _rope_tile(t[n], cos, sin, MLA_ROPE // 2, inverse=True) for n in range(len(t))])], []

    (dq_raw,) = rowwise(b_rope_q, [dq_att, cos_m, sin_m], [], [(dm.HM * MLA_QW, BF16)], tm=256, name=nm("rope_q_bwd"))
    dcqn = matmul(dq_raw, w["uq"], b_split=True, name=nm("mm_dcqn"))
    dw_uq = matmul(dq_raw, sv["cqn"], ta=True, out_split=True, tn=2048, name=nm("mm_dw_uq"))
    dckvn = matmul(dkv, w["ukv"], b_split=True, name=nm("mm_dckvn"))
    dw_ukv = matmul(dkv, sv["ckvn"], ta=True, out_split=True, tn=2048, name=nm("mm_dw_ukv"))

    def b_prep(dcqn, dckvn, cq, ckv, cos, sin, dkr_h, g_cq, g_ckv):
        outs, accs = [], []
        for dn, z, g in ((dcqn, cq, g_cq), (dckvn, ckv, g_ckv)):
            dn, z = dn.astype(F32), z.astype(F32)
            n = z * _rms(z)
            dng = dn * g
            outs.append(_rms(z) * (dng - n * jnp.mean(dng * n, axis=1, keepdims=True)))
            accs.append(_sum0(dn * n))
        dkr = dkr_h[0]
        for hh in range(1, dm.HM):
            dkr = dkr + dkr_h[hh]
        return [_cat(outs), _rope_tile(dkr, cos, sin, MLA_ROPE // 2, inverse=True)], accs

    dproj, d_kr, dg_cq, dg_ckv = rowwise(
        b_prep,
        [dcqn, dckvn, win(proj, dm.o_cq, dm.QR), win(proj, dm.o_ckv, dm.KR), cos_m, sin_m, dkr_heads],
        [g_cq, g_ckv],
        [(dm.QR + dm.KR, BF16), (LANE, BF16)],
        [dm.QR, dm.KR],
        tm=256,
        name=nm("mla_prep_bwd"),
        into=(dproj, dm.o_cq),
    )
    (dproj,) = rowwise(lambda a: ([a], []), [d_kr], [], [(LANE, BF16)], tm=512, name=nm("place_dkr"), into=(dproj, dm.o_kr))
    dws = dict(w_uq=dw_uq, w_ukv=dw_ukv, w_ret_proj=dw_ret, w_mla_proj=dw_mla, w_out=dw_out)
    host_dw = exchange_dw(dws, hosted) if exchange_dw else None
    dw_in, *hosted_dw = _as_list(
        matmul(dproj, sv["h"], ta=True, out_split=True, tn=2048, name=nm("mm_dw_in"), host=host_dw)
    )
    dws["w_in"] = dw_in
    host_dh = exchange_dh(dws) if exchange_dh else None
    dh, *hosted_dh = _as_list(
        matmul(dproj, w["in"], b_split=True, out_dtype=F32, name=nm("mm_dh"), tn=2048, tk=1920, host=host_dh)
    )

    def b_norm(dh, x, dx_res, g, scale):
        r = _rms(x)
        xn = x * r
        dxn = dh * g * (1.0 + scale)
        dx = dx_res + r * (dxn - xn * jnp.mean(dxn * xn, axis=1, keepdims=True))
        return [dx], [_sum0(dh), _sum0(dh * xn * g), _sum0(dh * (1.0 + scale) * xn)]

    dx, d_shift, d_scale, dg_norm = rowwise(
        b_norm, [dh, sv["x"], dx_out], [g_norm, scale], [(dm.D, F32)], [dm.D] * 3, tm=256, name=nm("norm_bwd")
    )
    dvec = dict(mod=jnp.concatenate([d_shift, d_scale, d_gate], axis=1), g_norm=dg_norm, g_cq=dg_cq, g_ckv=dg_ckv)
    return dx, dws, dvec, hosted, hosted_dh, hosted_dw


def adamw(w, g, m, v, name):
    shape = w.shape
    cols = shape[-1]
    view = lambda a: a.reshape(-1, cols)

    def f(w, g, m, v):
        m = ADAM_B1 * m + (1.0 - ADAM_B1) * g
        v = ADAM_B2 * v + (1.0 - ADAM_B2) * (g * g)
        m_hat = m / (1.0 - ADAM_B1**ADAM_STEP)
        v_hat = v / (1.0 - ADAM_B2**ADAM_STEP)
        delta = -ADAM_LR * (m_hat / (jnp.sqrt(v_hat) + ADAM_EPS) + ADAM_WD * w)
        return [delta, m, v], []

    tm = max(8, min(512, (400_000 // cols) // 8 * 8))
    delta, m, v = rowwise(f, [view(w), view(g), view(m), view(v)], [], [(cols, F32)] * 3, tm=tm, name=name)
    return delta.reshape(shape), m.reshape(shape), v.reshape(shape)


def _add_rows(fn, rows, cols, dtype, name):
    tm = max(8, min(512, (400_000 // cols) // 8 * 8))
    return rowwise(lambda *a: ([fn(*a)], []), rows, [], [(cols, dtype)], tm=tm, name=name)[0]


BIG = ("w_in", "w_uq", "w_ukv", "w_ret_proj", "w_mla_proj", "w_out")
COL_SHARDED = ("w_in", "w_uq", "w_ukv")


def kernel(x, c, positions, w_mod, b_mod, g_norm, w_in, g_cq, g_ckv, w_uq, w_ukv, w_ret_proj, w_mla_proj, w_out, g_final, loss_target, m_w_mod, m_b_mod, m_g_norm, m_w_in, m_g_cq, m_g_ckv, m_w_uq, m_w_ukv, m_w_ret_proj, m_w_mla_proj, m_w_out, m_g_final, v_w_mod, v_b_mod, v_g_norm, v_w_in, v_g_cq, v_g_ckv, v_w_uq, v_w_ukv, v_w_ret_proj, v_w_mla_proj, v_w_out, v_g_final):
    weights = dict(w_mod=w_mod, b_mod=b_mod, g_norm=g_norm, w_in=w_in, g_cq=g_cq, g_ckv=g_ckv, w_uq=w_uq, w_ukv=w_ukv,
                   w_ret_proj=w_ret_proj, w_mla_proj=w_mla_proj, w_out=w_out, g_final=g_final)
    m_in = dict(w_mod=m_w_mod, b_mod=m_b_mod, g_norm=m_g_norm, w_in=m_w_in, g_cq=m_g_cq, g_ckv=m_g_ckv, w_uq=m_w_uq,
                w_ukv=m_w_ukv, w_ret_proj=m_w_ret_proj, w_mla_proj=m_w_mla_proj, w_out=m_w_out, g_final=m_g_final)
    v_in = dict(w_mod=v_w_mod, b_mod=v_b_mod, g_norm=v_g_norm, w_in=v_w_in, g_cq=v_g_cq, g_ckv=v_g_ckv, w_uq=v_w_uq,
                w_ukv=v_w_ukv, w_ret_proj=v_w_ret_proj, w_mla_proj=v_w_mla_proj, w_out=v_w_out, g_final=v_g_final)
    order = ("w_mod", "b_mod", "g_norm", "w_in", "g_cq", "g_ckv", "w_uq", "w_ukv", "w_ret_proj", "w_mla_proj", "w_out", "g_final")

    x = x[0]
    target = loss_target[0]
    S, D = x.shape
    L = w_mod.shape[0]
    dm = Dims(S, D, L, w_ret_proj.shape[1] * 4 // RET_DV, w_mla_proj.shape[1] * 4 // MLA_DV, g_cq.shape[1], g_ckv.shape[1])
    my_x, my_y, my_c = _me()
    my_chip = 2 * my_x + my_y
    my_dev = 2 * my_chip + my_c
    C3 = w_mod.shape[2]

    for table in (weights, m_in, v_in):
        table["w_in"] = jnp.swapaxes(table["w_in"], 1, 2)
    prepped = [
        cast_split(weights["w_in"], BF16, "prep_w_in"),
        dm.pad_uq_rows(transpose_split(w_uq, BF16, "prep_w_uq")),
        transpose_split(w_ukv, BF16, "prep_w_ukv"),
        cast_split(w_ret_proj, BF16, "prep_w_ret"),
        cast_split(w_mla_proj, BF16, "prep_w_mla"),
        cast_split(w_out, BF16, "prep_w_out"),
    ]
    places = dm.placements()
    zero_rows = jnp.zeros((places[0].zero_rows[1], D // 2), BF16)
    w_keys = ("in", "uq", "ukv", "ret", "mla", "out")
    ahead, own = (0, 4), (1, 2, 3, 5)

    def gather_of(which, l):
        return gather_layer([prepped[n] for n in which], zero_rows, [places[n] for n in which], l, None)

    layer_w = [dict(zip(w_keys, gather_layer(prepped[:1], zero_rows, places[:1], 0, "l0_gather_w_in")))]

    c_all = allgather8(c, "gather_c").reshape(8, D)
    (c_act,) = rowwise(lambda z: ([_silu(z)], []), [c_all], [], [(D, BF16)], tm=8, name="silu_c")
    mod_part = jnp.stack([matmul(c_act, w_mod[l], out_dtype=F32, name=f"l{l}_mm_mod", tn=C3) for l in range(L)])
    mod_all = allgather8(mod_part, "gather_mod")
    mod_all = mod_all.reshape(4, 2, L, 8, C3)[:, 0].transpose(1, 2, 0, 3).reshape(L, 8, 3 * D) + b_mod[:, None, :]
    mod = lax.dynamic_index_in_dim(mod_all, my_dev, axis=1, keepdims=False)

    pos = positions[0]
    tabs = (*_rope_tables(pos, RET_DK, LANE), *_rope_tables(pos, MLA_ROPE, LANE))

    def vecs(l):
        return (mod[l : l + 1, :D], mod[l : l + 1, D : 2 * D], mod[l : l + 1, 2 * D :],
                g_norm[l : l + 1], g_cq[l : l + 1], g_ckv[l : l + 1])

    saved = []
    for l in range(L):
        mine = (1, 2, 3, 4, 5) if l == 0 else own
        host = gather_of(ahead, l + 1) if l + 1 < L else None
        x, sv, hosted, layer_w[l] = layer_fwd(
            dm, l, x, *vecs(l), layer_w[l], tabs, host=host, host_in=gather_of(mine, l), late_keys=[w_keys[n] for n in mine]
        )
        saved.append(sv)
        if host:
            layer_w.append(dict(zip([w_keys[n] for n in ahead], hosted)))

    def f_loss(x, t, g):
        xn = x * _rms(x)
        err = xn * g - t
        dy = err * (1.0 / D)
        dxn = dy * g
        dx = _rms(x) * (dxn - xn * jnp.mean(dxn * xn, axis=1, keepdims=True))
        part = jnp.sum(jnp.sum(err * err, axis=1, keepdims=True), axis=0, keepdims=True) * (0.5 / D)
        return [dx], [jnp.broadcast_to(part, (1, LANE)), _sum0(dy * xn)]

    dx, loss_part, dg_final = rowwise(f_loss, [x, target], [g_final.reshape(1, D)], [(D, F32)], [LANE, D], tm=256, name="loss_head")

    stacks = {n: None for n in BIG}
    dvec = {n: [None] * L for n in ("mod", "g_norm", "g_cq", "g_ckv")}

    def reduce_chips(l, from_chips):
        return [
            _add_rows(lambda a: ((a[3].astype(F32) + a[0].astype(F32)) + a[1].astype(F32)) + a[2].astype(F32), [r], r.shape[-1],
                      F32, f"l{l}_rs_add4_{n}")
            for n, r in zip(BIG, from_chips)
        ]

    def finish_grads(l, from_chips):
        mine = reduce_chips(l, from_chips)
        store_grads(l, mine, sibling_swap(mine, f"l{l}_rs_share"))

    def store_grads(l, mine, theirs):
        for n, a, b in zip(BIG, mine, theirs):
            if n == "w_uq":
                a, b = dm.unpad_uq_rows(a), dm.unpad_uq_rows(b)
            finish = untranspose_halves if n in ("w_uq", "w_ukv") else merge_halves
            stacks[n] = finish(a, b, my_c, stacks[n], l, L, f"l{l}_grad_{n}")

    def pair_sums(l, names, partial, from_sibling):
        return [add_pair(g, r, my_c, f"l{l}_rs_add2_{n}") for n, g, r in zip(names, partial, from_sibling)]

    def core_exchange(dws):
        return sibling_scatter([dws[n] for n in BIG], None)

    def last_exchange(names, which, tag):
        def build(dws, _=None):
            partial = [dws[n] for n in names]
            pairs = pair_sums(0, names, partial, sibling_scatter(partial, f"l0_rs_cores_{tag}"))
            return scatter_to_chips(pairs, which, None)

        return build

    pair = None
    for l in reversed(range(L)):
        host = scatter_to_chips(pair, places, None) if pair is not None else None
        halves = {}

        def share_halves(dws, from_chips, l=l, halves=halves):
            halves["mine"] = reduce_chips(l + 1, from_chips)
            return sibling_swap(halves["mine"], None)

        hooks = dict(exchange_dh=core_exchange, exchange_dw=share_halves if host else None)
        if l == 0:
            hooks = dict(exchange_dw=last_exchange(BIG[1:], places[1:], "rest"), exchange_dh=last_exchange(BIG[:1], places[:1], "in"))
        dx, dw_l, dv_l, hosted, hosted_dh, hosted_dw = layer_bwd(
            dm, l, dx, saved[l], *vecs(l), layer_w[l], tabs, host=host, **hooks
        )
        if host and l > 0:
            store_grads(l + 1, halves["mine"], hosted_dw)
        elif host:
            finish_grads(l + 1, hosted)
        for n in dvec:
            dvec[n][l] = dv_l[n]
        if l > 0:
            pair = pair_sums(l, BIG, [dw_l[n] for n in BIG], hosted_dh)
        else:
            finish_grads(0, [*hosted_dh, *hosted_dw])
    grad_x = dx[None]

    pieces = [loss_part] + [jnp.concatenate(dvec[n], axis=1) for n in ("mod", "g_norm", "g_cq", "g_ckv")] + [dg_final]
    widths = [p.shape[1] for p in pieces]
    small_all = allgather8(jnp.concatenate(pieces, axis=1), "gather_small").reshape(8, sum(widths))

    def sum8_body(a_ref, o_ref):
        acc = a_ref[0:1, :]
        for d in range(1, 8):
            acc = acc + a_ref[d : d + 1, :]
        o_ref[...] = acc

    small = _pcall(sum8_body, name="sum_small", out_shape=jax.ShapeDtypeStruct((1, sum(widths)), F32))(small_all)
    offs = np.cumsum([0] + widths)
    loss = small[0, 0]
    g_small = {
        "b_mod": small[0, offs[1] : offs[2]].reshape(L, 3 * D),
        "g_norm": small[0, offs[2] : offs[3]].reshape(L, D),
        "g_cq": small[0, offs[3] : offs[4]].reshape(L, dm.QR),
        "g_ckv": small[0, offs[4] : offs[5]].reshape(L, dm.KR),
        "g_final": small[0, offs[5] : offs[6]],
    }

    dmod_all = small_all[:, offs[1] : offs[2]].reshape(8, L, 3 * D)
    dmod_mine = lax.dynamic_slice_in_dim(dmod_all, my_chip * C3, C3, axis=2)
    pad8 = lambda a: jnp.pad(a, ((0, LANE - 8), (0, 0)))
    grads = dict(g_small)
    grads["w_mod"] = jnp.stack(
        [matmul(pad8(c_act), pad8(dmod_mine[:, l]), ta=True, out_dtype=F32, name=f"l{l}_mm_dw_mod", tn=C3) for l in range(L)]
    )

    for n in BIG:
        grads[n] = stacks[n]

    deltas, new_m, new_v = {}, {}, {}
    for n in order:
        wv, gv, mv, vv = weights[n], grads[n], m_in[n], v_in[n]
        if wv.ndim == 1:
            wv, gv, mv, vv = (a.reshape(1, -1) for a in (wv, gv, mv, vv))
        d_, m_, v_ = adamw(wv, gv, mv, vv, f"adamw_{n}")
        deltas[n], new_m[n], new_v[n] = (a.reshape(weights[n].shape) for a in (d_, m_, v_))
        grads[n] = grads[n].reshape(weights[n].shape)
    for table in (grads, deltas, new_m, new_v):
        table["w_in"] = jnp.swapaxes(table["w_in"], 1, 2)

    return (loss, grad_x, *[grads[n] for n in order], *[deltas[n] for n in order], *[new_m[n] for n in order],
            *[new_v[n] for n in order])
```
